```python
import jax, jax.numpy as jnp
from jax import lax
import numpy as np

D_MODEL = 1024
BATCH = 32
SEQ = 2048
DEPTH = 4

MEM_LEN = 256
MEM_HEADS = 4
MEM_HEAD_DIM = D_MODEL // 8
RET_HEADS = 4
RET_QK_DIM = D_MODEL // 8
RET_V_DIM = D_MODEL // 4
RET_CHUNK = 128
ROPE_BASE = 10000.0
POOL_WINDOWS = (2, 4, 8, 16)
POOL_GROUP = D_MODEL // 8
POOL_WIDTH = POOL_GROUP * len(POOL_WINDOWS)
FFN_HIDDEN = 4 * D_MODEL
N_BRANCHES = 3
EPS = 1e-6

RET_QK_W = RET_HEADS * RET_QK_DIM
RET_V_W = RET_HEADS * RET_V_DIM
MEM_Q_W = MEM_HEADS * MEM_HEAD_DIM
COL_SIZES = (RET_QK_W, RET_QK_W, RET_V_W, RET_V_W, POOL_WIDTH, MEM_Q_W, N_BRANCHES * D_MODEL)
COL_SPLITS = tuple(int(s) for s in np.cumsum(COL_SIZES)[:-1])
IN_PROJ_W = int(sum(COL_SIZES))

kernel_name = "hybrid_retention_pool_memory_encoder"


def rms_norm(x, g):
    xf = x.astype(jnp.float32)
    y = xf * lax.rsqrt(jnp.mean(jnp.square(xf), axis=-1, keepdims=True) + EPS)
    return (y * g.astype(jnp.float32)).astype(x.dtype)


def rotary(x, pos):
    d = x.shape[-1]
    inv = ROPE_BASE ** (-jnp.arange(0, d, 2, dtype=jnp.float32) / d)
    ang = pos.astype(jnp.float32)[:, None] * inv[None, :]
    cos = jnp.cos(ang)[None, :, None, :].astype(x.dtype)
    sin = jnp.sin(ang)[None, :, None, :].astype(x.dtype)
    x1, x2 = jnp.split(x, 2, axis=-1)
    return jnp.concatenate([x1 * cos - x2 * sin, x1 * sin + x2 * cos], axis=-1)


def retention_one_direction(q, k, v, log_g, strict):
    B, S, H, dk = q.shape
    dv = v.shape[-1]
    C = RET_CHUNK
    N = S // C
    q = q.reshape(B, N, C, H, dk)
    k = k.reshape(B, N, C, H, dk)
    v = v.reshape(B, N, C, H, dv)
    idx = jnp.arange(C, dtype=jnp.float32)
    diff = idx[:, None] - idx[None, :]
    mask = (diff > 0) if strict else (diff >= 0)
    dmat = jnp.where(mask[None], jnp.exp(log_g[:, None, None] * jnp.where(mask, diff, 0.0)[None]), 0.0)
    s = jnp.einsum('bnihd,bnjhd->bnhij', q, k) * dmat
    o_intra = jnp.einsum('bnhij,bnjhe->bnihe', s, v)
    zeta = jnp.exp(log_g[None, :] * (C - 1 - idx)[:, None])
    xi = jnp.exp(log_g[None, :] * (idx + 1)[:, None])
    chunk_decay = jnp.exp(log_g * C)
    qx = q * xi[:, :, None]
    kz = k * zeta[:, :, None]

    def step(state, inp):
        qc, kc, vc = inp
        out = jnp.einsum('bihd,bhde->bihe', qc, state)
        state = state * chunk_decay[None, :, None, None] + jnp.einsum('bjhd,bjhe->bhde', kc, vc)
        return state, out

    state0 = jnp.zeros((B, H, dk, dv), jnp.float32)
    _, o_cross = lax.scan(step, state0, (jnp.moveaxis(qx, 1, 0), jnp.moveaxis(kz, 1, 0), jnp.moveaxis(v, 1, 0)))
    o = o_intra.astype(jnp.float32) + jnp.moveaxis(o_cross, 0, 1)
    return o.reshape(B, S, H, dv)


def retention_branch(q, k, v, g, decay_logit, w_o, pos):
    B, S, _ = q.shape
    q = rotary(q.reshape(B, S, RET_HEADS, RET_QK_DIM), pos)
    k = rotary(k.reshape(B, S, RET_HEADS, RET_QK_DIM), pos) * (RET_QK_DIM ** -0.5)
    v = v.reshape(B, S, RET_HEADS, RET_V_DIM)
    log_g = jax.nn.log_sigmoid(decay_logit.astype(jnp.float32))
    o_fwd = retention_one_direction(q, k, v, log_g[0], strict=False)
    o_bwd = jnp.flip(retention_one_direction(jnp.flip(q, 1), jnp.flip(k, 1), jnp.flip(v, 1), log_g[1], strict=True), 1)
    o = o_fwd + o_bwd
    mu = jnp.mean(o, axis=-1, keepdims=True)
    var = jnp.mean(jnp.square(o - mu), axis=-1, keepdims=True)
    o = ((o - mu) * lax.rsqrt(var + EPS)).reshape(B, S, RET_V_W).astype(g.dtype)
    return (o * jax.nn.silu(g)) @ w_o


def pool_branch(p, w_grp, scale, w_o):
    B, S, _ = p.shape
    pf = p.astype(jnp.float32)
    cs = jnp.concatenate([jnp.zeros((B, 1, POOL_WIDTH), jnp.float32), jnp.cumsum(pf, axis=1)], axis=1)
    n = jnp.arange(S)
    groups = []
    for gi, w in enumerate(POOL_WINDOWS):
        lo = jnp.clip(n - w // 2, 0, S)
        hi = jnp.clip(n + w // 2, 0, S)
        sl = slice(gi * POOL_GROUP, (gi + 1) * POOL_GROUP)
        csg = cs[..., sl]
        mean = (csg[:, hi] - csg[:, lo]) / (hi - lo).astype(jnp.float32)[None, :, None]
        groups.append(mean - pf[..., sl])
    mixed = jnp.stack(groups, axis=2)
    y = jnp.einsum('bsgc,gcd->bsgd', mixed, w_grp.astype(jnp.float32)).reshape(B, S, POOL_WIDTH)
    y = (y * scale.astype(jnp.float32)).astype(p.dtype)
    return y @ w_o


def memory_branch(q, mem_n, w_kv, w_o):
    B, S, _ = q.shape
    M = mem_n.shape[1]
    q = q.reshape(B, S, MEM_HEADS, MEM_HEAD_DIM)
    kv = mem_n @ w_kv
    k, v = jnp.split(kv, 2, axis=-1)
    k = k.reshape(B, M, MEM_HEADS, MEM_HEAD_DIM)
    v = v.reshape(B, M, MEM_HEADS, MEM_HEAD_DIM)
    s = jnp.einsum('bshd,bmhd->bhsm', q, k).astype(jnp.float32) * (MEM_HEAD_DIM ** -0.5)
    a = jax.nn.softmax(s, axis=-1).astype(v.dtype)
    o = jnp.einsum('bhsm,bmhd->bshd', a, v).reshape(B, S, MEM_Q_W)
    return o @ w_o


def _fwd_setup_inputs(seed: int = 0) -> dict:
    key = jax.random.key(seed)
    ks = jax.random.split(key, 20)
    f32 = jnp.float32

    def dense(k, shape, fan_in):
        return jax.random.normal(k, shape, f32) * (fan_in ** -0.5)

    base = 1.0 - 2.0 ** (-5.0 - jnp.arange(RET_HEADS, dtype=f32))
    base_logit = jnp.log(base) - jnp.log1p(-base)
    decay_logit = base_logit[None, None, :] + 0.05 * jax.random.normal(ks[3], (DEPTH, 2, RET_HEADS), f32)
    return {
        "x": jax.random.normal(ks[0], (BATCH, SEQ, D_MODEL), f32),
        "mem": jax.random.normal(ks[1], (BATCH, MEM_LEN, D_MODEL), f32),
        "w_in": dense(ks[2], (DEPTH, D_MODEL, IN_PROJ_W), D_MODEL),
        "ret_decay_logit": decay_logit,
        "w_ret_o": dense(ks[4], (DEPTH, RET_V_W, D_MODEL), RET_V_W),
        "w_pool_grp": dense(ks[5], (DEPTH, len(POOL_WINDOWS), POOL_GROUP, POOL_GROUP), POOL_GROUP),
        "pool_scale": 1.0 + 0.1 * jax.random.normal(ks[6], (DEPTH, POOL_WIDTH), f32),
        "w_pool_o": dense(ks[7], (DEPTH, POOL_WIDTH, D_MODEL), POOL_WIDTH),
        "w_mem_kv": dense(ks[8], (DEPTH, D_MODEL, 2 * MEM_Q_W), D_MODEL),
        "w_mem_o": dense(ks[9], (DEPTH, MEM_Q_W, D_MODEL), MEM_Q_W),
        "w_out": dense(ks[10], (DEPTH, D_MODEL, D_MODEL), D_MODEL),
        "w_ff1": dense(ks[11], (DEPTH, D_MODEL, FFN_HIDDEN), D_MODEL),
        "w_ff2": dense(ks[12], (DEPTH, FFN_HIDDEN, D_MODEL), FFN_HIDDEN),
        "norm1_g": 1.0 + 0.05 * jax.random.normal(ks[13], (DEPTH, D_MODEL), f32),
        "norm2_g": 1.0 + 0.05 * jax.random.normal(ks[14], (DEPTH, D_MODEL), f32),
        "mem_norm_g": 1.0 + 0.05 * jax.random.normal(ks[15], (D_MODEL,), f32),
        "final_norm_g": 1.0 + 0.05 * jax.random.normal(ks[16], (D_MODEL,), f32),
    }


def _fwd_reference(x, mem, w_in, ret_decay_logit, w_ret_o, w_pool_grp, pool_scale, w_pool_o,
              w_mem_kv, w_mem_o, w_out, w_ff1, w_ff2, norm1_g, norm2_g, mem_norm_g, final_norm_g):
    S = x.shape[1]
    pos = jnp.arange(S)
    mem_n = rms_norm(mem, mem_norm_g)
    for l in range(DEPTH):
        h = rms_norm(x, norm1_g[l])
        proj = h @ w_in[l]
        q_r, k_r, v_r, g_r, p_in, q_m, gates = jnp.split(proj, COL_SPLITS, axis=-1)
        o_ret = retention_branch(q_r, k_r, v_r, g_r, ret_decay_logit[l], w_ret_o[l], pos)
        o_pool = pool_branch(p_in, w_pool_grp[l], pool_scale[l], w_pool_o[l])
        o_mem = memory_branch(q_m, mem_n, w_mem_kv[l], w_mem_o[l])
        gate_r, gate_p, gate_m = jnp.split(jax.nn.sigmoid(gates), N_BRANCHES, axis=-1)
        merged = (gate_r * o_ret + gate_p * o_pool + gate_m * o_mem).astype(x.dtype)
        x = x + merged @ w_out[l]
        h = rms_norm(x, norm2_g[l])
        x = x + jnp.square(jax.nn.relu(h @ w_ff1[l])) @ w_ff2[l]
    return rms_norm(x, final_norm_g)


import jax as _jax
import jax.numpy as _jnp

TWIN_FORMAT = 'train_step'
FWD_PARAMS = ['x', 'mem', 'w_in', 'ret_decay_logit', 'w_ret_o', 'w_pool_grp', 'pool_scale', 'w_pool_o', 'w_mem_kv', 'w_mem_o', 'w_out', 'w_ff1', 'w_ff2', 'norm1_g', 'norm2_g', 'mem_norm_g', 'final_norm_g']
TWIN_WEIGHTS = ['w_in', 'ret_decay_logit', 'w_ret_o', 'w_pool_grp', 'pool_scale', 'w_pool_o', 'w_mem_kv', 'w_mem_o', 'w_out', 'w_ff1', 'w_ff2', 'norm1_g', 'norm2_g', 'mem_norm_g', 'final_norm_g']
TWIN_DIFF_INPUT = 'x'
TWIN_INPUTS = ['x', 'mem', 'w_in', 'ret_decay_logit', 'w_ret_o', 'w_pool_grp', 'pool_scale', 'w_pool_o', 'w_mem_kv', 'w_mem_o', 'w_out', 'w_ff1', 'w_ff2', 'norm1_g', 'norm2_g', 'mem_norm_g', 'final_norm_g', 'loss_target', 'm_w_in', 'm_ret_decay_logit', 'm_w_ret_o', 'm_w_pool_grp', 'm_pool_scale', 'm_w_pool_o', 'm_w_mem_kv', 'm_w_mem_o', 'm_w_out', 'm_w_ff1', 'm_w_ff2', 'm_norm1_g', 'm_norm2_g', 'm_mem_norm_g', 'm_final_norm_g', 'v_w_in', 'v_ret_decay_logit', 'v_w_ret_o', 'v_w_pool_grp', 'v_pool_scale', 'v_w_pool_o', 'v_w_mem_kv', 'v_w_mem_o', 'v_w_out', 'v_w_ff1', 'v_w_ff2', 'v_norm1_g', 'v_norm2_g', 'v_mem_norm_g', 'v_final_norm_g']
TWIN_OUTPUTS = ['loss', 'grad_x', 'grad_w_in', 'grad_ret_decay_logit', 'grad_w_ret_o', 'grad_w_pool_grp', 'grad_pool_scale', 'grad_w_pool_o', 'grad_w_mem_kv', 'grad_w_mem_o', 'grad_w_out', 'grad_w_ff1', 'grad_w_ff2', 'grad_norm1_g', 'grad_norm2_g', 'grad_mem_norm_g', 'grad_final_norm_g', 'delta_w_in', 'delta_ret_decay_logit', 'delta_w_ret_o', 'delta_w_pool_grp', 'delta_pool_scale', 'delta_w_pool_o', 'delta_w_mem_kv', 'delta_w_mem_o', 'delta_w_out', 'delta_w_ff1', 'delta_w_ff2', 'delta_norm1_g', 'delta_norm2_g', 'delta_mem_norm_g', 'delta_final_norm_g', 'new_m_w_in', 'new_m_ret_decay_logit', 'new_m_w_ret_o', 'new_m_w_pool_grp', 'new_m_pool_scale', 'new_m_w_pool_o', 'new_m_w_mem_kv', 'new_m_w_mem_o', 'new_m_w_out', 'new_m_w_ff1', 'new_m_w_ff2', 'new_m_norm1_g', 'new_m_norm2_g', 'new_m_mem_norm_g', 'new_m_final_norm_g', 'new_v_w_in', 'new_v_ret_decay_logit', 'new_v_w_ret_o', 'new_v_w_pool_grp', 'new_v_pool_scale', 'new_v_w_pool_o', 'new_v_w_mem_kv', 'new_v_w_mem_o', 'new_v_w_out', 'new_v_w_ff1', 'new_v_w_ff2', 'new_v_norm1_g', 'new_v_norm2_g', 'new_v_mem_norm_g', 'new_v_final_norm_g']
TWIN_LEAF_KINDS = {'loss': 'loss', 'grad_x': 'grad_x', 'grad_w_in': 'grad_w', 'grad_ret_decay_logit': 'grad_w', 'grad_w_ret_o': 'grad_w', 'grad_w_pool_grp': 'grad_w', 'grad_pool_scale': 'grad_w', 'grad_w_pool_o': 'grad_w', 'grad_w_mem_kv': 'grad_w', 'grad_w_mem_o': 'grad_w', 'grad_w_out': 'grad_w', 'grad_w_ff1': 'grad_w', 'grad_w_ff2': 'grad_w', 'grad_norm1_g': 'grad_w', 'grad_norm2_g': 'grad_w', 'grad_mem_norm_g': 'grad_w', 'grad_final_norm_g': 'grad_w', 'delta_w_in': 'delta_w', 'delta_ret_decay_logit': 'delta_w', 'delta_w_ret_o': 'delta_w', 'delta_w_pool_grp': 'delta_w', 'delta_pool_scale': 'delta_w', 'delta_w_pool_o': 'delta_w', 'delta_w_mem_kv': 'delta_w', 'delta_w_mem_o': 'delta_w', 'delta_w_out': 'delta_w', 'delta_w_ff1': 'delta_w', 'delta_w_ff2': 'delta_w', 'delta_norm1_g': 'delta_w', 'delta_norm2_g': 'delta_w', 'delta_mem_norm_g': 'delta_w', 'delta_final_norm_g': 'delta_w', 'new_m_w_in': 'new_m', 'new_m_ret_decay_logit': 'new_m', 'new_m_w_ret_o': 'new_m', 'new_m_w_pool_grp': 'new_m', 'new_m_pool_scale': 'new_m', 'new_m_w_pool_o': 'new_m', 'new_m_w_mem_kv': 'new_m', 'new_m_w_mem_o': 'new_m', 'new_m_w_out': 'new_m', 'new_m_w_ff1': 'new_m', 'new_m_w_ff2': 'new_m', 'new_m_norm1_g': 'new_m', 'new_m_norm2_g': 'new_m', 'new_m_mem_norm_g': 'new_m', 'new_m_final_norm_g': 'new_m', 'new_v_w_in': 'new_v', 'new_v_ret_decay_logit': 'new_v', 'new_v_w_ret_o': 'new_v', 'new_v_w_pool_grp': 'new_v', 'new_v_pool_scale': 'new_v', 'new_v_w_pool_o': 'new_v', 'new_v_w_mem_kv': 'new_v', 'new_v_w_mem_o': 'new_v', 'new_v_w_out': 'new_v', 'new_v_w_ff1': 'new_v', 'new_v_w_ff2': 'new_v', 'new_v_norm1_g': 'new_v', 'new_v_norm2_g': 'new_v', 'new_v_mem_norm_g': 'new_v', 'new_v_final_norm_g': 'new_v'}


def _forward(args):
    return _fwd_reference(*[args[k] for k in FWD_PARAMS])


def _output_shape():
    out = _jax.eval_shape(lambda: _forward(_fwd_setup_inputs(0)))
    return out.shape, out.dtype

N_MICROBATCH = 1
ADAM_LR = 0.001
ADAM_B1 = 0.9
ADAM_B2 = 0.999
ADAM_EPS = 1e-08
ADAM_WD = 0.01
ADAM_STEP = 10
PER_EXAMPLE_BATCH_AXIS = {'x': 0, 'mem': 0, 'loss_target': 0}
SHARED_INPUTS = []
_WEIGHT_DTYPES = {'w_in': _jnp.float32, 'ret_decay_logit': _jnp.float32, 'w_ret_o': _jnp.float32, 'w_pool_grp': _jnp.float32, 'pool_scale': _jnp.float32, 'w_pool_o': _jnp.float32, 'w_mem_kv': _jnp.float32, 'w_mem_o': _jnp.float32, 'w_out': _jnp.float32, 'w_ff1': _jnp.float32, 'w_ff2': _jnp.float32, 'norm1_g': _jnp.float32, 'norm2_g': _jnp.float32, 'mem_norm_g': _jnp.float32, 'final_norm_g': _jnp.float32}
MOMENT_SCALE = {'w_in': 6.093079e-02, 'ret_decay_logit': 6.633769e-01, 'w_ret_o': 6.155241e-02, 'w_pool_grp': 1.285329e-01, 'pool_scale': 1.314376e-01, 'w_pool_o': 9.000998e-02, 'w_mem_kv': 1.722509e-02, 'w_mem_o': 1.318369e-02, 'w_out': 1.093927e-01, 'w_ff1': 1.060434e-01, 'w_ff2': 2.996316e-01, 'norm1_g': 1.710288e-01, 'norm2_g': 2.126499e-01, 'mem_norm_g': 4.250783e-02, 'final_norm_g': 6.565170e+01}


def _to_microbatches(a, axis):
    t = _jnp.moveaxis(a, axis, 0)
    t = t.reshape((N_MICROBATCH, t.shape[0] // N_MICROBATCH) + t.shape[1:])
    return _jnp.moveaxis(t, 1, axis + 1)


def setup_inputs(seed: int = 0) -> dict:
    inp = _fwd_setup_inputs(seed)
    key = _jax.random.fold_in(_jax.random.key(seed), 7919)
    shape, _ = _output_shape()
    out = dict(inp)
    out["loss_target"] = _jax.random.normal(_jax.random.fold_in(key, 0), shape, _jnp.float32)
    for i, name in enumerate(TWIN_WEIGHTS):
        w = inp[name].astype(_jnp.float32)
        if MOMENT_SCALE is None:
            s = _jnp.sqrt(_jnp.mean(_jnp.square(w)) + 1e-30)
        else:
            s = MOMENT_SCALE[name]
        km, kv = _jax.random.split(_jax.random.fold_in(key, i + 1))
        out[name] = w
        out["m_" + name] = s * _jax.random.normal(km, w.shape, _jnp.float32)
        out["v_" + name] = (s * s) * _jax.random.uniform(kv, w.shape, _jnp.float32, 0.5, 1.5)
    if N_MICROBATCH > 1:
        for name, axis in PER_EXAMPLE_BATCH_AXIS.items():
            out[name] = _to_microbatches(out[name], axis)
    return {'x': out['x'], 'mem': out['mem'], 'w_in': out['w_in'], 'ret_decay_logit': out['ret_decay_logit'], 'w_ret_o': out['w_ret_o'], 'w_pool_grp': out['w_pool_grp'], 'pool_scale': out['pool_scale'], 'w_pool_o': out['w_pool_o'], 'w_mem_kv': out['w_mem_kv'], 'w_mem_o': out['w_mem_o'], 'w_out': out['w_out'], 'w_ff1': out['w_ff1'], 'w_ff2': out['w_ff2'], 'norm1_g': out['norm1_g'], 'norm2_g': out['norm2_g'], 'mem_norm_g': out['mem_norm_g'], 'final_norm_g': out['final_norm_g'], 'loss_target': out['loss_target'], 'm_w_in': out['m_w_in'], 'm_ret_decay_logit': out['m_ret_decay_logit'], 'm_w_ret_o': out['m_w_ret_o'], 'm_w_pool_grp': out['m_w_pool_grp'], 'm_pool_scale': out['m_pool_scale'], 'm_w_pool_o': out['m_w_pool_o'], 'm_w_mem_kv': out['m_w_mem_kv'], 'm_w_mem_o': out['m_w_mem_o'], 'm_w_out': out['m_w_out'], 'm_w_ff1': out['m_w_ff1'], 'm_w_ff2': out['m_w_ff2'], 'm_norm1_g': out['m_norm1_g'], 'm_norm2_g': out['m_norm2_g'], 'm_mem_norm_g': out['m_mem_norm_g'], 'm_final_norm_g': out['m_final_norm_g'], 'v_w_in': out['v_w_in'], 'v_ret_decay_logit': out['v_ret_decay_logit'], 'v_w_ret_o': out['v_w_ret_o'], 'v_w_pool_grp': out['v_w_pool_grp'], 'v_pool_scale': out['v_pool_scale'], 'v_w_pool_o': out['v_w_pool_o'], 'v_w_mem_kv': out['v_w_mem_kv'], 'v_w_mem_o': out['v_w_mem_o'], 'v_w_out': out['v_w_out'], 'v_w_ff1': out['v_w_ff1'], 'v_w_ff2': out['v_w_ff2'], 'v_norm1_g': out['v_norm1_g'], 'v_norm2_g': out['v_norm2_g'], 'v_mem_norm_g': out['v_mem_norm_g'], 'v_final_norm_g': out['v_final_norm_g']}


def _loss(weights, diff, rest, loss_target):
    with _jax.named_scope("forward"):
        args = {**rest, TWIN_DIFF_INPUT: diff, **{k: w.astype(_WEIGHT_DTYPES[k]) for k, w in weights.items()}}
        y = _forward(args)
    with _jax.named_scope("loss_head"):
        err = _jnp.square(y.astype(_jnp.float32) - loss_target)
        return 0.5 * _jnp.sum(_jnp.mean(err, axis=-1)) if err.ndim else 0.5 * err


def _adamw(w, g, m, v):
    m = ADAM_B1 * m + (1.0 - ADAM_B1) * g
    v = ADAM_B2 * v + (1.0 - ADAM_B2) * _jnp.square(g)
    m_hat = m / (1.0 - ADAM_B1 ** ADAM_STEP)
    v_hat = v / (1.0 - ADAM_B2 ** ADAM_STEP)
    delta = -ADAM_LR * (m_hat / (_jnp.sqrt(v_hat) + ADAM_EPS) + ADAM_WD * w)
    return delta, m, v


def reference(x, mem, w_in, ret_decay_logit, w_ret_o, w_pool_grp, pool_scale, w_pool_o, w_mem_kv, w_mem_o, w_out, w_ff1, w_ff2, norm1_g, norm2_g, mem_norm_g, final_norm_g, loss_target, m_w_in, m_ret_decay_logit, m_w_ret_o, m_w_pool_grp, m_pool_scale, m_w_pool_o, m_w_mem_kv, m_w_mem_o, m_w_out, m_w_ff1, m_w_ff2, m_norm1_g, m_norm2_g, m_mem_norm_g, m_final_norm_g, v_w_in, v_ret_decay_logit, v_w_ret_o, v_w_pool_grp, v_pool_scale, v_w_pool_o, v_w_mem_kv, v_w_mem_o, v_w_out, v_w_ff1, v_w_ff2, v_norm1_g, v_norm2_g, v_mem_norm_g, v_final_norm_g):
    given = dict(x=x, mem=mem, w_in=w_in, ret_decay_logit=ret_decay_logit, w_ret_o=w_ret_o, w_pool_grp=w_pool_grp, pool_scale=pool_scale, w_pool_o=w_pool_o, w_mem_kv=w_mem_kv, w_mem_o=w_mem_o, w_out=w_out, w_ff1=w_ff1, w_ff2=w_ff2, norm1_g=norm1_g, norm2_g=norm2_g, mem_norm_g=mem_norm_g, final_norm_g=final_norm_g, loss_target=loss_target, m_w_in=m_w_in, m_ret_decay_logit=m_ret_decay_logit, m_w_ret_o=m_w_ret_o, m_w_pool_grp=m_w_pool_grp, m_pool_scale=m_pool_scale, m_w_pool_o=m_w_pool_o, m_w_mem_kv=m_w_mem_kv, m_w_mem_o=m_w_mem_o, m_w_out=m_w_out, m_w_ff1=m_w_ff1, m_w_ff2=m_w_ff2, m_norm1_g=m_norm1_g, m_norm2_g=m_norm2_g, m_mem_norm_g=m_mem_norm_g, m_final_norm_g=m_final_norm_g, v_w_in=v_w_in, v_ret_decay_logit=v_ret_decay_logit, v_w_ret_o=v_w_ret_o, v_w_pool_grp=v_w_pool_grp, v_pool_scale=v_pool_scale, v_w_pool_o=v_w_pool_o, v_w_mem_kv=v_w_mem_kv, v_w_mem_o=v_w_mem_o, v_w_out=v_w_out, v_w_ff1=v_w_ff1, v_w_ff2=v_w_ff2, v_norm1_g=v_norm1_g, v_norm2_g=v_norm2_g, v_mem_norm_g=v_mem_norm_g, v_final_norm_g=v_final_norm_g)
    weights = {n: given[n] for n in TWIN_WEIGHTS}
    shared = {n: given[n] for n in SHARED_INPUTS}
    per_example = {n: given[n] for n in ['x', 'mem']}
    grad_fn = _jax.value_and_grad(_loss, argnums=(0, 1))

    def one_microbatch(ex, loss_target):
        ex = dict(ex)
        diff = ex.pop(TWIN_DIFF_INPUT)
        return grad_fn(weights, diff, {**shared, **ex}, loss_target)

    if N_MICROBATCH == 1:
        loss, (grad_w, grad_x) = one_microbatch(per_example, given["loss_target"])
    else:
        def body(carry, xs):
            loss_sum, grad_sum = carry
            l_k, (gw_k, gx_k) = one_microbatch(xs[0], xs[1])
            with _jax.named_scope("update"):
                return (loss_sum + l_k, _jax.tree.map(_jnp.add, grad_sum, gw_k)), gx_k

        init = (_jnp.zeros((), _jnp.float32), _jax.tree.map(_jnp.zeros_like, weights))
        (loss, grad_w), grad_x = _jax.lax.scan(body, init, (per_example, given["loss_target"]))
    with _jax.named_scope("update"):
        delta_w, new_m, new_v = {}, {}, {}
        for n in TWIN_WEIGHTS:
            delta_w[n], new_m[n], new_v[n] = _adamw(weights[n], grad_w[n], given["m_" + n], given["v_" + n])
    return (loss, grad_x, *[grad_w[n] for n in TWIN_WEIGHTS], *[delta_w[n] for n in TWIN_WEIGHTS],
            *[new_m[n] for n in TWIN_WEIGHTS], *[new_v[n] for n in TWIN_WEIGHTS])
```

```python
import functools

import jax
import jax.numpy as jnp
from jax import lax
from jax.experimental import pallas as pl
from jax.experimental.pallas import tpu as pltpu

F32 = jnp.float32
_MXU = jnp.bfloat16
_ACT = jnp.bfloat16
_COMM = jnp.bfloat16

D_MODEL = 1024
N_PROJ = 7168
FFN_HIDDEN = 4096
MEM_LEN = 256
N_HEADS = 4
QK_DIM = 128
V_DIM = 256
POOL_W = 512
MEMQ_W = 512
DEPTH = 4
N_CHIPS = 4
EPS = 1e-6
ROPE_BASE = 10000.0

C_Q, C_K, C_V, C_G, C_P, C_QM, C_GATE = 0, 512, 1024, 2048, 3072, 3584, 4096

ADAM_LR = 0.001
ADAM_B1 = 0.9
ADAM_B2 = 0.999
ADAM_EPS = 1e-08
ADAM_WD = 0.01
ADAM_STEP = 10

VMEM_LIMIT_BYTES = 56 * 1024 * 1024
MESH = pl.DeviceIdType.MESH
ANY = pl.BlockSpec(memory_space=pl.ANY)

_DN = {
    "nn": (((1,), (0,)), ((), ())),
    "nt": (((1,), (1,)), ((), ())),
    "tn": (((0,), (0,)), ((), ())),
}


def _call(body, **kw):
    return pl.pallas_call(body, **kw)


def _cp(sem=None):
    return pltpu.CompilerParams(dimension_semantics=sem, vmem_limit_bytes=VMEM_LIMIT_BYTES)


def _dot(a, b, kind="nn"):
    return lax.dot_general(a.astype(_MXU), b.astype(_MXU), _DN[kind], preferred_element_type=F32)


def _sds(shape, dtype):
    return jax.ShapeDtypeStruct(shape, dtype)


def _rms_rows(x, g):
    r = lax.rsqrt(jnp.mean(x * x, axis=-1, keepdims=True) + EPS)
    return x * r * g


def _relu2(u):
    r = jnp.maximum(u.astype(F32), 0.0)
    return r * r


def _add_res(r, e):
    return r + e.astype(F32)


def _relu2_bwd(r, u):
    return r * (2.0 * jnp.maximum(u.astype(F32), 0.0))


def _mm(name, kind, a, b, a_spec, b_spec, out_shape, o_spec, grid, acc_shape, *,
        pro=None, pro_in=(), pro_specs=(), epi=None, epi_in=(), epi_specs=()):
    nk = grid[2]
    npro, nepi = len(pro_in), len(epi_in)

    def body(*refs):
        a_ref, b_ref = refs[0], refs[1]
        pro_refs = refs[2:2 + npro]
        epi_refs = refs[2 + npro:2 + npro + nepi]
        o_ref = refs[2 + npro + nepi]
        av = a_ref[...]
        if pro is not None:
            av = pro(av, *[r[...] for r in pro_refs])
        part = _dot(av, b_ref[...], kind)

        def finish(r):
            if epi is not None:
                r = epi(r, *[e[...] for e in epi_refs])
            o_ref[...] = r.astype(o_ref.dtype)

        if nk == 1:
            finish(part)
        else:
            acc = refs[-1]
            k = pl.program_id(2)

            @pl.when(k == 0)
            def _():
                acc[...] = part

            @pl.when(k > 0)
            def _():
                acc[...] += part

            @pl.when(k == nk - 1)
            def _():
                finish(acc[...])

    scratch = [] if nk == 1 else [pltpu.VMEM(acc_shape, F32)]
    return _call(
        body, out_shape=out_shape, grid=grid,
        in_specs=[a_spec, b_spec, *pro_specs, *epi_specs], out_specs=o_spec,
        scratch_shapes=scratch, name=name,
        compiler_params=_cp(("parallel", "parallel", "arbitrary")),
    )(a, b, *pro_in, *epi_in)


def _row_tile(n, cap):
    t = min(n, cap)
    assert n % t == 0, (n, t)
    return t


def _rms_proj(name, x, g, w, n_cols, tn):
    T = x.shape[0]
    tm = _row_tile(T, 1024)
    per = (n_cols // N_CHIPS) // tn
    return _mm(
        name, "nn", x, w,
        pl.BlockSpec((tm, D_MODEL), lambda i, j, k: (i, 0)),
        pl.BlockSpec((None, D_MODEL, tn), lambda i, j, k: (j // per, 0, j % per)),
        _sds((T, n_cols), _ACT),
        pl.BlockSpec((tm, tn), lambda i, j, k: (i, j)),
        (T // tm, n_cols // tn, 1), None,
        pro=_rms_rows, pro_in=(g,), pro_specs=(pl.BlockSpec((1, D_MODEL), lambda i, j, k: (0, 0)),),
    )


def _rms_wgrad(name, x, g, dy, n_cols, tn):
    T = x.shape[0]
    tt = _row_tile(T, 512)
    per = (n_cols // N_CHIPS) // tn
    return _mm(
        name, "tn", x, dy,
        pl.BlockSpec((tt, D_MODEL), lambda i, j, k: (k, 0)),
        pl.BlockSpec((tt, tn), lambda i, j, k: (k, j)),
        _sds((N_CHIPS, D_MODEL, n_cols // N_CHIPS), _COMM),
        pl.BlockSpec((None, D_MODEL, tn), lambda i, j, k: (j // per, 0, j % per)),
        (1, n_cols // tn, T // tt), (D_MODEL, tn),
        pro=_rms_rows, pro_in=(g,), pro_specs=(pl.BlockSpec((1, D_MODEL), lambda i, j, k: (0, 0)),),
    )


def _dgrad_rms(name, dy, w, x, g, dres, tk):
    T, kd = dy.shape
    tm = _row_tile(T, 512)
    nk = kd // tk
    per = (kd // N_CHIPS) // tk

    def body(dy_ref, w_ref, x_ref, g_ref, dres_ref, dx_ref, dg_ref, acc):
        i = pl.program_id(0)
        k = pl.program_id(1)
        part = _dot(dy_ref[...], w_ref[...], "nt")

        @pl.when(k == 0)
        def _():
            acc[...] = part

        @pl.when(k > 0)
        def _():
            acc[...] += part

        @pl.when(k == nk - 1)
        def _():
            dh = acc[...]
            x = x_ref[...]
            r = lax.rsqrt(jnp.mean(x * x, axis=-1, keepdims=True) + EPS)
            xh = x * r
            dxh = dh * g_ref[...]
            dx = r * (dxh - xh * jnp.mean(dxh * xh, axis=-1, keepdims=True))
            dx_ref[...] = dres_ref[...] + dx
            dgp = jnp.sum(dh * xh, axis=0, keepdims=True)

            @pl.when(i == 0)
            def _():
                dg_ref[...] = dgp

            @pl.when(i > 0)
            def _():
                dg_ref[...] += dgp

    return _call(
        body,
        out_shape=(_sds((T, D_MODEL), F32), _sds((1, D_MODEL), F32)),
        grid=(T // tm, nk),
        in_specs=[
            pl.BlockSpec((tm, tk), lambda i, k: (i, k)),
            pl.BlockSpec((None, D_MODEL, tk), lambda i, k: (k // per, 0, k % per)),
            pl.BlockSpec((tm, D_MODEL), lambda i, k: (i, 0)),
            pl.BlockSpec((1, D_MODEL), lambda i, k: (0, 0)),
            pl.BlockSpec((tm, D_MODEL), lambda i, k: (i, 0)),
        ],
        out_specs=(
            pl.BlockSpec((tm, D_MODEL), lambda i, k: (i, 0)),
            pl.BlockSpec((1, D_MODEL), lambda i, k: (0, 0)),
        ),
        scratch_shapes=[pltpu.VMEM((tm, D_MODEL), F32)],
        name=name, compiler_params=_cp(("arbitrary", "arbitrary")),
    )(dy, w, x, g, dres)


def _rowsharded_mm(name, a, w, res, *, pro=None):
    T, K = a.shape
    ks = K // N_CHIPS
    tm = _row_tile(T, 512)
    epi = dict(epi=_add_res, epi_in=(res,), epi_specs=(pl.BlockSpec((tm, D_MODEL), lambda i, j, k: (i, 0)),)) if res is not None else {}
    return _mm(
        name, "nn", a, w,
        pl.BlockSpec((tm, ks), lambda i, j, k: (i, k)),
        pl.BlockSpec((None, ks, D_MODEL), lambda i, j, k: (k, 0, 0)),
        _sds((T, D_MODEL), F32 if res is not None else _ACT),
        pl.BlockSpec((tm, D_MODEL), lambda i, j, k: (i, 0)),
        (T // tm, 1, N_CHIPS), (tm, D_MODEL), pro=pro, **epi,
    )


def _rowsharded_dgrad(name, dy, w, *, epi=None, epi_in=None, out_dtype=_ACT, res=None):
    T = dy.shape[0]
    ks = w.shape[1]
    tm = _row_tile(T, 512)
    kw = {}
    if epi is not None:
        kw = dict(epi=epi, epi_in=(epi_in,), epi_specs=(pl.BlockSpec((tm, ks), lambda i, j, k: (i, j)),))
    return _mm(
        name, "nt", dy, w,
        pl.BlockSpec((tm, D_MODEL), lambda i, j, k: (i, 0)),
        pl.BlockSpec((None, ks, D_MODEL), lambda i, j, k: (j, 0, 0)),
        _sds((T, ks * N_CHIPS), out_dtype),
        pl.BlockSpec((tm, ks), lambda i, j, k: (i, j)),
        (T // tm, N_CHIPS, 1), None, **kw,
    )


def _rowsharded_wgrad(name, a, dy, K, *, pro=None):
    T = a.shape[0]
    ks = K // N_CHIPS
    tt = _row_tile(T, 512)
    return _mm(
        name, "tn", a, dy,
        pl.BlockSpec((tt, ks), lambda i, j, k: (k, i)),
        pl.BlockSpec((tt, D_MODEL), lambda i, j, k: (k, 0)),
        _sds((N_CHIPS, ks, D_MODEL), _COMM),
        pl.BlockSpec((None, ks, D_MODEL), lambda i, j, k: (i, 0, 0)),
        (N_CHIPS, 1, T // tt), (ks, D_MODEL), pro=pro,
    )


def _colsharded_wgrad(name, a, dy, dy_col0, n_cols):
    T, K = a.shape
    ns = n_cols // N_CHIPS
    tt = _row_tile(T, 512)
    off = dy_col0 // ns
    return _mm(
        name, "tn", a, dy,
        pl.BlockSpec((tt, K), lambda i, j, k: (k, 0)),
        pl.BlockSpec((tt, ns), lambda i, j, k: (k, off + j)),
        _sds((N_CHIPS, K, ns), _COMM),
        pl.BlockSpec((None, K, ns), lambda i, j, k: (j, 0, 0)),
        (1, N_CHIPS, T // tt), (K, ns),
    )


def _rot(x, cos, sin):
    return x * cos + pltpu.roll(x, QK_DIM // 2, 1) * sin


def _rot_bwd(d, cos, sin):
    return d * cos + pltpu.roll(d * sin, QK_DIM // 2, 1)


def _decay(qi, ki, tq, tk, lgf, lgb):
    n = qi * tq + lax.broadcasted_iota(jnp.int32, (tq, tk), 0)
    m = ki * tk + lax.broadcasted_iota(jnp.int32, (tq, tk), 1)
    diff = (n - m).astype(F32)
    return diff, jnp.exp(jnp.where(diff >= 0.0, lgf * diff, -(lgb * diff)))


def _group_norm_gate(o, g):
    mu = jnp.mean(o, axis=-1, keepdims=True)
    oc = o - mu
    var = jnp.mean(oc * oc, axis=-1, keepdims=True)
    on = oc * lax.rsqrt(var + EPS)
    return on * (g * jax.nn.sigmoid(g))


def _retention_fwd(proj, cos, sin, lg, B, S):
    T = B * S
    tq = tk = _row_tile(S, 512)
    nq, nk = S // tq, S // tk
    scale = QK_DIM ** -0.5

    def body(lg_ref, q_ref, k_ref, v_ref, g_ref, cq_ref, sq_ref, ck_ref, sk_ref, o_ref, y_ref, qr_scr, acc):
        h = pl.program_id(1)
        qi = pl.program_id(2)
        ki = pl.program_id(3)

        @pl.when(ki == 0)
        def _():
            qr_scr[...] = _rot(q_ref[...].astype(F32), cq_ref[...], sq_ref[...]).astype(_MXU)
            acc[...] = jnp.zeros_like(acc)

        kr = _rot(k_ref[...].astype(F32), ck_ref[...], sk_ref[...]) * scale
        s = _dot(qr_scr[...], kr, "nt")
        _, dec = _decay(qi, ki, tq, tk, lg_ref[0, h], lg_ref[1, h])
        acc[...] += _dot(s * dec, v_ref[...])

        @pl.when(ki == nk - 1)
        def _():
            o = acc[...]
            o_ref[...] = o
            y_ref[...] = _group_norm_gate(o, g_ref[...].astype(F32)).astype(y_ref.dtype)

    hq = C_K // QK_DIM
    hv = C_V // V_DIM
    hg = C_G // V_DIM
    return _call(
        body,
        out_shape=(_sds((T, N_HEADS * V_DIM), F32), _sds((T, N_HEADS * V_DIM), _ACT)),
        grid=(B, N_HEADS, nq, nk),
        in_specs=[
            pl.BlockSpec(memory_space=pltpu.SMEM),
            pl.BlockSpec((tq, QK_DIM), lambda b, h, qi, ki: (b * nq + qi, h)),
            pl.BlockSpec((tk, QK_DIM), lambda b, h, qi, ki: (b * nk + ki, hq + h)),
            pl.BlockSpec((tk, V_DIM), lambda b, h, qi, ki: (b * nk + ki, hv + h)),
            pl.BlockSpec((tq, V_DIM), lambda b, h, qi, ki: (b * nq + qi, hg + h)),
            pl.BlockSpec((tq, QK_DIM), lambda b, h, qi, ki: (qi, 0)),
            pl.BlockSpec((tq, QK_DIM), lambda b, h, qi, ki: (qi, 0)),
            pl.BlockSpec((tk, QK_DIM), lambda b, h, qi, ki: (ki, 0)),
            pl.BlockSpec((tk, QK_DIM), lambda b, h, qi, ki: (ki, 0)),
        ],
        out_specs=(
            pl.BlockSpec((tq, V_DIM), lambda b, h, qi, ki: (b * nq + qi, h)),
            pl.BlockSpec((tq, V_DIM), lambda b, h, qi, ki: (b * nq + qi, h)),
        ),
        scratch_shapes=[pltpu.VMEM((tq, QK_DIM), _MXU), pltpu.VMEM((tq, V_DIM), F32)],
        name="retention_fwd",
        compiler_params=_cp(("parallel", "parallel", "parallel", "arbitrary")),
    )(lg, proj, proj, proj, proj, cos, sin, cos, sin)


def _gn_gate_bwd(dy, o, proj):
    T = dy.shape[0]
    tm = _row_tile(T, 512)

    def body(dy_ref, o_ref, g_ref, do_ref, dg_ref):
        for h in range(N_HEADS):
            sl = slice(h * V_DIM, (h + 1) * V_DIM)
            o = o_ref[:, sl]
            g = g_ref[:, sl].astype(F32)
            d = dy_ref[:, sl].astype(F32)
            mu = jnp.mean(o, axis=-1, keepdims=True)
            oc = o - mu
            rstd = lax.rsqrt(jnp.mean(oc * oc, axis=-1, keepdims=True) + EPS)
            on = oc * rstd
            sg = jax.nn.sigmoid(g)
            don = d * (g * sg)
            dg_ref[:, sl] = (d * on * (sg * (1.0 + g * (1.0 - sg)))).astype(dg_ref.dtype)
            do = rstd * (don - jnp.mean(don, axis=-1, keepdims=True) - on * jnp.mean(don * on, axis=-1, keepdims=True))
            do_ref[:, sl] = do.astype(do_ref.dtype)

    wide = N_HEADS * V_DIM
    return _call(
        body,
        out_shape=(_sds((T, wide), _ACT), _sds((T, wide), _ACT)),
        grid=(T // tm,),
        in_specs=[
            pl.BlockSpec((tm, wide), lambda i: (i, 0)),
            pl.BlockSpec((tm, wide), lambda i: (i, 0)),
            pl.BlockSpec((tm, wide), lambda i: (i, C_G // wide)),
        ],
        out_specs=(pl.BlockSpec((tm, wide), lambda i: (i, 0)), pl.BlockSpec((tm, wide), lambda i: (i, 0))),
        name="gn_gate_bwd", compiler_params=_cp(("parallel",)),
    )(dy, o, proj)


def _retention_bwd(proj, d_o, cos, sin, lg, B, S):
    T = B * S
    tq = tk = _row_tile(S, 512)
    nq, nk = S // tq, S // tk
    scale = QK_DIM ** -0.5

    def body(lg_ref, q_ref, k_ref, v_ref, do_ref, cq_ref, sq_ref, ck_ref, sk_ref,
             dq_ref, dk_ref, dv_ref, dlg_ref, kr_scr, dq_acc, dk_acc, dv_acc, gl_acc):
        h = pl.program_id(1)
        ki = pl.program_id(2)
        qi = pl.program_id(3)
        rows = pl.ds(pl.multiple_of(qi * tq, tq), tq)

        @pl.when(qi == 0)
        def _():
            kr_scr[...] = (_rot(k_ref[...].astype(F32), ck_ref[...], sk_ref[...]) * scale).astype(_MXU)
            dk_acc[...] = jnp.zeros_like(dk_acc)
            dv_acc[...] = jnp.zeros_like(dv_acc)

        @pl.when(jnp.logical_and(qi == 0, ki == 0))
        def _():
            gl_acc[...] = jnp.zeros_like(gl_acc)

        @pl.when(ki == 0)
        def _():
            dq_acc[rows, :] = jnp.zeros((tq, QK_DIM), F32)

        qr = _rot(q_ref[...].astype(F32), cq_ref[...], sq_ref[...]).astype(_MXU)
        kr = kr_scr[...]
        d_out = do_ref[...]
        s = _dot(qr, kr, "nt")
        diff, dec = _decay(qi, ki, tq, tk, lg_ref[0, h], lg_ref[1, h])
        p = s * dec
        dv_acc[...] += _dot(p, d_out, "tn")
        dp = _dot(d_out, v_ref[...], "nt")
        ds = dp * dec
        dq_acc[rows, :] += _dot(ds, kr)
        dk_acc[...] += _dot(ds, qr, "tn")
        gd = dp * p * diff
        gl_acc[0:1, :] += jnp.sum(jnp.where(diff >= 0.0, gd, 0.0), axis=0, keepdims=True)
        gl_acc[1:2, :] += jnp.sum(jnp.where(diff < 0.0, -gd, 0.0), axis=0, keepdims=True)

        @pl.when(qi == nq - 1)
        def _():
            dk_ref[...] = _rot_bwd(dk_acc[...] * scale, ck_ref[...], sk_ref[...]).astype(dk_ref.dtype)
            dv_ref[...] = dv_acc[...].astype(dv_ref.dtype)

        @pl.when(ki == nk - 1)
        def _():
            dq_ref[rows, :] = _rot_bwd(dq_acc[rows, :], cq_ref[...], sq_ref[...]).astype(dq_ref.dtype)

        @pl.when(jnp.logical_and(qi == nq - 1, ki == nk - 1))
        def _():
            dlg_ref[...] = gl_acc[...]

    hq = C_K // QK_DIM
    hv = C_V // V_DIM
    return _call(
        body,
        out_shape=(
            _sds((T, N_HEADS * QK_DIM), _ACT), _sds((T, N_HEADS * QK_DIM), _ACT),
            _sds((T, N_HEADS * V_DIM), _ACT), _sds((B, N_HEADS, 2, tk), F32),
        ),
        grid=(B, N_HEADS, nk, nq),
        in_specs=[
            pl.BlockSpec(memory_space=pltpu.SMEM),
            pl.BlockSpec((tq, QK_DIM), lambda b, h, ki, qi: (b * nq + qi, h)),
            pl.BlockSpec((tk, QK_DIM), lambda b, h, ki, qi: (b * nk + ki, hq + h)),
            pl.BlockSpec((tk, V_DIM), lambda b, h, ki, qi: (b * nk + ki, hv + h)),
            pl.BlockSpec((tq, V_DIM), lambda b, h, ki, qi: (b * nq + qi, h)),
            pl.BlockSpec((tq, QK_DIM), lambda b, h, ki, qi: (qi, 0)),
            pl.BlockSpec((tq, QK_DIM), lambda b, h, ki, qi: (qi, 0)),
            pl.BlockSpec((tk, QK_DIM), lambda b, h, ki, qi: (ki, 0)),
            pl.BlockSpec((tk, QK_DIM), lambda b, h, ki, qi: (ki, 0)),
        ],
        out_specs=(
            pl.BlockSpec((S, QK_DIM), lambda b, h, ki, qi: (b, h)),
            pl.BlockSpec((tk, QK_DIM), lambda b, h, ki, qi: (b * nk + ki, h)),
            pl.BlockSpec((tk, V_DIM), lambda b, h, ki, qi: (b * nk + ki, h)),
            pl.BlockSpec((None, None, 2, tk), lambda b, h, ki, qi: (b, h, 0, 0)),
        ),
        scratch_shapes=[
            pltpu.VMEM((tk, QK_DIM), _MXU), pltpu.VMEM((S, QK_DIM), F32),
            pltpu.VMEM((tk, QK_DIM), F32), pltpu.VMEM((tk, V_DIM), F32), pltpu.VMEM((2, tk), F32),
        ],
        name="retention_bwd",
        compiler_params=_cp(("parallel", "parallel", "arbitrary", "arbitrary")),
    )(lg, proj, proj, proj, d_o, cos, sin, cos, sin)


POOL_PAD = 16


def _pad_rows(v):
    z = jnp.zeros((POOL_PAD, v.shape[1]), F32)
    return jnp.concatenate([z, v, z], axis=0)


def _window_sums(first, length, levels):
    s = first
    step = 1
    for _ in range(levels - 1):
        s = pltpu.roll(s, step, 0) + pltpu.roll(s, length - step, 0)
        step *= 2
    return s


def _pool_counts(S, hw):
    n = lax.broadcasted_iota(jnp.int32, (S, 1), 0)
    return (jnp.minimum(n + hw, S) - jnp.maximum(n - hw, 0)).astype(F32)


def _pool_mixed(pf, S, g):
    length = S + 2 * POOL_PAD
    xp = _pad_rows(pf)
    s = _window_sums(xp + pltpu.roll(xp, 1, 0), length, g + 1)[POOL_PAD:POOL_PAD + S]
    return s / _pool_counts(S, 1 << g) - pf


def _pool_mixed_bwd(dmixed, S, g):
    length = S + 2 * POOL_PAD
    ep = _pad_rows(dmixed / _pool_counts(S, 1 << g))
    t = _window_sums(ep + pltpu.roll(ep, length - 1, 0), length, g + 1)[POOL_PAD:POOL_PAD + S]
    return t - dmixed


def _pool_fwd(proj, w_grp, scale, B, S):
    T = B * S
    G = POOL_W // 4

    def body(p_ref, wg_ref, sc_ref, y_ref):
        for g in range(4):
            sl = slice(g * G, (g + 1) * G)
            mixed = _pool_mixed(p_ref[:, sl].astype(F32), S, g)
            y_ref[:, sl] = (_dot(mixed, wg_ref[g]) * sc_ref[:, sl]).astype(y_ref.dtype)

    return _call(
        body, out_shape=_sds((T, POOL_W), _ACT), grid=(B,),
        in_specs=[
            pl.BlockSpec((S, POOL_W), lambda b: (b, C_P // POOL_W)),
            pl.BlockSpec((4, G, G), lambda b: (0, 0, 0)),
            pl.BlockSpec((1, POOL_W), lambda b: (0, 0)),
        ],
        out_specs=pl.BlockSpec((S, POOL_W), lambda b: (b, 0)),
        name="pool_fwd", compiler_params=_cp(("parallel",)),
    )(proj, w_grp, scale)


def _pool_bwd(proj, dy, w_grp, scale, B, S):
    T = B * S
    G = POOL_W // 4

    def body(p_ref, dy_ref, wg_ref, sc_ref, dp_ref, dwg_ref, dsc_ref):
        b = pl.program_id(0)
        for g in range(4):
            sl = slice(g * G, (g + 1) * G)
            mixed = _pool_mixed(p_ref[:, sl].astype(F32), S, g)
            z = _dot(mixed, wg_ref[g])
            d = dy_ref[:, sl].astype(F32)
            dsc = jnp.sum(d * z, axis=0, keepdims=True)
            dz = d * sc_ref[:, sl]
            dwg = _dot(mixed, dz, "tn")
            dmixed = _dot(dz, wg_ref[g], "nt")
            dp_ref[:, sl] = _pool_mixed_bwd(dmixed, S, g).astype(dp_ref.dtype)

            @pl.when(b == 0)
            def _():
                dwg_ref[g] = dwg
                dsc_ref[:, sl] = dsc

            @pl.when(b > 0)
            def _():
                dwg_ref[g] += dwg
                dsc_ref[:, sl] += dsc

    return _call(
        body,
        out_shape=(_sds((T, POOL_W), _ACT), _sds((4, G, G), F32), _sds((1, POOL_W), F32)),
        grid=(B,),
        in_specs=[
            pl.BlockSpec((S, POOL_W), lambda b: (b, C_P // POOL_W)),
            pl.BlockSpec((S, POOL_W), lambda b: (b, 0)),
            pl.BlockSpec((4, G, G), lambda b: (0, 0, 0)),
            pl.BlockSpec((1, POOL_W), lambda b: (0, 0)),
        ],
        out_specs=(
            pl.BlockSpec((S, POOL_W), lambda b: (b, 0)),
            pl.BlockSpec((4, G, G), lambda b: (0, 0, 0)),
            pl.BlockSpec((1, POOL_W), lambda b: (0, 0)),
        ),
        name="pool_bwd", compiler_params=_cp(("arbitrary",)),
    )(proj, dy, w_grp, scale)


def _mem_softmax(q, k):
    s = _dot(q, k, "nt") * (QK_DIM ** -0.5)
    e = jnp.exp(s - jnp.max(s, axis=-1, keepdims=True))
    return e / jnp.sum(e, axis=-1, keepdims=True)


def _mem_attn_fwd(proj, kv, B, S):
    T = B * S
    tq = _row_tile(S, 512)
    nq = S // tq

    def body(q_ref, kv_ref, o_ref):
        for h in range(N_HEADS):
            sl = slice(h * QK_DIM, (h + 1) * QK_DIM)
            a = _mem_softmax(q_ref[:, sl], kv_ref[:, sl])
            o_ref[:, sl] = _dot(a, kv_ref[:, MEMQ_W + h * QK_DIM:MEMQ_W + (h + 1) * QK_DIM]).astype(o_ref.dtype)

    return _call(
        body, out_shape=_sds((T, MEMQ_W), _ACT), grid=(B, nq),
        in_specs=[
            pl.BlockSpec((tq, MEMQ_W), lambda b, i: (b * nq + i, C_QM // MEMQ_W)),
            pl.BlockSpec((MEM_LEN, 2 * MEMQ_W), lambda b, i: (b, 0)),
        ],
        out_specs=pl.BlockSpec((tq, MEMQ_W), lambda b, i: (b * nq + i, 0)),
        name="mem_attn_fwd", compiler_params=_cp(("parallel", "parallel")),
    )(proj, kv)


def _mem_attn_bwd(proj, kv, d_o, B, S):
    T = B * S
    tq = _row_tile(S, 512)
    nq = S // tq
    scale = QK_DIM ** -0.5

    def body(q_ref, kv_ref, do_ref, dq_ref, dkv_ref):
        i = pl.program_id(1)
        for h in range(N_HEADS):
            sl = slice(h * QK_DIM, (h + 1) * QK_DIM)
            vsl = slice(MEMQ_W + h * QK_DIM, MEMQ_W + (h + 1) * QK_DIM)
            q = q_ref[:, sl]
            a = _mem_softmax(q, kv_ref[:, sl])
            d = do_ref[:, sl]
            da = _dot(d, kv_ref[:, vsl], "nt")
            ds = a * (da - jnp.sum(a * da, axis=-1, keepdims=True)) * scale
            dq_ref[:, sl] = _dot(ds, kv_ref[:, sl]).astype(dq_ref.dtype)
            dk = _dot(ds, q, "tn")
            dv = _dot(a, d, "tn")

            @pl.when(i == 0)
            def _():
                dkv_ref[:, sl] = dk
                dkv_ref[:, vsl] = dv

            @pl.when(i > 0)
            def _():
                dkv_ref[:, sl] += dk
                dkv_ref[:, vsl] += dv

    return _call(
        body,
        out_shape=(_sds((T, MEMQ_W), _ACT), _sds((B * MEM_LEN, 2 * MEMQ_W), F32)),
        grid=(B, nq),
        in_specs=[
            pl.BlockSpec((tq, MEMQ_W), lambda b, i: (b * nq + i, C_QM // MEMQ_W)),
            pl.BlockSpec((MEM_LEN, 2 * MEMQ_W), lambda b, i: (b, 0)),
            pl.BlockSpec((tq, MEMQ_W), lambda b, i: (b * nq + i, 0)),
        ],
        out_specs=(
            pl.BlockSpec((tq, MEMQ_W), lambda b, i: (b * nq + i, 0)),
            pl.BlockSpec((MEM_LEN, 2 * MEMQ_W), lambda b, i: (b, 0)),
        ),
        name="mem_attn_bwd", compiler_params=_cp(("parallel", "arbitrary")),
    )(proj, kv, d_o)


def _mem_norm(mem2d, g):
    M = mem2d.shape[0]
    tm = _row_tile(M, 512)

    def body(x_ref, g_ref, o_ref):
        o_ref[...] = _rms_rows(x_ref[...], g_ref[...]).astype(o_ref.dtype)

    return _call(
        body, out_shape=_sds((M, D_MODEL), _ACT), grid=(M // tm,),
        in_specs=[pl.BlockSpec((tm, D_MODEL), lambda i: (i, 0)), pl.BlockSpec((1, D_MODEL), lambda i: (0, 0))],
        out_specs=pl.BlockSpec((tm, D_MODEL), lambda i: (i, 0)),
        name="mem_norm", compiler_params=_cp(("parallel",)),
    )(mem2d, g)


def _mem_norm_wgrad(mem2d, d_memn):
    M = mem2d.shape[0]
    tm = _row_tile(M, 512)

    def body(x_ref, d_ref, dg_ref):
        i = pl.program_id(0)
        x = x_ref[...]
        xh = x * lax.rsqrt(jnp.mean(x * x, axis=-1, keepdims=True) + EPS)
        dg = jnp.sum(d_ref[...] * xh, axis=0, keepdims=True)

        @pl.when(i == 0)
        def _():
            dg_ref[...] = dg

        @pl.when(i > 0)
        def _():
            dg_ref[...] += dg

    return _call(
        body, out_shape=_sds((1, D_MODEL), F32), grid=(M // tm,),
        in_specs=[pl.BlockSpec((tm, D_MODEL), lambda i: (i, 0)), pl.BlockSpec((tm, D_MODEL), lambda i: (i, 0))],
        out_specs=pl.BlockSpec((1, D_MODEL), lambda i: (0, 0)),
        name="mem_norm_wgrad", compiler_params=_cp(("arbitrary",)),
    )(mem2d, d_memn)


def _row_mm(a, w_ref):
    ks = w_ref.shape[1]
    out = _dot(a[:, 0:ks], w_ref[0])
    for k in range(1, N_CHIPS):
        out += _dot(a[:, k * ks:(k + 1) * ks], w_ref[k])
    return out


def _row_mm_t(d, w_ref):
    return jnp.concatenate([_dot(d, w_ref[k], "nt") for k in range(N_CHIPS)], axis=1)


def _col_mm(a, w_ref):
    return jnp.concatenate([_dot(a, w_ref[k]) for k in range(N_CHIPS)], axis=1)


def _col_mm_t(d, w_ref):
    ns = w_ref.shape[2]
    out = _dot(d[:, 0:ns], w_ref[0], "nt")
    for k in range(1, N_CHIPS):
        out += _dot(d[:, k * ns:(k + 1) * ns], w_ref[k], "nt")
    return out


def _full_spec(w):
    nd = w.ndim
    return pl.BlockSpec(w.shape, lambda i: (0,) * nd)


def _merge_fwd(x, proj, y_r, y_p, o_m, w_ret_o, w_pool_o, w_mem_o, w_out):
    T = x.shape[0]
    tm = _row_tile(T, 256)

    def body(x_ref, gr_ref, gp_ref, gm_ref, yr_ref, yp_ref, om_ref, wr_ref, wp_ref, wm_ref, wo_ref,
             x1_ref, mg_ref, o3_ref):
        o_r = _row_mm(yr_ref[...], wr_ref)
        o_p = _col_mm(yp_ref[...], wp_ref)
        o_q = _col_mm(om_ref[...], wm_ref)
        merged = (jax.nn.sigmoid(gr_ref[...].astype(F32)) * o_r + jax.nn.sigmoid(gp_ref[...].astype(F32)) * o_p
                  + jax.nn.sigmoid(gm_ref[...].astype(F32)) * o_q)
        mg = merged.astype(mg_ref.dtype)
        mg_ref[...] = mg
        o3_ref[:, 0:D_MODEL] = o_r.astype(o3_ref.dtype)
        o3_ref[:, D_MODEL:2 * D_MODEL] = o_p.astype(o3_ref.dtype)
        o3_ref[:, 2 * D_MODEL:3 * D_MODEL] = o_q.astype(o3_ref.dtype)
        x1_ref[...] = x_ref[...] + _row_mm(mg, wo_ref)

    gb = C_GATE // D_MODEL
    row = lambda w: pl.BlockSpec((tm, w), lambda i: (i, 0))
    return _call(
        body,
        out_shape=(_sds((T, D_MODEL), F32), _sds((T, D_MODEL), _ACT), _sds((T, 3 * D_MODEL), _ACT)),
        grid=(T // tm,),
        in_specs=[
            row(D_MODEL),
            pl.BlockSpec((tm, D_MODEL), lambda i: (i, gb)),
            pl.BlockSpec((tm, D_MODEL), lambda i: (i, gb + 1)),
            pl.BlockSpec((tm, D_MODEL), lambda i: (i, gb + 2)),
            row(D_MODEL), row(POOL_W), row(MEMQ_W),
            _full_spec(w_ret_o), _full_spec(w_pool_o), _full_spec(w_mem_o), _full_spec(w_out),
        ],
        out_specs=(row(D_MODEL), row(D_MODEL), row(3 * D_MODEL)),
        name="merge_fwd", compiler_params=_cp(("parallel",)),
    )(x, proj, proj, proj, y_r, y_p, o_m, w_ret_o, w_pool_o, w_mem_o, w_out)


def _merge_bwd(dx1, proj, o3, w_ret_o, w_pool_o, w_mem_o, w_out):
    T = dx1.shape[0]
    tm = _row_tile(T, 256)

    def body(dx_ref, gr_ref, gp_ref, gm_ref, o3_ref, wr_ref, wp_ref, wm_ref, wo_ref,
             do3_ref, dgate_ref, dyr_ref, dyp_ref, dom_ref):
        dmerged = _row_mm_t(dx_ref[...], wo_ref)
        douts = []
        for n, g_ref in enumerate((gr_ref, gp_ref, gm_ref)):
            sl = slice(n * D_MODEL, (n + 1) * D_MODEL)
            gate = jax.nn.sigmoid(g_ref[...].astype(F32))
            d_out = (dmerged * gate).astype(do3_ref.dtype)
            do3_ref[:, sl] = d_out
            dgate_ref[:, sl] = (dmerged * o3_ref[:, sl].astype(F32) * gate * (1.0 - gate)).astype(dgate_ref.dtype)
            douts.append(d_out)
        dyr_ref[...] = _row_mm_t(douts[0], wr_ref).astype(dyr_ref.dtype)
        dyp_ref[...] = _col_mm_t(douts[1], wp_ref).astype(dyp_ref.dtype)
        dom_ref[...] = _col_mm_t(douts[2], wm_ref).astype(dom_ref.dtype)

    gb = C_GATE // D_MODEL
    row = lambda w: pl.BlockSpec((tm, w), lambda i: (i, 0))
    return _call(
        body,
        out_shape=(
            _sds((T, 3 * D_MODEL), _ACT), _sds((T, 3 * D_MODEL), _ACT),
            _sds((T, D_MODEL), _ACT), _sds((T, POOL_W), _ACT), _sds((T, MEMQ_W), _ACT),
        ),
        grid=(T // tm,),
        in_specs=[
            row(D_MODEL),
            pl.BlockSpec((tm, D_MODEL), lambda i: (i, gb)),
            pl.BlockSpec((tm, D_MODEL), lambda i: (i, gb + 1)),
            pl.BlockSpec((tm, D_MODEL), lambda i: (i, gb + 2)),
            row(3 * D_MODEL),
            _full_spec(w_ret_o), _full_spec(w_pool_o), _full_spec(w_mem_o), _full_spec(w_out),
        ],
        out_specs=(row(3 * D_MODEL), row(3 * D_MODEL), row(D_MODEL), row(POOL_W), row(MEMQ_W)),
        name="merge_bwd", compiler_params=_cp(("parallel",)),
    )(dx1, proj, proj, proj, o3, w_ret_o, w_pool_o, w_mem_o, w_out)


def _loss_head(x, g, target):
    T = x.shape[0]
    tm = _row_tile(T, 512)

    def body(x_ref, g_ref, t_ref, dx_ref, sq_ref, dg_ref):
        i = pl.program_id(0)
        x = x_ref[...]
        gg = g_ref[...]
        r = lax.rsqrt(jnp.mean(x * x, axis=-1, keepdims=True) + EPS)
        xh = x * r
        err = xh * gg - t_ref[...]
        dy = err * (1.0 / D_MODEL)
        dxh = dy * gg
        dx_ref[...] = r * (dxh - xh * jnp.mean(dxh * xh, axis=-1, keepdims=True))
        sq = jnp.sum(err * err, axis=0, keepdims=True)
        dg = jnp.sum(dy * xh, axis=0, keepdims=True)

        @pl.when(i == 0)
        def _():
            sq_ref[...] = sq
            dg_ref[...] = dg

        @pl.when(i > 0)
        def _():
            sq_ref[...] += sq
            dg_ref[...] += dg

    vec = pl.BlockSpec((1, D_MODEL), lambda i: (0, 0))
    row = pl.BlockSpec((tm, D_MODEL), lambda i: (i, 0))
    return _call(
        body,
        out_shape=(_sds((T, D_MODEL), F32), _sds((1, D_MODEL), F32), _sds((1, D_MODEL), F32)),
        grid=(T // tm,), in_specs=[row, vec, row], out_specs=(row, vec, vec),
        name="loss_head", compiler_params=_cp(("arbitrary",)),
    )(x, g, target)


def _block_rows(rows, cols, itemsize, cap_bytes=2 << 20):
    t = rows
    while t * cols * itemsize > cap_bytes and t % 2 == 0 and (t // 2) % 16 == 0:
        t //= 2
    return t


def _cast(w2d, dtype):
    R, C = w2d.shape
    tr = _block_rows(R, C, 4)

    def body(w_ref, o_ref):
        o_ref[...] = w_ref[...].astype(o_ref.dtype)

    spec = pl.BlockSpec((tr, C), lambda i: (i, 0))
    return _call(body, out_shape=_sds((R, C), dtype), grid=(R // tr,), in_specs=[spec], out_specs=spec,
                 name="cast_weights", compiler_params=_cp(("parallel",)))(w2d)


def _adamw(w, g, m, v):
    R, C = w.shape
    tr = _block_rows(R, C, 4, 1 << 20)

    def body(w_ref, g_ref, m_ref, v_ref, d_ref, nm_ref, nv_ref):
        g = g_ref[...]
        m = ADAM_B1 * m_ref[...] + (1.0 - ADAM_B1) * g
        v = ADAM_B2 * v_ref[...] + (1.0 - ADAM_B2) * (g * g)
        m_hat = m / (1.0 - ADAM_B1 ** ADAM_STEP)
        v_hat = v / (1.0 - ADAM_B2 ** ADAM_STEP)
        d_ref[...] = -ADAM_LR * (m_hat / (jnp.sqrt(v_hat) + ADAM_EPS) + ADAM_WD * w_ref[...])
        nm_ref[...] = m
        nv_ref[...] = v

    spec = pl.BlockSpec((tr, C), lambda i: (i, 0))
    out = _sds((R, C), F32)
    return _call(body, out_shape=(out, out, out), grid=(R // tr,), in_specs=[spec] * 4, out_specs=(spec,) * 3,
                 name="adamw", compiler_params=_cp(("parallel",)))(w, g, m, v)


def _add_sibling_half(g_full, land, my_c):
    _, R, C = g_full.shape
    hr = R // 2
    tr = _block_rows(hr, C, 2, 1 << 20)
    nb = hr // tr

    def body(c_ref, g_ref, l_ref, o_ref):
        o_ref[...] = (g_ref[...].astype(F32) + l_ref[...].astype(F32)).astype(o_ref.dtype)

    grid_spec = pltpu.PrefetchScalarGridSpec(
        num_scalar_prefetch=1, grid=(N_CHIPS, nb),
        in_specs=[
            pl.BlockSpec((None, tr, C), lambda k, i, c: (k, c[0] * nb + i, 0)),
            pl.BlockSpec((None, tr, C), lambda k, i, c: (k, i, 0)),
        ],
        out_specs=pl.BlockSpec((None, tr, C), lambda k, i, c: (k, i, 0)),
    )
    return _call(body, out_shape=_sds((N_CHIPS, hr, C), _COMM), grid_spec=grid_spec,
                 name="add_sibling_half", compiler_params=_cp(("parallel", "parallel")))(my_c, g_full, land)


def _add_chips(land):
    _, hr, C = land.shape
    tr = _block_rows(hr, C, 4, 1 << 20)

    def body(a_ref, b_ref, c_ref, d_ref, o_ref):
        o_ref[...] = ((a_ref[...].astype(F32) + b_ref[...].astype(F32)) + c_ref[...].astype(F32)) + d_ref[...].astype(F32)

    specs = [pl.BlockSpec((None, tr, C), functools.partial(lambda k, i: (k, i, 0), k)) for k in range(N_CHIPS)]
    return _call(body, out_shape=_sds((hr, C), F32), grid=(hr // tr,), in_specs=specs,
                 out_specs=pl.BlockSpec((tr, C), lambda i: (i, 0)),
                 name="add_chips", compiler_params=_cp(("parallel",)))(land, land, land, land)


def _place():
    x, y, c = lax.axis_index("x"), lax.axis_index("y"), lax.axis_index("c")
    chips = [(1 - x, y), (x, 1 - y), (1 - x, 1 - y)]
    return x, y, c, 2 * x + y, chips


def _remote(src, dst, send_sem, recv_sem, dev):
    return pltpu.make_async_remote_copy(src_ref=src, dst_ref=dst, send_sem=send_sem, recv_sem=recv_sem,
                                        device_id=dev, device_id_type=MESH)


def _allgather_weights(shards, layer):
    n = len(shards)

    def body(*refs):
        ins, outs = refs[:n], refs[n:2 * n]
        send_sems, recv_sems, local_sems = refs[2 * n:]
        x, y, c, me, chips = _place()
        sibling = (x, y, 1 - c)
        waits = []
        for w in range(n):
            hr = ins[w].shape[1] // 2
            mine = pl.ds(c * hr, hr)
            local = pltpu.make_async_copy(ins[w].at[layer], outs[w].at[me], local_sems.at[w])
            local.start()
            waits.append(local.wait)
            for j, chip in enumerate(chips):
                cp = _remote(ins[w].at[layer, mine], outs[w].at[me, mine], send_sems.at[w, j], recv_sems.at[w, j], (*chip, c))
                cp.start()
                waits.append(cp.wait_send)
        for w in range(n):
            hr = ins[w].shape[1] // 2
            mine = pl.ds(c * hr, hr)
            for j, chip in enumerate(chips):
                kc = 2 * chip[0] + chip[1]
                got = outs[w].at[kc, mine]
                _remote(got, got, send_sems.at[w, j], recv_sems.at[w, j], (*chip, c)).wait_recv()
                fwd = _remote(got, got, send_sems.at[w, 3 + j], recv_sems.at[w, 3 + j], sibling)
                fwd.start()
                waits.append(fwd.wait_send)
        for w in range(n):
            hr = ins[w].shape[1] // 2
            theirs = pl.ds((1 - c) * hr, hr)
            for j, chip in enumerate(chips):
                kc = 2 * chip[0] + chip[1]
                got = outs[w].at[kc, theirs]
                _remote(got, got, send_sems.at[w, 3 + j], recv_sems.at[w, 3 + j], sibling).wait_recv()
        for wait in waits:
            wait()

    out_shape = tuple(_sds((N_CHIPS,) + s.shape[1:], s.dtype) for s in shards)
    return _call(
        body, out_shape=out_shape, in_specs=[ANY] * n, out_specs=(ANY,) * n,
        scratch_shapes=[pltpu.SemaphoreType.DMA((n, 6)), pltpu.SemaphoreType.DMA((n, 6)), pltpu.SemaphoreType.DMA((n,))],
        name="allgather_weights",
    )(*shards)


def _swap_sibling_halves(grads):
    n = len(grads)

    def body(*refs):
        ins, outs = refs[:n], refs[n:2 * n]
        send_sems, recv_sems = refs[2 * n:]
        x, y, c, _, _ = _place()
        copies = []
        for w in range(n):
            hr = ins[w].shape[1] // 2
            cp = _remote(ins[w].at[:, pl.ds((1 - c) * hr, hr)], outs[w], send_sems.at[w], recv_sems.at[w], (x, y, 1 - c))
            cp.start()
            copies.append(cp)
        for cp in copies:
            cp.wait()

    out_shape = tuple(_sds((N_CHIPS, g.shape[1] // 2, g.shape[2]), g.dtype) for g in grads)
    return _call(
        body, out_shape=out_shape, in_specs=[ANY] * n, out_specs=(ANY,) * n,
        scratch_shapes=[pltpu.SemaphoreType.DMA((n,)), pltpu.SemaphoreType.DMA((n,))],
        name="swap_sibling_halves",
    )(*grads)


def _scatter_chip_sums(sums):
    n = len(sums)

    def body(*refs):
        ins, outs = refs[:n], refs[n:2 * n]
        send_sems, recv_sems, local_sems = refs[2 * n:]
        x, y, c, me, chips = _place()
        waits = []
        for w in range(n):
            local = pltpu.make_async_copy(ins[w].at[me], outs[w].at[me], local_sems.at[w])
            local.start()
            waits.append(local.wait)
            for j, chip in enumerate(chips):
                kc = 2 * chip[0] + chip[1]
                cp = _remote(ins[w].at[kc], outs[w].at[me], send_sems.at[w, j], recv_sems.at[w, j], (*chip, c))
                cp.start()
                waits.append(cp.wait_send)
        for w in range(n):
            for j, chip in enumerate(chips):
                kc = 2 * chip[0] + chip[1]
                got = outs[w].at[kc]
                _remote(got, got, send_sems.at[w, j], recv_sems.at[w, j], (*chip, c)).wait_recv()
        for wait in waits:
            wait()

    out_shape = tuple(_sds(s.shape, s.dtype) for s in sums)
    return _call(
        body, out_shape=out_shape, in_specs=[ANY] * n, out_specs=(ANY,) * n,
        scratch_shapes=[pltpu.SemaphoreType.DMA((n, 3)), pltpu.SemaphoreType.DMA((n, 3)), pltpu.SemaphoreType.DMA((n,))],
        name="scatter_chip_sums",
    )(*sums)


def _join_sibling_halves(halves):
    n = len(halves)

    def body(*refs):
        ins, outs = refs[:n], refs[n:2 * n]
        send_sems, recv_sems, local_sems = refs[2 * n:]
        x, y, c, _, _ = _place()
        waits = []
        for w in range(n):
            hr = ins[w].shape[0]
            mine = outs[w].at[pl.ds(c * hr, hr)]
            local = pltpu.make_async_copy(ins[w], mine, local_sems.at[w])
            local.start()
            cp = _remote(ins[w], mine, send_sems.at[w], recv_sems.at[w], (x, y, 1 - c))
            cp.start()
            waits += [local.wait, cp.wait]
        for wait in waits:
            wait()

    out_shape = tuple(_sds((2 * h.shape[0], h.shape[1]), h.dtype) for h in halves)
    return _call(
        body, out_shape=out_shape, in_specs=[ANY] * n, out_specs=(ANY,) * n,
        scratch_shapes=[pltpu.SemaphoreType.DMA((n,)), pltpu.SemaphoreType.DMA((n,)), pltpu.SemaphoreType.DMA((n,))],
        name="join_sibling_halves",
    )(*halves)


def _reduce_scatter_grads(grads, my_c):
    land = _swap_sibling_halves(grads)
    sums = [_add_sibling_half(g, l, my_c) for g, l in zip(grads, land)]
    parts = _scatter_chip_sums(sums)
    halves = [_add_chips(p) for p in parts]
    return _join_sibling_halves(halves)


def _allreduce_small(v):
    R = v.shape[0]

    def body(v_ref, out_ref, sib_ref, chip_ref, sum_ref, send_sems, recv_sems):
        x, y, c, me, chips = _place()
        swap = _remote(v_ref, sib_ref, send_sems.at[0], recv_sems.at[0], (x, y, 1 - c))
        swap.start()
        swap.wait()
        sum_ref[...] = v_ref[...] + sib_ref[...]
        copies = []
        for j, chip in enumerate(chips):
            cp = _remote(sum_ref, chip_ref.at[me], send_sems.at[1 + j], recv_sems.at[1 + j], (*chip, c))
            cp.start()
            copies.append(cp)
        chip_ref[me] = sum_ref[...]
        for cp in copies:
            cp.wait()
        out_ref[...] = ((chip_ref[0] + chip_ref[1]) + chip_ref[2]) + chip_ref[3]

    vm = pl.BlockSpec(memory_space=pltpu.VMEM)
    return _call(
        body, out_shape=_sds((R, 128), F32), in_specs=[vm], out_specs=vm,
        scratch_shapes=[
            pltpu.VMEM((R, 128), F32), pltpu.VMEM((N_CHIPS, R, 128), F32), pltpu.VMEM((R, 128), F32),
            pltpu.SemaphoreType.DMA((4,)), pltpu.SemaphoreType.DMA((4,)),
        ],
        name="allreduce_small", compiler_params=_cp(),
    )(v)


def _pack_small(parts):
    rows = []
    for p in parts:
        flat = p.reshape(-1).astype(F32)
        pad = (-flat.shape[0]) % 128
        rows.append(jnp.pad(flat, (0, pad)).reshape(-1, 128))
    packed = jnp.concatenate(rows, axis=0)
    pad_rows = (-packed.shape[0]) % 8
    return jnp.pad(packed, ((0, pad_rows), (0, 0)))


def _unpack_small(packed, like):
    out, r = [], 0
    for p in like:
        n = p.size
        nr = -(-n // 128)
        out.append(packed[r:r + nr].reshape(-1)[:n].reshape(p.shape))
        r += nr
    return out


BIG = ("w_in", "w_ret_o", "w_pool_o", "w_mem_kv", "w_mem_o", "w_out", "w_ff1", "w_ff2")


def _rope_tables(S):
    inv = ROPE_BASE ** (-jnp.arange(0, QK_DIM, 2, dtype=F32) / QK_DIM)
    ang = jnp.arange(S).astype(F32)[:, None] * inv[None, :]
    cos, sin = jnp.cos(ang), jnp.sin(ang)
    return jnp.concatenate([cos, cos], axis=1), jnp.concatenate([-sin, sin], axis=1)


def _layer_fwd(x, W, small, memn, cos, sin, B, S):
    g1, g2, lg, w_grp, scale = small
    proj = _rms_proj("in_proj", x, g1, W["w_in"], N_PROJ, 896)
    o, y_r = _retention_fwd(proj, cos, sin, lg, B, S)
    y_p = _pool_fwd(proj, w_grp, scale, B, S)
    kv = _rowsharded_mm("mem_kv", memn, W["w_mem_kv"], None)
    o_m = _mem_attn_fwd(proj, kv, B, S)
    x1, merged, o3 = _merge_fwd(x, proj, y_r, y_p, o_m, W["w_ret_o"], W["w_pool_o"], W["w_mem_o"], W["w_out"])
    u = _rms_proj("ff1", x1, g2, W["w_ff1"], FFN_HIDDEN, 1024)
    x2 = _rowsharded_mm("ff2", u, W["w_ff2"], x1, pro=_relu2)
    saved = dict(x=x, proj=proj, o=o, y_r=y_r, y_p=y_p, kv=kv, o_m=o_m, merged=merged, o3=o3, x1=x1, u=u)
    return x2, saved


def _layer_bwd(dx2, sv, W, small, memn, d_memn, cos, sin, B, S):
    g1, g2, lg, w_grp, scale = small
    x, proj, x1, u = sv["x"], sv["proj"], sv["x1"], sv["u"]
    grads = {}
    du = _rowsharded_dgrad("ff2_dgrad", dx2, W["w_ff2"], epi=_relu2_bwd, epi_in=u)
    grads["w_ff2"] = _rowsharded_wgrad("ff2_wgrad", u, dx2, FFN_HIDDEN, pro=_relu2)
    dx1, dg2 = _dgrad_rms("ff1_dgrad", du, W["w_ff1"], x1, g2, dx2, 1024)
    grads["w_ff1"] = _rms_wgrad("ff1_wgrad", x1, g2, du, FFN_HIDDEN, 1024)
    do3, dgates, dy_r, dy_p, do_m = _merge_bwd(dx1, proj, sv["o3"], W["w_ret_o"], W["w_pool_o"], W["w_mem_o"], W["w_out"])
    grads["w_out"] = _rowsharded_wgrad("out_wgrad", sv["merged"], dx1, D_MODEL)
    grads["w_ret_o"] = _rowsharded_wgrad("ret_o_wgrad", sv["y_r"], do3, D_MODEL)
    grads["w_pool_o"] = _colsharded_wgrad("pool_o_wgrad", sv["y_p"], do3, D_MODEL, D_MODEL)
    grads["w_mem_o"] = _colsharded_wgrad("mem_o_wgrad", sv["o_m"], do3, 2 * D_MODEL, D_MODEL)
    d_o, dg_r = _gn_gate_bwd(dy_r, sv["o"], proj)
    dq_r, dk_r, dv_r, dlg = _retention_bwd(proj, d_o, cos, sin, lg, B, S)
    dp, dw_grp, dscale = _pool_bwd(proj, dy_p, w_grp, scale, B, S)
    dq_m, dkv = _mem_attn_bwd(proj, sv["kv"], do_m, B, S)
    grads["w_mem_kv"] = _rowsharded_wgrad("mem_kv_wgrad", memn, dkv, D_MODEL)
    if d_memn is None:
        d_memn = _rowsharded_dgrad("mem_kv_dgrad", dkv, W["w_mem_kv"], out_dtype=F32)
    else:
        d_memn = _rowsharded_dgrad("mem_kv_dgrad_acc", dkv, W["w_mem_kv"], epi=_add_res, epi_in=d_memn, out_dtype=F32)
    dproj = jnp.concatenate([dq_r, dk_r, dv_r, dg_r, dp, dq_m, dgates], axis=1)
    grads["w_in"] = _rms_wgrad("in_wgrad", x, g1, dproj, N_PROJ, 896)
    dx, dg1 = _dgrad_rms("in_dgrad", dproj, W["w_in"], x, g1, dx1, 896)
    small_grads = dict(g1=dg1, g2=dg2, lg=jnp.sum(dlg, axis=(0, 3)).T, w_grp=dw_grp, scale=dscale)
    return dx, grads, small_grads, d_memn


def kernel(x, mem, w_in, ret_decay_logit, w_ret_o, w_pool_grp, pool_scale, w_pool_o, w_mem_kv, w_mem_o, w_out, w_ff1, w_ff2, norm1_g, norm2_g, mem_norm_g, final_norm_g, loss_target, m_w_in, m_ret_decay_logit, m_w_ret_o, m_w_pool_grp, m_pool_scale, m_w_pool_o, m_w_mem_kv, m_w_mem_o, m_w_out, m_w_ff1, m_w_ff2, m_norm1_g, m_norm2_g, m_mem_norm_g, m_final_norm_g, v_w_in, v_ret_decay_logit, v_w_ret_o, v_w_pool_grp, v_pool_scale, v_w_pool_o, v_w_mem_kv, v_w_mem_o, v_w_out, v_w_ff1, v_w_ff2, v_norm1_g, v_norm2_g, v_mem_norm_g, v_final_norm_g):
    B, S, _ = x.shape
    T = B * S
    big_w = dict(w_in=w_in, w_ret_o=w_ret_o, w_pool_o=w_pool_o, w_mem_kv=w_mem_kv, w_mem_o=w_mem_o, w_out=w_out, w_ff1=w_ff1, w_ff2=w_ff2)
    big_m = dict(w_in=m_w_in, w_ret_o=m_w_ret_o, w_pool_o=m_w_pool_o, w_mem_kv=m_w_mem_kv, w_mem_o=m_w_mem_o, w_out=m_w_out, w_ff1=m_w_ff1, w_ff2=m_w_ff2)
    big_v = dict(w_in=v_w_in, w_ret_o=v_w_ret_o, w_pool_o=v_w_pool_o, w_mem_kv=v_w_mem_kv, w_mem_o=v_w_mem_o, w_out=v_w_out, w_ff1=v_w_ff1, w_ff2=v_w_ff2)
    small_w = [ret_decay_logit, w_pool_grp, pool_scale, norm1_g, norm2_g, mem_norm_g, final_norm_g]
    small_m = [m_ret_decay_logit, m_w_pool_grp, m_pool_scale, m_norm1_g, m_norm2_g, m_mem_norm_g, m_final_norm_g]
    small_v = [v_ret_decay_logit, v_w_pool_grp, v_pool_scale, v_norm1_g, v_norm2_g, v_mem_norm_g, v_final_norm_g]
    my_c = lax.axis_index("c").astype(jnp.int32).reshape(1)

    shards = [_cast(big_w[n].reshape(-1, big_w[n].shape[-1]), _COMM).reshape(big_w[n].shape) for n in BIG]
    weights = [dict(zip(BIG, _allgather_weights(shards, l))) for l in range(DEPTH)]

    cos, sin = _rope_tables(S)
    log_g = jax.nn.log_sigmoid(ret_decay_logit.astype(F32))
    mem2d = mem.reshape(B * MEM_LEN, D_MODEL)
    memn = _mem_norm(mem2d, mem_norm_g.reshape(1, D_MODEL))
    smalls = [(norm1_g[l].reshape(1, D_MODEL), norm2_g[l].reshape(1, D_MODEL), log_g[l], w_pool_grp[l],
               pool_scale[l].reshape(1, POOL_W)) for l in range(DEPTH)]

    h = x.reshape(T, D_MODEL)
    saved = []
    for l in range(DEPTH):
        h, sv = _layer_fwd(h, weights[l], smalls[l], memn, cos, sin, B, S)
        saved.append(sv)

    dh, sq, d_final_g = _loss_head(h, final_norm_g.reshape(1, D_MODEL), loss_target.reshape(T, D_MODEL))
    loss = lax.psum(0.5 * jnp.sum(sq) / D_MODEL, ("x", "y", "c"))

    big_grads = [None] * DEPTH
    small_grads = [None] * DEPTH
    d_memn = None
    for l in reversed(range(DEPTH)):
        dh, grads, small_grads[l], d_memn = _layer_bwd(dh, saved[l], weights[l], smalls[l], memn, d_memn, cos, sin, B, S)
        big_grads[l] = dict(zip(BIG, _reduce_scatter_grads([grads[n] for n in BIG], my_c)))
    d_mem_g = _mem_norm_wgrad(mem2d, d_memn)

    d_logit = jnp.stack([sg["lg"] for sg in small_grads]) * jax.nn.sigmoid(-ret_decay_logit.astype(F32))
    small_g_local = [
        d_logit, jnp.stack([sg["w_grp"] for sg in small_grads]),
        jnp.stack([sg["scale"].reshape(POOL_W) for sg in small_grads]),
        jnp.stack([sg["g1"].reshape(D_MODEL) for sg in small_grads]),
        jnp.stack([sg["g2"].reshape(D_MODEL) for sg in small_grads]),
        d_mem_g.reshape(D_MODEL), d_final_g.reshape(D_MODEL),
    ]
    small_g = _allreduce_small(_pack_small(small_g_local))
    s_delta, s_m, s_v = _adamw(_pack_small(small_w), small_g, _pack_small(small_m), _pack_small(small_v))
    small_g, s_delta, s_m, s_v = (_unpack_small(a, small_w) for a in (small_g, s_delta, s_m, s_v))

    big_out = {}
    for n in BIG:
        w = big_w[n]
        g = jnp.stack([big_grads[l][n] for l in range(DEPTH)])
        flat = lambda a: a.reshape(-1, a.shape[-1])
        d, nm, nv = _adamw(flat(w), flat(g), flat(big_m[n]), flat(big_v[n]))
        big_out[n] = (g, d.reshape(w.shape), nm.reshape(w.shape), nv.reshape(w.shape))

    order = ["w_in", "ret_decay_logit", "w_ret_o", "w_pool_grp", "pool_scale", "w_pool_o", "w_mem_kv", "w_mem_o",
             "w_out", "w_ff1", "w_ff2", "norm1_g", "norm2_g", "mem_norm_g", "final_norm_g"]
    small_names = ["ret_decay_logit", "w_pool_grp", "pool_scale", "norm1_g", "norm2_g", "mem_norm_g", "final_norm_g"]
    outs = [[], [], [], []]
    for n in order:
        if n in big_out:
            vals = big_out[n]
        else:
            i = small_names.index(n)
            vals = (small_g[i], s_delta[i], s_m[i], s_v[i])
        for k in range(4):
            outs[k].append(vals[k])
    return (loss, dh.reshape(B, S, D_MODEL), *outs[0], *outs[1], *outs[2], *outs[3])
```

```python
import functools

import jax
import jax.numpy as jnp
from jax import lax
from jax.experimental import pallas as pl
from jax.experimental.pallas import tpu as pltpu

F32 = jnp.float32
_MXU = jnp.bfloat16
_ACT = jnp.bfloat16
_COMM = jnp.bfloat16

D_MODEL = 1024
N_PROJ = 7168
FFN_HIDDEN = 4096
MEM_LEN = 256
N_HEADS = 4
QK_DIM = 128
V_DIM = 256
POOL_W = 512
MEMQ_W = 512
DEPTH = 4
N_CHIPS = 4
EPS = 1e-6
ROPE_BASE = 10000.0

C_Q, C_K, C_V, C_G, C_P, C_QM, C_GATE = 0, 512, 1024, 2048, 3072, 3584, 4096

ADAM_LR = 0.001
ADAM_B1 = 0.9
ADAM_B2 = 0.999
ADAM_EPS = 1e-08
ADAM_WD = 0.01
ADAM_STEP = 10

VMEM_LIMIT_BYTES = 56 * 1024 * 1024
MESH = pl.DeviceIdType.MESH
ANY = pl.BlockSpec(memory_space=pl.ANY)

_DN = {
    "nn": (((1,), (0,)), ((), ())),
    "nt": (((1,), (1,)), ((), ())),
    "tn": (((0,), (0,)), ((), ())),
}


def _call(body, **kw):
    return pl.pallas_call(body, **kw)


def _cp(sem=None):
    return pltpu.CompilerParams(dimension_semantics=sem, vmem_limit_bytes=VMEM_LIMIT_BYTES)


def _dot(a, b, kind="nn"):
    return lax.dot_general(a.astype(_MXU), b.astype(_MXU), _DN[kind], preferred_element_type=F32)


def _sds(shape, dtype):
    return jax.ShapeDtypeStruct(shape, dtype)


def _rms_rows(x, g):
    r = lax.rsqrt(jnp.mean(x * x, axis=-1, keepdims=True) + EPS)
    return x * r * g


def _relu2(u):
    r = jnp.maximum(u.astype(F32), 0.0)
    return r * r


def _add_res(r, e):
    return r + e.astype(F32)


def _relu2_bwd(r, u):
    return r * (2.0 * jnp.maximum(u.astype(F32), 0.0))


def _mm(name, kind, a, b, a_spec, b_spec, out_shape, o_spec, grid, acc_shape, *,
        pro=None, pro_in=(), pro_specs=(), epi=None, epi_in=(), epi_specs=()):
    nk = grid[2]
    npro, nepi = len(pro_in), len(epi_in)

    def body(*refs):
        a_ref, b_ref = refs[0], refs[1]
        pro_refs = refs[2:2 + npro]
        epi_refs = refs[2 + npro:2 + npro + nepi]
        o_ref = refs[2 + npro + nepi]
        av = a_ref[...]
        if pro is not None:
            av = pro(av, *[r[...] for r in pro_refs])
        part = _dot(av, b_ref[...], kind)

        def finish(r):
            if epi is not None:
                r = epi(r, *[e[...] for e in epi_refs])
            o_ref[...] = r.astype(o_ref.dtype)

        if nk == 1:
            finish(part)
        else:
            acc = refs[-1]
            k = pl.program_id(2)

            @pl.when(k == 0)
            def _():
                acc[...] = part

            @pl.when(k > 0)
            def _():
                acc[...] += part

            @pl.when(k == nk - 1)
            def _():
                finish(acc[...])

    scratch = [] if nk == 1 else [pltpu.VMEM(acc_shape, F32)]
    return _call(
        body, out_shape=out_shape, grid=grid,
        in_specs=[a_spec, b_spec, *pro_specs, *epi_specs], out_specs=o_spec,
        scratch_shapes=scratch, name=name,
        compiler_params=_cp(("parallel", "parallel", "arbitrary")),
    )(a, b, *pro_in, *epi_in)


def _row_tile(n, cap):
    t = min(n, cap)
    assert n % t == 0, (n, t)
    return t


def _resident(w):
    nd = w.ndim
    return pl.BlockSpec(w.shape, lambda i: (0,) * nd, pipeline_mode=pl.Buffered(1))


def _rms_proj(name, x, g, w, n_cols):
    T = x.shape[0]
    tm = _row_tile(T, 512)
    ns = n_cols // N_CHIPS

    def body(x_ref, g_ref, w_ref, o_ref, h_ref):
        h = _rms_rows(x_ref[...], g_ref[...]).astype(h_ref.dtype)
        h_ref[...] = h
        for k in range(N_CHIPS):
            o_ref[:, k * ns:(k + 1) * ns] = _dot(h, w_ref[k]).astype(o_ref.dtype)

    row = pl.BlockSpec((tm, D_MODEL), lambda i: (i, 0))
    return _call(
        body, out_shape=(_sds((T, n_cols), _ACT), _sds((T, D_MODEL), _ACT)), grid=(T // tm,),
        in_specs=[row, pl.BlockSpec((1, D_MODEL), lambda i: (0, 0)), _resident(w)],
        out_specs=(pl.BlockSpec((tm, n_cols), lambda i: (i, 0)), row),
        name=name, compiler_params=_cp(("parallel",)),
    )(x, g, w)


def _colsharded_wgrad_full(name, h, dy, n_cols, tn):
    T = h.shape[0]
    tt = _row_tile(T, 1024)
    per = (n_cols // N_CHIPS) // tn
    return _mm(
        name, "tn", h, dy,
        pl.BlockSpec((tt, D_MODEL), lambda i, j, k: (k, 0)),
        pl.BlockSpec((tt, tn), lambda i, j, k: (k, j)),
        _sds((N_CHIPS, D_MODEL, n_cols // N_CHIPS), _COMM),
        pl.BlockSpec((None, D_MODEL, tn), lambda i, j, k: (j // per, 0, j % per)),
        (1, n_cols // tn, T // tt), (D_MODEL, tn),
    )


def _dgrad_rms(name, dy, w, x, g, dres):
    T, kd = dy.shape
    tm = _row_tile(T, 512)
    ns = kd // N_CHIPS

    def body(dy_ref, w_ref, x_ref, g_ref, dres_ref, dx_ref, dg_ref):
        i = pl.program_id(0)
        dh = _dot(dy_ref[:, 0:ns], w_ref[0], "nt")
        for k in range(1, N_CHIPS):
            dh += _dot(dy_ref[:, k * ns:(k + 1) * ns], w_ref[k], "nt")
        x = x_ref[...]
        r = lax.rsqrt(jnp.mean(x * x, axis=-1, keepdims=True) + EPS)
        xh = x * r
        dxh = dh * g_ref[...]
        dx_ref[...] = dres_ref[...] + r * (dxh - xh * jnp.mean(dxh * xh, axis=-1, keepdims=True))
        dgp = jnp.sum(dh * xh, axis=0, keepdims=True)

        @pl.when(i == 0)
        def _():
            dg_ref[...] = dgp

        @pl.when(i > 0)
        def _():
            dg_ref[...] += dgp

    row = pl.BlockSpec((tm, D_MODEL), lambda i: (i, 0))
    vec = pl.BlockSpec((1, D_MODEL), lambda i: (0, 0))
    return _call(
        body,
        out_shape=(_sds((T, D_MODEL), F32), _sds((1, D_MODEL), F32)),
        grid=(T // tm,),
        in_specs=[pl.BlockSpec((tm, kd), lambda i: (i, 0)), _resident(w), row, vec, row],
        out_specs=(row, vec),
        name=name, compiler_params=_cp(("arbitrary",)),
    )(dy, w, x, g, dres)


def _ffn_out(x1, u, w):
    T = u.shape[0]
    tm = _row_tile(T, 512)
    ks = w.shape[1]

    def body(x_ref, u_ref, w_ref, o_ref):
        acc = x_ref[...]
        for k in range(N_CHIPS):
            acc += _dot(_relu2(u_ref[:, k * ks:(k + 1) * ks]), w_ref[k])
        o_ref[...] = acc

    row = pl.BlockSpec((tm, D_MODEL), lambda i: (i, 0))
    return _call(
        body, out_shape=_sds((T, D_MODEL), F32), grid=(T // tm,),
        in_specs=[row, pl.BlockSpec((tm, FFN_HIDDEN), lambda i: (i, 0)), _resident(w)], out_specs=row,
        name="ff2", compiler_params=_cp(("parallel",)),
    )(x1, u, w)


def _ffn_out_dgrad(dx2, u, w):
    T = u.shape[0]
    tm = _row_tile(T, 512)
    ks = w.shape[1]

    def body(d_ref, u_ref, w_ref, o_ref):
        d = d_ref[...].astype(_MXU)
        for k in range(N_CHIPS):
            sl = slice(k * ks, (k + 1) * ks)
            o_ref[:, sl] = _relu2_bwd(_dot(d, w_ref[k], "nt"), u_ref[:, sl]).astype(o_ref.dtype)

    wide = pl.BlockSpec((tm, FFN_HIDDEN), lambda i: (i, 0))
    return _call(
        body, out_shape=_sds((T, FFN_HIDDEN), _ACT), grid=(T // tm,),
        in_specs=[pl.BlockSpec((tm, D_MODEL), lambda i: (i, 0)), wide, _resident(w)], out_specs=wide,
        name="ff2_dgrad", compiler_params=_cp(("parallel",)),
    )(dx2, u, w)


def _rowsharded_mm(name, a, w, res, *, pro=None):
    T, K = a.shape
    ks = K // N_CHIPS
    tm = _row_tile(T, 512)
    epi = dict(epi=_add_res, epi_in=(res,), epi_specs=(pl.BlockSpec((tm, D_MODEL), lambda i, j, k: (i, 0)),)) if res is not None else {}
    return _mm(
        name, "nn", a, w,
        pl.BlockSpec((tm, ks), lambda i, j, k: (i, k)),
        pl.BlockSpec((None, ks, D_MODEL), lambda i, j, k: (k, 0, 0)),
        _sds((T, D_MODEL), F32 if res is not None else _ACT),
        pl.BlockSpec((tm, D_MODEL), lambda i, j, k: (i, 0)),
        (T // tm, 1, N_CHIPS), (tm, D_MODEL), pro=pro, **epi,
    )


def _rowsharded_dgrad(name, dy, w, *, epi=None, epi_in=None, out_dtype=_ACT, res=None):
    T = dy.shape[0]
    ks = w.shape[1]
    tm = _row_tile(T, 512)
    kw = {}
    if epi is not None:
        kw = dict(epi=epi, epi_in=(epi_in,), epi_specs=(pl.BlockSpec((tm, ks), lambda i, j, k: (i, j)),))
    return _mm(
        name, "nt", dy, w,
        pl.BlockSpec((tm, D_MODEL), lambda i, j, k: (i, 0)),
        pl.BlockSpec((None, ks, D_MODEL), lambda i, j, k: (j, 0, 0)),
        _sds((T, ks * N_CHIPS), out_dtype),
        pl.BlockSpec((tm, ks), lambda i, j, k: (i, j)),
        (T // tm, N_CHIPS, 1), None, **kw,
    )


def _rowsharded_wgrad(name, a, dy, K, *, pro=None):
    T = a.shape[0]
    ks = K // N_CHIPS
    tt = _row_tile(T, 1024)
    return _mm(
        name, "tn", a, dy,
        pl.BlockSpec((tt, ks), lambda i, j, k: (k, i)),
        pl.BlockSpec((tt, D_MODEL), lambda i, j, k: (k, 0)),
        _sds((N_CHIPS, ks, D_MODEL), _COMM),
        pl.BlockSpec((None, ks, D_MODEL), lambda i, j, k: (i, 0, 0)),
        (N_CHIPS, 1, T // tt), (ks, D_MODEL), pro=pro,
    )


def _mixer_wgrads(merged, y_r, y_p, o_m, dx1, do3):
    T = merged.shape[0]
    tt = _row_tile(T, 512)
    nt = T // tt
    rs = D_MODEL // N_CHIPS

    def body(mg_ref, yr_ref, yp_ref, om_ref, dx_ref, do3_ref, g_out, g_ret, g_pool, g_mem, a_out, a_ret, a_pool, a_mem):
        t = pl.program_id(0)

        def accumulate(acc, part):
            @pl.when(t == 0)
            def _():
                acc[...] = part

            @pl.when(t > 0)
            def _():
                acc[...] += part

        accumulate(a_out, _dot(mg_ref[...], dx_ref[...], "tn"))
        accumulate(a_ret, _dot(yr_ref[...], do3_ref[:, 0:D_MODEL], "tn"))
        accumulate(a_pool, _dot(yp_ref[...], do3_ref[:, D_MODEL:2 * D_MODEL], "tn"))
        accumulate(a_mem, _dot(om_ref[...], do3_ref[:, 2 * D_MODEL:3 * D_MODEL], "tn"))

        @pl.when(t == nt - 1)
        def _():
            for k in range(N_CHIPS):
                rows = slice(k * rs, (k + 1) * rs)
                g_out[k] = a_out[rows, :].astype(g_out.dtype)
                g_ret[k] = a_ret[rows, :].astype(g_ret.dtype)
                g_pool[k] = a_pool[:, rows].astype(g_pool.dtype)
                g_mem[k] = a_mem[:, rows].astype(g_mem.dtype)

    row = lambda w: pl.BlockSpec((tt, w), lambda t: (t, 0))
    whole = lambda s: pl.BlockSpec(s, lambda t: (0, 0, 0))
    rsh, csh = (N_CHIPS, rs, D_MODEL), (N_CHIPS, POOL_W, rs)
    return _call(
        body,
        out_shape=(_sds(rsh, _COMM), _sds(rsh, _COMM), _sds(csh, _COMM), _sds(csh, _COMM)),
        grid=(nt,),
        in_specs=[row(D_MODEL), row(D_MODEL), row(POOL_W), row(MEMQ_W), row(D_MODEL), row(3 * D_MODEL)],
        out_specs=(whole(rsh), whole(rsh), whole(csh), whole(csh)),
        scratch_shapes=[pltpu.VMEM((D_MODEL, D_MODEL), F32), pltpu.VMEM((D_MODEL, D_MODEL), F32),
                        pltpu.VMEM((POOL_W, D_MODEL), F32), pltpu.VMEM((MEMQ_W, D_MODEL), F32)],
        name="mixer_wgrads", compiler_params=_cp(("arbitrary",)),
    )(merged, y_r, y_p, o_m, dx1, do3)


def _rot(x, cos, sin):
    return x * cos + pltpu.roll(x, QK_DIM // 2, 1) * sin


def _rot_bwd(d, cos, sin):
    return d * cos + pltpu.roll(d * sin, QK_DIM // 2, 1)


def _tile_diff(qi, ki, tq, tk):
    n = qi * tq + lax.broadcasted_iota(jnp.int32, (tq, tk), 0)
    m = ki * tk + lax.broadcasted_iota(jnp.int32, (tq, tk), 1)
    return (n - m).astype(F32)


def _decay(qi, ki, tq, tk, lgf, lgb):
    diff = _tile_diff(qi, ki, tq, tk)
    return diff, jnp.exp(jnp.where(diff >= 0.0, lgf * diff, -(lgb * diff)))


def _row_index(t):
    return lax.broadcasted_iota(jnp.int32, (t, QK_DIM), 0).astype(F32)


def _q_decay_fwd(t, lgf):
    return jnp.exp(lgf * _row_index(t))


def _q_decay_bwd(t, lgb):
    return jnp.exp(lgb * (float(t) - _row_index(t)))


def _k_decay_fwd(t, lgf, tiles_apart):
    return jnp.exp(lgf * ((tiles_apart * t).astype(F32) - _row_index(t)))


def _k_decay_bwd(t, lgb, tiles_apart):
    return jnp.exp(lgb * (((tiles_apart - 1) * t).astype(F32) + _row_index(t)))


def _group_norm_gate(o, g):
    mu = jnp.mean(o, axis=-1, keepdims=True)
    oc = o - mu
    var = jnp.mean(oc * oc, axis=-1, keepdims=True)
    on = oc * lax.rsqrt(var + EPS)
    return on * (g * jax.nn.sigmoid(g))


def _retention_fwd(proj, cos, sin, lg, B, S):
    T = B * S
    tq = tk = _row_tile(S, 512)
    nq, nk = S // tq, S // tk
    scale = QK_DIM ** -0.5

    def body(lg_ref, q_ref, k_ref, v_ref, g_ref, cq_ref, sq_ref, ck_ref, sk_ref, o_ref, y_ref, q0_scr, qf_scr, qb_scr, acc):
        h = pl.program_id(1)
        qi = pl.program_id(2)
        ki = pl.program_id(3)
        lgf, lgb = lg_ref[0, h], lg_ref[1, h]

        @pl.when(ki == 0)
        def _():
            qr = _rot(q_ref[...].astype(F32), cq_ref[...], sq_ref[...])
            q0_scr[...] = qr.astype(_MXU)
            qf_scr[...] = (qr * _q_decay_fwd(tq, lgf)).astype(_MXU)
            qb_scr[...] = (qr * _q_decay_bwd(tq, lgb)).astype(_MXU)
            acc[...] = jnp.zeros_like(acc)

        kr = _rot(k_ref[...].astype(F32), ck_ref[...], sk_ref[...]) * scale

        @pl.when(ki < qi)
        def _():
            s = _dot(qf_scr[...], kr * _k_decay_fwd(tk, lgf, qi - ki), "nt")
            acc[...] += _dot(s, v_ref[...])

        @pl.when(ki > qi)
        def _():
            s = _dot(qb_scr[...], kr * _k_decay_bwd(tk, lgb, ki - qi), "nt")
            acc[...] += _dot(s, v_ref[...])

        @pl.when(ki == qi)
        def _():
            s = _dot(q0_scr[...], kr, "nt")
            _, dec = _decay(qi, ki, tq, tk, lgf, lgb)
            acc[...] += _dot(s * dec, v_ref[...])

        @pl.when(ki == nk - 1)
        def _():
            o = acc[...]
            o_ref[...] = o
            y_ref[...] = _group_norm_gate(o, g_ref[...].astype(F32)).astype(y_ref.dtype)

    hq = C_K // QK_DIM
    hv = C_V // V_DIM
    hg = C_G // V_DIM
    return _call(
        body,
        out_shape=(_sds((T, N_HEADS * V_DIM), F32), _sds((T, N_HEADS * V_DIM), _ACT)),
        grid=(B, N_HEADS, nq, nk),
        in_specs=[
            pl.BlockSpec(memory_space=pltpu.SMEM),
            pl.BlockSpec((tq, QK_DIM), lambda b, h, qi, ki: (b * nq + qi, h)),
            pl.BlockSpec((tk, QK_DIM), lambda b, h, qi, ki: (b * nk + ki, hq + h)),
            pl.BlockSpec((tk, V_DIM), lambda b, h, qi, ki: (b * nk + ki, hv + h)),
            pl.BlockSpec((tq, V_DIM), lambda b, h, qi, ki: (b * nq + qi, hg + h)),
            pl.BlockSpec((tq, QK_DIM), lambda b, h, qi, ki: (qi, 0)),
            pl.BlockSpec((tq, QK_DIM), lambda b, h, qi, ki: (qi, 0)),
            pl.BlockSpec((tk, QK_DIM), lambda b, h, qi, ki: (ki, 0)),
            pl.BlockSpec((tk, QK_DIM), lambda b, h, qi, ki: (ki, 0)),
        ],
        out_specs=(
            pl.BlockSpec((tq, V_DIM), lambda b, h, qi, ki: (b * nq + qi, h)),
            pl.BlockSpec((tq, V_DIM), lambda b, h, qi, ki: (b * nq + qi, h)),
        ),
        scratch_shapes=[pltpu.VMEM((tq, QK_DIM), _MXU), pltpu.VMEM((tq, QK_DIM), _MXU), pltpu.VMEM((tq, QK_DIM), _MXU),
                        pltpu.VMEM((tq, V_DIM), F32)],
        name="retention_fwd",
        compiler_params=_cp(("parallel", "parallel", "parallel", "arbitrary")),
    )(lg, proj, proj, proj, proj, cos, sin, cos, sin)


def _gn_gate_bwd(dy, o, proj):
    T = dy.shape[0]
    tm = _row_tile(T, 512)

    def body(dy_ref, o_ref, g_ref, do_ref, dg_ref):
        for h in range(N_HEADS):
            sl = slice(h * V_DIM, (h + 1) * V_DIM)
            o = o_ref[:, sl]
            g = g_ref[:, sl].astype(F32)
            d = dy_ref[:, sl].astype(F32)
            mu = jnp.mean(o, axis=-1, keepdims=True)
            oc = o - mu
            rstd = lax.rsqrt(jnp.mean(oc * oc, axis=-1, keepdims=True) + EPS)
            on = oc * rstd
            sg = jax.nn.sigmoid(g)
            don = d * (g * sg)
            dg_ref[:, sl] = (d * on * (sg * (1.0 + g * (1.0 - sg)))).astype(dg_ref.dtype)
            do = rstd * (don - jnp.mean(don, axis=-1, keepdims=True) - on * jnp.mean(don * on, axis=-1, keepdims=True))
            do_ref[:, sl] = do.astype(do_ref.dtype)

    wide = N_HEADS * V_DIM
    return _call(
        body,
        out_shape=(_sds((T, wide), _ACT), _sds((T, wide), _ACT)),
        grid=(T // tm,),
        in_specs=[
            pl.BlockSpec((tm, wide), lambda i: (i, 0)),
            pl.BlockSpec((tm, wide), lambda i: (i, 0)),
            pl.BlockSpec((tm, wide), lambda i: (i, C_G // wide)),
        ],
        out_specs=(pl.BlockSpec((tm, wide), lambda i: (i, 0)), pl.BlockSpec((tm, wide), lambda i: (i, 0))),
        name="gn_gate_bwd", compiler_params=_cp(("parallel",)),
    )(dy, o, proj)


def _retention_bwd(proj, d_o, cos, sin, lg, B, S):
    T = B * S
    tq = tk = _row_tile(S, 512)
    nq, nk = S // tq, S // tk
    scale = QK_DIM ** -0.5

    def body(lg_ref, q_ref, k_ref, v_ref, do_ref, cq_ref, sq_ref, ck_ref, sk_ref,
             dq_ref, dk_ref, dv_ref, dlg_ref, kr_scr, dq_acc, dk_acc, dv_acc, gl_acc):
        h = pl.program_id(1)
        ki = pl.program_id(2)
        qi = pl.program_id(3)
        rows = pl.ds(pl.multiple_of(qi * tq, tq), tq)

        lgf, lgb = lg_ref[0, h], lg_ref[1, h]

        @pl.when(qi == 0)
        def _():
            kr_scr[...] = _rot(k_ref[...].astype(F32), ck_ref[...], sk_ref[...]) * scale
            dk_acc[...] = jnp.zeros_like(dk_acc)
            dv_acc[...] = jnp.zeros_like(dv_acc)

        @pl.when(jnp.logical_and(qi == 0, ki == 0))
        def _():
            gl_acc[...] = jnp.zeros_like(gl_acc)

        @pl.when(ki == 0)
        def _():
            dq_acc[rows, :] = jnp.zeros((tq, QK_DIM), F32)

        def one_sided(q_factor, k_factor, row, sign):
            qt = (_rot(q_ref[...].astype(F32), cq_ref[...], sq_ref[...]) * q_factor).astype(_MXU)
            kt = (kr_scr[...] * k_factor).astype(_MXU)
            d_out = do_ref[...]
            p = _dot(qt, kt, "nt")
            dv_acc[...] += _dot(p, d_out, "tn")
            dp = _dot(d_out, v_ref[...], "nt")
            dq_acc[rows, :] += _dot(dp, kt) * q_factor
            dk_acc[...] += _dot(dp, qt, "tn") * k_factor
            diff = _tile_diff(qi, ki, tq, tk)
            gl_acc[row:row + 1, :] += sign * jnp.sum(dp * p * diff, axis=0, keepdims=True)

        @pl.when(ki < qi)
        def _():
            one_sided(_q_decay_fwd(tq, lgf), _k_decay_fwd(tk, lgf, qi - ki), 0, 1.0)

        @pl.when(ki > qi)
        def _():
            one_sided(_q_decay_bwd(tq, lgb), _k_decay_bwd(tk, lgb, ki - qi), 1, -1.0)

        @pl.when(ki == qi)
        def _():
            qr = _rot(q_ref[...].astype(F32), cq_ref[...], sq_ref[...]).astype(_MXU)
            kr = kr_scr[...].astype(_MXU)
            d_out = do_ref[...]
            s = _dot(qr, kr, "nt")
            diff, dec = _decay(qi, ki, tq, tk, lgf, lgb)
            p = s * dec
            dv_acc[...] += _dot(p, d_out, "tn")
            dp = _dot(d_out, v_ref[...], "nt")
            ds = dp * dec
            dq_acc[rows, :] += _dot(ds, kr)
            dk_acc[...] += _dot(ds, qr, "tn")
            gd = dp * p * diff
            gl_acc[0:1, :] += jnp.sum(jnp.where(diff >= 0.0, gd, 0.0), axis=0, keepdims=True)
            gl_acc[1:2, :] += jnp.sum(jnp.where(diff < 0.0, -gd, 0.0), axis=0, keepdims=True)

        @pl.when(qi == nq - 1)
        def _():
            dk_ref[...] = _rot_bwd(dk_acc[...] * scale, ck_ref[...], sk_ref[...]).astype(dk_ref.dtype)
            dv_ref[...] = dv_acc[...].astype(dv_ref.dtype)

        @pl.when(ki == nk - 1)
        def _():
            dq_ref[rows, :] = _rot_bwd(dq_acc[rows, :], cq_ref[...], sq_ref[...]).astype(dq_ref.dtype)

        @pl.when(jnp.logical_and(qi == nq - 1, ki == nk - 1))
        def _():
            dlg_ref[...] = gl_acc[...]

    hq = C_K // QK_DIM
    hv = C_V // V_DIM
    return _call(
        body,
        out_shape=(
            _sds((T, N_HEADS * QK_DIM), _ACT), _sds((T, N_HEADS * QK_DIM), _ACT),
            _sds((T, N_HEADS * V_DIM), _ACT), _sds((B, N_HEADS, 2, tk), F32),
        ),
        grid=(B, N_HEADS, nk, nq),
        in_specs=[
            pl.BlockSpec(memory_space=pltpu.SMEM),
            pl.BlockSpec((tq, QK_DIM), lambda b, h, ki, qi: (b * nq + qi, h)),
            pl.BlockSpec((tk, QK_DIM), lambda b, h, ki, qi: (b * nk + ki, hq + h)),
            pl.BlockSpec((tk, V_DIM), lambda b, h, ki, qi: (b * nk + ki, hv + h)),
            pl.BlockSpec((tq, V_DIM), lambda b, h, ki, qi: (b * nq + qi, h)),
            pl.BlockSpec((tq, QK_DIM), lambda b, h, ki, qi: (qi, 0)),
            pl.BlockSpec((tq, QK_DIM), lambda b, h, ki, qi: (qi, 0)),
            pl.BlockSpec((tk, QK_DIM), lambda b, h, ki, qi: (ki, 0)),
            pl.BlockSpec((tk, QK_DIM), lambda b, h, ki, qi: (ki, 0)),
        ],
        out_specs=(
            pl.BlockSpec((S, QK_DIM), lambda b, h, ki, qi: (b, h)),
            pl.BlockSpec((tk, QK_DIM), lambda b, h, ki, qi: (b * nk + ki, h)),
            pl.BlockSpec((tk, V_DIM), lambda b, h, ki, qi: (b * nk + ki, h)),
            pl.BlockSpec((None, None, 2, tk), lambda b, h, ki, qi: (b, h, 0, 0)),
        ),
        scratch_shapes=[
            pltpu.VMEM((tk, QK_DIM), F32), pltpu.VMEM((S, QK_DIM), F32),
            pltpu.VMEM((tk, QK_DIM), F32), pltpu.VMEM((tk, V_DIM), F32), pltpu.VMEM((2, tk), F32),
        ],
        name="retention_bwd",
        compiler_params=_cp(("parallel", "parallel", "arbitrary", "arbitrary")),
    )(lg, proj, proj, proj, d_o, cos, sin, cos, sin)


POOL_PAD = 16


def _pad_rows(v):
    z = jnp.zeros((POOL_PAD, v.shape[1]), F32)
    return jnp.concatenate([z, v, z], axis=0)


def _window_sums(first, length, levels):
    s = first
    step = 1
    for _ in range(levels - 1):
        s = pltpu.roll(s, step, 0) + pltpu.roll(s, length - step, 0)
        step *= 2
    return s


def _pool_counts(S, hw):
    n = lax.broadcasted_iota(jnp.int32, (S, 1), 0)
    return (jnp.minimum(n + hw, S) - jnp.maximum(n - hw, 0)).astype(F32)


def _pool_mixed(pf, S, g):
    length = S + 2 * POOL_PAD
    xp = _pad_rows(pf)
    s = _window_sums(xp + pltpu.roll(xp, 1, 0), length, g + 1)[POOL_PAD:POOL_PAD + S]
    return s / _pool_counts(S, 1 << g) - pf


def _pool_mixed_bwd(dmixed, S, g):
    length = S + 2 * POOL_PAD
    ep = _pad_rows(dmixed / _pool_counts(S, 1 << g))
    t = _window_sums(ep + pltpu.roll(ep, length - 1, 0), length, g + 1)[POOL_PAD:POOL_PAD + S]
    return t - dmixed


def _pool_fwd(proj, w_grp, scale, B, S):
    T = B * S
    G = POOL_W // 4

    def body(p_ref, wg_ref, sc_ref, y_ref):
        for g in range(4):
            sl = slice(g * G, (g + 1) * G)
            mixed = _pool_mixed(p_ref[:, sl].astype(F32), S, g)
            y_ref[:, sl] = (_dot(mixed, wg_ref[g]) * sc_ref[:, sl]).astype(y_ref.dtype)

    return _call(
        body, out_shape=_sds((T, POOL_W), _ACT), grid=(B,),
        in_specs=[
            pl.BlockSpec((S, POOL_W), lambda b: (b, C_P // POOL_W)),
            pl.BlockSpec((4, G, G), lambda b: (0, 0, 0)),
            pl.BlockSpec((1, POOL_W), lambda b: (0, 0)),
        ],
        out_specs=pl.BlockSpec((S, POOL_W), lambda b: (b, 0)),
        name="pool_fwd", compiler_params=_cp(("parallel",)),
    )(proj, w_grp, scale)


def _pool_bwd(proj, dy, w_grp, scale, B, S):
    T = B * S
    G = POOL_W // 4

    def body(p_ref, dy_ref, wg_ref, sc_ref, dp_ref, dwg_ref, dsc_ref):
        b = pl.program_id(0)
        for g in range(4):
            sl = slice(g * G, (g + 1) * G)
            mixed = _pool_mixed(p_ref[:, sl].astype(F32), S, g)
            z = _dot(mixed, wg_ref[g])
            d = dy_ref[:, sl].astype(F32)
            dsc = jnp.sum(d * z, axis=0, keepdims=True)
            dz = d * sc_ref[:, sl]
            dwg = _dot(mixed, dz, "tn")
            dmixed = _dot(dz, wg_ref[g], "nt")
            dp_ref[:, sl] = _pool_mixed_bwd(dmixed, S, g).astype(dp_ref.dtype)

            @pl.when(b == 0)
            def _():
                dwg_ref[g] = dwg
                dsc_ref[:, sl] = dsc

            @pl.when(b > 0)
            def _():
                dwg_ref[g] += dwg
                dsc_ref[:, sl] += dsc

    return _call(
        body,
        out_shape=(_sds((T, POOL_W), _ACT), _sds((4, G, G), F32), _sds((1, POOL_W), F32)),
        grid=(B,),
        in_specs=[
            pl.BlockSpec((S, POOL_W), lambda b: (b, C_P // POOL_W)),
            pl.BlockSpec((S, POOL_W), lambda b: (b, 0)),
            pl.BlockSpec((4, G, G), lambda b: (0, 0, 0)),
            pl.BlockSpec((1, POOL_W), lambda b: (0, 0)),
        ],
        out_specs=(
            pl.BlockSpec((S, POOL_W), lambda b: (b, 0)),
            pl.BlockSpec((4, G, G), lambda b: (0, 0, 0)),
            pl.BlockSpec((1, POOL_W), lambda b: (0, 0)),
        ),
        name="pool_bwd", compiler_params=_cp(("arbitrary",)),
    )(proj, dy, w_grp, scale)


def _mem_softmax(q, k):
    s = _dot(q, k, "nt") * (QK_DIM ** -0.5)
    e = jnp.exp(s - jnp.max(s, axis=-1, keepdims=True))
    return e / jnp.sum(e, axis=-1, keepdims=True)


def _mem_attn_fwd(proj, kv, B, S):
    T = B * S
    tq = _row_tile(S, 512)
    nq = S // tq

    def body(q_ref, kv_ref, o_ref):
        for h in range(N_HEADS):
            sl = slice(h * QK_DIM, (h + 1) * QK_DIM)
            a = _mem_softmax(q_ref[:, sl], kv_ref[:, sl])
            o_ref[:, sl] = _dot(a, kv_ref[:, MEMQ_W + h * QK_DIM:MEMQ_W + (h + 1) * QK_DIM]).astype(o_ref.dtype)

    return _call(
        body, out_shape=_sds((T, MEMQ_W), _ACT), grid=(B, nq),
        in_specs=[
            pl.BlockSpec((tq, MEMQ_W), lambda b, i: (b * nq + i, C_QM // MEMQ_W)),
            pl.BlockSpec((MEM_LEN, 2 * MEMQ_W), lambda b, i: (b, 0)),
        ],
        out_specs=pl.BlockSpec((tq, MEMQ_W), lambda b, i: (b * nq + i, 0)),
        name="mem_attn_fwd", compiler_params=_cp(("parallel", "parallel")),
    )(proj, kv)


def _mem_attn_bwd(proj, kv, d_o, B, S):
    T = B * S
    tq = _row_tile(S, 512)
    nq = S // tq
    scale = QK_DIM ** -0.5

    def body(q_ref, kv_ref, do_ref, dq_ref, dkv_ref):
        i = pl.program_id(1)
        for h in range(N_HEADS):
            sl = slice(h * QK_DIM, (h + 1) * QK_DIM)
            vsl = slice(MEMQ_W + h * QK_DIM, MEMQ_W + (h + 1) * QK_DIM)
            q = q_ref[:, sl]
            a = _mem_softmax(q, kv_ref[:, sl])
            d = do_ref[:, sl]
            da = _dot(d, kv_ref[:, vsl], "nt")
            ds = a * (da - jnp.sum(a * da, axis=-1, keepdims=True)) * scale
            dq_ref[:, sl] = _dot(ds, kv_ref[:, sl]).astype(dq_ref.dtype)
            dk = _dot(ds, q, "tn")
            dv = _dot(a, d, "tn")

            @pl.when(i == 0)
            def _():
                dkv_ref[:, sl] = dk
                dkv_ref[:, vsl] = dv

            @pl.when(i > 0)
            def _():
                dkv_ref[:, sl] += dk
                dkv_ref[:, vsl] += dv

    return _call(
        body,
        out_shape=(_sds((T, MEMQ_W), _ACT), _sds((B * MEM_LEN, 2 * MEMQ_W), F32)),
        grid=(B, nq),
        in_specs=[
            pl.BlockSpec((tq, MEMQ_W), lambda b, i: (b * nq + i, C_QM // MEMQ_W)),
            pl.BlockSpec((MEM_LEN, 2 * MEMQ_W), lambda b, i: (b, 0)),
            pl.BlockSpec((tq, MEMQ_W), lambda b, i: (b * nq + i, 0)),
        ],
        out_specs=(
            pl.BlockSpec((tq, MEMQ_W), lambda b, i: (b * nq + i, 0)),
            pl.BlockSpec((MEM_LEN, 2 * MEMQ_W), lambda b, i: (b, 0)),
        ),
        name="mem_attn_bwd", compiler_params=_cp(("parallel", "arbitrary")),
    )(proj, kv, d_o)


def _mem_norm(mem2d, g):
    M = mem2d.shape[0]
    tm = _row_tile(M, 512)

    def body(x_ref, g_ref, o_ref):
        o_ref[...] = _rms_rows(x_ref[...], g_ref[...]).astype(o_ref.dtype)

    return _call(
        body, out_shape=_sds((M, D_MODEL), _ACT), grid=(M // tm,),
        in_specs=[pl.BlockSpec((tm, D_MODEL), lambda i: (i, 0)), pl.BlockSpec((1, D_MODEL), lambda i: (0, 0))],
        out_specs=pl.BlockSpec((tm, D_MODEL), lambda i: (i, 0)),
        name="mem_norm", compiler_params=_cp(("parallel",)),
    )(mem2d, g)


def _mem_norm_wgrad(mem2d, d_memn):
    M = mem2d.shape[0]
    tm = _row_tile(M, 512)

    def body(x_ref, d_ref, dg_ref):
        i = pl.program_id(0)
        x = x_ref[...]
        xh = x * lax.rsqrt(jnp.mean(x * x, axis=-1, keepdims=True) + EPS)
        dg = jnp.sum(d_ref[...] * xh, axis=0, keepdims=True)

        @pl.when(i == 0)
        def _():
            dg_ref[...] = dg

        @pl.when(i > 0)
        def _():
            dg_ref[...] += dg

    return _call(
        body, out_shape=_sds((1, D_MODEL), F32), grid=(M // tm,),
        in_specs=[pl.BlockSpec((tm, D_MODEL), lambda i: (i, 0)), pl.BlockSpec((tm, D_MODEL), lambda i: (i, 0))],
        out_specs=pl.BlockSpec((1, D_MODEL), lambda i: (0, 0)),
        name="mem_norm_wgrad", compiler_params=_cp(("arbitrary",)),
    )(mem2d, d_memn)


def _row_mm(a, w_ref):
    ks = w_ref.shape[1]
    out = _dot(a[:, 0:ks], w_ref[0])
    for k in range(1, N_CHIPS):
        out += _dot(a[:, k * ks:(k + 1) * ks], w_ref[k])
    return out


def _row_mm_t(d, w_ref):
    return jnp.concatenate([_dot(d, w_ref[k], "nt") for k in range(N_CHIPS)], axis=1)


def _col_mm(a, w_ref):
    return jnp.concatenate([_dot(a, w_ref[k]) for k in range(N_CHIPS)], axis=1)


def _col_mm_t(d, w_ref):
    ns = w_ref.shape[2]
    out = _dot(d[:, 0:ns], w_ref[0], "nt")
    for k in range(1, N_CHIPS):
        out += _dot(d[:, k * ns:(k + 1) * ns], w_ref[k], "nt")
    return out


def _full_spec(w):
    nd = w.ndim
    return pl.BlockSpec(w.shape, lambda i: (0,) * nd)


def _merge_fwd(x, proj, y_r, y_p, o_m, w_ret_o, w_pool_o, w_mem_o, w_out):
    T = x.shape[0]
    tm = _row_tile(T, 256)

    def body(x_ref, gr_ref, gp_ref, gm_ref, yr_ref, yp_ref, om_ref, wr_ref, wp_ref, wm_ref, wo_ref,
             x1_ref, mg_ref, o3_ref):
        o_r = _row_mm(yr_ref[...], wr_ref)
        o_p = _col_mm(yp_ref[...], wp_ref)
        o_q = _col_mm(om_ref[...], wm_ref)
        merged = (jax.nn.sigmoid(gr_ref[...].astype(F32)) * o_r + jax.nn.sigmoid(gp_ref[...].astype(F32)) * o_p
                  + jax.nn.sigmoid(gm_ref[...].astype(F32)) * o_q)
        mg = merged.astype(mg_ref.dtype)
        mg_ref[...] = mg
        o3_ref[:, 0:D_MODEL] = o_r.astype(o3_ref.dtype)
        o3_ref[:, D_MODEL:2 * D_MODEL] = o_p.astype(o3_ref.dtype)
        o3_ref[:, 2 * D_MODEL:3 * D_MODEL] = o_q.astype(o3_ref.dtype)
        x1_ref[...] = x_ref[...] + _row_mm(mg, wo_ref)

    gb = C_GATE // D_MODEL
    row = lambda w: pl.BlockSpec((tm, w), lambda i: (i, 0))
    return _call(
        body,
        out_shape=(_sds((T, D_MODEL), F32), _sds((T, D_MODEL), _ACT), _sds((T, 3 * D_MODEL), _ACT)),
        grid=(T // tm,),
        in_specs=[
            row(D_MODEL),
            pl.BlockSpec((tm, D_MODEL), lambda i: (i, gb)),
            pl.BlockSpec((tm, D_MODEL), lambda i: (i, gb + 1)),
            pl.BlockSpec((tm, D_MODEL), lambda i: (i, gb + 2)),
            row(D_MODEL), row(POOL_W), row(MEMQ_W),
            _full_spec(w_ret_o), _full_spec(w_pool_o), _full_spec(w_mem_o), _full_spec(w_out),
        ],
        out_specs=(row(D_MODEL), row(D_MODEL), row(3 * D_MODEL)),
        name="merge_fwd", compiler_params=_cp(("parallel",)),
    )(x, proj, proj, proj, y_r, y_p, o_m, w_ret_o, w_pool_o, w_mem_o, w_out)


def _merge_bwd(dx1, proj, o3, w_ret_o, w_pool_o, w_mem_o, w_out):
    T = dx1.shape[0]
    tm = _row_tile(T, 256)

    def body(dx_ref, gr_ref, gp_ref, gm_ref, o3_ref, wr_ref, wp_ref, wm_ref, wo_ref,
             do3_ref, dgate_ref, dyr_ref, dyp_ref, dom_ref):
        dmerged = _row_mm_t(dx_ref[...], wo_ref)
        douts = []
        for n, g_ref in enumerate((gr_ref, gp_ref, gm_ref)):
            sl = slice(n * D_MODEL, (n + 1) * D_MODEL)
            gate = jax.nn.sigmoid(g_ref[...].astype(F32))
            d_out = (dmerged * gate).astype(do3_ref.dtype)
            do3_ref[:, sl] = d_out
            dgate_ref[:, sl] = (dmerged * o3_ref[:, sl].astype(F32) * gate * (1.0 - gate)).astype(dgate_ref.dtype)
            douts.append(d_out)
        dyr_ref[...] = _row_mm_t(douts[0], wr_ref).astype(dyr_ref.dtype)
        dyp_ref[...] = _col_mm_t(douts[1], wp_ref).astype(dyp_ref.dtype)
        dom_ref[...] = _col_mm_t(douts[2], wm_ref).astype(dom_ref.dtype)

    gb = C_GATE // D_MODEL
    row = lambda w: pl.BlockSpec((tm, w), lambda i: (i, 0))
    return _call(
        body,
        out_shape=(
            _sds((T, 3 * D_MODEL), _ACT), _sds((T, 3 * D_MODEL), _ACT),
            _sds((T, D_MODEL), _ACT), _sds((T, POOL_W), _ACT), _sds((T, MEMQ_W), _ACT),
        ),
        grid=(T // tm,),
        in_specs=[
            row(D_MODEL),
            pl.BlockSpec((tm, D_MODEL), lambda i: (i, gb)),
            pl.BlockSpec((tm, D_MODEL), lambda i: (i, gb + 1)),
            pl.BlockSpec((tm, D_MODEL), lambda i: (i, gb + 2)),
            row(3 * D_MODEL),
            _full_spec(w_ret_o), _full_spec(w_pool_o), _full_spec(w_mem_o), _full_spec(w_out),
        ],
        out_specs=(row(3 * D_MODEL), row(3 * D_MODEL), row(D_MODEL), row(POOL_W), row(MEMQ_W)),
        name="merge_bwd", compiler_params=_cp(("parallel",)),
    )(dx1, proj, proj, proj, o3, w_ret_o, w_pool_o, w_mem_o, w_out)


def _loss_head(x, g, target):
    T = x.shape[0]
    tm = _row_tile(T, 512)

    def body(x_ref, g_ref, t_ref, dx_ref, sq_ref, dg_ref):
        i = pl.program_id(0)
        x = x_ref[...]
        gg = g_ref[...]
        r = lax.rsqrt(jnp.mean(x * x, axis=-1, keepdims=True) + EPS)
        xh = x * r
        err = xh * gg - t_ref[...]
        dy = err * (1.0 / D_MODEL)
        dxh = dy * gg
        dx_ref[...] = r * (dxh - xh * jnp.mean(dxh * xh, axis=-1, keepdims=True))
        sq = jnp.sum(err * err, axis=0, keepdims=True)
        dg = jnp.sum(dy * xh, axis=0, keepdims=True)

        @pl.when(i == 0)
        def _():
            sq_ref[...] = sq
            dg_ref[...] = dg

        @pl.when(i > 0)
        def _():
            sq_ref[...] += sq
            dg_ref[...] += dg

    vec = pl.BlockSpec((1, D_MODEL), lambda i: (0, 0))
    row = pl.BlockSpec((tm, D_MODEL), lambda i: (i, 0))
    return _call(
        body,
        out_shape=(_sds((T, D_MODEL), F32), _sds((1, D_MODEL), F32), _sds((1, D_MODEL), F32)),
        grid=(T // tm,), in_specs=[row, vec, row], out_specs=(row, vec, vec),
        name="loss_head", compiler_params=_cp(("arbitrary",)),
    )(x, g, target)


def _block_rows(rows, cols, itemsize, cap_bytes=2 << 20):
    t = rows
    while t * cols * itemsize > cap_bytes and t % 2 == 0 and (t // 2) % 16 == 0:
        t //= 2
    return t


def _cast_into_slot(w3d, layer, me):
    _, R, C = w3d.shape
    tr = _block_rows(R, C, 4)

    def body(me_ref, w_ref, o_ref):
        o_ref[...] = w_ref[...].astype(o_ref.dtype)

    grid_spec = pltpu.PrefetchScalarGridSpec(
        num_scalar_prefetch=1, grid=(R // tr,),
        in_specs=[pl.BlockSpec((None, tr, C), lambda i, me: (layer, i, 0))],
        out_specs=pl.BlockSpec((None, tr, C), lambda i, me: (me[0], i, 0)),
    )
    return _call(body, out_shape=_sds((N_CHIPS, R, C), _COMM), grid_spec=grid_spec,
                 name="cast_into_slot", compiler_params=_cp(("parallel",)))(me, w3d)


def _adamw(w, g, m, v):
    R, C = w.shape
    tr = _block_rows(R, C, 4, 1 << 20)

    def body(w_ref, g_ref, m_ref, v_ref, d_ref, nm_ref, nv_ref):
        g = g_ref[...]
        m = ADAM_B1 * m_ref[...] + (1.0 - ADAM_B1) * g
        v = ADAM_B2 * v_ref[...] + (1.0 - ADAM_B2) * (g * g)
        m_hat = m / (1.0 - ADAM_B1 ** ADAM_STEP)
        v_hat = v / (1.0 - ADAM_B2 ** ADAM_STEP)
        d_ref[...] = -ADAM_LR * (m_hat / (jnp.sqrt(v_hat) + ADAM_EPS) + ADAM_WD * w_ref[...])
        nm_ref[...] = m
        nv_ref[...] = v

    spec = pl.BlockSpec((tr, C), lambda i: (i, 0))
    out = _sds((R, C), F32)
    return _call(body, out_shape=(out, out, out), grid=(R // tr,), in_specs=[spec] * 4, out_specs=(spec,) * 3,
                 name="adamw", compiler_params=_cp(("parallel",)))(w, g, m, v)


def _add_sibling_half(g_full, land, my_c):
    _, R, C = g_full.shape
    hr = R // 2
    tr = _block_rows(hr, C, 2, 1 << 20)
    nb = hr // tr

    def body(c_ref, g_ref, l_ref, o_ref):
        o_ref[...] = (g_ref[...].astype(F32) + l_ref[...].astype(F32)).astype(o_ref.dtype)

    grid_spec = pltpu.PrefetchScalarGridSpec(
        num_scalar_prefetch=1, grid=(N_CHIPS, nb),
        in_specs=[
            pl.BlockSpec((None, tr, C), lambda k, i, c: (k, c[0] * nb + i, 0)),
            pl.BlockSpec((None, tr, C), lambda k, i, c: (k, i, 0)),
        ],
        out_specs=pl.BlockSpec((None, tr, C), lambda k, i, c: (k, i, 0)),
    )
    return _call(body, out_shape=_sds((N_CHIPS, hr, C), _COMM), grid_spec=grid_spec,
                 name="add_sibling_half", compiler_params=_cp(("parallel", "parallel")))(my_c, g_full, land)


def _add_chips(land, sums, g_all, layer, my_chip, my_core):
    _, hr, C = land.shape
    tr = _block_rows(hr, C, 4, 1 << 20)
    nb = hr // tr

    def body(me_ref, core_ref, own_ref, a_ref, b_ref, c_ref, *rest):
        o_ref = rest[-1]
        o_ref[...] = ((own_ref[...].astype(F32) + a_ref[...].astype(F32)) + b_ref[...].astype(F32)) + c_ref[...].astype(F32)

    def other(d):
        return pl.BlockSpec((None, tr, C), lambda i, me, core: ((me[0] + d) % N_CHIPS, i, 0))

    operands = [my_chip, my_core, sums, land, land, land]
    in_specs = [other(0), other(1), other(2), other(3)]
    aliases = {}
    if g_all is not None:
        operands.append(g_all)
        in_specs.append(ANY)
        aliases = {len(operands) - 1: 0}
    grid_spec = pltpu.PrefetchScalarGridSpec(
        num_scalar_prefetch=2, grid=(nb,), in_specs=in_specs,
        out_specs=pl.BlockSpec((None, tr, C), lambda i, me, core: (layer, core[0] * nb + i, 0)),
    )
    return _call(body, out_shape=_sds((DEPTH, 2 * hr, C), F32), grid_spec=grid_spec, input_output_aliases=aliases,
                 name="add_chips", compiler_params=_cp(("parallel",)))(*operands)


def _place():
    x, y, c = lax.axis_index("x"), lax.axis_index("y"), lax.axis_index("c")
    chips = [(1 - x, y), (x, 1 - y), (1 - x, 1 - y)]
    return x, y, c, 2 * x + y, chips


def _remote(src, dst, send_sem, recv_sem, dev):
    return pltpu.make_async_remote_copy(src_ref=src, dst_ref=dst, send_sem=send_sem, recv_sem=recv_sem,
                                        device_id=dev, device_id_type=MESH)


def _allgather_weights(bufs):
    n = len(bufs)

    def body(*refs):
        outs = refs[n:2 * n]
        send_sems, recv_sems = refs[2 * n:]
        x, y, c, me, chips = _place()
        sibling = (x, y, 1 - c)
        waits = []
        for w in range(n):
            hr = outs[w].shape[1] // 2
            own = outs[w].at[me, pl.ds(c * hr, hr)]
            for j, chip in enumerate(chips):
                cp = _remote(own, own, send_sems.at[w, j], recv_sems.at[w, j], (*chip, c))
                cp.start()
                waits.append(cp.wait_send)
        for w in range(n):
            hr = outs[w].shape[1] // 2
            mine = pl.ds(c * hr, hr)
            for j, chip in enumerate(chips):
                kc = 2 * chip[0] + chip[1]
                got = outs[w].at[kc, mine]
                _remote(got, got, send_sems.at[w, j], recv_sems.at[w, j], (*chip, c)).wait_recv()
                fwd = _remote(got, got, send_sems.at[w, 3 + j], recv_sems.at[w, 3 + j], sibling)
                fwd.start()
                waits.append(fwd.wait_send)
        for w in range(n):
            hr = outs[w].shape[1] // 2
            theirs = pl.ds((1 - c) * hr, hr)
            for j, chip in enumerate(chips):
                kc = 2 * chip[0] + chip[1]
                got = outs[w].at[kc, theirs]
                _remote(got, got, send_sems.at[w, 3 + j], recv_sems.at[w, 3 + j], sibling).wait_recv()
        for wait in waits:
            wait()

    out_shape = tuple(_sds(b.shape, b.dtype) for b in bufs)
    return _call(
        body, out_shape=out_shape, in_specs=[ANY] * n, out_specs=(ANY,) * n,
        input_output_aliases={i: i for i in range(n)},
        scratch_shapes=[pltpu.SemaphoreType.DMA((n, 6)), pltpu.SemaphoreType.DMA((n, 6))],
        name="allgather_weights",
    )(*bufs)


def _swap_sibling_halves(grads):
    n = len(grads)

    def body(*refs):
        ins, outs = refs[:n], refs[n:2 * n]
        send_sems, recv_sems = refs[2 * n:]
        x, y, c, _, _ = _place()
        copies = []
        for w in range(n):
            hr = ins[w].shape[1] // 2
            cp = _remote(ins[w].at[:, pl.ds((1 - c) * hr, hr)], outs[w], send_sems.at[w], recv_sems.at[w], (x, y, 1 - c))
            cp.start()
            copies.append(cp)
        for cp in copies:
            cp.wait()

    out_shape = tuple(_sds((N_CHIPS, g.shape[1] // 2, g.shape[2]), g.dtype) for g in grads)
    return _call(
        body, out_shape=out_shape, in_specs=[ANY] * n, out_specs=(ANY,) * n,
        scratch_shapes=[pltpu.SemaphoreType.DMA((n,)), pltpu.SemaphoreType.DMA((n,))],
        name="swap_sibling_halves",
    )(*grads)


def _scatter_chip_sums(sums):
    n = len(sums)

    def body(*refs):
        ins, outs = refs[:n], refs[n:2 * n]
        send_sems, recv_sems = refs[2 * n:]
        x, y, c, me, chips = _place()
        waits = []
        for w in range(n):
            for j, chip in enumerate(chips):
                kc = 2 * chip[0] + chip[1]
                cp = _remote(ins[w].at[kc], outs[w].at[me], send_sems.at[w, j], recv_sems.at[w, j], (*chip, c))
                cp.start()
                waits.append(cp.wait_send)
        for w in range(n):
            for j, chip in enumerate(chips):
                kc = 2 * chip[0] + chip[1]
                got = outs[w].at[kc]
                _remote(got, got, send_sems.at[w, j], recv_sems.at[w, j], (*chip, c)).wait_recv()
        for wait in waits:
            wait()

    out_shape = tuple(_sds(s.shape, s.dtype) for s in sums)
    return _call(
        body, out_shape=out_shape, in_specs=[ANY] * n, out_specs=(ANY,) * n,
        scratch_shapes=[pltpu.SemaphoreType.DMA((n, 3)), pltpu.SemaphoreType.DMA((n, 3))],
        name="scatter_chip_sums",
    )(*sums)


def _join_sibling_halves(g_alls, layer):
    n = len(g_alls)

    def body(*refs):
        outs = refs[n:2 * n]
        send_sems, recv_sems = refs[2 * n:]
        x, y, c, _, _ = _place()
        copies = []
        for w in range(n):
            hr = outs[w].shape[1] // 2
            mine = outs[w].at[layer, pl.ds(c * hr, hr)]
            cp = _remote(mine, mine, send_sems.at[w], recv_sems.at[w], (x, y, 1 - c))
            cp.start()
            copies.append(cp)
        for cp in copies:
            cp.wait()

    out_shape = tuple(_sds(g.shape, g.dtype) for g in g_alls)
    return _call(
        body, out_shape=out_shape, in_specs=[ANY] * n, out_specs=(ANY,) * n,
        input_output_aliases={i: i for i in range(n)},
        scratch_shapes=[pltpu.SemaphoreType.DMA((n,)), pltpu.SemaphoreType.DMA((n,))],
        name="join_sibling_halves",
    )(*g_alls)


def _reduce_scatter_grads(grads, g_alls, layer, my_chip, my_core):
    land = _swap_sibling_halves(grads)
    sums = [_add_sibling_half(g, l, my_core) for g, l in zip(grads, land)]
    parts = _scatter_chip_sums(sums)
    g_alls = [_add_chips(p, s, ga, layer, my_chip, my_core) for p, s, ga in zip(parts, sums, g_alls)]
    return _join_sibling_halves(g_alls, layer)


def _allreduce_small(v):
    R = v.shape[0]

    def body(v_ref, out_ref, sib_ref, chip_ref, sum_ref, send_sems, recv_sems):
        x, y, c, me, chips = _place()
        swap = _remote(v_ref, sib_ref, send_sems.at[0], recv_sems.at[0], (x, y, 1 - c))
        swap.start()
        swap.wait()
        sum_ref[...] = v_ref[...] + sib_ref[...]
        copies = []
        for j, chip in enumerate(chips):
            cp = _remote(sum_ref, chip_ref.at[me], send_sems.at[1 + j], recv_sems.at[1 + j], (*chip, c))
            cp.start()
            copies.append(cp)
        chip_ref[me] = sum_ref[...]
        for cp in copies:
            cp.wait()
        out_ref[...] = ((chip_ref[0] + chip_ref[1]) + chip_ref[2]) + chip_ref[3]

    vm = pl.BlockSpec(memory_space=pltpu.VMEM)
    return _call(
        body, out_shape=_sds((R, 128), F32), in_specs=[vm], out_specs=vm,
        scratch_shapes=[
            pltpu.VMEM((R, 128), F32), pltpu.VMEM((N_CHIPS, R, 128), F32), pltpu.VMEM((R, 128), F32),
            pltpu.SemaphoreType.DMA((4,)), pltpu.SemaphoreType.DMA((4,)),
        ],
        name="allreduce_small", compiler_params=_cp(),
    )(v)


def _pack_small(parts):
    rows = []
    for p in parts:
        flat = p.reshape(-1).astype(F32)
        pad = (-flat.shape[0]) % 128
        rows.append(jnp.pad(flat, (0, pad)).reshape(-1, 128))
    packed = jnp.concatenate(rows, axis=0)
    pad_rows = (-packed.shape[0]) % 8
    return jnp.pad(packed, ((0, pad_rows), (0, 0)))


def _unpack_small(packed, like):
    out, r = [], 0
    for p in like:
        n = p.size
        nr = -(-n // 128)
        out.append(packed[r:r + nr].reshape(-1)[:n].reshape(p.shape))
        r += nr
    return out


BIG = ("w_in", "w_ret_o", "w_pool_o", "w_mem_kv", "w_mem_o", "w_out", "w_ff1", "w_ff2")


def _rope_tables(S):
    inv = ROPE_BASE ** (-jnp.arange(0, QK_DIM, 2, dtype=F32) / QK_DIM)
    ang = jnp.arange(S).astype(F32)[:, None] * inv[None, :]
    cos, sin = jnp.cos(ang), jnp.sin(ang)
    return jnp.concatenate([cos, cos], axis=1), jnp.concatenate([-sin, sin], axis=1)


def _layer_fwd(x, W, small, memn, cos, sin, B, S):
    g1, g2, lg, w_grp, scale = small
    proj, h1 = _rms_proj("in_proj", x, g1, W["w_in"], N_PROJ)
    o, y_r = _retention_fwd(proj, cos, sin, lg, B, S)
    y_p = _pool_fwd(proj, w_grp, scale, B, S)
    kv = _rowsharded_mm("mem_kv", memn, W["w_mem_kv"], None)
    o_m = _mem_attn_fwd(proj, kv, B, S)
    x1, merged, o3 = _merge_fwd(x, proj, y_r, y_p, o_m, W["w_ret_o"], W["w_pool_o"], W["w_mem_o"], W["w_out"])
    u, h2 = _rms_proj("ff1", x1, g2, W["w_ff1"], FFN_HIDDEN)
    x2 = _ffn_out(x1, u, W["w_ff2"])
    saved = dict(x=x, proj=proj, h1=h1, o=o, y_r=y_r, y_p=y_p, kv=kv, o_m=o_m, merged=merged, o3=o3, x1=x1, h2=h2, u=u)
    return x2, saved


def _layer_bwd(dx2, sv, W, small, memn, d_memn, cos, sin, B, S):
    g1, g2, lg, w_grp, scale = small
    x, proj, x1, u = sv["x"], sv["proj"], sv["x1"], sv["u"]
    grads = {}
    du = _ffn_out_dgrad(dx2, u, W["w_ff2"])
    grads["w_ff2"] = _rowsharded_wgrad("ff2_wgrad", u, dx2, FFN_HIDDEN, pro=_relu2)
    dx1, dg2 = _dgrad_rms("ff1_dgrad", du, W["w_ff1"], x1, g2, dx2)
    grads["w_ff1"] = _colsharded_wgrad_full("ff1_wgrad", sv["h2"], du, FFN_HIDDEN, 1024)
    do3, dgates, dy_r, dy_p, do_m = _merge_bwd(dx1, proj, sv["o3"], W["w_ret_o"], W["w_pool_o"], W["w_mem_o"], W["w_out"])
    grads["w_out"], grads["w_ret_o"], grads["w_pool_o"], grads["w_mem_o"] = _mixer_wgrads(
        sv["merged"], sv["y_r"], sv["y_p"], sv["o_m"], dx1, do3)
    d_o, dg_r = _gn_gate_bwd(dy_r, sv["o"], proj)
    dq_r, dk_r, dv_r, dlg = _retention_bwd(proj, d_o, cos, sin, lg, B, S)
    dp, dw_grp, dscale = _pool_bwd(proj, dy_p, w_grp, scale, B, S)
    dq_m, dkv = _mem_attn_bwd(proj, sv["kv"], do_m, B, S)
    grads["w_mem_kv"] = _rowsharded_wgrad("mem_kv_wgrad", memn, dkv, D_MODEL)
    if d_memn is None:
        d_memn = _rowsharded_dgrad("mem_kv_dgrad", dkv, W["w_mem_kv"], out_dtype=F32)
    else:
        d_memn = _rowsharded_dgrad("mem_kv_dgrad_acc", dkv, W["w_mem_kv"], epi=_add_res, epi_in=d_memn, out_dtype=F32)
    dproj = jnp.concatenate([dq_r, dk_r, dv_r, dg_r, dp, dq_m, dgates], axis=1)
    grads["w_in"] = _colsharded_wgrad_full("in_wgrad", sv["h1"], dproj, N_PROJ, 1792)
    dx, dg1 = _dgrad_rms("in_dgrad", dproj, W["w_in"], x, g1, dx1)
    small_grads = dict(g1=dg1, g2=dg2, lg=jnp.sum(dlg, axis=(0, 3)).T, w_grp=dw_grp, scale=dscale)
    return dx, grads, small_grads, d_memn


def kernel(x, mem, w_in, ret_decay_logit, w_ret_o, w_pool_grp, pool_scale, w_pool_o, w_mem_kv, w_mem_o, w_out, w_ff1, w_ff2, norm1_g, norm2_g, mem_norm_g, final_norm_g, loss_target, m_w_in, m_ret_decay_logit, m_w_ret_o, m_w_pool_grp, m_pool_scale, m_w_pool_o, m_w_mem_kv, m_w_mem_o, m_w_out, m_w_ff1, m_w_ff2, m_norm1_g, m_norm2_g, m_mem_norm_g, m_final_norm_g, v_w_in, v_ret_decay_logit, v_w_ret_o, v_w_pool_grp, v_pool_scale, v_w_pool_o, v_w_mem_kv, v_w_mem_o, v_w_out, v_w_ff1, v_w_ff2, v_norm1_g, v_norm2_g, v_mem_norm_g, v_final_norm_g):
    B, S, _ = x.shape
    T = B * S
    big_w = dict(w_in=w_in, w_ret_o=w_ret_o, w_pool_o=w_pool_o, w_mem_kv=w_mem_kv, w_mem_o=w_mem_o, w_out=w_out, w_ff1=w_ff1, w_ff2=w_ff2)
    big_m = dict(w_in=m_w_in, w_ret_o=m_w_ret_o, w_pool_o=m_w_pool_o, w_mem_kv=m_w_mem_kv, w_mem_o=m_w_mem_o, w_out=m_w_out, w_ff1=m_w_ff1, w_ff2=m_w_ff2)
    big_v = dict(w_in=v_w_in, w_ret_o=v_w_ret_o, w_pool_o=v_w_pool_o, w_mem_kv=v_w_mem_kv, w_mem_o=v_w_mem_o, w_out=v_w_out, w_ff1=v_w_ff1, w_ff2=v_w_ff2)
    small_w = [ret_decay_logit, w_pool_grp, pool_scale, norm1_g, norm2_g, mem_norm_g, final_norm_g]
    small_m = [m_ret_decay_logit, m_w_pool_grp, m_pool_scale, m_norm1_g, m_norm2_g, m_mem_norm_g, m_final_norm_g]
    small_v = [v_ret_decay_logit, v_w_pool_grp, v_pool_scale, v_norm1_g, v_norm2_g, v_mem_norm_g, v_final_norm_g]
    my_chip = (2 * lax.axis_index("x") + lax.axis_index("y")).astype(jnp.int32).reshape(1)
    my_core = lax.axis_index("c").astype(jnp.int32).reshape(1)

    weights = [dict(zip(BIG, _allgather_weights([_cast_into_slot(big_w[n], l, my_chip) for n in BIG])))
               for l in range(DEPTH)]

    cos, sin = _rope_tables(S)
    log_g = jax.nn.log_sigmoid(ret_decay_logit.astype(F32))
    mem2d = mem.reshape(B * MEM_LEN, D_MODEL)
    memn = _mem_norm(mem2d, mem_norm_g.reshape(1, D_MODEL))
    smalls = [(norm1_g[l].reshape(1, D_MODEL), norm2_g[l].reshape(1, D_MODEL), log_g[l], w_pool_grp[l],
               pool_scale[l].reshape(1, POOL_W)) for l in range(DEPTH)]

    h = x.reshape(T, D_MODEL)
    saved = []
    for l in range(DEPTH):
        h, sv = _layer_fwd(h, weights[l], smalls[l], memn, cos, sin, B, S)
        saved.append(sv)

    dh, sq, d_final_g = _loss_head(h, final_norm_g.reshape(1, D_MODEL), loss_target.reshape(T, D_MODEL))
    loss = lax.psum(0.5 * jnp.sum(sq) / D_MODEL, ("x", "y", "c"))

    big_grads = [None] * len(BIG)
    small_grads = [None] * DEPTH
    d_memn = None
    for l in reversed(range(DEPTH)):
        dh, grads, small_grads[l], d_memn = _layer_bwd(dh, saved[l], weights[l], smalls[l], memn, d_memn, cos, sin, B, S)
        big_grads = _reduce_scatter_grads([grads[n] for n in BIG], big_grads, l, my_chip, my_core)
    big_grads = dict(zip(BIG, big_grads))
    d_mem_g = _mem_norm_wgrad(mem2d, d_memn)

    d_logit = jnp.stack([sg["lg"] for sg in small_grads]) * jax.nn.sigmoid(-ret_decay_logit.astype(F32))
    small_g_local = [
        d_logit, jnp.stack([sg["w_grp"] for sg in small_grads]),
        jnp.stack([sg["scale"].reshape(POOL_W) for sg in small_grads]),
        jnp.stack([sg["g1"].reshape(D_MODEL) for sg in small_grads]),
        jnp.stack([sg["g2"].reshape(D_MODEL) for sg in small_grads]),
        d_mem_g.reshape(D_MODEL), d_final_g.reshape(D_MODEL),
    ]
    small_g = _allreduce_small(_pack_small(small_g_local))
    s_delta, s_m, s_v = _adamw(_pack_small(small_w), small_g, _pack_small(small_m), _pack_small(small_v))
    small_g, s_delta, s_m, s_v = (_unpack_small(a, small_w) for a in (small_g, s_delta, s_m, s_v))

    big_out = {}
    for n in BIG:
        w = big_w[n]
        g = big_grads[n]
        flat = lambda a: a.reshape(-1, a.shape[-1])
        d, nm, nv = _adamw(flat(w), flat(g), flat(big_m[n]), flat(big_v[n]))
        big_out[n] = (g, d.reshape(w.shape), nm.reshape(w.shape), nv.reshape(w.shape))

    order = ["w_in", "ret_decay_logit", "w_ret_o", "w_pool_grp", "pool_scale", "w_pool_o", "w_mem_kv", "w_mem_o",
             "w_out", "w_ff1", "w_ff2", "norm1_g", "norm2_g", "mem_norm_g", "final_norm_g"]
    small_names = ["ret_decay_logit", "w_pool_grp", "pool_scale", "norm1_g", "norm2_g", "mem_norm_g", "final_norm_g"]
    outs = [[], [], [], []]
    for n in order:
        if n in big_out:
            vals = big_out[n]
        else:
            i = small_names.index(n)
            vals = (small_g[i], s_delta[i], s_m[i], s_v[i])
        for k in range(4):
            outs[k].append(vals[k])
    return (loss, dh.reshape(B, S, D_MODEL), *outs[0], *outs[1], *outs[2], *outs[3])
```

```python
import functools

import jax
import jax.numpy as jnp
from jax import lax
from jax.experimental import pallas as pl
from jax.experimental.pallas import tpu as pltpu

F32 = jnp.float32
_MXU = jnp.bfloat16
_ACT = jnp.bfloat16
_COMM = jnp.bfloat16

D_MODEL = 1024
N_PROJ = 7168
FFN_HIDDEN = 4096
MEM_LEN = 256
N_HEADS = 4
QK_DIM = 128
V_DIM = 256
POOL_W = 512
MEMQ_W = 512
DEPTH = 4
N_CHIPS = 4
EPS = 1e-6
ROPE_BASE = 10000.0

C_Q, C_K, C_V, C_G, C_P, C_QM, C_GATE = 0, 512, 1024, 2048, 3072, 3584, 4096

ADAM_LR = 0.001
ADAM_B1 = 0.9
ADAM_B2 = 0.999
ADAM_EPS = 1e-08
ADAM_WD = 0.01
ADAM_STEP = 10

VMEM_LIMIT_BYTES = 56 * 1024 * 1024
MESH = pl.DeviceIdType.MESH
ANY = pl.BlockSpec(memory_space=pl.ANY)

_DN = {
    "nn": (((1,), (0,)), ((), ())),
    "nt": (((1,), (1,)), ((), ())),
    "tn": (((0,), (0,)), ((), ())),
}


def _call(body, **kw):
    return pl.pallas_call(body, **kw)


def _cp(sem=None):
    return pltpu.CompilerParams(dimension_semantics=sem, vmem_limit_bytes=VMEM_LIMIT_BYTES)


def _dot(a, b, kind="nn"):
    return lax.dot_general(a.astype(_MXU), b.astype(_MXU), _DN[kind], preferred_element_type=F32)


def _sds(shape, dtype):
    return jax.ShapeDtypeStruct(shape, dtype)


def _rms_rows(x, g):
    r = lax.rsqrt(jnp.mean(x * x, axis=-1, keepdims=True) + EPS)
    return x * r * g


def _relu2(u):
    r = jnp.maximum(u.astype(F32), 0.0)
    return r * r


def _add_res(r, e):
    return r + e.astype(F32)


def _relu2_bwd(r, u):
    return r * (2.0 * jnp.maximum(u.astype(F32), 0.0))


def _mm(name, kind, a, b, a_spec, b_spec, out_shape, o_spec, grid, acc_shape, *,
        pro=None, pro_in=(), pro_specs=(), epi=None, epi_in=(), epi_specs=()):
    nk = grid[2]
    npro, nepi = len(pro_in), len(epi_in)

    def body(*refs):
        a_ref, b_ref = refs[0], refs[1]
        pro_refs = refs[2:2 + npro]
        epi_refs = refs[2 + npro:2 + npro + nepi]
        o_ref = refs[2 + npro + nepi]
        av = a_ref[...]
        if pro is not None:
            av = pro(av, *[r[...] for r in pro_refs])
        part = _dot(av, b_ref[...], kind)

        def finish(r):
            if epi is not None:
                r = epi(r, *[e[...] for e in epi_refs])
            o_ref[...] = r.astype(o_ref.dtype)

        if nk == 1:
            finish(part)
        else:
            acc = refs[-1]
            k = pl.program_id(2)

            @pl.when(k == 0)
            def _():
                acc[...] = part

            @pl.when(k > 0)
            def _():
                acc[...] += part

            @pl.when(k == nk - 1)
            def _():
                finish(acc[...])

    scratch = [] if nk == 1 else [pltpu.VMEM(acc_shape, F32)]
    return _call(
        body, out_shape=out_shape, grid=grid,
        in_specs=[a_spec, b_spec, *pro_specs, *epi_specs], out_specs=o_spec,
        scratch_shapes=scratch, name=name,
        compiler_params=_cp(("parallel", "parallel", "arbitrary")),
    )(a, b, *pro_in, *epi_in)


def _row_tile(n, cap):
    t = min(n, cap)
    assert n % t == 0, (n, t)
    return t


def _resident(w):
    nd = w.ndim
    return pl.BlockSpec(w.shape, lambda i: (0,) * nd, pipeline_mode=pl.Buffered(1))


def _rms_proj(name, x, g, w, n_cols):
    T = x.shape[0]
    tm = _row_tile(T, 512)
    ns = n_cols // N_CHIPS

    def body(x_ref, g_ref, w_ref, o_ref, h_ref):
        h = _rms_rows(x_ref[...], g_ref[...]).astype(h_ref.dtype)
        h_ref[...] = h
        for k in range(N_CHIPS):
            o_ref[:, k * ns:(k + 1) * ns] = _dot(h, w_ref[k]).astype(o_ref.dtype)

    row = pl.BlockSpec((tm, D_MODEL), lambda i: (i, 0))
    return _call(
        body, out_shape=(_sds((T, n_cols), _ACT), _sds((T, D_MODEL), _ACT)), grid=(T // tm,),
        in_specs=[row, pl.BlockSpec((1, D_MODEL), lambda i: (0, 0)), _resident(w)],
        out_specs=(pl.BlockSpec((tm, n_cols), lambda i: (i, 0)), row),
        name=name, compiler_params=_cp(("parallel",)),
    )(x, g, w)


def _colsharded_wgrad_full(name, h, dy, n_cols, tn):
    T = h.shape[0]
    tt = _row_tile(T, 1024)
    per = (n_cols // N_CHIPS) // tn
    return _mm(
        name, "tn", h, dy,
        pl.BlockSpec((tt, D_MODEL), lambda i, j, k: (k, 0)),
        pl.BlockSpec((tt, tn), lambda i, j, k: (k, j)),
        _sds((N_CHIPS, D_MODEL, n_cols // N_CHIPS), _COMM),
        pl.BlockSpec((None, D_MODEL, tn), lambda i, j, k: (j // per, 0, j % per)),
        (1, n_cols // tn, T // tt), (D_MODEL, tn),
    )


def _dgrad_rms(name, dy, w, x, g, dres):
    T, kd = dy.shape
    tm = _row_tile(T, 512)
    ns = kd // N_CHIPS

    def body(dy_ref, w_ref, x_ref, g_ref, dres_ref, dx_ref, dg_ref):
        i = pl.program_id(0)
        dh = _dot(dy_ref[:, 0:ns], w_ref[0], "nt")
        for k in range(1, N_CHIPS):
            dh += _dot(dy_ref[:, k * ns:(k + 1) * ns], w_ref[k], "nt")
        x = x_ref[...]
        r = lax.rsqrt(jnp.mean(x * x, axis=-1, keepdims=True) + EPS)
        xh = x * r
        dxh = dh * g_ref[...]
        dx_ref[...] = dres_ref[...] + r * (dxh - xh * jnp.mean(dxh * xh, axis=-1, keepdims=True))
        dgp = jnp.sum(dh * xh, axis=0, keepdims=True)

        @pl.when(i == 0)
        def _():
            dg_ref[...] = dgp

        @pl.when(i > 0)
        def _():
            dg_ref[...] += dgp

    row = pl.BlockSpec((tm, D_MODEL), lambda i: (i, 0))
    vec = pl.BlockSpec((1, D_MODEL), lambda i: (0, 0))
    return _call(
        body,
        out_shape=(_sds((T, D_MODEL), F32), _sds((1, D_MODEL), F32)),
        grid=(T // tm,),
        in_specs=[pl.BlockSpec((tm, kd), lambda i: (i, 0)), _resident(w), row, vec, row],
        out_specs=(row, vec),
        name=name, compiler_params=_cp(("arbitrary",)),
    )(dy, w, x, g, dres)


def _ffn_out(x1, u, w):
    T = u.shape[0]
    tm = _row_tile(T, 512)
    ks = w.shape[1]

    def body(x_ref, u_ref, w_ref, o_ref):
        acc = x_ref[...]
        for k in range(N_CHIPS):
            acc += _dot(_relu2(u_ref[:, k * ks:(k + 1) * ks]), w_ref[k])
        o_ref[...] = acc

    row = pl.BlockSpec((tm, D_MODEL), lambda i: (i, 0))
    return _call(
        body, out_shape=_sds((T, D_MODEL), F32), grid=(T // tm,),
        in_specs=[row, pl.BlockSpec((tm, FFN_HIDDEN), lambda i: (i, 0)), _resident(w)], out_specs=row,
        name="ff2", compiler_params=_cp(("parallel",)),
    )(x1, u, w)


def _ffn_out_dgrad(dx2, u, w):
    T = u.shape[0]
    tm = _row_tile(T, 512)
    ks = w.shape[1]

    def body(d_ref, u_ref, w_ref, o_ref):
        d = d_ref[...].astype(_MXU)
        for k in range(N_CHIPS):
            sl = slice(k * ks, (k + 1) * ks)
            o_ref[:, sl] = _relu2_bwd(_dot(d, w_ref[k], "nt"), u_ref[:, sl]).astype(o_ref.dtype)

    wide = pl.BlockSpec((tm, FFN_HIDDEN), lambda i: (i, 0))
    return _call(
        body, out_shape=_sds((T, FFN_HIDDEN), _ACT), grid=(T // tm,),
        in_specs=[pl.BlockSpec((tm, D_MODEL), lambda i: (i, 0)), wide, _resident(w)], out_specs=wide,
        name="ff2_dgrad", compiler_params=_cp(("parallel",)),
    )(dx2, u, w)


def _rowsharded_mm(name, a, w, res, *, pro=None):
    T, K = a.shape
    ks = K // N_CHIPS
    tm = _row_tile(T, 512)
    epi = dict(epi=_add_res, epi_in=(res,), epi_specs=(pl.BlockSpec((tm, D_MODEL), lambda i, j, k: (i, 0)),)) if res is not None else {}
    return _mm(
        name, "nn", a, w,
        pl.BlockSpec((tm, ks), lambda i, j, k: (i, k)),
        pl.BlockSpec((None, ks, D_MODEL), lambda i, j, k: (k, 0, 0)),
        _sds((T, D_MODEL), F32 if res is not None else _ACT),
        pl.BlockSpec((tm, D_MODEL), lambda i, j, k: (i, 0)),
        (T // tm, 1, N_CHIPS), (tm, D_MODEL), pro=pro, **epi,
    )


def _rowsharded_dgrad(name, dy, w, *, epi=None, epi_in=None, out_dtype=_ACT, res=None):
    T = dy.shape[0]
    ks = w.shape[1]
    tm = _row_tile(T, 512)
    kw = {}
    if epi is not None:
        kw = dict(epi=epi, epi_in=(epi_in,), epi_specs=(pl.BlockSpec((tm, ks), lambda i, j, k: (i, j)),))
    return _mm(
        name, "nt", dy, w,
        pl.BlockSpec((tm, D_MODEL), lambda i, j, k: (i, 0)),
        pl.BlockSpec((None, ks, D_MODEL), lambda i, j, k: (j, 0, 0)),
        _sds((T, ks * N_CHIPS), out_dtype),
        pl.BlockSpec((tm, ks), lambda i, j, k: (i, j)),
        (T // tm, N_CHIPS, 1), None, **kw,
    )


def _rowsharded_wgrad(name, a, dy, K, *, pro=None):
    T = a.shape[0]
    ks = K // N_CHIPS
    tt = _row_tile(T, 1024)
    return _mm(
        name, "tn", a, dy,
        pl.BlockSpec((tt, ks), lambda i, j, k: (k, i)),
        pl.BlockSpec((tt, D_MODEL), lambda i, j, k: (k, 0)),
        _sds((N_CHIPS, ks, D_MODEL), _COMM),
        pl.BlockSpec((None, ks, D_MODEL), lambda i, j, k: (i, 0, 0)),
        (N_CHIPS, 1, T // tt), (ks, D_MODEL), pro=pro,
    )


def _mixer_wgrads(merged, y_r, y_p, o_m, dx1, do3):
    T = merged.shape[0]
    tt = _row_tile(T, 512)
    nt = T // tt
    rs = D_MODEL // N_CHIPS

    def body(mg_ref, yr_ref, yp_ref, om_ref, dx_ref, do3_ref, g_out, g_ret, g_pool, g_mem, a_out, a_ret, a_pool, a_mem):
        t = pl.program_id(0)

        def accumulate(acc, part):
            @pl.when(t == 0)
            def _():
                acc[...] = part

            @pl.when(t > 0)
            def _():
                acc[...] += part

        accumulate(a_out, _dot(mg_ref[...], dx_ref[...], "tn"))
        accumulate(a_ret, _dot(yr_ref[...], do3_ref[:, 0:D_MODEL], "tn"))
        accumulate(a_pool, _dot(yp_ref[...], do3_ref[:, D_MODEL:2 * D_MODEL], "tn"))
        accumulate(a_mem, _dot(om_ref[...], do3_ref[:, 2 * D_MODEL:3 * D_MODEL], "tn"))

        @pl.when(t == nt - 1)
        def _():
            for k in range(N_CHIPS):
                rows = slice(k * rs, (k + 1) * rs)
                g_out[k] = a_out[rows, :].astype(g_out.dtype)
                g_ret[k] = a_ret[rows, :].astype(g_ret.dtype)
                g_pool[k] = a_pool[:, rows].astype(g_pool.dtype)
                g_mem[k] = a_mem[:, rows].astype(g_mem.dtype)

    row = lambda w: pl.BlockSpec((tt, w), lambda t: (t, 0))
    whole = lambda s: pl.BlockSpec(s, lambda t: (0, 0, 0))
    rsh, csh = (N_CHIPS, rs, D_MODEL), (N_CHIPS, POOL_W, rs)
    return _call(
        body,
        out_shape=(_sds(rsh, _COMM), _sds(rsh, _COMM), _sds(csh, _COMM), _sds(csh, _COMM)),
        grid=(nt,),
        in_specs=[row(D_MODEL), row(D_MODEL), row(POOL_W), row(MEMQ_W), row(D_MODEL), row(3 * D_MODEL)],
        out_specs=(whole(rsh), whole(rsh), whole(csh), whole(csh)),
        scratch_shapes=[pltpu.VMEM((D_MODEL, D_MODEL), F32), pltpu.VMEM((D_MODEL, D_MODEL), F32),
                        pltpu.VMEM((POOL_W, D_MODEL), F32), pltpu.VMEM((MEMQ_W, D_MODEL), F32)],
        name="mixer_wgrads", compiler_params=_cp(("arbitrary",)),
    )(merged, y_r, y_p, o_m, dx1, do3)


def _rot(x, cos, sin):
    return x * cos + pltpu.roll(x, QK_DIM // 2, 1) * sin


def _rot_bwd(d, cos, sin):
    return d * cos + pltpu.roll(d * sin, QK_DIM // 2, 1)


def _tile_diff(qi, ki, tq, tk):
    n = qi * tq + lax.broadcasted_iota(jnp.int32, (tq, tk), 0)
    m = ki * tk + lax.broadcasted_iota(jnp.int32, (tq, tk), 1)
    return (n - m).astype(F32)


def _decay(qi, ki, tq, tk, lgf, lgb):
    diff = _tile_diff(qi, ki, tq, tk)
    return diff, jnp.exp(jnp.where(diff >= 0.0, lgf * diff, -(lgb * diff)))


def _head(h, width):
    return slice(h * width, (h + 1) * width)


def _row_index(t):
    return lax.broadcasted_iota(jnp.int32, (t, QK_DIM), 0).astype(F32)


def _q_decay_fwd(t, lgf):
    return jnp.exp(lgf * _row_index(t))


def _q_decay_bwd(t, lgb):
    return jnp.exp(lgb * (float(t) - _row_index(t)))


def _k_decay_fwd(t, lgf, tiles_apart):
    return jnp.exp(lgf * ((tiles_apart * t).astype(F32) - _row_index(t)))


def _k_decay_bwd(t, lgb, tiles_apart):
    return jnp.exp(lgb * (((tiles_apart - 1) * t).astype(F32) + _row_index(t)))


def _group_norm_gate(o, g):
    mu = jnp.mean(o, axis=-1, keepdims=True)
    oc = o - mu
    var = jnp.mean(oc * oc, axis=-1, keepdims=True)
    on = oc * lax.rsqrt(var + EPS)
    return on * (g * jax.nn.sigmoid(g))


def _retention_fwd(proj, cos, sin, lg, B, S):
    T = B * S
    tq = tk = _row_tile(S, 512)
    nq, nk = S // tq, S // tk
    scale = QK_DIM ** -0.5

    def body(lg_ref, q_ref, k_ref, v_ref, g_ref, cq_ref, sq_ref, ck_ref, sk_ref, o_ref, y_ref, q0_scr, qf_scr, qb_scr, acc):
        qi = pl.program_id(1)
        ki = pl.program_id(2)

        @pl.when(ki == 0)
        def _():
            for h in range(N_HEADS):
                sl = _head(h, QK_DIM)
                qr = _rot(q_ref[:, sl].astype(F32), cq_ref[...], sq_ref[...])
                q0_scr[:, sl] = qr.astype(_MXU)
                qf_scr[:, sl] = (qr * _q_decay_fwd(tq, lg_ref[0, h])).astype(_MXU)
                qb_scr[:, sl] = (qr * _q_decay_bwd(tq, lg_ref[1, h])).astype(_MXU)
            acc[...] = jnp.zeros_like(acc)

        def keys(h):
            return _rot(k_ref[:, _head(h, QK_DIM)].astype(F32), ck_ref[...], sk_ref[...]) * scale

        @pl.when(ki < qi)
        def _():
            for h in range(N_HEADS):
                s = _dot(qf_scr[:, _head(h, QK_DIM)], keys(h) * _k_decay_fwd(tk, lg_ref[0, h], qi - ki), "nt")
                acc[:, _head(h, V_DIM)] += _dot(s, v_ref[:, _head(h, V_DIM)])

        @pl.when(ki > qi)
        def _():
            for h in range(N_HEADS):
                s = _dot(qb_scr[:, _head(h, QK_DIM)], keys(h) * _k_decay_bwd(tk, lg_ref[1, h], ki - qi), "nt")
                acc[:, _head(h, V_DIM)] += _dot(s, v_ref[:, _head(h, V_DIM)])

        @pl.when(ki == qi)
        def _():
            for h in range(N_HEADS):
                s = _dot(q0_scr[:, _head(h, QK_DIM)], keys(h), "nt")
                _, dec = _decay(qi, ki, tq, tk, lg_ref[0, h], lg_ref[1, h])
                acc[:, _head(h, V_DIM)] += _dot(s * dec, v_ref[:, _head(h, V_DIM)])

        @pl.when(ki == nk - 1)
        def _():
            for h in range(N_HEADS):
                sl = _head(h, V_DIM)
                o = acc[:, sl]
                o_ref[:, sl] = o
                y_ref[:, sl] = _group_norm_gate(o, g_ref[:, sl].astype(F32)).astype(y_ref.dtype)

    qw, vw = N_HEADS * QK_DIM, N_HEADS * V_DIM
    return _call(
        body,
        out_shape=(_sds((T, vw), F32), _sds((T, vw), _ACT)),
        grid=(B, nq, nk),
        in_specs=[
            pl.BlockSpec(memory_space=pltpu.SMEM),
            pl.BlockSpec((tq, qw), lambda b, qi, ki: (b * nq + qi, C_Q // qw)),
            pl.BlockSpec((tk, qw), lambda b, qi, ki: (b * nk + ki, C_K // qw)),
            pl.BlockSpec((tk, vw), lambda b, qi, ki: (b * nk + ki, C_V // vw)),
            pl.BlockSpec((tq, vw), lambda b, qi, ki: (b * nq + qi, C_G // vw)),
            pl.BlockSpec((tq, QK_DIM), lambda b, qi, ki: (qi, 0)),
            pl.BlockSpec((tq, QK_DIM), lambda b, qi, ki: (qi, 0)),
            pl.BlockSpec((tk, QK_DIM), lambda b, qi, ki: (ki, 0)),
            pl.BlockSpec((tk, QK_DIM), lambda b, qi, ki: (ki, 0)),
        ],
        out_specs=(
            pl.BlockSpec((tq, vw), lambda b, qi, ki: (b * nq + qi, 0)),
            pl.BlockSpec((tq, vw), lambda b, qi, ki: (b * nq + qi, 0)),
        ),
        scratch_shapes=[pltpu.VMEM((tq, qw), _MXU), pltpu.VMEM((tq, qw), _MXU), pltpu.VMEM((tq, qw), _MXU),
                        pltpu.VMEM((tq, vw), F32)],
        name="retention_fwd",
        compiler_params=_cp(("parallel", "parallel", "arbitrary")),
    )(lg, proj, proj, proj, proj, cos, sin, cos, sin)


def _gn_gate_bwd(dy, o, proj, dproj):
    T = dy.shape[0]
    tm = _row_tile(T, 512)

    def body(dy_ref, o_ref, g_ref, dproj_in, do_ref, dg_ref):
        for h in range(N_HEADS):
            sl = slice(h * V_DIM, (h + 1) * V_DIM)
            o = o_ref[:, sl]
            g = g_ref[:, sl].astype(F32)
            d = dy_ref[:, sl].astype(F32)
            mu = jnp.mean(o, axis=-1, keepdims=True)
            oc = o - mu
            rstd = lax.rsqrt(jnp.mean(oc * oc, axis=-1, keepdims=True) + EPS)
            on = oc * rstd
            sg = jax.nn.sigmoid(g)
            don = d * (g * sg)
            dg_ref[:, sl] = (d * on * (sg * (1.0 + g * (1.0 - sg)))).astype(dg_ref.dtype)
            do = rstd * (don - jnp.mean(don, axis=-1, keepdims=True) - on * jnp.mean(don * on, axis=-1, keepdims=True))
            do_ref[:, sl] = do.astype(do_ref.dtype)

    wide = N_HEADS * V_DIM
    return _call(
        body,
        out_shape=(_sds((T, wide), _ACT), _sds(dproj.shape, dproj.dtype)),
        grid=(T // tm,),
        in_specs=[
            pl.BlockSpec((tm, wide), lambda i: (i, 0)),
            pl.BlockSpec((tm, wide), lambda i: (i, 0)),
            pl.BlockSpec((tm, wide), lambda i: (i, C_G // wide)),
            ANY,
        ],
        out_specs=(pl.BlockSpec((tm, wide), lambda i: (i, 0)), pl.BlockSpec((tm, wide), lambda i: (i, C_G // wide))),
        input_output_aliases={3: 1},
        name="gn_gate_bwd", compiler_params=_cp(("parallel",)),
    )(dy, o, proj, dproj)


def _retention_bwd(proj, d_o, dproj, cos, sin, lg, B, S):
    T = B * S
    tq = tk = _row_tile(S, 512)
    nq, nk = S // tq, S // tk
    scale = QK_DIM ** -0.5
    qw, vw = N_HEADS * QK_DIM, N_HEADS * V_DIM

    def body(lg_ref, q_ref, k_ref, v_ref, do_ref, cq_ref, sq_ref, ck_ref, sk_ref, dproj_in,
             dqkv_ref, dlg_ref, kr_scr, dq_acc, dk_acc, dv_acc, gl_acc):
        ki = pl.program_id(1)
        qi = pl.program_id(2)
        q_rows = pl.ds(pl.multiple_of(qi * tq, tq), tq)
        k_rows = pl.ds(pl.multiple_of(ki * tk, tk), tk)

        @pl.when(qi == 0)
        def _():
            for h in range(N_HEADS):
                sl = _head(h, QK_DIM)
                kr_scr[:, sl] = _rot(k_ref[:, sl].astype(F32), ck_ref[...], sk_ref[...]) * scale
            dk_acc[...] = jnp.zeros_like(dk_acc)
            dv_acc[...] = jnp.zeros_like(dv_acc)

        @pl.when(jnp.logical_and(qi == 0, ki == 0))
        def _():
            gl_acc[...] = jnp.zeros_like(gl_acc)

        @pl.when(ki == 0)
        def _():
            dq_acc[q_rows, :] = jnp.zeros((tq, qw), F32)

        def queries(h):
            return _rot(q_ref[:, _head(h, QK_DIM)].astype(F32), cq_ref[...], sq_ref[...])

        def one_sided(h, q_factor, k_factor, side, sign):
            sl, vsl = _head(h, QK_DIM), _head(h, V_DIM)
            qt = (queries(h) * q_factor).astype(_MXU)
            kt = (kr_scr[:, sl] * k_factor).astype(_MXU)
            d_out = do_ref[:, vsl]
            p = _dot(qt, kt, "nt")
            dv_acc[:, vsl] += _dot(p, d_out, "tn")
            dp = _dot(d_out, v_ref[:, vsl], "nt")
            dq_acc[q_rows, sl] += _dot(dp, kt) * q_factor
            dk_acc[:, sl] += _dot(dp, qt, "tn") * k_factor
            diff = _tile_diff(qi, ki, tq, tk)
            row = 2 * h + side
            gl_acc[row:row + 1, :] += sign * jnp.sum(dp * p * diff, axis=0, keepdims=True)

        @pl.when(ki < qi)
        def _():
            for h in range(N_HEADS):
                one_sided(h, _q_decay_fwd(tq, lg_ref[0, h]), _k_decay_fwd(tk, lg_ref[0, h], qi - ki), 0, 1.0)

        @pl.when(ki > qi)
        def _():
            for h in range(N_HEADS):
                one_sided(h, _q_decay_bwd(tq, lg_ref[1, h]), _k_decay_bwd(tk, lg_ref[1, h], ki - qi), 1, -1.0)

        @pl.when(ki == qi)
        def _():
            for h in range(N_HEADS):
                sl, vsl = _head(h, QK_DIM), _head(h, V_DIM)
                qr = queries(h).astype(_MXU)
                kr = kr_scr[:, sl].astype(_MXU)
                d_out = do_ref[:, vsl]
                s = _dot(qr, kr, "nt")
                diff, dec = _decay(qi, ki, tq, tk, lg_ref[0, h], lg_ref[1, h])
                p = s * dec
                dv_acc[:, vsl] += _dot(p, d_out, "tn")
                dp = _dot(d_out, v_ref[:, vsl], "nt")
                ds = dp * dec
                dq_acc[q_rows, sl] += _dot(ds, kr)
                dk_acc[:, sl] += _dot(ds, qr, "tn")
                gd = dp * p * diff
                gl_acc[2 * h:2 * h + 1, :] += jnp.sum(jnp.where(diff >= 0.0, gd, 0.0), axis=0, keepdims=True)
                gl_acc[2 * h + 1:2 * h + 2, :] += jnp.sum(jnp.where(diff < 0.0, -gd, 0.0), axis=0, keepdims=True)

        @pl.when(qi == nq - 1)
        def _():
            for h in range(N_HEADS):
                sl = _head(h, QK_DIM)
                dk = _rot_bwd(dk_acc[:, sl] * scale, ck_ref[...], sk_ref[...])
                dqkv_ref[k_rows, qw + h * QK_DIM:qw + (h + 1) * QK_DIM] = dk.astype(dqkv_ref.dtype)
            dqkv_ref[k_rows, 2 * qw:2 * qw + vw] = dv_acc[...].astype(dqkv_ref.dtype)

        @pl.when(ki == nk - 1)
        def _():
            for h in range(N_HEADS):
                sl = _head(h, QK_DIM)
                dqkv_ref[q_rows, sl] = _rot_bwd(dq_acc[q_rows, sl], cq_ref[...], sq_ref[...]).astype(dqkv_ref.dtype)

        @pl.when(jnp.logical_and(qi == nq - 1, ki == nk - 1))
        def _():
            dlg_ref[...] = gl_acc[...]

    return _call(
        body,
        out_shape=(_sds(dproj.shape, dproj.dtype), _sds((B, 2 * N_HEADS, tk), F32)),
        grid=(B, nk, nq),
        in_specs=[
            pl.BlockSpec(memory_space=pltpu.SMEM),
            pl.BlockSpec((tq, qw), lambda b, ki, qi: (b * nq + qi, C_Q // qw)),
            pl.BlockSpec((tk, qw), lambda b, ki, qi: (b * nk + ki, C_K // qw)),
            pl.BlockSpec((tk, vw), lambda b, ki, qi: (b * nk + ki, C_V // vw)),
            pl.BlockSpec((tq, vw), lambda b, ki, qi: (b * nq + qi, 0)),
            pl.BlockSpec((tq, QK_DIM), lambda b, ki, qi: (qi, 0)),
            pl.BlockSpec((tq, QK_DIM), lambda b, ki, qi: (qi, 0)),
            pl.BlockSpec((tk, QK_DIM), lambda b, ki, qi: (ki, 0)),
            pl.BlockSpec((tk, QK_DIM), lambda b, ki, qi: (ki, 0)),
            ANY,
        ],
        out_specs=(
            pl.BlockSpec((S, 2 * qw + vw), lambda b, ki, qi: (b, 0)),
            pl.BlockSpec((None, 2 * N_HEADS, tk), lambda b, ki, qi: (b, 0, 0)),
        ),
        scratch_shapes=[
            pltpu.VMEM((tk, qw), F32), pltpu.VMEM((S, qw), F32),
            pltpu.VMEM((tk, qw), F32), pltpu.VMEM((tk, vw), F32), pltpu.VMEM((2 * N_HEADS, tk), F32),
        ],
        input_output_aliases={9: 0},
        name="retention_bwd",
        compiler_params=_cp(("parallel", "arbitrary", "arbitrary")),
    )(lg, proj, proj, proj, d_o, cos, sin, cos, sin, dproj)


POOL_PAD = 16


def _pad_rows(v):
    z = jnp.zeros((POOL_PAD, v.shape[1]), F32)
    return jnp.concatenate([z, v, z], axis=0)


def _window_sums(first, length, levels):
    s = first
    step = 1
    for _ in range(levels - 1):
        s = pltpu.roll(s, step, 0) + pltpu.roll(s, length - step, 0)
        step *= 2
    return s


def _pool_counts(S, hw):
    n = lax.broadcasted_iota(jnp.int32, (S, 1), 0)
    return (jnp.minimum(n + hw, S) - jnp.maximum(n - hw, 0)).astype(F32)


def _pool_mixed(pf, S, g):
    length = S + 2 * POOL_PAD
    xp = _pad_rows(pf)
    s = _window_sums(xp + pltpu.roll(xp, 1, 0), length, g + 1)[POOL_PAD:POOL_PAD + S]
    return s / _pool_counts(S, 1 << g) - pf


def _pool_mixed_bwd(dmixed, S, g):
    length = S + 2 * POOL_PAD
    ep = _pad_rows(dmixed / _pool_counts(S, 1 << g))
    t = _window_sums(ep + pltpu.roll(ep, length - 1, 0), length, g + 1)[POOL_PAD:POOL_PAD + S]
    return t - dmixed


def _pool_fwd(proj, w_grp, scale, B, S):
    T = B * S
    G = POOL_W // 4

    def body(p_ref, wg_ref, sc_ref, y_ref):
        for g in range(4):
            sl = slice(g * G, (g + 1) * G)
            mixed = _pool_mixed(p_ref[:, sl].astype(F32), S, g)
            y_ref[:, sl] = (_dot(mixed, wg_ref[g]) * sc_ref[:, sl]).astype(y_ref.dtype)

    return _call(
        body, out_shape=_sds((T, POOL_W), _ACT), grid=(B,),
        in_specs=[
            pl.BlockSpec((S, POOL_W), lambda b: (b, C_P // POOL_W)),
            pl.BlockSpec((4, G, G), lambda b: (0, 0, 0)),
            pl.BlockSpec((1, POOL_W), lambda b: (0, 0)),
        ],
        out_specs=pl.BlockSpec((S, POOL_W), lambda b: (b, 0)),
        name="pool_fwd", compiler_params=_cp(("parallel",)),
    )(proj, w_grp, scale)


def _pool_bwd(proj, dy, dproj, w_grp, scale, B, S):
    G = POOL_W // 4

    def body(p_ref, dy_ref, wg_ref, sc_ref, dproj_in, dp_ref, dwg_ref, dsc_ref):
        b = pl.program_id(0)
        for g in range(4):
            sl = slice(g * G, (g + 1) * G)
            mixed = _pool_mixed(p_ref[:, sl].astype(F32), S, g)
            z = _dot(mixed, wg_ref[g])
            d = dy_ref[:, sl].astype(F32)
            dsc = jnp.sum(d * z, axis=0, keepdims=True)
            dz = d * sc_ref[:, sl]
            dwg = _dot(mixed, dz, "tn")
            dmixed = _dot(dz, wg_ref[g], "nt")
            dp_ref[:, sl] = _pool_mixed_bwd(dmixed, S, g).astype(dp_ref.dtype)

            @pl.when(b == 0)
            def _():
                dwg_ref[g] = dwg
                dsc_ref[:, sl] = dsc

            @pl.when(b > 0)
            def _():
                dwg_ref[g] += dwg
                dsc_ref[:, sl] += dsc

    return _call(
        body,
        out_shape=(_sds(dproj.shape, dproj.dtype), _sds((4, G, G), F32), _sds((1, POOL_W), F32)),
        grid=(B,),
        in_specs=[
            pl.BlockSpec((S, POOL_W), lambda b: (b, C_P // POOL_W)),
            pl.BlockSpec((S, POOL_W), lambda b: (b, 0)),
            pl.BlockSpec((4, G, G), lambda b: (0, 0, 0)),
            pl.BlockSpec((1, POOL_W), lambda b: (0, 0)),
            ANY,
        ],
        out_specs=(
            pl.BlockSpec((S, POOL_W), lambda b: (b, C_P // POOL_W)),
            pl.BlockSpec((4, G, G), lambda b: (0, 0, 0)),
            pl.BlockSpec((1, POOL_W), lambda b: (0, 0)),
        ),
        input_output_aliases={4: 0},
        name="pool_bwd", compiler_params=_cp(("arbitrary",)),
    )(proj, dy, w_grp, scale, dproj)


def _mem_softmax(q, k):
    s = _dot(q, k, "nt") * (QK_DIM ** -0.5)
    e = jnp.exp(s - jnp.max(s, axis=-1, keepdims=True))
    return e / jnp.sum(e, axis=-1, keepdims=True)


def _mem_attn_fwd(proj, kv, B, S):
    T = B * S
    tq = _row_tile(S, 512)
    nq = S // tq

    def body(q_ref, kv_ref, o_ref):
        for h in range(N_HEADS):
            sl = slice(h * QK_DIM, (h + 1) * QK_DIM)
            a = _mem_softmax(q_ref[:, sl], kv_ref[:, sl])
            o_ref[:, sl] = _dot(a, kv_ref[:, MEMQ_W + h * QK_DIM:MEMQ_W + (h + 1) * QK_DIM]).astype(o_ref.dtype)

    return _call(
        body, out_shape=_sds((T, MEMQ_W), _ACT), grid=(B, nq),
        in_specs=[
            pl.BlockSpec((tq, MEMQ_W), lambda b, i: (b * nq + i, C_QM // MEMQ_W)),
            pl.BlockSpec((MEM_LEN, 2 * MEMQ_W), lambda b, i: (b, 0)),
        ],
        out_specs=pl.BlockSpec((tq, MEMQ_W), lambda b, i: (b * nq + i, 0)),
        name="mem_attn_fwd", compiler_params=_cp(("parallel", "parallel")),
    )(proj, kv)


def _mem_attn_bwd(proj, kv, d_o, dproj, B, S):
    tq = _row_tile(S, 512)
    nq = S // tq
    scale = QK_DIM ** -0.5

    def body(q_ref, kv_ref, do_ref, dproj_in, dq_ref, dkv_ref):
        i = pl.program_id(1)
        for h in range(N_HEADS):
            sl = slice(h * QK_DIM, (h + 1) * QK_DIM)
            vsl = slice(MEMQ_W + h * QK_DIM, MEMQ_W + (h + 1) * QK_DIM)
            q = q_ref[:, sl]
            a = _mem_softmax(q, kv_ref[:, sl])
            d = do_ref[:, sl]
            da = _dot(d, kv_ref[:, vsl], "nt")
            ds = a * (da - jnp.sum(a * da, axis=-1, keepdims=True)) * scale
            dq_ref[:, sl] = _dot(ds, kv_ref[:, sl]).astype(dq_ref.dtype)
            dk = _dot(ds, q, "tn")
            dv = _dot(a, d, "tn")

            @pl.when(i == 0)
            def _():
                dkv_ref[:, sl] = dk
                dkv_ref[:, vsl] = dv

            @pl.when(i > 0)
            def _():
                dkv_ref[:, sl] += dk
                dkv_ref[:, vsl] += dv

    return _call(
        body,
        out_shape=(_sds(dproj.shape, dproj.dtype), _sds((B * MEM_LEN, 2 * MEMQ_W), F32)),
        grid=(B, nq),
        in_specs=[
            pl.BlockSpec((tq, MEMQ_W), lambda b, i: (b * nq + i, C_QM // MEMQ_W)),
            pl.BlockSpec((MEM_LEN, 2 * MEMQ_W), lambda b, i: (b, 0)),
            pl.BlockSpec((tq, MEMQ_W), lambda b, i: (b * nq + i, 0)),
            ANY,
        ],
        out_specs=(
            pl.BlockSpec((tq, MEMQ_W), lambda b, i: (b * nq + i, C_QM // MEMQ_W)),
            pl.BlockSpec((MEM_LEN, 2 * MEMQ_W), lambda b, i: (b, 0)),
        ),
        input_output_aliases={3: 0},
        name="mem_attn_bwd", compiler_params=_cp(("parallel", "arbitrary")),
    )(proj, kv, d_o, dproj)


def _mem_norm(mem2d, g):
    M = mem2d.shape[0]
    tm = _row_tile(M, 512)

    def body(x_ref, g_ref, o_ref):
        o_ref[...] = _rms_rows(x_ref[...], g_ref[...]).astype(o_ref.dtype)

    return _call(
        body, out_shape=_sds((M, D_MODEL), _ACT), grid=(M // tm,),
        in_specs=[pl.BlockSpec((tm, D_MODEL), lambda i: (i, 0)), pl.BlockSpec((1, D_MODEL), lambda i: (0, 0))],
        out_specs=pl.BlockSpec((tm, D_MODEL), lambda i: (i, 0)),
        name="mem_norm", compiler_params=_cp(("parallel",)),
    )(mem2d, g)


def _mem_norm_wgrad(mem2d, d_memn):
    M = mem2d.shape[0]
    tm = _row_tile(M, 512)

    def body(x_ref, d_ref, dg_ref):
        i = pl.program_id(0)
        x = x_ref[...]
        xh = x * lax.rsqrt(jnp.mean(x * x, axis=-1, keepdims=True) + EPS)
        dg = jnp.sum(d_ref[...] * xh, axis=0, keepdims=True)

        @pl.when(i == 0)
        def _():
            dg_ref[...] = dg

        @pl.when(i > 0)
        def _():
            dg_ref[...] += dg

    return _call(
        body, out_shape=_sds((1, D_MODEL), F32), grid=(M // tm,),
        in_specs=[pl.BlockSpec((tm, D_MODEL), lambda i: (i, 0)), pl.BlockSpec((tm, D_MODEL), lambda i: (i, 0))],
        out_specs=pl.BlockSpec((1, D_MODEL), lambda i: (0, 0)),
        name="mem_norm_wgrad", compiler_params=_cp(("arbitrary",)),
    )(mem2d, d_memn)


def _row_mm(a, w_ref):
    ks = w_ref.shape[1]
    out = _dot(a[:, 0:ks], w_ref[0])
    for k in range(1, N_CHIPS):
        out += _dot(a[:, k * ks:(k + 1) * ks], w_ref[k])
    return out


def _row_mm_t(d, w_ref):
    return jnp.concatenate([_dot(d, w_ref[k], "nt") for k in range(N_CHIPS)], axis=1)


def _col_mm(a, w_ref):
    return jnp.concatenate([_dot(a, w_ref[k]) for k in range(N_CHIPS)], axis=1)


def _col_mm_t(d, w_ref):
    ns = w_ref.shape[2]
    out = _dot(d[:, 0:ns], w_ref[0], "nt")
    for k in range(1, N_CHIPS):
        out += _dot(d[:, k * ns:(k + 1) * ns], w_ref[k], "nt")
    return out


def _full_spec(w):
    nd = w.ndim
    return pl.BlockSpec(w.shape, lambda i: (0,) * nd)


def _merge_fwd(x, proj, y_r, y_p, o_m, w_ret_o, w_pool_o, w_mem_o, w_out):
    T = x.shape[0]
    tm = _row_tile(T, 256)

    def body(x_ref, gr_ref, gp_ref, gm_ref, yr_ref, yp_ref, om_ref, wr_ref, wp_ref, wm_ref, wo_ref,
             x1_ref, mg_ref, o3_ref):
        o_r = _row_mm(yr_ref[...], wr_ref)
        o_p = _col_mm(yp_ref[...], wp_ref)
        o_q = _col_mm(om_ref[...], wm_ref)
        merged = (jax.nn.sigmoid(gr_ref[...].astype(F32)) * o_r + jax.nn.sigmoid(gp_ref[...].astype(F32)) * o_p
                  + jax.nn.sigmoid(gm_ref[...].astype(F32)) * o_q)
        mg = merged.astype(mg_ref.dtype)
        mg_ref[...] = mg
        o3_ref[:, 0:D_MODEL] = o_r.astype(o3_ref.dtype)
        o3_ref[:, D_MODEL:2 * D_MODEL] = o_p.astype(o3_ref.dtype)
        o3_ref[:, 2 * D_MODEL:3 * D_MODEL] = o_q.astype(o3_ref.dtype)
        x1_ref[...] = x_ref[...] + _row_mm(mg, wo_ref)

    gb = C_GATE // D_MODEL
    row = lambda w: pl.BlockSpec((tm, w), lambda i: (i, 0))
    return _call(
        body,
        out_shape=(_sds((T, D_MODEL), F32), _sds((T, D_MODEL), _ACT), _sds((T, 3 * D_MODEL), _ACT)),
        grid=(T // tm,),
        in_specs=[
            row(D_MODEL),
            pl.BlockSpec((tm, D_MODEL), lambda i: (i, gb)),
            pl.BlockSpec((tm, D_MODEL), lambda i: (i, gb + 1)),
            pl.BlockSpec((tm, D_MODEL), lambda i: (i, gb + 2)),
            row(D_MODEL), row(POOL_W), row(MEMQ_W),
            _full_spec(w_ret_o), _full_spec(w_pool_o), _full_spec(w_mem_o), _full_spec(w_out),
        ],
        out_specs=(row(D_MODEL), row(D_MODEL), row(3 * D_MODEL)),
        name="merge_fwd", compiler_params=_cp(("parallel",)),
    )(x, proj, proj, proj, y_r, y_p, o_m, w_ret_o, w_pool_o, w_mem_o, w_out)


def _merge_bwd(dx1, proj, o3, w_ret_o, w_pool_o, w_mem_o, w_out):
    T = dx1.shape[0]
    tm = _row_tile(T, 256)

    def body(dx_ref, gr_ref, gp_ref, gm_ref, o3_ref, wr_ref, wp_ref, wm_ref, wo_ref,
             do3_ref, dgate_ref, dyr_ref, dyp_ref, dom_ref):
        dmerged = _row_mm_t(dx_ref[...], wo_ref)
        douts = []
        dgate_ref[:, 0:MEMQ_W] = jnp.zeros((tm, MEMQ_W), dgate_ref.dtype)
        for n, g_ref in enumerate((gr_ref, gp_ref, gm_ref)):
            sl = slice(n * D_MODEL, (n + 1) * D_MODEL)
            gate = jax.nn.sigmoid(g_ref[...].astype(F32))
            d_out = (dmerged * gate).astype(do3_ref.dtype)
            do3_ref[:, sl] = d_out
            dgate_ref[:, MEMQ_W + n * D_MODEL:MEMQ_W + (n + 1) * D_MODEL] = (
                dmerged * o3_ref[:, sl].astype(F32) * gate * (1.0 - gate)).astype(dgate_ref.dtype)
            douts.append(d_out)
        dyr_ref[...] = _row_mm_t(douts[0], wr_ref).astype(dyr_ref.dtype)
        dyp_ref[...] = _col_mm_t(douts[1], wp_ref).astype(dyp_ref.dtype)
        dom_ref[...] = _col_mm_t(douts[2], wm_ref).astype(dom_ref.dtype)

    gb = C_GATE // D_MODEL
    row = lambda w: pl.BlockSpec((tm, w), lambda i: (i, 0))
    return _call(
        body,
        out_shape=(
            _sds((T, 3 * D_MODEL), _ACT), _sds((T, N_PROJ), _ACT),
            _sds((T, D_MODEL), _ACT), _sds((T, POOL_W), _ACT), _sds((T, MEMQ_W), _ACT),
        ),
        grid=(T // tm,),
        in_specs=[
            row(D_MODEL),
            pl.BlockSpec((tm, D_MODEL), lambda i: (i, gb)),
            pl.BlockSpec((tm, D_MODEL), lambda i: (i, gb + 1)),
            pl.BlockSpec((tm, D_MODEL), lambda i: (i, gb + 2)),
            row(3 * D_MODEL),
            _full_spec(w_ret_o), _full_spec(w_pool_o), _full_spec(w_mem_o), _full_spec(w_out),
        ],
        out_specs=(row(3 * D_MODEL), pl.BlockSpec((tm, N_PROJ - C_QM), lambda i: (i, 1)),
                   row(D_MODEL), row(POOL_W), row(MEMQ_W)),
        name="merge_bwd", compiler_params=_cp(("parallel",)),
    )(dx1, proj, proj, proj, o3, w_ret_o, w_pool_o, w_mem_o, w_out)


def _loss_head(x, g, target):
    T = x.shape[0]
    tm = _row_tile(T, 512)

    def body(x_ref, g_ref, t_ref, dx_ref, sq_ref, dg_ref):
        i = pl.program_id(0)
        x = x_ref[...]
        gg = g_ref[...]
        r = lax.rsqrt(jnp.mean(x * x, axis=-1, keepdims=True) + EPS)
        xh = x * r
        err = xh * gg - t_ref[...]
        dy = err * (1.0 / D_MODEL)
        dxh = dy * gg
        dx_ref[...] = r * (dxh - xh * jnp.mean(dxh * xh, axis=-1, keepdims=True))
        sq = jnp.sum(err * err, axis=0, keepdims=True)
        dg = jnp.sum(dy * xh, axis=0, keepdims=True)

        @pl.when(i == 0)
        def _():
            sq_ref[...] = sq
            dg_ref[...] = dg

        @pl.when(i > 0)
        def _():
            sq_ref[...] += sq
            dg_ref[...] += dg

    vec = pl.BlockSpec((1, D_MODEL), lambda i: (0, 0))
    row = pl.BlockSpec((tm, D_MODEL), lambda i: (i, 0))
    return _call(
        body,
        out_shape=(_sds((T, D_MODEL), F32), _sds((1, D_MODEL), F32), _sds((1, D_MODEL), F32)),
        grid=(T // tm,), in_specs=[row, vec, row], out_specs=(row, vec, vec),
        name="loss_head", compiler_params=_cp(("arbitrary",)),
    )(x, g, target)


def _block_rows(rows, cols, itemsize, cap_bytes=2 << 20):
    t = rows
    while t * cols * itemsize > cap_bytes and t % 2 == 0 and (t // 2) % 16 == 0:
        t //= 2
    return t


def _cast_into_slot(w3d, layer, me):
    _, R, C = w3d.shape
    tr = _block_rows(R, C, 4)

    def body(me_ref, w_ref, o_ref):
        o_ref[...] = w_ref[...].astype(o_ref.dtype)

    grid_spec = pltpu.PrefetchScalarGridSpec(
        num_scalar_prefetch=1, grid=(R // tr,),
        in_specs=[pl.BlockSpec((None, tr, C), lambda i, me: (layer, i, 0))],
        out_specs=pl.BlockSpec((None, tr, C), lambda i, me: (me[0], i, 0)),
    )
    return _call(body, out_shape=_sds((N_CHIPS, R, C), _COMM), grid_spec=grid_spec,
                 name="cast_into_slot", compiler_params=_cp(("parallel",)))(me, w3d)


def _adamw(w, g, m, v):
    R, C = w.shape
    tr = _block_rows(R, C, 4, 1 << 20)

    def body(w_ref, g_ref, m_ref, v_ref, d_ref, nm_ref, nv_ref):
        g = g_ref[...]
        m = ADAM_B1 * m_ref[...] + (1.0 - ADAM_B1) * g
        v = ADAM_B2 * v_ref[...] + (1.0 - ADAM_B2) * (g * g)
        m_hat = m / (1.0 - ADAM_B1 ** ADAM_STEP)
        v_hat = v / (1.0 - ADAM_B2 ** ADAM_STEP)
        d_ref[...] = -ADAM_LR * (m_hat / (jnp.sqrt(v_hat) + ADAM_EPS) + ADAM_WD * w_ref[...])
        nm_ref[...] = m
        nv_ref[...] = v

    spec = pl.BlockSpec((tr, C), lambda i: (i, 0))
    out = _sds((R, C), F32)
    return _call(body, out_shape=(out, out, out), grid=(R // tr,), in_specs=[spec] * 4, out_specs=(spec,) * 3,
                 name="adamw", compiler_params=_cp(("parallel",)))(w, g, m, v)


def _add_sibling_half(g_full, land, my_c):
    _, R, C = g_full.shape
    hr = R // 2
    tr = _block_rows(hr, C, 2, 1 << 20)
    nb = hr // tr

    def body(c_ref, g_ref, l_ref, o_ref):
        o_ref[...] = (g_ref[...].astype(F32) + l_ref[...].astype(F32)).astype(o_ref.dtype)

    grid_spec = pltpu.PrefetchScalarGridSpec(
        num_scalar_prefetch=1, grid=(N_CHIPS, nb),
        in_specs=[
            pl.BlockSpec((None, tr, C), lambda k, i, c: (k, c[0] * nb + i, 0)),
            pl.BlockSpec((None, tr, C), lambda k, i, c: (k, i, 0)),
        ],
        out_specs=pl.BlockSpec((None, tr, C), lambda k, i, c: (k, i, 0)),
    )
    return _call(body, out_shape=_sds((N_CHIPS, hr, C), _COMM), grid_spec=grid_spec,
                 name="add_sibling_half", compiler_params=_cp(("parallel", "parallel")))(my_c, g_full, land)


def _add_chips(land, sums, g_all, layer, my_chip, my_core):
    _, hr, C = land.shape
    tr = _block_rows(hr, C, 4, 1 << 20)
    nb = hr // tr

    def body(me_ref, core_ref, own_ref, a_ref, b_ref, c_ref, *rest):
        o_ref = rest[-1]
        o_ref[...] = ((own_ref[...].astype(F32) + a_ref[...].astype(F32)) + b_ref[...].astype(F32)) + c_ref[...].astype(F32)

    def other(d):
        return pl.BlockSpec((None, tr, C), lambda i, me, core: ((me[0] + d) % N_CHIPS, i, 0))

    operands = [my_chip, my_core, sums, land, land, land]
    in_specs = [other(0), other(1), other(2), other(3)]
    aliases = {}
    if g_all is not None:
        operands.append(g_all)
        in_specs.append(ANY)
        aliases = {len(operands) - 1: 0}
    grid_spec = pltpu.PrefetchScalarGridSpec(
        num_scalar_prefetch=2, grid=(nb,), in_specs=in_specs,
        out_specs=pl.BlockSpec((None, tr, C), lambda i, me, core: (layer, core[0] * nb + i, 0)),
    )
    return _call(body, out_shape=_sds((DEPTH, 2 * hr, C), F32), grid_spec=grid_spec, input_output_aliases=aliases,
                 name="add_chips", compiler_params=_cp(("parallel",)))(*operands)


def _place():
    x, y, c = lax.axis_index("x"), lax.axis_index("y"), lax.axis_index("c")
    chips = [(1 - x, y), (x, 1 - y), (1 - x, 1 - y)]
    return x, y, c, 2 * x + y, chips


def _remote(src, dst, send_sem, recv_sem, dev):
    return pltpu.make_async_remote_copy(src_ref=src, dst_ref=dst, send_sem=send_sem, recv_sem=recv_sem,
                                        device_id=dev, device_id_type=MESH)


def _allgather_weights(bufs):
    n = len(bufs)

    def body(*refs):
        outs = refs[n:2 * n]
        send_sems, recv_sems = refs[2 * n:]
        x, y, c, me, chips = _place()
        sibling = (x, y, 1 - c)
        waits = []
        for w in range(n):
            hr = outs[w].shape[1] // 2
            own = outs[w].at[me, pl.ds(c * hr, hr)]
            for j, chip in enumerate(chips):
                cp = _remote(own, own, send_sems.at[w, j], recv_sems.at[w, j], (*chip, c))
                cp.start()
                waits.append(cp.wait_send)
        for w in range(n):
            hr = outs[w].shape[1] // 2
            mine = pl.ds(c * hr, hr)
            for j, chip in enumerate(chips):
                kc = 2 * chip[0] + chip[1]
                got = outs[w].at[kc, mine]
                _remote(got, got, send_sems.at[w, j], recv_sems.at[w, j], (*chip, c)).wait_recv()
                fwd = _remote(got, got, send_sems.at[w, 3 + j], recv_sems.at[w, 3 + j], sibling)
                fwd.start()
                waits.append(fwd.wait_send)
        for w in range(n):
            hr = outs[w].shape[1] // 2
            theirs = pl.ds((1 - c) * hr, hr)
            for j, chip in enumerate(chips):
                kc = 2 * chip[0] + chip[1]
                got = outs[w].at[kc, theirs]
                _remote(got, got, send_sems.at[w, 3 + j], recv_sems.at[w, 3 + j], sibling).wait_recv()
        for wait in waits:
            wait()

    out_shape = tuple(_sds(b.shape, b.dtype) for b in bufs)
    return _call(
        body, out_shape=out_shape, in_specs=[ANY] * n, out_specs=(ANY,) * n,
        input_output_aliases={i: i for i in range(n)},
        scratch_shapes=[pltpu.SemaphoreType.DMA((n, 6)), pltpu.SemaphoreType.DMA((n, 6))],
        name="allgather_weights",
    )(*bufs)


def _swap_sibling_halves(grads):
    n = len(grads)

    def body(*refs):
        ins, outs = refs[:n], refs[n:2 * n]
        send_sems, recv_sems = refs[2 * n:]
        x, y, c, _, _ = _place()
        copies = []
        for w in range(n):
            hr = ins[w].shape[1] // 2
            cp = _remote(ins[w].at[:, pl.ds((1 - c) * hr, hr)], outs[w], send_sems.at[w], recv_sems.at[w], (x, y, 1 - c))
            cp.start()
            copies.append(cp)
        for cp in copies:
            cp.wait()

    out_shape = tuple(_sds((N_CHIPS, g.shape[1] // 2, g.shape[2]), g.dtype) for g in grads)
    return _call(
        body, out_shape=out_shape, in_specs=[ANY] * n, out_specs=(ANY,) * n,
        scratch_shapes=[pltpu.SemaphoreType.DMA((n,)), pltpu.SemaphoreType.DMA((n,))],
        name="swap_sibling_halves",
    )(*grads)


def _scatter_chip_sums(sums):
    n = len(sums)

    def body(*refs):
        ins, outs = refs[:n], refs[n:2 * n]
        send_sems, recv_sems = refs[2 * n:]
        x, y, c, me, chips = _place()
        waits = []
        for w in range(n):
            for j, chip in enumerate(chips):
                kc = 2 * chip[0] + chip[1]
                cp = _remote(ins[w].at[kc], outs[w].at[me], send_sems.at[w, j], recv_sems.at[w, j], (*chip, c))
                cp.start()
                waits.append(cp.wait_send)
        for w in range(n):
            for j, chip in enumerate(chips):
                kc = 2 * chip[0] + chip[1]
                got = outs[w].at[kc]
                _remote(got, got, send_sems.at[w, j], recv_sems.at[w, j], (*chip, c)).wait_recv()
        for wait in waits:
            wait()

    out_shape = tuple(_sds(s.shape, s.dtype) for s in sums)
    return _call(
        body, out_shape=out_shape, in_specs=[ANY] * n, out_specs=(ANY,) * n,
        scratch_shapes=[pltpu.SemaphoreType.DMA((n, 3)), pltpu.SemaphoreType.DMA((n, 3))],
        name="scatter_chip_sums",
    )(*sums)


def _join_sibling_halves(g_alls, layer):
    n = len(g_alls)

    def body(*refs):
        outs = refs[n:2 * n]
        send_sems, recv_sems = refs[2 * n:]
        x, y, c, _, _ = _place()
        copies = []
        for w in range(n):
            hr = outs[w].shape[1] // 2
            mine = outs[w].at[layer, pl.ds(c * hr, hr)]
            cp = _remote(mine, mine, send_sems.at[w], recv_sems.at[w], (x, y, 1 - c))
            cp.start()
            copies.append(cp)
        for cp in copies:
            cp.wait()

    out_shape = tuple(_sds(g.shape, g.dtype) for g in g_alls)
    return _call(
        body, out_shape=out_shape, in_specs=[ANY] * n, out_specs=(ANY,) * n,
        input_output_aliases={i: i for i in range(n)},
        scratch_shapes=[pltpu.SemaphoreType.DMA((n,)), pltpu.SemaphoreType.DMA((n,))],
        name="join_sibling_halves",
    )(*g_alls)


def _reduce_scatter_grads(grads, g_alls, layer, my_chip, my_core):
    land = _swap_sibling_halves(grads)
    sums = [_add_sibling_half(g, l, my_core) for g, l in zip(grads, land)]
    parts = _scatter_chip_sums(sums)
    g_alls = [_add_chips(p, s, ga, layer, my_chip, my_core) for p, s, ga in zip(parts, sums, g_alls)]
    return _join_sibling_halves(g_alls, layer)


def _allreduce_small(v):
    R = v.shape[0]

    def body(v_ref, out_ref, sib_ref, chip_ref, sum_ref, send_sems, recv_sems):
        x, y, c, me, chips = _place()
        swap = _remote(v_ref, sib_ref, send_sems.at[0], recv_sems.at[0], (x, y, 1 - c))
        swap.start()
        swap.wait()
        sum_ref[...] = v_ref[...] + sib_ref[...]
        copies = []
        for j, chip in enumerate(chips):
            cp = _remote(sum_ref, chip_ref.at[me], send_sems.at[1 + j], recv_sems.at[1 + j], (*chip, c))
            cp.start()
            copies.append(cp)
        chip_ref[me] = sum_ref[...]
        for cp in copies:
            cp.wait()
        out_ref[...] = ((chip_ref[0] + chip_ref[1]) + chip_ref[2]) + chip_ref[3]

    vm = pl.BlockSpec(memory_space=pltpu.VMEM)
    return _call(
        body, out_shape=_sds((R, 128), F32), in_specs=[vm], out_specs=vm,
        scratch_shapes=[
            pltpu.VMEM((R, 128), F32), pltpu.VMEM((N_CHIPS, R, 128), F32), pltpu.VMEM((R, 128), F32),
            pltpu.SemaphoreType.DMA((4,)), pltpu.SemaphoreType.DMA((4,)),
        ],
        name="allreduce_small", compiler_params=_cp(),
    )(v)


def _pack_small(parts):
    rows = []
    for p in parts:
        flat = p.reshape(-1).astype(F32)
        pad = (-flat.shape[0]) % 128
        rows.append(jnp.pad(flat, (0, pad)).reshape(-1, 128))
    packed = jnp.concatenate(rows, axis=0)
    pad_rows = (-packed.shape[0]) % 8
    return jnp.pad(packed, ((0, pad_rows), (0, 0)))


def _unpack_small(packed, like):
    out, r = [], 0
    for p in like:
        n = p.size
        nr = -(-n // 128)
        out.append(packed[r:r + nr].reshape(-1)[:n].reshape(p.shape))
        r += nr
    return out


BIG = ("w_in", "w_ret_o", "w_pool_o", "w_mem_kv", "w_mem_o", "w_out", "w_ff1", "w_ff2")


def _rope_tables(S):
    inv = ROPE_BASE ** (-jnp.arange(0, QK_DIM, 2, dtype=F32) / QK_DIM)
    ang = jnp.arange(S).astype(F32)[:, None] * inv[None, :]
    cos, sin = jnp.cos(ang), jnp.sin(ang)
    return jnp.concatenate([cos, cos], axis=1), jnp.concatenate([-sin, sin], axis=1)


def _layer_fwd(x, W, small, memn, cos, sin, B, S):
    g1, g2, lg, w_grp, scale = small
    proj, h1 = _rms_proj("in_proj", x, g1, W["w_in"], N_PROJ)
    o, y_r = _retention_fwd(proj, cos, sin, lg, B, S)
    y_p = _pool_fwd(proj, w_grp, scale, B, S)
    kv = _rowsharded_mm("mem_kv", memn, W["w_mem_kv"], None)
    o_m = _mem_attn_fwd(proj, kv, B, S)
    x1, merged, o3 = _merge_fwd(x, proj, y_r, y_p, o_m, W["w_ret_o"], W["w_pool_o"], W["w_mem_o"], W["w_out"])
    u, h2 = _rms_proj("ff1", x1, g2, W["w_ff1"], FFN_HIDDEN)
    x2 = _ffn_out(x1, u, W["w_ff2"])
    saved = dict(x=x, proj=proj, h1=h1, o=o, y_r=y_r, y_p=y_p, kv=kv, o_m=o_m, merged=merged, o3=o3, x1=x1, h2=h2, u=u)
    return x2, saved


def _layer_bwd(dx2, sv, W, small, memn, d_memn, cos, sin, B, S):
    g1, g2, lg, w_grp, scale = small
    x, proj, x1, u = sv["x"], sv["proj"], sv["x1"], sv["u"]
    grads = {}
    du = _ffn_out_dgrad(dx2, u, W["w_ff2"])
    grads["w_ff2"] = _rowsharded_wgrad("ff2_wgrad", u, dx2, FFN_HIDDEN, pro=_relu2)
    dx1, dg2 = _dgrad_rms("ff1_dgrad", du, W["w_ff1"], x1, g2, dx2)
    grads["w_ff1"] = _colsharded_wgrad_full("ff1_wgrad", sv["h2"], du, FFN_HIDDEN, 1024)
    do3, dproj, dy_r, dy_p, do_m = _merge_bwd(dx1, proj, sv["o3"], W["w_ret_o"], W["w_pool_o"], W["w_mem_o"], W["w_out"])
    grads["w_out"], grads["w_ret_o"], grads["w_pool_o"], grads["w_mem_o"] = _mixer_wgrads(
        sv["merged"], sv["y_r"], sv["y_p"], sv["o_m"], dx1, do3)
    d_o, dproj = _gn_gate_bwd(dy_r, sv["o"], proj, dproj)
    dproj, dlg = _retention_bwd(proj, d_o, dproj, cos, sin, lg, B, S)
    dproj, dw_grp, dscale = _pool_bwd(proj, dy_p, dproj, w_grp, scale, B, S)
    dproj, dkv = _mem_attn_bwd(proj, sv["kv"], do_m, dproj, B, S)
    grads["w_mem_kv"] = _rowsharded_wgrad("mem_kv_wgrad", memn, dkv, D_MODEL)
    if d_memn is None:
        d_memn = _rowsharded_dgrad("mem_kv_dgrad", dkv, W["w_mem_kv"], out_dtype=F32)
    else:
        d_memn = _rowsharded_dgrad("mem_kv_dgrad_acc", dkv, W["w_mem_kv"], epi=_add_res, epi_in=d_memn, out_dtype=F32)
    grads["w_in"] = _colsharded_wgrad_full("in_wgrad", sv["h1"], dproj, N_PROJ, 1792)
    dx, dg1 = _dgrad_rms("in_dgrad", dproj, W["w_in"], x, g1, dx1)
    small_grads = dict(g1=dg1, g2=dg2, lg=jnp.sum(dlg, axis=(0, 2)).reshape(N_HEADS, 2).T, w_grp=dw_grp, scale=dscale)
    return dx, grads, small_grads, d_memn


def kernel(x, mem, w_in, ret_decay_logit, w_ret_o, w_pool_grp, pool_scale, w_pool_o, w_mem_kv, w_mem_o, w_out, w_ff1, w_ff2, norm1_g, norm2_g, mem_norm_g, final_norm_g, loss_target, m_w_in, m_ret_decay_logit, m_w_ret_o, m_w_pool_grp, m_pool_scale, m_w_pool_o, m_w_mem_kv, m_w_mem_o, m_w_out, m_w_ff1, m_w_ff2, m_norm1_g, m_norm2_g, m_mem_norm_g, m_final_norm_g, v_w_in, v_ret_decay_logit, v_w_ret_o, v_w_pool_grp, v_pool_scale, v_w_pool_o, v_w_mem_kv, v_w_mem_o, v_w_out, v_w_ff1, v_w_ff2, v_norm1_g, v_norm2_g, v_mem_norm_g, v_final_norm_g):
    B, S, _ = x.shape
    T = B * S
    big_w = dict(w_in=w_in, w_ret_o=w_ret_o, w_pool_o=w_pool_o, w_mem_kv=w_mem_kv, w_mem_o=w_mem_o, w_out=w_out, w_ff1=w_ff1, w_ff2=w_ff2)
    big_m = dict(w_in=m_w_in, w_ret_o=m_w_ret_o, w_pool_o=m_w_pool_o, w_mem_kv=m_w_mem_kv, w_mem_o=m_w_mem_o, w_out=m_w_out, w_ff1=m_w_ff1, w_ff2=m_w_ff2)
    big_v = dict(w_in=v_w_in, w_ret_o=v_w_ret_o, w_pool_o=v_w_pool_o, w_mem_kv=v_w_mem_kv, w_mem_o=v_w_mem_o, w_out=v_w_out, w_ff1=v_w_ff1, w_ff2=v_w_ff2)
    small_w = [ret_decay_logit, w_pool_grp, pool_scale, norm1_g, norm2_g, mem_norm_g, final_norm_g]
    small_m = [m_ret_decay_logit, m_w_pool_grp, m_pool_scale, m_norm1_g, m_norm2_g, m_mem_norm_g, m_final_norm_g]
    small_v = [v_ret_decay_logit, v_w_pool_grp, v_pool_scale, v_norm1_g, v_norm2_g, v_mem_norm_g, v_final_norm_g]
    my_chip = (2 * lax.axis_index("x") + lax.axis_index("y")).astype(jnp.int32).reshape(1)
    my_core = lax.axis_index("c").astype(jnp.int32).reshape(1)

    weights = [dict(zip(BIG, _allgather_weights([_cast_into_slot(big_w[n], l, my_chip) for n in BIG])))
               for l in range(DEPTH)]

    cos, sin = _rope_tables(S)
    log_g = jax.nn.log_sigmoid(ret_decay_logit.astype(F32))
    mem2d = mem.reshape(B * MEM_LEN, D_MODEL)
    memn = _mem_norm(mem2d, mem_norm_g.reshape(1, D_MODEL))
    smalls = [(norm1_g[l].reshape(1, D_MODEL), norm2_g[l].reshape(1, D_MODEL), log_g[l], w_pool_grp[l],
               pool_scale[l].reshape(1, POOL_W)) for l in range(DEPTH)]

    h = x.reshape(T, D_MODEL)
    saved = []
    for l in range(DEPTH):
        h, sv = _layer_fwd(h, weights[l], smalls[l], memn, cos, sin, B, S)
        saved.append(sv)

    dh, sq, d_final_g = _loss_head(h, final_norm_g.reshape(1, D_MODEL), loss_target.reshape(T, D_MODEL))
    loss = lax.psum(0.5 * jnp.sum(sq) / D_MODEL, ("x", "y", "c"))

    big_grads = [None] * len(BIG)
    small_grads = [None] * DEPTH
    d_memn = None
    for l in reversed(range(DEPTH)):
        dh, grads, small_grads[l], d_memn = _layer_bwd(dh, saved[l], weights[l], smalls[l], memn, d_memn, cos, sin, B, S)
        big_grads = _reduce_scatter_grads([grads[n] for n in BIG], big_grads, l, my_chip, my_core)
    big_grads = dict(zip(BIG, big_grads))
    d_mem_g = _mem_norm_wgrad(mem2d, d_memn)

    d_logit = jnp.stack([sg["lg"] for sg in small_grads]) * jax.nn.sigmoid(-ret_decay_logit.astype(F32))
    small_g_local = [
        d_logit, jnp.stack([sg["w_grp"] for sg in small_grads]),
        jnp.stack([sg["scale"].reshape(POOL_W) for sg in small_grads]),
        jnp.stack([sg["g1"].reshape(D_MODEL) for sg in small_grads]),
        jnp.stack([sg["g2"].reshape(D_MODEL) for sg in small_grads]),
        d_mem_g.reshape(D_MODEL), d_final_g.reshape(D_MODEL),
    ]
    small_g = _allreduce_small(_pack_small(small_g_local))
    s_delta, s_m, s_v = _adamw(_pack_small(small_w), small_g, _pack_small(small_m), _pack_small(small_v))
    small_g, s_delta, s_m, s_v = (_unpack_small(a, small_w) for a in (small_g, s_delta, s_m, s_v))

    big_out = {}
    for n in BIG:
        w = big_w[n]
        g = big_grads[n]
        flat = lambda a: a.reshape(-1, a.shape[-1])
        d, nm, nv = _adamw(flat(w), flat(g), flat(big_m[n]), flat(big_v[n]))
        big_out[n] = (g, d.reshape(w.shape), nm.reshape(w.shape), nv.reshape(w.shape))

    order = ["w_in", "ret_decay_logit", "w_ret_o", "w_pool_grp", "pool_scale", "w_pool_o", "w_mem_kv", "w_mem_o",
             "w_out", "w_ff1", "w_ff2", "norm1_g", "norm2_g", "mem_norm_g", "final_norm_g"]
    small_names = ["ret_decay_logit", "w_pool_grp", "pool_scale", "norm1_g", "norm2_g", "mem_norm_g", "final_norm_g"]
    outs = [[], [], [], []]
    for n in order:
        if n in big_out:
            vals = big_out[n]
        else:
            i = small_names.index(n)
            vals = (small_g[i], s_delta[i], s_m[i], s_v[i])
        for k in range(4):
            outs[k].append(vals[k])
    return (loss, dh.reshape(B, S, D_MODEL), *outs[0], *outs[1], *outs[2], *outs[3])
```

```python
import functools

import jax
import jax.numpy as jnp
from jax import lax
from jax.experimental import pallas as pl
from jax.experimental.pallas import tpu as pltpu

F32 = jnp.float32
_MXU = jnp.bfloat16
_ACT = jnp.bfloat16
_COMM = jnp.bfloat16

D_MODEL = 1024
N_PROJ = 7168
FFN_HIDDEN = 4096
MEM_LEN = 256
N_HEADS = 4
QK_DIM = 128
V_DIM = 256
POOL_W = 512
MEMQ_W = 512
DEPTH = 4
N_CHIPS = 4
EPS = 1e-6
ROPE_BASE = 10000.0

C_Q, C_K, C_V, C_G, C_P, C_QM, C_GATE = 0, 512, 1024, 2048, 3072, 3584, 4096

ADAM_LR = 0.001
ADAM_B1 = 0.9
ADAM_B2 = 0.999
ADAM_EPS = 1e-08
ADAM_WD = 0.01
ADAM_STEP = 10

VMEM_LIMIT_BYTES = 56 * 1024 * 1024
MESH = pl.DeviceIdType.MESH
ANY = pl.BlockSpec(memory_space=pl.ANY)

_DN = {
    "nn": (((1,), (0,)), ((), ())),
    "nt": (((1,), (1,)), ((), ())),
    "tn": (((0,), (0,)), ((), ())),
}


def _call(body, **kw):
    return pl.pallas_call(body, **kw)


def _cp(sem=None):
    return pltpu.CompilerParams(dimension_semantics=sem, vmem_limit_bytes=VMEM_LIMIT_BYTES)


def _dot(a, b, kind="nn"):
    return lax.dot_general(a.astype(_MXU), b.astype(_MXU), _DN[kind], preferred_element_type=F32)


def _sds(shape, dtype):
    return jax.ShapeDtypeStruct(shape, dtype)


def _rms_rows(x, g):
    r = lax.rsqrt(jnp.mean(x * x, axis=-1, keepdims=True) + EPS)
    return x * r * g


def _relu2(u):
    r = jnp.maximum(u.astype(F32), 0.0)
    return r * r


def _add_res(r, e):
    return r + e.astype(F32)


def _relu2_bwd(r, u):
    return r * (2.0 * jnp.maximum(u.astype(F32), 0.0))


def _mm(name, kind, a, b, a_spec, b_spec, out_shape, o_spec, grid, acc_shape, *,
        pro=None, pro_in=(), pro_specs=(), epi=None, epi_in=(), epi_specs=()):
    nk = grid[2]
    npro, nepi = len(pro_in), len(epi_in)

    def body(*refs):
        a_ref, b_ref = refs[0], refs[1]
        pro_refs = refs[2:2 + npro]
        epi_refs = refs[2 + npro:2 + npro + nepi]
        o_ref = refs[2 + npro + nepi]
        av = a_ref[...]
        if pro is not None:
            av = pro(av, *[r[...] for r in pro_refs])
        part = _dot(av, b_ref[...], kind)

        def finish(r):
            if epi is not None:
                r = epi(r, *[e[...] for e in epi_refs])
            o_ref[...] = r.astype(o_ref.dtype)

        if nk == 1:
            finish(part)
        else:
            acc = refs[-1]
            k = pl.program_id(2)

            @pl.when(k == 0)
            def _():
                acc[...] = part

            @pl.when(k > 0)
            def _():
                acc[...] += part

            @pl.when(k == nk - 1)
            def _():
                finish(acc[...])

    scratch = [] if nk == 1 else [pltpu.VMEM(acc_shape, F32)]
    return _call(
        body, out_shape=out_shape, grid=grid,
        in_specs=[a_spec, b_spec, *pro_specs, *epi_specs], out_specs=o_spec,
        scratch_shapes=scratch, name=name,
        compiler_params=_cp(("parallel", "parallel", "arbitrary")),
    )(a, b, *pro_in, *epi_in)


def _row_tile(n, cap):
    t = min(n, cap)
    assert n % t == 0, (n, t)
    return t


def _resident(w):
    nd = w.ndim
    return pl.BlockSpec(w.shape, lambda i: (0,) * nd, pipeline_mode=pl.Buffered(1))


def _rms_proj(name, x, g, w, n_cols, gather=()):
    T = x.shape[0]
    tm = _row_tile(T, 512)
    ns = n_cols // N_CHIPS
    steps = T // tm
    ng = len(gather)

    def body(x_ref, g_ref, w_ref, *rest):
        o_ref, h_ref = rest[ng], rest[ng + 1]
        bufs, sems = rest[ng + 2:2 * ng + 2], rest[2 * ng + 2:]
        if ng:
            @pl.when(pl.program_id(0) == 0)
            def _():
                _gather_start(bufs, *sems)

        h = _rms_rows(x_ref[...], g_ref[...]).astype(h_ref.dtype)
        h_ref[...] = h
        for k in range(N_CHIPS):
            o_ref[:, k * ns:(k + 1) * ns] = _dot(h, w_ref[k]).astype(o_ref.dtype)

        if ng:
            @pl.when(pl.program_id(0) == steps - 1)
            def _():
                _gather_finish(bufs, *sems)

    row = pl.BlockSpec((tm, D_MODEL), lambda i: (i, 0))
    out = _call(
        body,
        out_shape=(_sds((T, n_cols), _ACT), _sds((T, D_MODEL), _ACT), *[_sds(b.shape, b.dtype) for b in gather]),
        grid=(steps,),
        in_specs=[row, pl.BlockSpec((1, D_MODEL), lambda i: (0, 0)), _resident(w)] + [ANY] * ng,
        out_specs=(pl.BlockSpec((tm, n_cols), lambda i: (i, 0)), row) + (ANY,) * ng,
        input_output_aliases={3 + i: 2 + i for i in range(ng)},
        scratch_shapes=_gather_sems(ng) if ng else [],
        name=name, compiler_params=_cp(("arbitrary",)),
    )(x, g, w, *gather)
    return out[0], out[1], list(out[2:])


def _colsharded_wgrad_full(name, h, dy, n_cols, tn):
    T = h.shape[0]
    tt = _row_tile(T, 1024)
    per = (n_cols // N_CHIPS) // tn
    return _mm(
        name, "tn", h, dy,
        pl.BlockSpec((tt, D_MODEL), lambda i, j, k: (k, 0)),
        pl.BlockSpec((tt, tn), lambda i, j, k: (k, j)),
        _sds((N_CHIPS, D_MODEL, n_cols // N_CHIPS), _COMM),
        pl.BlockSpec((None, D_MODEL, tn), lambda i, j, k: (j // per, 0, j % per)),
        (1, n_cols // tn, T // tt), (D_MODEL, tn),
    )


def _dgrad_rms(name, dy, w, x, g, dres):
    T, kd = dy.shape
    tm = _row_tile(T, 512)
    ns = kd // N_CHIPS

    def body(dy_ref, w_ref, x_ref, g_ref, dres_ref, dx_ref, dg_ref):
        i = pl.program_id(0)
        dh = _dot(dy_ref[:, 0:ns], w_ref[0], "nt")
        for k in range(1, N_CHIPS):
            dh += _dot(dy_ref[:, k * ns:(k + 1) * ns], w_ref[k], "nt")
        x = x_ref[...]
        r = lax.rsqrt(jnp.mean(x * x, axis=-1, keepdims=True) + EPS)
        xh = x * r
        dxh = dh * g_ref[...]
        dx_ref[...] = dres_ref[...] + r * (dxh - xh * jnp.mean(dxh * xh, axis=-1, keepdims=True))
        dgp = jnp.sum(dh * xh, axis=0, keepdims=True)

        @pl.when(i == 0)
        def _():
            dg_ref[...] = dgp

        @pl.when(i > 0)
        def _():
            dg_ref[...] += dgp

    row = pl.BlockSpec((tm, D_MODEL), lambda i: (i, 0))
    vec = pl.BlockSpec((1, D_MODEL), lambda i: (0, 0))
    return _call(
        body,
        out_shape=(_sds((T, D_MODEL), F32), _sds((1, D_MODEL), F32)),
        grid=(T // tm,),
        in_specs=[pl.BlockSpec((tm, kd), lambda i: (i, 0)), _resident(w), row, vec, row],
        out_specs=(row, vec),
        name=name, compiler_params=_cp(("arbitrary",)),
    )(dy, w, x, g, dres)


def _ffn_out(x1, u, w):
    T = u.shape[0]
    tm = _row_tile(T, 512)
    ks = w.shape[1]

    def body(x_ref, u_ref, w_ref, o_ref):
        acc = x_ref[...]
        for k in range(N_CHIPS):
            acc += _dot(_relu2(u_ref[:, k * ks:(k + 1) * ks]), w_ref[k])
        o_ref[...] = acc

    row = pl.BlockSpec((tm, D_MODEL), lambda i: (i, 0))
    return _call(
        body, out_shape=_sds((T, D_MODEL), F32), grid=(T // tm,),
        in_specs=[row, pl.BlockSpec((tm, FFN_HIDDEN), lambda i: (i, 0)), _resident(w)], out_specs=row,
        name="ff2", compiler_params=_cp(("parallel",)),
    )(x1, u, w)


def _ffn_out_dgrad(dx2, u, w):
    T = u.shape[0]
    tm = _row_tile(T, 512)
    ks = w.shape[1]

    def body(d_ref, u_ref, w_ref, o_ref):
        d = d_ref[...].astype(_MXU)
        for k in range(N_CHIPS):
            sl = slice(k * ks, (k + 1) * ks)
            o_ref[:, sl] = _relu2_bwd(_dot(d, w_ref[k], "nt"), u_ref[:, sl]).astype(o_ref.dtype)

    wide = pl.BlockSpec((tm, FFN_HIDDEN), lambda i: (i, 0))
    return _call(
        body, out_shape=_sds((T, FFN_HIDDEN), _ACT), grid=(T // tm,),
        in_specs=[pl.BlockSpec((tm, D_MODEL), lambda i: (i, 0)), wide, _resident(w)], out_specs=wide,
        name="ff2_dgrad", compiler_params=_cp(("parallel",)),
    )(dx2, u, w)


def _rowsharded_mm(name, a, w, res, *, pro=None):
    T, K = a.shape
    ks = K // N_CHIPS
    tm = _row_tile(T, 512)
    epi = dict(epi=_add_res, epi_in=(res,), epi_specs=(pl.BlockSpec((tm, D_MODEL), lambda i, j, k: (i, 0)),)) if res is not None else {}
    return _mm(
        name, "nn", a, w,
        pl.BlockSpec((tm, ks), lambda i, j, k: (i, k)),
        pl.BlockSpec((None, ks, D_MODEL), lambda i, j, k: (k, 0, 0)),
        _sds((T, D_MODEL), F32 if res is not None else _ACT),
        pl.BlockSpec((tm, D_MODEL), lambda i, j, k: (i, 0)),
        (T // tm, 1, N_CHIPS), (tm, D_MODEL), pro=pro, **epi,
    )


def _rowsharded_dgrad(name, dy, w, *, epi=None, epi_in=None, out_dtype=_ACT, res=None):
    T = dy.shape[0]
    ks = w.shape[1]
    tm = _row_tile(T, 512)
    kw = {}
    if epi is not None:
        kw = dict(epi=epi, epi_in=(epi_in,), epi_specs=(pl.BlockSpec((tm, ks), lambda i, j, k: (i, j)),))
    return _mm(
        name, "nt", dy, w,
        pl.BlockSpec((tm, D_MODEL), lambda i, j, k: (i, 0)),
        pl.BlockSpec((None, ks, D_MODEL), lambda i, j, k: (j, 0, 0)),
        _sds((T, ks * N_CHIPS), out_dtype),
        pl.BlockSpec((tm, ks), lambda i, j, k: (i, j)),
        (T // tm, N_CHIPS, 1), None, **kw,
    )


def _rowsharded_wgrad(name, a, dy, K, *, pro=None):
    T = a.shape[0]
    ks = K // N_CHIPS
    tt = _row_tile(T, 1024)
    return _mm(
        name, "tn", a, dy,
        pl.BlockSpec((tt, ks), lambda i, j, k: (k, i)),
        pl.BlockSpec((tt, D_MODEL), lambda i, j, k: (k, 0)),
        _sds((N_CHIPS, ks, D_MODEL), _COMM),
        pl.BlockSpec((None, ks, D_MODEL), lambda i, j, k: (i, 0, 0)),
        (N_CHIPS, 1, T // tt), (ks, D_MODEL), pro=pro,
    )


def _mixer_wgrads(merged, y_r, y_p, o_m, dx1, do3):
    T = merged.shape[0]
    tt = _row_tile(T, 512)
    nt = T // tt
    rs = D_MODEL // N_CHIPS

    def body(mg_ref, yr_ref, yp_ref, om_ref, dx_ref, do3_ref, g_out, g_ret, g_pool, g_mem, a_out, a_ret, a_pool, a_mem):
        t = pl.program_id(0)

        def accumulate(acc, part):
            @pl.when(t == 0)
            def _():
                acc[...] = part

            @pl.when(t > 0)
            def _():
                acc[...] += part

        accumulate(a_out, _dot(mg_ref[...], dx_ref[...], "tn"))
        accumulate(a_ret, _dot(yr_ref[...], do3_ref[:, 0:D_MODEL], "tn"))
        accumulate(a_pool, _dot(yp_ref[...], do3_ref[:, D_MODEL:2 * D_MODEL], "tn"))
        accumulate(a_mem, _dot(om_ref[...], do3_ref[:, 2 * D_MODEL:3 * D_MODEL], "tn"))

        @pl.when(t == nt - 1)
        def _():
            for k in range(N_CHIPS):
                rows = slice(k * rs, (k + 1) * rs)
                g_out[k] = a_out[rows, :].astype(g_out.dtype)
                g_ret[k] = a_ret[rows, :].astype(g_ret.dtype)
                g_pool[k] = a_pool[:, rows].astype(g_pool.dtype)
                g_mem[k] = a_mem[:, rows].astype(g_mem.dtype)

    row = lambda w: pl.BlockSpec((tt, w), lambda t: (t, 0))
    whole = lambda s: pl.BlockSpec(s, lambda t: (0, 0, 0))
    rsh, csh = (N_CHIPS, rs, D_MODEL), (N_CHIPS, POOL_W, rs)
    return _call(
        body,
        out_shape=(_sds(rsh, _COMM), _sds(rsh, _COMM), _sds(csh, _COMM), _sds(csh, _COMM)),
        grid=(nt,),
        in_specs=[row(D_MODEL), row(D_MODEL), row(POOL_W), row(MEMQ_W), row(D_MODEL), row(3 * D_MODEL)],
        out_specs=(whole(rsh), whole(rsh), whole(csh), whole(csh)),
        scratch_shapes=[pltpu.VMEM((D_MODEL, D_MODEL), F32), pltpu.VMEM((D_MODEL, D_MODEL), F32),
                        pltpu.VMEM((POOL_W, D_MODEL), F32), pltpu.VMEM((MEMQ_W, D_MODEL), F32)],
        name="mixer_wgrads", compiler_params=_cp(("arbitrary",)),
    )(merged, y_r, y_p, o_m, dx1, do3)


def _rot(x, cos, sin):
    return x * cos + pltpu.roll(x, QK_DIM // 2, 1) * sin


def _rot_bwd(d, cos, sin):
    return d * cos + pltpu.roll(d * sin, QK_DIM // 2, 1)


def _tile_diff(qi, ki, tq, tk):
    n = qi * tq + lax.broadcasted_iota(jnp.int32, (tq, tk), 0)
    m = ki * tk + lax.broadcasted_iota(jnp.int32, (tq, tk), 1)
    return (n - m).astype(F32)


def _decay(qi, ki, tq, tk, lgf, lgb):
    diff = _tile_diff(qi, ki, tq, tk)
    return diff, jnp.exp(jnp.where(diff >= 0.0, lgf * diff, -(lgb * diff)))


def _head(h, width):
    return slice(h * width, (h + 1) * width)


def _row_index(t):
    return lax.broadcasted_iota(jnp.int32, (t, QK_DIM), 0).astype(F32)


def _q_decay_fwd(t, lgf):
    return jnp.exp(lgf * _row_index(t))


def _q_decay_bwd(t, lgb):
    return jnp.exp(lgb * (float(t) - _row_index(t)))


def _k_decay_fwd(t, lgf, tiles_apart):
    return jnp.exp(lgf * ((tiles_apart * t).astype(F32) - _row_index(t)))


def _k_decay_bwd(t, lgb, tiles_apart):
    return jnp.exp(lgb * (((tiles_apart - 1) * t).astype(F32) + _row_index(t)))


def _group_norm_gate(o, g):
    mu = jnp.mean(o, axis=-1, keepdims=True)
    oc = o - mu
    var = jnp.mean(oc * oc, axis=-1, keepdims=True)
    on = oc * lax.rsqrt(var + EPS)
    return on * (g * jax.nn.sigmoid(g))


def _retention_fwd(proj, cos, sin, lg, B, S, gather=()):
    T = B * S
    tq = tk = _row_tile(S, 512)
    nq, nk = S // tq, S // tk
    scale = QK_DIM ** -0.5

    ng = len(gather)

    def body(lg_ref, q_ref, k_ref, v_ref, g_ref, cq_ref, sq_ref, ck_ref, sk_ref, *rest):
        o_ref, y_ref = rest[ng], rest[ng + 1]
        bufs = rest[ng + 2:2 * ng + 2]
        q0_scr, qf_scr, qb_scr, acc = rest[2 * ng + 2:2 * ng + 6]
        sems = rest[2 * ng + 6:]
        b = pl.program_id(0)
        qi = pl.program_id(1)
        ki = pl.program_id(2)
        if ng:
            @pl.when(jnp.logical_and(b == 0, jnp.logical_and(qi == 0, ki == 0)))
            def _():
                _gather_start(bufs, *sems)

        @pl.when(ki == 0)
        def _():
            for h in range(N_HEADS):
                sl = _head(h, QK_DIM)
                qr = _rot(q_ref[:, sl].astype(F32), cq_ref[...], sq_ref[...])
                q0_scr[:, sl] = qr.astype(_MXU)
                qf_scr[:, sl] = (qr * _q_decay_fwd(tq, lg_ref[0, h])).astype(_MXU)
                qb_scr[:, sl] = (qr * _q_decay_bwd(tq, lg_ref[1, h])).astype(_MXU)
            acc[...] = jnp.zeros_like(acc)

        def keys(h):
            return _rot(k_ref[:, _head(h, QK_DIM)].astype(F32), ck_ref[...], sk_ref[...]) * scale

        @pl.when(ki < qi)
        def _():
            for h in range(N_HEADS):
                s = _dot(qf_scr[:, _head(h, QK_DIM)], keys(h) * _k_decay_fwd(tk, lg_ref[0, h], qi - ki), "nt")
                acc[:, _head(h, V_DIM)] += _dot(s, v_ref[:, _head(h, V_DIM)])

        @pl.when(ki > qi)
        def _():
            for h in range(N_HEADS):
                s = _dot(qb_scr[:, _head(h, QK_DIM)], keys(h) * _k_decay_bwd(tk, lg_ref[1, h], ki - qi), "nt")
                acc[:, _head(h, V_DIM)] += _dot(s, v_ref[:, _head(h, V_DIM)])

        @pl.when(ki == qi)
        def _():
            for h in range(N_HEADS):
                s = _dot(q0_scr[:, _head(h, QK_DIM)], keys(h), "nt")
                _, dec = _decay(qi, ki, tq, tk, lg_ref[0, h], lg_ref[1, h])
                acc[:, _head(h, V_DIM)] += _dot(s * dec, v_ref[:, _head(h, V_DIM)])

        @pl.when(ki == nk - 1)
        def _():
            for h in range(N_HEADS):
                sl = _head(h, V_DIM)
                o = acc[:, sl]
                o_ref[:, sl] = o
                y_ref[:, sl] = _group_norm_gate(o, g_ref[:, sl].astype(F32)).astype(y_ref.dtype)

        if ng:
            @pl.when(jnp.logical_and(b == B - 1, jnp.logical_and(qi == nq - 1, ki == nk - 1)))
            def _():
                _gather_finish(bufs, *sems)

    qw, vw = N_HEADS * QK_DIM, N_HEADS * V_DIM
    out = _call(
        body,
        out_shape=(_sds((T, vw), F32), _sds((T, vw), _ACT), *[_sds(g.shape, g.dtype) for g in gather]),
        grid=(B, nq, nk),
        in_specs=[
            pl.BlockSpec(memory_space=pltpu.SMEM),
            pl.BlockSpec((tq, qw), lambda b, qi, ki: (b * nq + qi, C_Q // qw)),
            pl.BlockSpec((tk, qw), lambda b, qi, ki: (b * nk + ki, C_K // qw)),
            pl.BlockSpec((tk, vw), lambda b, qi, ki: (b * nk + ki, C_V // vw)),
            pl.BlockSpec((tq, vw), lambda b, qi, ki: (b * nq + qi, C_G // vw)),
            pl.BlockSpec((tq, QK_DIM), lambda b, qi, ki: (qi, 0)),
            pl.BlockSpec((tq, QK_DIM), lambda b, qi, ki: (qi, 0)),
            pl.BlockSpec((tk, QK_DIM), lambda b, qi, ki: (ki, 0)),
            pl.BlockSpec((tk, QK_DIM), lambda b, qi, ki: (ki, 0)),
        ] + [ANY] * ng,
        out_specs=(
            pl.BlockSpec((tq, vw), lambda b, qi, ki: (b * nq + qi, 0)),
            pl.BlockSpec((tq, vw), lambda b, qi, ki: (b * nq + qi, 0)),
        ) + (ANY,) * ng,
        input_output_aliases={9 + i: 2 + i for i in range(ng)},
        scratch_shapes=[pltpu.VMEM((tq, qw), _MXU), pltpu.VMEM((tq, qw), _MXU), pltpu.VMEM((tq, qw), _MXU),
                        pltpu.VMEM((tq, vw), F32)] + (_gather_sems(ng) if ng else []),
        name="retention_fwd",
        compiler_params=_cp(("arbitrary", "arbitrary", "arbitrary")),
    )(lg, proj, proj, proj, proj, cos, sin, cos, sin, *gather)
    return out[0], out[1], list(out[2:])


def _gn_gate_bwd(dy, o, proj, dproj):
    T = dy.shape[0]
    tm = _row_tile(T, 512)

    def body(dy_ref, o_ref, g_ref, dproj_in, do_ref, dg_ref):
        for h in range(N_HEADS):
            sl = slice(h * V_DIM, (h + 1) * V_DIM)
            o = o_ref[:, sl]
            g = g_ref[:, sl].astype(F32)
            d = dy_ref[:, sl].astype(F32)
            mu = jnp.mean(o, axis=-1, keepdims=True)
            oc = o - mu
            rstd = lax.rsqrt(jnp.mean(oc * oc, axis=-1, keepdims=True) + EPS)
            on = oc * rstd
            sg = jax.nn.sigmoid(g)
            don = d * (g * sg)
            dg_ref[:, sl] = (d * on * (sg * (1.0 + g * (1.0 - sg)))).astype(dg_ref.dtype)
            do = rstd * (don - jnp.mean(don, axis=-1, keepdims=True) - on * jnp.mean(don * on, axis=-1, keepdims=True))
            do_ref[:, sl] = do.astype(do_ref.dtype)

    wide = N_HEADS * V_DIM
    return _call(
        body,
        out_shape=(_sds((T, wide), _ACT), _sds(dproj.shape, dproj.dtype)),
        grid=(T // tm,),
        in_specs=[
            pl.BlockSpec((tm, wide), lambda i: (i, 0)),
            pl.BlockSpec((tm, wide), lambda i: (i, 0)),
            pl.BlockSpec((tm, wide), lambda i: (i, C_G // wide)),
            ANY,
        ],
        out_specs=(pl.BlockSpec((tm, wide), lambda i: (i, 0)), pl.BlockSpec((tm, wide), lambda i: (i, C_G // wide))),
        input_output_aliases={3: 1},
        name="gn_gate_bwd", compiler_params=_cp(("parallel",)),
    )(dy, o, proj, dproj)


def _retention_bwd(proj, d_o, dproj, cos, sin, lg, B, S, scatter=()):
    T = B * S
    tq = tk = _row_tile(S, 512)
    nq, nk = S // tq, S // tk
    scale = QK_DIM ** -0.5
    qw, vw = N_HEADS * QK_DIM, N_HEADS * V_DIM

    ns = len(scatter)

    def body(lg_ref, q_ref, k_ref, v_ref, do_ref, cq_ref, sq_ref, ck_ref, sk_ref, dproj_in, *rest):
        sums = rest[:ns]
        dqkv_ref, dlg_ref = rest[ns], rest[ns + 1]
        parts = rest[ns + 2:2 * ns + 2]
        kr_scr, dq_acc, dk_acc, dv_acc, gl_acc = rest[2 * ns + 2:2 * ns + 7]
        sems = rest[2 * ns + 7:]
        b = pl.program_id(0)
        ki = pl.program_id(1)
        qi = pl.program_id(2)
        q_rows = pl.ds(pl.multiple_of(qi * tq, tq), tq)
        k_rows = pl.ds(pl.multiple_of(ki * tk, tk), tk)
        if ns:
            @pl.when(jnp.logical_and(b == 0, jnp.logical_and(qi == 0, ki == 0)))
            def _():
                _scatter_start(sums, parts, *sems)

        @pl.when(qi == 0)
        def _():
            for h in range(N_HEADS):
                sl = _head(h, QK_DIM)
                kr_scr[:, sl] = _rot(k_ref[:, sl].astype(F32), ck_ref[...], sk_ref[...]) * scale
            dk_acc[...] = jnp.zeros_like(dk_acc)
            dv_acc[...] = jnp.zeros_like(dv_acc)

        @pl.when(jnp.logical_and(qi == 0, ki == 0))
        def _():
            gl_acc[...] = jnp.zeros_like(gl_acc)

        @pl.when(ki == 0)
        def _():
            dq_acc[q_rows, :] = jnp.zeros((tq, qw), F32)

        def queries(h):
            return _rot(q_ref[:, _head(h, QK_DIM)].astype(F32), cq_ref[...], sq_ref[...])

        def one_sided(h, q_factor, k_factor, side, sign):
            sl, vsl = _head(h, QK_DIM), _head(h, V_DIM)
            qt = (queries(h) * q_factor).astype(_MXU)
            kt = (kr_scr[:, sl] * k_factor).astype(_MXU)
            d_out = do_ref[:, vsl]
            p = _dot(qt, kt, "nt")
            dv_acc[:, vsl] += _dot(p, d_out, "tn")
            dp = _dot(d_out, v_ref[:, vsl], "nt")
            dq_acc[q_rows, sl] += _dot(dp, kt) * q_factor
            dk_acc[:, sl] += _dot(dp, qt, "tn") * k_factor
            diff = _tile_diff(qi, ki, tq, tk)
            row = 2 * h + side
            gl_acc[row:row + 1, :] += sign * jnp.sum(dp * p * diff, axis=0, keepdims=True)

        @pl.when(ki < qi)
        def _():
            for h in range(N_HEADS):
                one_sided(h, _q_decay_fwd(tq, lg_ref[0, h]), _k_decay_fwd(tk, lg_ref[0, h], qi - ki), 0, 1.0)

        @pl.when(ki > qi)
        def _():
            for h in range(N_HEADS):
                one_sided(h, _q_decay_bwd(tq, lg_ref[1, h]), _k_decay_bwd(tk, lg_ref[1, h], ki - qi), 1, -1.0)

        @pl.when(ki == qi)
        def _():
            for h in range(N_HEADS):
                sl, vsl = _head(h, QK_DIM), _head(h, V_DIM)
                qr = queries(h).astype(_MXU)
                kr = kr_scr[:, sl].astype(_MXU)
                d_out = do_ref[:, vsl]
                s = _dot(qr, kr, "nt")
                diff, dec = _decay(qi, ki, tq, tk, lg_ref[0, h], lg_ref[1, h])
                p = s * dec
                dv_acc[:, vsl] += _dot(p, d_out, "tn")
                dp = _dot(d_out, v_ref[:, vsl], "nt")
                ds = dp * dec
                dq_acc[q_rows, sl] += _dot(ds, kr)
                dk_acc[:, sl] += _dot(ds, qr, "tn")
                gd = dp * p * diff
                gl_acc[2 * h:2 * h + 1, :] += jnp.sum(jnp.where(diff >= 0.0, gd, 0.0), axis=0, keepdims=True)
                gl_acc[2 * h + 1:2 * h + 2, :] += jnp.sum(jnp.where(diff < 0.0, -gd, 0.0), axis=0, keepdims=True)

        @pl.when(qi == nq - 1)
        def _():
            for h in range(N_HEADS):
                sl = _head(h, QK_DIM)
                dk = _rot_bwd(dk_acc[:, sl] * scale, ck_ref[...], sk_ref[...])
                dqkv_ref[k_rows, qw + h * QK_DIM:qw + (h + 1) * QK_DIM] = dk.astype(dqkv_ref.dtype)
            dqkv_ref[k_rows, 2 * qw:2 * qw + vw] = dv_acc[...].astype(dqkv_ref.dtype)

        @pl.when(ki == nk - 1)
        def _():
            for h in range(N_HEADS):
                sl = _head(h, QK_DIM)
                dqkv_ref[q_rows, sl] = _rot_bwd(dq_acc[q_rows, sl], cq_ref[...], sq_ref[...]).astype(dqkv_ref.dtype)

        @pl.when(jnp.logical_and(qi == nq - 1, ki == nk - 1))
        def _():
            dlg_ref[...] = gl_acc[...]

        if ns:
            @pl.when(jnp.logical_and(b == B - 1, jnp.logical_and(qi == nq - 1, ki == nk - 1)))
            def _():
                _scatter_finish(sums, parts, *sems)

    out = _call(
        body,
        out_shape=(_sds(dproj.shape, dproj.dtype), _sds((B, 2 * N_HEADS, tk), F32), *_scatter_shapes(scatter)),
        grid=(B, nk, nq),
        in_specs=[
            pl.BlockSpec(memory_space=pltpu.SMEM),
            pl.BlockSpec((tq, qw), lambda b, ki, qi: (b * nq + qi, C_Q // qw)),
            pl.BlockSpec((tk, qw), lambda b, ki, qi: (b * nk + ki, C_K // qw)),
            pl.BlockSpec((tk, vw), lambda b, ki, qi: (b * nk + ki, C_V // vw)),
            pl.BlockSpec((tq, vw), lambda b, ki, qi: (b * nq + qi, 0)),
            pl.BlockSpec((tq, QK_DIM), lambda b, ki, qi: (qi, 0)),
            pl.BlockSpec((tq, QK_DIM), lambda b, ki, qi: (qi, 0)),
            pl.BlockSpec((tk, QK_DIM), lambda b, ki, qi: (ki, 0)),
            pl.BlockSpec((tk, QK_DIM), lambda b, ki, qi: (ki, 0)),
            ANY,
        ] + [ANY] * ns,
        out_specs=(
            pl.BlockSpec((S, 2 * qw + vw), lambda b, ki, qi: (b, 0)),
            pl.BlockSpec((None, 2 * N_HEADS, tk), lambda b, ki, qi: (b, 0, 0)),
        ) + (ANY,) * ns,
        scratch_shapes=[
            pltpu.VMEM((tk, qw), F32), pltpu.VMEM((S, qw), F32),
            pltpu.VMEM((tk, qw), F32), pltpu.VMEM((tk, vw), F32), pltpu.VMEM((2 * N_HEADS, tk), F32),
        ] + (_scatter_sems(ns) if ns else []),
        input_output_aliases={9: 0},
        name="retention_bwd",
        compiler_params=_cp(("arbitrary", "arbitrary", "arbitrary")),
    )(lg, proj, proj, proj, d_o, cos, sin, cos, sin, dproj, *scatter)
    return out[0], out[1], list(out[2:])


POOL_PAD = 16


def _pad_rows(v):
    z = jnp.zeros((POOL_PAD, v.shape[1]), F32)
    return jnp.concatenate([z, v, z], axis=0)


def _window_sums(first, length, levels):
    s = first
    step = 1
    for _ in range(levels - 1):
        s = pltpu.roll(s, step, 0) + pltpu.roll(s, length - step, 0)
        step *= 2
    return s


def _pool_counts(S, hw):
    n = lax.broadcasted_iota(jnp.int32, (S, 1), 0)
    return (jnp.minimum(n + hw, S) - jnp.maximum(n - hw, 0)).astype(F32)


def _pool_mixed(pf, S, g):
    length = S + 2 * POOL_PAD
    xp = _pad_rows(pf)
    s = _window_sums(xp + pltpu.roll(xp, 1, 0), length, g + 1)[POOL_PAD:POOL_PAD + S]
    return s / _pool_counts(S, 1 << g) - pf


def _pool_mixed_bwd(dmixed, S, g):
    length = S + 2 * POOL_PAD
    ep = _pad_rows(dmixed / _pool_counts(S, 1 << g))
    t = _window_sums(ep + pltpu.roll(ep, length - 1, 0), length, g + 1)[POOL_PAD:POOL_PAD + S]
    return t - dmixed


def _pool_fwd(proj, w_grp, scale, B, S):
    T = B * S
    G = POOL_W // 4

    def body(p_ref, wg_ref, sc_ref, y_ref):
        for g in range(4):
            sl = slice(g * G, (g + 1) * G)
            mixed = _pool_mixed(p_ref[:, sl].astype(F32), S, g)
            y_ref[:, sl] = (_dot(mixed, wg_ref[g]) * sc_ref[:, sl]).astype(y_ref.dtype)

    return _call(
        body, out_shape=_sds((T, POOL_W), _ACT), grid=(B,),
        in_specs=[
            pl.BlockSpec((S, POOL_W), lambda b: (b, C_P // POOL_W)),
            pl.BlockSpec((4, G, G), lambda b: (0, 0, 0)),
            pl.BlockSpec((1, POOL_W), lambda b: (0, 0)),
        ],
        out_specs=pl.BlockSpec((S, POOL_W), lambda b: (b, 0)),
        name="pool_fwd", compiler_params=_cp(("parallel",)),
    )(proj, w_grp, scale)


def _pool_bwd(proj, dy, dproj, w_grp, scale, B, S):
    G = POOL_W // 4

    def body(p_ref, dy_ref, wg_ref, sc_ref, dproj_in, dp_ref, dwg_ref, dsc_ref):
        b = pl.program_id(0)
        for g in range(4):
            sl = slice(g * G, (g + 1) * G)
            mixed = _pool_mixed(p_ref[:, sl].astype(F32), S, g)
            z = _dot(mixed, wg_ref[g])
            d = dy_ref[:, sl].astype(F32)
            dsc = jnp.sum(d * z, axis=0, keepdims=True)
            dz = d * sc_ref[:, sl]
            dwg = _dot(mixed, dz, "tn")
            dmixed = _dot(dz, wg_ref[g], "nt")
            dp_ref[:, sl] = _pool_mixed_bwd(dmixed, S, g).astype(dp_ref.dtype)

            @pl.when(b == 0)
            def _():
                dwg_ref[g] = dwg
                dsc_ref[:, sl] = dsc

            @pl.when(b > 0)
            def _():
                dwg_ref[g] += dwg
                dsc_ref[:, sl] += dsc

    return _call(
        body,
        out_shape=(_sds(dproj.shape, dproj.dtype), _sds((4, G, G), F32), _sds((1, POOL_W), F32)),
        grid=(B,),
        in_specs=[
            pl.BlockSpec((S, POOL_W), lambda b: (b, C_P // POOL_W)),
            pl.BlockSpec((S, POOL_W), lambda b: (b, 0)),
            pl.BlockSpec((4, G, G), lambda b: (0, 0, 0)),
            pl.BlockSpec((1, POOL_W), lambda b: (0, 0)),
            ANY,
        ],
        out_specs=(
            pl.BlockSpec((S, POOL_W), lambda b: (b, C_P // POOL_W)),
            pl.BlockSpec((4, G, G), lambda b: (0, 0, 0)),
            pl.BlockSpec((1, POOL_W), lambda b: (0, 0)),
        ),
        input_output_aliases={4: 0},
        name="pool_bwd", compiler_params=_cp(("arbitrary",)),
    )(proj, dy, w_grp, scale, dproj)


def _mem_softmax(q, k):
    s = _dot(q, k, "nt") * (QK_DIM ** -0.5)
    e = jnp.exp(s - jnp.max(s, axis=-1, keepdims=True))
    return e / jnp.sum(e, axis=-1, keepdims=True)


def _mem_attn_fwd(proj, kv, B, S):
    T = B * S
    tq = _row_tile(S, 512)
    nq = S // tq

    def body(q_ref, kv_ref, o_ref):
        for h in range(N_HEADS):
            sl = slice(h * QK_DIM, (h + 1) * QK_DIM)
            a = _mem_softmax(q_ref[:, sl], kv_ref[:, sl])
            o_ref[:, sl] = _dot(a, kv_ref[:, MEMQ_W + h * QK_DIM:MEMQ_W + (h + 1) * QK_DIM]).astype(o_ref.dtype)

    return _call(
        body, out_shape=_sds((T, MEMQ_W), _ACT), grid=(B, nq),
        in_specs=[
            pl.BlockSpec((tq, MEMQ_W), lambda b, i: (b * nq + i, C_QM // MEMQ_W)),
            pl.BlockSpec((MEM_LEN, 2 * MEMQ_W), lambda b, i: (b, 0)),
        ],
        out_specs=pl.BlockSpec((tq, MEMQ_W), lambda b, i: (b * nq + i, 0)),
        name="mem_attn_fwd", compiler_params=_cp(("parallel", "parallel")),
    )(proj, kv)


def _mem_attn_bwd(proj, kv, d_o, dproj, B, S):
    tq = _row_tile(S, 512)
    nq = S // tq
    scale = QK_DIM ** -0.5

    def body(q_ref, kv_ref, do_ref, dproj_in, dq_ref, dkv_ref):
        i = pl.program_id(1)
        for h in range(N_HEADS):
            sl = slice(h * QK_DIM, (h + 1) * QK_DIM)
            vsl = slice(MEMQ_W + h * QK_DIM, MEMQ_W + (h + 1) * QK_DIM)
            q = q_ref[:, sl]
            a = _mem_softmax(q, kv_ref[:, sl])
            d = do_ref[:, sl]
            da = _dot(d, kv_ref[:, vsl], "nt")
            ds = a * (da - jnp.sum(a * da, axis=-1, keepdims=True)) * scale
            dq_ref[:, sl] = _dot(ds, kv_ref[:, sl]).astype(dq_ref.dtype)
            dk = _dot(ds, q, "tn")
            dv = _dot(a, d, "tn")

            @pl.when(i == 0)
            def _():
                dkv_ref[:, sl] = dk
                dkv_ref[:, vsl] = dv

            @pl.when(i > 0)
            def _():
                dkv_ref[:, sl] += dk
                dkv_ref[:, vsl] += dv

    return _call(
        body,
        out_shape=(_sds(dproj.shape, dproj.dtype), _sds((B * MEM_LEN, 2 * MEMQ_W), F32)),
        grid=(B, nq),
        in_specs=[
            pl.BlockSpec((tq, MEMQ_W), lambda b, i: (b * nq + i, C_QM // MEMQ_W)),
            pl.BlockSpec((MEM_LEN, 2 * MEMQ_W), lambda b, i: (b, 0)),
            pl.BlockSpec((tq, MEMQ_W), lambda b, i: (b * nq + i, 0)),
            ANY,
        ],
        out_specs=(
            pl.BlockSpec((tq, MEMQ_W), lambda b, i: (b * nq + i, C_QM // MEMQ_W)),
            pl.BlockSpec((MEM_LEN, 2 * MEMQ_W), lambda b, i: (b, 0)),
        ),
        input_output_aliases={3: 0},
        name="mem_attn_bwd", compiler_params=_cp(("parallel", "arbitrary")),
    )(proj, kv, d_o, dproj)


def _mem_norm(mem2d, g):
    M = mem2d.shape[0]
    tm = _row_tile(M, 512)

    def body(x_ref, g_ref, o_ref):
        o_ref[...] = _rms_rows(x_ref[...], g_ref[...]).astype(o_ref.dtype)

    return _call(
        body, out_shape=_sds((M, D_MODEL), _ACT), grid=(M // tm,),
        in_specs=[pl.BlockSpec((tm, D_MODEL), lambda i: (i, 0)), pl.BlockSpec((1, D_MODEL), lambda i: (0, 0))],
        out_specs=pl.BlockSpec((tm, D_MODEL), lambda i: (i, 0)),
        name="mem_norm", compiler_params=_cp(("parallel",)),
    )(mem2d, g)


def _mem_norm_wgrad(mem2d, d_memn):
    M = mem2d.shape[0]
    tm = _row_tile(M, 512)

    def body(x_ref, d_ref, dg_ref):
        i = pl.program_id(0)
        x = x_ref[...]
        xh = x * lax.rsqrt(jnp.mean(x * x, axis=-1, keepdims=True) + EPS)
        dg = jnp.sum(d_ref[...] * xh, axis=0, keepdims=True)

        @pl.when(i == 0)
        def _():
            dg_ref[...] = dg

        @pl.when(i > 0)
        def _():
            dg_ref[...] += dg

    return _call(
        body, out_shape=_sds((1, D_MODEL), F32), grid=(M // tm,),
        in_specs=[pl.BlockSpec((tm, D_MODEL), lambda i: (i, 0)), pl.BlockSpec((tm, D_MODEL), lambda i: (i, 0))],
        out_specs=pl.BlockSpec((1, D_MODEL), lambda i: (0, 0)),
        name="mem_norm_wgrad", compiler_params=_cp(("arbitrary",)),
    )(mem2d, d_memn)


def _row_mm(a, w_ref):
    ks = w_ref.shape[1]
    out = _dot(a[:, 0:ks], w_ref[0])
    for k in range(1, N_CHIPS):
        out += _dot(a[:, k * ks:(k + 1) * ks], w_ref[k])
    return out


def _row_mm_t(d, w_ref):
    return jnp.concatenate([_dot(d, w_ref[k], "nt") for k in range(N_CHIPS)], axis=1)


def _col_mm(a, w_ref):
    return jnp.concatenate([_dot(a, w_ref[k]) for k in range(N_CHIPS)], axis=1)


def _col_mm_t(d, w_ref):
    ns = w_ref.shape[2]
    out = _dot(d[:, 0:ns], w_ref[0], "nt")
    for k in range(1, N_CHIPS):
        out += _dot(d[:, k * ns:(k + 1) * ns], w_ref[k], "nt")
    return out


def _full_spec(w):
    nd = w.ndim
    return pl.BlockSpec(w.shape, lambda i: (0,) * nd)


def _merge_fwd(x, proj, y_r, y_p, o_m, w_ret_o, w_pool_o, w_mem_o, w_out):
    T = x.shape[0]
    tm = _row_tile(T, 256)

    def body(x_ref, gr_ref, gp_ref, gm_ref, yr_ref, yp_ref, om_ref, wr_ref, wp_ref, wm_ref, wo_ref,
             x1_ref, mg_ref, o3_ref):
        o_r = _row_mm(yr_ref[...], wr_ref)
        o_p = _col_mm(yp_ref[...], wp_ref)
        o_q = _col_mm(om_ref[...], wm_ref)
        merged = (jax.nn.sigmoid(gr_ref[...].astype(F32)) * o_r + jax.nn.sigmoid(gp_ref[...].astype(F32)) * o_p
                  + jax.nn.sigmoid(gm_ref[...].astype(F32)) * o_q)
        mg = merged.astype(mg_ref.dtype)
        mg_ref[...] = mg
        o3_ref[:, 0:D_MODEL] = o_r.astype(o3_ref.dtype)
        o3_ref[:, D_MODEL:2 * D_MODEL] = o_p.astype(o3_ref.dtype)
        o3_ref[:, 2 * D_MODEL:3 * D_MODEL] = o_q.astype(o3_ref.dtype)
        x1_ref[...] = x_ref[...] + _row_mm(mg, wo_ref)

    gb = C_GATE // D_MODEL
    row = lambda w: pl.BlockSpec((tm, w), lambda i: (i, 0))
    return _call(
        body,
        out_shape=(_sds((T, D_MODEL), F32), _sds((T, D_MODEL), _ACT), _sds((T, 3 * D_MODEL), _ACT)),
        grid=(T // tm,),
        in_specs=[
            row(D_MODEL),
            pl.BlockSpec((tm, D_MODEL), lambda i: (i, gb)),
            pl.BlockSpec((tm, D_MODEL), lambda i: (i, gb + 1)),
            pl.BlockSpec((tm, D_MODEL), lambda i: (i, gb + 2)),
            row(D_MODEL), row(POOL_W), row(MEMQ_W),
            _full_spec(w_ret_o), _full_spec(w_pool_o), _full_spec(w_mem_o), _full_spec(w_out),
        ],
        out_specs=(row(D_MODEL), row(D_MODEL), row(3 * D_MODEL)),
        name="merge_fwd", compiler_params=_cp(("parallel",)),
    )(x, proj, proj, proj, y_r, y_p, o_m, w_ret_o, w_pool_o, w_mem_o, w_out)


def _merge_bwd(dx1, proj, o3, w_ret_o, w_pool_o, w_mem_o, w_out):
    T = dx1.shape[0]
    tm = _row_tile(T, 256)

    def body(dx_ref, gr_ref, gp_ref, gm_ref, o3_ref, wr_ref, wp_ref, wm_ref, wo_ref,
             do3_ref, dgate_ref, dyr_ref, dyp_ref, dom_ref):
        dmerged = _row_mm_t(dx_ref[...], wo_ref)
        douts = []
        dgate_ref[:, 0:MEMQ_W] = jnp.zeros((tm, MEMQ_W), dgate_ref.dtype)
        for n, g_ref in enumerate((gr_ref, gp_ref, gm_ref)):
            sl = slice(n * D_MODEL, (n + 1) * D_MODEL)
            gate = jax.nn.sigmoid(g_ref[...].astype(F32))
            d_out = (dmerged * gate).astype(do3_ref.dtype)
            do3_ref[:, sl] = d_out
            dgate_ref[:, MEMQ_W + n * D_MODEL:MEMQ_W + (n + 1) * D_MODEL] = (
                dmerged * o3_ref[:, sl].astype(F32) * gate * (1.0 - gate)).astype(dgate_ref.dtype)
            douts.append(d_out)
        dyr_ref[...] = _row_mm_t(douts[0], wr_ref).astype(dyr_ref.dtype)
        dyp_ref[...] = _col_mm_t(douts[1], wp_ref).astype(dyp_ref.dtype)
        dom_ref[...] = _col_mm_t(douts[2], wm_ref).astype(dom_ref.dtype)

    gb = C_GATE // D_MODEL
    row = lambda w: pl.BlockSpec((tm, w), lambda i: (i, 0))
    return _call(
        body,
        out_shape=(
            _sds((T, 3 * D_MODEL), _ACT), _sds((T, N_PROJ), _ACT),
            _sds((T, D_MODEL), _ACT), _sds((T, POOL_W), _ACT), _sds((T, MEMQ_W), _ACT),
        ),
        grid=(T // tm,),
        in_specs=[
            row(D_MODEL),
            pl.BlockSpec((tm, D_MODEL), lambda i: (i, gb)),
            pl.BlockSpec((tm, D_MODEL), lambda i: (i, gb + 1)),
            pl.BlockSpec((tm, D_MODEL), lambda i: (i, gb + 2)),
            row(3 * D_MODEL),
            _full_spec(w_ret_o), _full_spec(w_pool_o), _full_spec(w_mem_o), _full_spec(w_out),
        ],
        out_specs=(row(3 * D_MODEL), pl.BlockSpec((tm, N_PROJ - C_QM), lambda i: (i, 1)),
                   row(D_MODEL), row(POOL_W), row(MEMQ_W)),
        name="merge_bwd", compiler_params=_cp(("parallel",)),
    )(dx1, proj, proj, proj, o3, w_ret_o, w_pool_o, w_mem_o, w_out)


def _loss_head(x, g, target):
    T = x.shape[0]
    tm = _row_tile(T, 512)

    def body(x_ref, g_ref, t_ref, dx_ref, sq_ref, dg_ref):
        i = pl.program_id(0)
        x = x_ref[...]
        gg = g_ref[...]
        r = lax.rsqrt(jnp.mean(x * x, axis=-1, keepdims=True) + EPS)
        xh = x * r
        err = xh * gg - t_ref[...]
        dy = err * (1.0 / D_MODEL)
        dxh = dy * gg
        dx_ref[...] = r * (dxh - xh * jnp.mean(dxh * xh, axis=-1, keepdims=True))
        sq = jnp.sum(err * err, axis=0, keepdims=True)
        dg = jnp.sum(dy * xh, axis=0, keepdims=True)

        @pl.when(i == 0)
        def _():
            sq_ref[...] = sq
            dg_ref[...] = dg

        @pl.when(i > 0)
        def _():
            sq_ref[...] += sq
            dg_ref[...] += dg

    vec = pl.BlockSpec((1, D_MODEL), lambda i: (0, 0))
    row = pl.BlockSpec((tm, D_MODEL), lambda i: (i, 0))
    return _call(
        body,
        out_shape=(_sds((T, D_MODEL), F32), _sds((1, D_MODEL), F32), _sds((1, D_MODEL), F32)),
        grid=(T // tm,), in_specs=[row, vec, row], out_specs=(row, vec, vec),
        name="loss_head", compiler_params=_cp(("arbitrary",)),
    )(x, g, target)


def _block_rows(rows, cols, itemsize, cap_bytes=2 << 20):
    t = rows
    while t * cols * itemsize > cap_bytes and t % 2 == 0 and (t // 2) % 16 == 0:
        t //= 2
    return t


def _cast_into_slot(w3d, layer, me):
    _, R, C = w3d.shape
    tr = _block_rows(R, C, 4)

    def body(me_ref, w_ref, o_ref):
        o_ref[...] = w_ref[...].astype(o_ref.dtype)

    grid_spec = pltpu.PrefetchScalarGridSpec(
        num_scalar_prefetch=1, grid=(R // tr,),
        in_specs=[pl.BlockSpec((None, tr, C), lambda i, me: (layer, i, 0))],
        out_specs=pl.BlockSpec((None, tr, C), lambda i, me: (me[0], i, 0)),
    )
    return _call(body, out_shape=_sds((N_CHIPS, R, C), _COMM), grid_spec=grid_spec,
                 name="cast_into_slot", compiler_params=_cp(("parallel",)))(me, w3d)


def _adamw(w, g, m, v):
    R, C = w.shape
    tr = _block_rows(R, C, 4, 1 << 20)

    def body(w_ref, g_ref, m_ref, v_ref, d_ref, nm_ref, nv_ref):
        g = g_ref[...]
        m = ADAM_B1 * m_ref[...] + (1.0 - ADAM_B1) * g
        v = ADAM_B2 * v_ref[...] + (1.0 - ADAM_B2) * (g * g)
        m_hat = m / (1.0 - ADAM_B1 ** ADAM_STEP)
        v_hat = v / (1.0 - ADAM_B2 ** ADAM_STEP)
        d_ref[...] = -ADAM_LR * (m_hat / (jnp.sqrt(v_hat) + ADAM_EPS) + ADAM_WD * w_ref[...])
        nm_ref[...] = m
        nv_ref[...] = v

    spec = pl.BlockSpec((tr, C), lambda i: (i, 0))
    out = _sds((R, C), F32)
    return _call(body, out_shape=(out, out, out), grid=(R // tr,), in_specs=[spec] * 4, out_specs=(spec,) * 3,
                 name="adamw", compiler_params=_cp(("parallel",)))(w, g, m, v)


def _add_sibling_half(g_full, land, my_c):
    _, R, C = g_full.shape
    hr = R // 2
    tr = _block_rows(hr, C, 2, 1 << 20)
    nb = hr // tr

    def body(c_ref, g_ref, l_ref, o_ref):
        o_ref[...] = (g_ref[...].astype(F32) + l_ref[...].astype(F32)).astype(o_ref.dtype)

    grid_spec = pltpu.PrefetchScalarGridSpec(
        num_scalar_prefetch=1, grid=(N_CHIPS, nb),
        in_specs=[
            pl.BlockSpec((None, tr, C), lambda k, i, c: (k, c[0] * nb + i, 0)),
            pl.BlockSpec((None, tr, C), lambda k, i, c: (k, i, 0)),
        ],
        out_specs=pl.BlockSpec((None, tr, C), lambda k, i, c: (k, i, 0)),
    )
    return _call(body, out_shape=_sds((N_CHIPS, hr, C), _COMM), grid_spec=grid_spec,
                 name="add_sibling_half", compiler_params=_cp(("parallel", "parallel")))(my_c, g_full, land)


def _add_chips(land, sums, g_all, layer, my_chip, my_core):
    _, hr, C = land.shape
    tr = _block_rows(hr, C, 4, 1 << 20)
    nb = hr // tr

    def body(me_ref, core_ref, own_ref, a_ref, b_ref, c_ref, *rest):
        o_ref = rest[-1]
        o_ref[...] = ((own_ref[...].astype(F32) + a_ref[...].astype(F32)) + b_ref[...].astype(F32)) + c_ref[...].astype(F32)

    def other(d):
        return pl.BlockSpec((None, tr, C), lambda i, me, core: ((me[0] + d) % N_CHIPS, i, 0))

    operands = [my_chip, my_core, sums, land, land, land]
    in_specs = [other(0), other(1), other(2), other(3)]
    aliases = {}
    if g_all is not None:
        operands.append(g_all)
        in_specs.append(ANY)
        aliases = {len(operands) - 1: 0}
    grid_spec = pltpu.PrefetchScalarGridSpec(
        num_scalar_prefetch=2, grid=(nb,), in_specs=in_specs,
        out_specs=pl.BlockSpec((None, tr, C), lambda i, me, core: (layer, core[0] * nb + i, 0)),
    )
    return _call(body, out_shape=_sds((DEPTH, 2 * hr, C), F32), grid_spec=grid_spec, input_output_aliases=aliases,
                 name="add_chips", compiler_params=_cp(("parallel",)))(*operands)


def _place():
    x, y, c = lax.axis_index("x"), lax.axis_index("y"), lax.axis_index("c")
    chips = [(1 - x, y), (x, 1 - y), (1 - x, 1 - y)]
    return x, y, c, 2 * x + y, chips


def _remote(src, dst, send_sem, recv_sem, dev):
    return pltpu.make_async_remote_copy(src_ref=src, dst_ref=dst, send_sem=send_sem, recv_sem=recv_sem,
                                        device_id=dev, device_id_type=MESH)


def _gather_sems(n):
    return [pltpu.SemaphoreType.DMA((n, 6)), pltpu.SemaphoreType.DMA((n, 6))]


def _gather_start(bufs, send_sems, recv_sems):
    x, y, c, me, chips = _place()
    for w, buf in enumerate(bufs):
        hr = buf.shape[1] // 2
        own = buf.at[me, pl.ds(c * hr, hr)]
        for j, chip in enumerate(chips):
            _remote(own, own, send_sems.at[w, j], recv_sems.at[w, j], (*chip, c)).start()


def _gather_finish(bufs, send_sems, recv_sems):
    x, y, c, me, chips = _place()
    sibling = (x, y, 1 - c)
    waits = []
    for w, buf in enumerate(bufs):
        hr = buf.shape[1] // 2
        mine = pl.ds(c * hr, hr)
        for j, chip in enumerate(chips):
            kc = 2 * chip[0] + chip[1]
            got = buf.at[kc, mine]
            first = _remote(got, got, send_sems.at[w, j], recv_sems.at[w, j], (*chip, c))
            first.wait_recv()
            fwd = _remote(got, got, send_sems.at[w, 3 + j], recv_sems.at[w, 3 + j], sibling)
            fwd.start()
            waits += [first.wait_send, fwd.wait_send]
    for w, buf in enumerate(bufs):
        hr = buf.shape[1] // 2
        theirs = pl.ds((1 - c) * hr, hr)
        for j, chip in enumerate(chips):
            kc = 2 * chip[0] + chip[1]
            got = buf.at[kc, theirs]
            _remote(got, got, send_sems.at[w, 3 + j], recv_sems.at[w, 3 + j], sibling).wait_recv()
    for wait in waits:
        wait()


def _allgather_weights(bufs):
    n = len(bufs)

    def body(*refs):
        outs = refs[n:2 * n]
        send_sems, recv_sems = refs[2 * n:]
        _gather_start(outs, send_sems, recv_sems)
        _gather_finish(outs, send_sems, recv_sems)

    out_shape = tuple(_sds(b.shape, b.dtype) for b in bufs)
    return _call(
        body, out_shape=out_shape, in_specs=[ANY] * n, out_specs=(ANY,) * n,
        input_output_aliases={i: i for i in range(n)},
        scratch_shapes=_gather_sems(n),
        name="allgather_weights",
    )(*bufs)


def _swap_sibling_halves(grads):
    n = len(grads)

    def body(*refs):
        ins, outs = refs[:n], refs[n:2 * n]
        send_sems, recv_sems = refs[2 * n:]
        x, y, c, _, _ = _place()
        copies = []
        for w in range(n):
            hr = ins[w].shape[1] // 2
            cp = _remote(ins[w].at[:, pl.ds((1 - c) * hr, hr)], outs[w], send_sems.at[w], recv_sems.at[w], (x, y, 1 - c))
            cp.start()
            copies.append(cp)
        for cp in copies:
            cp.wait()

    out_shape = tuple(_sds((N_CHIPS, g.shape[1] // 2, g.shape[2]), g.dtype) for g in grads)
    return _call(
        body, out_shape=out_shape, in_specs=[ANY] * n, out_specs=(ANY,) * n,
        scratch_shapes=[pltpu.SemaphoreType.DMA((n,)), pltpu.SemaphoreType.DMA((n,))],
        name="swap_sibling_halves",
    )(*grads)


def _scatter_sems(n):
    return [pltpu.SemaphoreType.DMA((n, 3)), pltpu.SemaphoreType.DMA((n, 3))]


def _scatter_shapes(sums):
    return tuple(_sds(s.shape, s.dtype) for s in sums)


def _scatter_start(sums, parts, send_sems, recv_sems):
    x, y, c, me, chips = _place()
    for w in range(len(sums)):
        for j, chip in enumerate(chips):
            kc = 2 * chip[0] + chip[1]
            _remote(sums[w].at[kc], parts[w].at[me], send_sems.at[w, j], recv_sems.at[w, j], (*chip, c)).start()


def _scatter_finish(sums, parts, send_sems, recv_sems):
    x, y, c, me, chips = _place()
    for w in range(len(sums)):
        for j, chip in enumerate(chips):
            kc = 2 * chip[0] + chip[1]
            _remote(sums[w].at[kc], parts[w].at[kc], send_sems.at[w, j], recv_sems.at[w, j], (*chip, c)).wait()


def _scatter_chip_sums(sums):
    n = len(sums)

    def body(*refs):
        ins, outs = refs[:n], refs[n:2 * n]
        send_sems, recv_sems = refs[2 * n:]
        _scatter_start(ins, outs, send_sems, recv_sems)
        _scatter_finish(ins, outs, send_sems, recv_sems)

    return _call(
        body, out_shape=_scatter_shapes(sums), in_specs=[ANY] * n, out_specs=(ANY,) * n,
        scratch_shapes=_scatter_sems(n), name="scatter_chip_sums",
    )(*sums)


def _join_sibling_halves(g_alls, layer):
    n = len(g_alls)

    def body(*refs):
        outs = refs[n:2 * n]
        send_sems, recv_sems = refs[2 * n:]
        x, y, c, _, _ = _place()
        copies = []
        for w in range(n):
            hr = outs[w].shape[1] // 2
            mine = outs[w].at[layer, pl.ds(c * hr, hr)]
            cp = _remote(mine, mine, send_sems.at[w], recv_sems.at[w], (x, y, 1 - c))
            cp.start()
            copies.append(cp)
        for cp in copies:
            cp.wait()

    out_shape = tuple(_sds(g.shape, g.dtype) for g in g_alls)
    return _call(
        body, out_shape=out_shape, in_specs=[ANY] * n, out_specs=(ANY,) * n,
        input_output_aliases={i: i for i in range(n)},
        scratch_shapes=[pltpu.SemaphoreType.DMA((n,)), pltpu.SemaphoreType.DMA((n,))],
        name="join_sibling_halves",
    )(*g_alls)


def _chip_sums(grads, my_core):
    land = _swap_sibling_halves(grads)
    return [_add_sibling_half(g, l, my_core) for g, l in zip(grads, land)]


def _finish_reduce_scatter(parts, sums, g_alls, layer, my_chip, my_core):
    g_alls = [_add_chips(p, s, ga, layer, my_chip, my_core) for p, s, ga in zip(parts, sums, g_alls)]
    return _join_sibling_halves(g_alls, layer)


def _allreduce_small(v):
    R = v.shape[0]

    def body(v_ref, out_ref, sib_ref, chip_ref, sum_ref, send_sems, recv_sems):
        x, y, c, me, chips = _place()
        swap = _remote(v_ref, sib_ref, send_sems.at[0], recv_sems.at[0], (x, y, 1 - c))
        swap.start()
        swap.wait()
        sum_ref[...] = v_ref[...] + sib_ref[...]
        copies = []
        for j, chip in enumerate(chips):
            cp = _remote(sum_ref, chip_ref.at[me], send_sems.at[1 + j], recv_sems.at[1 + j], (*chip, c))
            cp.start()
            copies.append(cp)
        chip_ref[me] = sum_ref[...]
        for cp in copies:
            cp.wait()
        out_ref[...] = ((chip_ref[0] + chip_ref[1]) + chip_ref[2]) + chip_ref[3]

    vm = pl.BlockSpec(memory_space=pltpu.VMEM)
    return _call(
        body, out_shape=_sds((R, 128), F32), in_specs=[vm], out_specs=vm,
        scratch_shapes=[
            pltpu.VMEM((R, 128), F32), pltpu.VMEM((N_CHIPS, R, 128), F32), pltpu.VMEM((R, 128), F32),
            pltpu.SemaphoreType.DMA((4,)), pltpu.SemaphoreType.DMA((4,)),
        ],
        name="allreduce_small", compiler_params=_cp(),
    )(v)


def _pack_small(parts):
    rows = []
    for p in parts:
        flat = p.reshape(-1).astype(F32)
        pad = (-flat.shape[0]) % 128
        rows.append(jnp.pad(flat, (0, pad)).reshape(-1, 128))
    packed = jnp.concatenate(rows, axis=0)
    pad_rows = (-packed.shape[0]) % 8
    return jnp.pad(packed, ((0, pad_rows), (0, 0)))


def _unpack_small(packed, like):
    out, r = [], 0
    for p in like:
        n = p.size
        nr = -(-n // 128)
        out.append(packed[r:r + nr].reshape(-1)[:n].reshape(p.shape))
        r += nr
    return out


BIG = ("w_in", "w_ret_o", "w_pool_o", "w_mem_kv", "w_mem_o", "w_out", "w_ff1", "w_ff2")


def _rope_tables(S):
    inv = ROPE_BASE ** (-jnp.arange(0, QK_DIM, 2, dtype=F32) / QK_DIM)
    ang = jnp.arange(S).astype(F32)[:, None] * inv[None, :]
    cos, sin = jnp.cos(ang), jnp.sin(ang)
    return jnp.concatenate([cos, cos], axis=1), jnp.concatenate([-sin, sin], axis=1)


GATHER_WITH_IN_PROJ = ("w_in", "w_ret_o", "w_pool_o")


def _layer_fwd(x, W, small, memn, cos, sin, B, S, next_bufs):
    g1, g2, lg, w_grp, scale = small
    first = [next_bufs[n] for n in BIG if n in GATHER_WITH_IN_PROJ] if next_bufs else []
    second = [next_bufs[n] for n in BIG if n not in GATHER_WITH_IN_PROJ] if next_bufs else []
    proj, h1, first = _rms_proj("in_proj", x, g1, W["w_in"], N_PROJ, gather=first)
    o, y_r, second = _retention_fwd(proj, cos, sin, lg, B, S, gather=second)
    next_w = None
    if next_bufs:
        next_w = dict(zip([n for n in BIG if n in GATHER_WITH_IN_PROJ], first))
        next_w.update(zip([n for n in BIG if n not in GATHER_WITH_IN_PROJ], second))
    y_p = _pool_fwd(proj, w_grp, scale, B, S)
    kv = _rowsharded_mm("mem_kv", memn, W["w_mem_kv"], None)
    o_m = _mem_attn_fwd(proj, kv, B, S)
    x1, merged, o3 = _merge_fwd(x, proj, y_r, y_p, o_m, W["w_ret_o"], W["w_pool_o"], W["w_mem_o"], W["w_out"])
    u, h2, _ = _rms_proj("ff1", x1, g2, W["w_ff1"], FFN_HIDDEN)
    x2 = _ffn_out(x1, u, W["w_ff2"])
    saved = dict(x=x, proj=proj, h1=h1, o=o, y_r=y_r, y_p=y_p, kv=kv, o_m=o_m, merged=merged, o3=o3, x1=x1, h2=h2, u=u)
    return x2, saved, next_w


def _layer_bwd(dx2, sv, W, small, memn, d_memn, cos, sin, B, S, scatter):
    g1, g2, lg, w_grp, scale = small
    x, proj, x1, u = sv["x"], sv["proj"], sv["x1"], sv["u"]
    grads = {}
    du = _ffn_out_dgrad(dx2, u, W["w_ff2"])
    grads["w_ff2"] = _rowsharded_wgrad("ff2_wgrad", u, dx2, FFN_HIDDEN, pro=_relu2)
    dx1, dg2 = _dgrad_rms("ff1_dgrad", du, W["w_ff1"], x1, g2, dx2)
    grads["w_ff1"] = _colsharded_wgrad_full("ff1_wgrad", sv["h2"], du, FFN_HIDDEN, 1024)
    do3, dproj, dy_r, dy_p, do_m = _merge_bwd(dx1, proj, sv["o3"], W["w_ret_o"], W["w_pool_o"], W["w_mem_o"], W["w_out"])
    grads["w_out"], grads["w_ret_o"], grads["w_pool_o"], grads["w_mem_o"] = _mixer_wgrads(
        sv["merged"], sv["y_r"], sv["y_p"], sv["o_m"], dx1, do3)
    d_o, dproj = _gn_gate_bwd(dy_r, sv["o"], proj, dproj)
    dproj, dlg, parts = _retention_bwd(proj, d_o, dproj, cos, sin, lg, B, S, scatter=scatter)
    dproj, dw_grp, dscale = _pool_bwd(proj, dy_p, dproj, w_grp, scale, B, S)
    dproj, dkv = _mem_attn_bwd(proj, sv["kv"], do_m, dproj, B, S)
    grads["w_mem_kv"] = _rowsharded_wgrad("mem_kv_wgrad", memn, dkv, D_MODEL)
    if d_memn is None:
        d_memn = _rowsharded_dgrad("mem_kv_dgrad", dkv, W["w_mem_kv"], out_dtype=F32)
    else:
        d_memn = _rowsharded_dgrad("mem_kv_dgrad_acc", dkv, W["w_mem_kv"], epi=_add_res, epi_in=d_memn, out_dtype=F32)
    grads["w_in"] = _colsharded_wgrad_full("in_wgrad", sv["h1"], dproj, N_PROJ, 1792)
    dx, dg1 = _dgrad_rms("in_dgrad", dproj, W["w_in"], x, g1, dx1)
    small_grads = dict(g1=dg1, g2=dg2, lg=jnp.sum(dlg, axis=(0, 2)).reshape(N_HEADS, 2).T, w_grp=dw_grp, scale=dscale)
    return dx, grads, small_grads, d_memn, parts


def kernel(x, mem, w_in, ret_decay_logit, w_ret_o, w_pool_grp, pool_scale, w_pool_o, w_mem_kv, w_mem_o, w_out, w_ff1, w_ff2, norm1_g, norm2_g, mem_norm_g, final_norm_g, loss_target, m_w_in, m_ret_decay_logit, m_w_ret_o, m_w_pool_grp, m_pool_scale, m_w_pool_o, m_w_mem_kv, m_w_mem_o, m_w_out, m_w_ff1, m_w_ff2, m_norm1_g, m_norm2_g, m_mem_norm_g, m_final_norm_g, v_w_in, v_ret_decay_logit, v_w_ret_o, v_w_pool_grp, v_pool_scale, v_w_pool_o, v_w_mem_kv, v_w_mem_o, v_w_out, v_w_ff1, v_w_ff2, v_norm1_g, v_norm2_g, v_mem_norm_g, v_final_norm_g):
    B, S, _ = x.shape
    T = B * S
    big_w = dict(w_in=w_in, w_ret_o=w_ret_o, w_pool_o=w_pool_o, w_mem_kv=w_mem_kv, w_mem_o=w_mem_o, w_out=w_out, w_ff1=w_ff1, w_ff2=w_ff2)
    big_m = dict(w_in=m_w_in, w_ret_o=m_w_ret_o, w_pool_o=m_w_pool_o, w_mem_kv=m_w_mem_kv, w_mem_o=m_w_mem_o, w_out=m_w_out, w_ff1=m_w_ff1, w_ff2=m_w_ff2)
    big_v = dict(w_in=v_w_in, w_ret_o=v_w_ret_o, w_pool_o=v_w_pool_o, w_mem_kv=v_w_mem_kv, w_mem_o=v_w_mem_o, w_out=v_w_out, w_ff1=v_w_ff1, w_ff2=v_w_ff2)
    small_w = [ret_decay_logit, w_pool_grp, pool_scale, norm1_g, norm2_g, mem_norm_g, final_norm_g]
    small_m = [m_ret_decay_logit, m_w_pool_grp, m_pool_scale, m_norm1_g, m_norm2_g, m_mem_norm_g, m_final_norm_g]
    small_v = [v_ret_decay_logit, v_w_pool_grp, v_pool_scale, v_norm1_g, v_norm2_g, v_mem_norm_g, v_final_norm_g]
    my_chip = (2 * lax.axis_index("x") + lax.axis_index("y")).astype(jnp.int32).reshape(1)
    my_core = lax.axis_index("c").astype(jnp.int32).reshape(1)

    bufs = [{n: _cast_into_slot(big_w[n], l, my_chip) for n in BIG} for l in range(DEPTH)]
    weights = [dict(zip(BIG, _allgather_weights([bufs[0][n] for n in BIG])))]

    cos, sin = _rope_tables(S)
    log_g = jax.nn.log_sigmoid(ret_decay_logit.astype(F32))
    mem2d = mem.reshape(B * MEM_LEN, D_MODEL)
    memn = _mem_norm(mem2d, mem_norm_g.reshape(1, D_MODEL))
    smalls = [(norm1_g[l].reshape(1, D_MODEL), norm2_g[l].reshape(1, D_MODEL), log_g[l], w_pool_grp[l],
               pool_scale[l].reshape(1, POOL_W)) for l in range(DEPTH)]

    h = x.reshape(T, D_MODEL)
    saved = []
    for l in range(DEPTH):
        h, sv, next_w = _layer_fwd(h, weights[l], smalls[l], memn, cos, sin, B, S, bufs[l + 1] if l + 1 < DEPTH else None)
        saved.append(sv)
        weights.append(next_w)

    dh, sq, d_final_g = _loss_head(h, final_norm_g.reshape(1, D_MODEL), loss_target.reshape(T, D_MODEL))
    loss = lax.psum(0.5 * jnp.sum(sq) / D_MODEL, ("x", "y", "c"))

    big_grads = [None] * len(BIG)
    small_grads = [None] * DEPTH
    d_memn = None
    sums = []
    for l in reversed(range(DEPTH)):
        dh, grads, small_grads[l], d_memn, parts = _layer_bwd(
            dh, saved[l], weights[l], smalls[l], memn, d_memn, cos, sin, B, S, sums)
        if sums:
            big_grads = _finish_reduce_scatter(parts, sums, big_grads, l + 1, my_chip, my_core)
        sums = _chip_sums([grads[n] for n in BIG], my_core)
    big_grads = _finish_reduce_scatter(_scatter_chip_sums(sums), sums, big_grads, 0, my_chip, my_core)
    big_grads = dict(zip(BIG, big_grads))
    d_mem_g = _mem_norm_wgrad(mem2d, d_memn)

    d_logit = jnp.stack([sg["lg"] for sg in small_grads]) * jax.nn.sigmoid(-ret_decay_logit.astype(F32))
    small_g_local = [
        d_logit, jnp.stack([sg["w_grp"] for sg in small_grads]),
        jnp.stack([sg["scale"].reshape(POOL_W) for sg in small_grads]),
        jnp.stack([sg["g1"].reshape(D_MODEL) for sg in small_grads]),
        jnp.stack([sg["g2"].reshape(D_MODEL) for sg in small_grads]),
        d_mem_g.reshape(D_MODEL), d_final_g.reshape(D_MODEL),
    ]
    small_g = _allreduce_small(_pack_small(small_g_local))
    s_delta, s_m, s_v = _adamw(_pack_small(small_w), small_g, _pack_small(small_m), _pack_small(small_v))
    small_g, s_delta, s_m, s_v = (_unpack_small(a, small_w) for a in (small_g, s_delta, s_m, s_v))

    big_out = {}
    for n in BIG:
        w = big_w[n]
        g = big_grads[n]
        flat = lambda a: a.reshape(-1, a.shape[-1])
        d, nm, nv = _adamw(flat(w), flat(g), flat(big_m[n]), flat(big_v[n]))
        big_out[n] = (g, d.reshape(w.shape), nm.reshape(w.shape), nv.reshape(w.shape))

    order = ["w_in", "ret_decay_logit", "w_ret_o", "w_pool_grp", "pool_scale", "w_pool_o", "w_mem_kv", "w_mem_o",
             "w_out", "w_ff1", "w_ff2", "norm1_g", "norm2_g", "mem_norm_g", "final_norm_g"]
    small_names = ["ret_decay_logit", "w_pool_grp", "pool_scale", "norm1_g", "norm2_g", "mem_norm_g", "final_norm_g"]
    outs = [[], [], [], []]
    for n in order:
        if n in big_out:
            vals = big_out[n]
        else:
            i = small_names.index(n)
            vals = (small_g[i], s_delta[i], s_m[i], s_v[i])
        for k in range(4):
            outs[k].append(vals[k])
    return (loss, dh.reshape(B, S, D_MODEL), *outs[0], *outs[1], *outs[2], *outs[3])
```

```python
import functools

import jax
import jax.numpy as jnp
from jax import lax
from jax.experimental import pallas as pl
from jax.experimental.pallas import tpu as pltpu

F32 = jnp.float32
_MXU = jnp.bfloat16
_ACT = jnp.bfloat16
_COMM = jnp.bfloat16

D_MODEL = 1024
N_PROJ = 7168
FFN_HIDDEN = 4096
MEM_LEN = 256
N_HEADS = 4
QK_DIM = 128
V_DIM = 256
POOL_W = 512
MEMQ_W = 512
DEPTH = 4
N_CHIPS = 4
EPS = 1e-6
ROPE_BASE = 10000.0

C_Q, C_K, C_V, C_G, C_P, C_QM, C_GATE = 0, 512, 1024, 2048, 3072, 3584, 4096

ADAM_LR = 0.001
ADAM_B1 = 0.9
ADAM_B2 = 0.999
ADAM_EPS = 1e-08
ADAM_WD = 0.01
ADAM_STEP = 10

VMEM_LIMIT_BYTES = 56 * 1024 * 1024
MESH = pl.DeviceIdType.MESH
ANY = pl.BlockSpec(memory_space=pl.ANY)

_DN = {
    "nn": (((1,), (0,)), ((), ())),
    "nt": (((1,), (1,)), ((), ())),
    "tn": (((0,), (0,)), ((), ())),
}


def _call(body, **kw):
    return pl.pallas_call(body, **kw)


def _cp(sem=None):
    return pltpu.CompilerParams(dimension_semantics=sem, vmem_limit_bytes=VMEM_LIMIT_BYTES)


def _dot(a, b, kind="nn"):
    return lax.dot_general(a.astype(_MXU), b.astype(_MXU), _DN[kind], preferred_element_type=F32)


def _sds(shape, dtype):
    return jax.ShapeDtypeStruct(shape, dtype)


def _rms_rows(x, g):
    r = lax.rsqrt(jnp.mean(x * x, axis=-1, keepdims=True) + EPS)
    return x * r * g


def _relu2(u):
    r = jnp.maximum(u.astype(F32), 0.0)
    return r * r


def _add_res(r, e):
    return r + e.astype(F32)


def _relu2_bwd(r, u):
    return r * (2.0 * jnp.maximum(u.astype(F32), 0.0))


def _mm(name, kind, a, b, a_spec, b_spec, out_shape, o_spec, grid, acc_shape, *,
        pro=None, pro_in=(), pro_specs=(), epi=None, epi_in=(), epi_specs=()):
    nk = grid[2]
    npro, nepi = len(pro_in), len(epi_in)

    def body(*refs):
        a_ref, b_ref = refs[0], refs[1]
        pro_refs = refs[2:2 + npro]
        epi_refs = refs[2 + npro:2 + npro + nepi]
        o_ref = refs[2 + npro + nepi]
        av = a_ref[...]
        if pro is not None:
            av = pro(av, *[r[...] for r in pro_refs])
        part = _dot(av, b_ref[...], kind)

        def finish(r):
            if epi is not None:
                r = epi(r, *[e[...] for e in epi_refs])
            o_ref[...] = r.astype(o_ref.dtype)

        if nk == 1:
            finish(part)
        else:
            acc = refs[-1]
            k = pl.program_id(2)

            @pl.when(k == 0)
            def _():
                acc[...] = part

            @pl.when(k > 0)
            def _():
                acc[...] += part

            @pl.when(k == nk - 1)
            def _():
                finish(acc[...])

    scratch = [] if nk == 1 else [pltpu.VMEM(acc_shape, F32)]
    return _call(
        body, out_shape=out_shape, grid=grid,
        in_specs=[a_spec, b_spec, *pro_specs, *epi_specs], out_specs=o_spec,
        scratch_shapes=scratch, name=name,
        compiler_params=_cp(("parallel", "parallel", "arbitrary")),
    )(a, b, *pro_in, *epi_in)


def _row_tile(n, cap):
    t = min(n, cap)
    assert n % t == 0, (n, t)
    return t


def _resident(w):
    nd = w.ndim
    return pl.BlockSpec(w.shape, lambda i: (0,) * nd, pipeline_mode=pl.Buffered(1))


def _rms_proj(name, x, g, w, n_cols, gather=()):
    T = x.shape[0]
    tm = _row_tile(T, 512)
    ns = n_cols // N_CHIPS
    steps = T // tm
    ng = len(gather)

    def body(x_ref, g_ref, w_ref, *rest):
        o_ref, h_ref = rest[ng], rest[ng + 1]
        bufs, sems = rest[ng + 2:2 * ng + 2], rest[2 * ng + 2:]
        if ng:
            @pl.when(pl.program_id(0) == 0)
            def _():
                _gather_start(bufs, *sems)

        h = _rms_rows(x_ref[...], g_ref[...]).astype(h_ref.dtype)
        h_ref[...] = h
        for k in range(N_CHIPS):
            o_ref[:, k * ns:(k + 1) * ns] = _dot(h, w_ref[k]).astype(o_ref.dtype)

        if ng:
            @pl.when(pl.program_id(0) == steps - 1)
            def _():
                _gather_finish(bufs, *sems)

    row = pl.BlockSpec((tm, D_MODEL), lambda i: (i, 0))
    out = _call(
        body,
        out_shape=(_sds((T, n_cols), _ACT), _sds((T, D_MODEL), _ACT), *[_sds(b.shape, b.dtype) for b in gather]),
        grid=(steps,),
        in_specs=[row, pl.BlockSpec((1, D_MODEL), lambda i: (0, 0)), _resident(w)] + [ANY] * ng,
        out_specs=(pl.BlockSpec((tm, n_cols), lambda i: (i, 0)), row) + (ANY,) * ng,
        input_output_aliases={3 + i: 2 + i for i in range(ng)},
        scratch_shapes=_gather_sems(ng) if ng else [],
        name=name, compiler_params=_cp(("arbitrary",)),
    )(x, g, w, *gather)
    return out[0], out[1], list(out[2:])


def _colsharded_wgrad_full(name, h, dy, n_cols, tn):
    T = h.shape[0]
    tt = _row_tile(T, 1024)
    per = (n_cols // N_CHIPS) // tn
    return _mm(
        name, "tn", h, dy,
        pl.BlockSpec((tt, D_MODEL), lambda i, j, k: (k, 0)),
        pl.BlockSpec((tt, tn), lambda i, j, k: (k, j)),
        _sds((N_CHIPS, D_MODEL, n_cols // N_CHIPS), _COMM),
        pl.BlockSpec((None, D_MODEL, tn), lambda i, j, k: (j // per, 0, j % per)),
        (1, n_cols // tn, T // tt), (D_MODEL, tn),
    )


def _dgrad_rms(name, dy, w, x, g, dres, scatter=()):
    T, kd = dy.shape
    tm = _row_tile(T, 512)
    ns = kd // N_CHIPS
    steps = T // tm
    nsc = len(scatter)

    def body(dy_ref, w_ref, x_ref, g_ref, dres_ref, *rest):
        sums = rest[:nsc]
        dx_ref, dg_ref = rest[nsc], rest[nsc + 1]
        parts, sems = rest[nsc + 2:2 * nsc + 2], rest[2 * nsc + 2:]
        i = pl.program_id(0)
        if nsc:
            @pl.when(i == 0)
            def _():
                _scatter_start(sums, parts, *sems)

        dh = _dot(dy_ref[:, 0:ns], w_ref[0], "nt")
        for k in range(1, N_CHIPS):
            dh += _dot(dy_ref[:, k * ns:(k + 1) * ns], w_ref[k], "nt")
        x = x_ref[...]
        r = lax.rsqrt(jnp.mean(x * x, axis=-1, keepdims=True) + EPS)
        xh = x * r
        dxh = dh * g_ref[...]
        dx_ref[...] = dres_ref[...] + r * (dxh - xh * jnp.mean(dxh * xh, axis=-1, keepdims=True))
        dgp = jnp.sum(dh * xh, axis=0, keepdims=True)

        @pl.when(i == 0)
        def _():
            dg_ref[...] = dgp

        @pl.when(i > 0)
        def _():
            dg_ref[...] += dgp

        if nsc:
            @pl.when(i == steps - 1)
            def _():
                _scatter_finish(sums, parts, *sems)

    row = pl.BlockSpec((tm, D_MODEL), lambda i: (i, 0))
    vec = pl.BlockSpec((1, D_MODEL), lambda i: (0, 0))
    out = _call(
        body,
        out_shape=(_sds((T, D_MODEL), F32), _sds((1, D_MODEL), F32), *_scatter_shapes(scatter)),
        grid=(steps,),
        in_specs=[pl.BlockSpec((tm, kd), lambda i: (i, 0)), _resident(w), row, vec, row] + [ANY] * nsc,
        out_specs=(row, vec) + (ANY,) * nsc,
        scratch_shapes=_scatter_sems(nsc) if nsc else [],
        name=name, compiler_params=_cp(("arbitrary",)),
    )(dy, w, x, g, dres, *scatter)
    return out[0], out[1], list(out[2:])


def _ffn_out(x1, u, w):
    T = u.shape[0]
    tm = _row_tile(T, 512)
    ks = w.shape[1]

    def body(x_ref, u_ref, w_ref, o_ref):
        acc = x_ref[...]
        for k in range(N_CHIPS):
            acc += _dot(_relu2(u_ref[:, k * ks:(k + 1) * ks]), w_ref[k])
        o_ref[...] = acc

    row = pl.BlockSpec((tm, D_MODEL), lambda i: (i, 0))
    return _call(
        body, out_shape=_sds((T, D_MODEL), F32), grid=(T // tm,),
        in_specs=[row, pl.BlockSpec((tm, FFN_HIDDEN), lambda i: (i, 0)), _resident(w)], out_specs=row,
        name="ff2", compiler_params=_cp(("parallel",)),
    )(x1, u, w)


def _ffn_out_dgrad(dx2, u, w):
    T = u.shape[0]
    tm = _row_tile(T, 512)
    ks = w.shape[1]

    def body(d_ref, u_ref, w_ref, o_ref):
        d = d_ref[...].astype(_MXU)
        for k in range(N_CHIPS):
            sl = slice(k * ks, (k + 1) * ks)
            o_ref[:, sl] = _relu2_bwd(_dot(d, w_ref[k], "nt"), u_ref[:, sl]).astype(o_ref.dtype)

    wide = pl.BlockSpec((tm, FFN_HIDDEN), lambda i: (i, 0))
    return _call(
        body, out_shape=_sds((T, FFN_HIDDEN), _ACT), grid=(T // tm,),
        in_specs=[pl.BlockSpec((tm, D_MODEL), lambda i: (i, 0)), wide, _resident(w)], out_specs=wide,
        name="ff2_dgrad", compiler_params=_cp(("parallel",)),
    )(dx2, u, w)


def _rowsharded_mm(name, a, w, res, *, pro=None):
    T, K = a.shape
    ks = K // N_CHIPS
    tm = _row_tile(T, 512)
    epi = dict(epi=_add_res, epi_in=(res,), epi_specs=(pl.BlockSpec((tm, D_MODEL), lambda i, j, k: (i, 0)),)) if res is not None else {}
    return _mm(
        name, "nn", a, w,
        pl.BlockSpec((tm, ks), lambda i, j, k: (i, k)),
        pl.BlockSpec((None, ks, D_MODEL), lambda i, j, k: (k, 0, 0)),
        _sds((T, D_MODEL), F32 if res is not None else _ACT),
        pl.BlockSpec((tm, D_MODEL), lambda i, j, k: (i, 0)),
        (T // tm, 1, N_CHIPS), (tm, D_MODEL), pro=pro, **epi,
    )


def _rowsharded_dgrad(name, dy, w, *, epi=None, epi_in=None, out_dtype=_ACT, res=None):
    T = dy.shape[0]
    ks = w.shape[1]
    tm = _row_tile(T, 512)
    kw = {}
    if epi is not None:
        kw = dict(epi=epi, epi_in=(epi_in,), epi_specs=(pl.BlockSpec((tm, ks), lambda i, j, k: (i, j)),))
    return _mm(
        name, "nt", dy, w,
        pl.BlockSpec((tm, D_MODEL), lambda i, j, k: (i, 0)),
        pl.BlockSpec((None, ks, D_MODEL), lambda i, j, k: (j, 0, 0)),
        _sds((T, ks * N_CHIPS), out_dtype),
        pl.BlockSpec((tm, ks), lambda i, j, k: (i, j)),
        (T // tm, N_CHIPS, 1), None, **kw,
    )


def _rowsharded_wgrad(name, a, dy, K, *, pro=None):
    T = a.shape[0]
    ks = K // N_CHIPS
    tt = _row_tile(T, 1024)
    return _mm(
        name, "tn", a, dy,
        pl.BlockSpec((tt, ks), lambda i, j, k: (k, i)),
        pl.BlockSpec((tt, D_MODEL), lambda i, j, k: (k, 0)),
        _sds((N_CHIPS, ks, D_MODEL), _COMM),
        pl.BlockSpec((None, ks, D_MODEL), lambda i, j, k: (i, 0, 0)),
        (N_CHIPS, 1, T // tt), (ks, D_MODEL), pro=pro,
    )


def _mixer_wgrads(merged, y_r, y_p, o_m, dx1, do3):
    T = merged.shape[0]
    tt = _row_tile(T, 512)
    nt = T // tt
    rs = D_MODEL // N_CHIPS

    def body(mg_ref, yr_ref, yp_ref, om_ref, dx_ref, do3_ref, g_out, g_ret, g_pool, g_mem, a_out, a_ret, a_pool, a_mem):
        t = pl.program_id(0)

        def accumulate(acc, part):
            @pl.when(t == 0)
            def _():
                acc[...] = part

            @pl.when(t > 0)
            def _():
                acc[...] += part

        accumulate(a_out, _dot(mg_ref[...], dx_ref[...], "tn"))
        accumulate(a_ret, _dot(yr_ref[...], do3_ref[:, 0:D_MODEL], "tn"))
        accumulate(a_pool, _dot(yp_ref[...], do3_ref[:, D_MODEL:2 * D_MODEL], "tn"))
        accumulate(a_mem, _dot(om_ref[...], do3_ref[:, 2 * D_MODEL:3 * D_MODEL], "tn"))

        @pl.when(t == nt - 1)
        def _():
            for k in range(N_CHIPS):
                rows = slice(k * rs, (k + 1) * rs)
                g_out[k] = a_out[rows, :].astype(g_out.dtype)
                g_ret[k] = a_ret[rows, :].astype(g_ret.dtype)
                g_pool[k] = a_pool[:, rows].astype(g_pool.dtype)
                g_mem[k] = a_mem[:, rows].astype(g_mem.dtype)

    row = lambda w: pl.BlockSpec((tt, w), lambda t: (t, 0))
    whole = lambda s: pl.BlockSpec(s, lambda t: (0, 0, 0))
    rsh, csh = (N_CHIPS, rs, D_MODEL), (N_CHIPS, POOL_W, rs)
    return _call(
        body,
        out_shape=(_sds(rsh, _COMM), _sds(rsh, _COMM), _sds(csh, _COMM), _sds(csh, _COMM)),
        grid=(nt,),
        in_specs=[row(D_MODEL), row(D_MODEL), row(POOL_W), row(MEMQ_W), row(D_MODEL), row(3 * D_MODEL)],
        out_specs=(whole(rsh), whole(rsh), whole(csh), whole(csh)),
        scratch_shapes=[pltpu.VMEM((D_MODEL, D_MODEL), F32), pltpu.VMEM((D_MODEL, D_MODEL), F32),
                        pltpu.VMEM((POOL_W, D_MODEL), F32), pltpu.VMEM((MEMQ_W, D_MODEL), F32)],
        name="mixer_wgrads", compiler_params=_cp(("arbitrary",)),
    )(merged, y_r, y_p, o_m, dx1, do3)


def _rot(x, cos, sin):
    return x * cos + pltpu.roll(x, QK_DIM // 2, 1) * sin


def _rot_bwd(d, cos, sin):
    return d * cos + pltpu.roll(d * sin, QK_DIM // 2, 1)


def _tile_diff(qi, ki, tq, tk):
    n = qi * tq + lax.broadcasted_iota(jnp.int32, (tq, tk), 0)
    m = ki * tk + lax.broadcasted_iota(jnp.int32, (tq, tk), 1)
    return (n - m).astype(F32)


def _decay(qi, ki, tq, tk, lgf, lgb):
    diff = _tile_diff(qi, ki, tq, tk)
    return diff, jnp.exp(jnp.where(diff >= 0.0, lgf * diff, -(lgb * diff)))


def _head(h, width):
    return slice(h * width, (h + 1) * width)


def _row_index(t):
    return lax.broadcasted_iota(jnp.int32, (t, QK_DIM), 0).astype(F32)


def _q_decay_fwd(t, lgf):
    return jnp.exp(lgf * _row_index(t))


def _q_decay_bwd(t, lgb):
    return jnp.exp(lgb * (float(t) - _row_index(t)))


def _k_decay_fwd(t, lgf, tiles_apart):
    return jnp.exp(lgf * ((tiles_apart * t).astype(F32) - _row_index(t)))


def _k_decay_bwd(t, lgb, tiles_apart):
    return jnp.exp(lgb * (((tiles_apart - 1) * t).astype(F32) + _row_index(t)))


def _group_norm_gate(o, g):
    mu = jnp.mean(o, axis=-1, keepdims=True)
    oc = o - mu
    var = jnp.mean(oc * oc, axis=-1, keepdims=True)
    on = oc * lax.rsqrt(var + EPS)
    return on * (g * jax.nn.sigmoid(g))


def _retention_fwd(proj, cos, sin, lg, B, S, gather=()):
    T = B * S
    tq = tk = _row_tile(S, 512)
    nq, nk = S // tq, S // tk
    scale = QK_DIM ** -0.5

    ng = len(gather)

    def body(lg_ref, q_ref, k_ref, v_ref, g_ref, cq_ref, sq_ref, ck_ref, sk_ref, *rest):
        o_ref, y_ref = rest[ng], rest[ng + 1]
        bufs = rest[ng + 2:2 * ng + 2]
        q0_scr, qf_scr, qb_scr, acc = rest[2 * ng + 2:2 * ng + 6]
        sems = rest[2 * ng + 6:]
        b = pl.program_id(0)
        qi = pl.program_id(1)
        ki = pl.program_id(2)
        if ng:
            @pl.when(jnp.logical_and(b == 0, jnp.logical_and(qi == 0, ki == 0)))
            def _():
                _gather_start(bufs, *sems)

        @pl.when(ki == 0)
        def _():
            for h in range(N_HEADS):
                sl = _head(h, QK_DIM)
                qr = _rot(q_ref[:, sl].astype(F32), cq_ref[...], sq_ref[...])
                q0_scr[:, sl] = qr.astype(_MXU)
                qf_scr[:, sl] = (qr * _q_decay_fwd(tq, lg_ref[0, h])).astype(_MXU)
                qb_scr[:, sl] = (qr * _q_decay_bwd(tq, lg_ref[1, h])).astype(_MXU)
            acc[...] = jnp.zeros_like(acc)

        def keys(h):
            return _rot(k_ref[:, _head(h, QK_DIM)].astype(F32), ck_ref[...], sk_ref[...]) * scale

        @pl.when(ki < qi)
        def _():
            for h in range(N_HEADS):
                s = _dot(qf_scr[:, _head(h, QK_DIM)], keys(h) * _k_decay_fwd(tk, lg_ref[0, h], qi - ki), "nt")
                acc[:, _head(h, V_DIM)] += _dot(s, v_ref[:, _head(h, V_DIM)])

        @pl.when(ki > qi)
        def _():
            for h in range(N_HEADS):
                s = _dot(qb_scr[:, _head(h, QK_DIM)], keys(h) * _k_decay_bwd(tk, lg_ref[1, h], ki - qi), "nt")
                acc[:, _head(h, V_DIM)] += _dot(s, v_ref[:, _head(h, V_DIM)])

        @pl.when(ki == qi)
        def _():
            for h in range(N_HEADS):
                s = _dot(q0_scr[:, _head(h, QK_DIM)], keys(h), "nt")
                _, dec = _decay(qi, ki, tq, tk, lg_ref[0, h], lg_ref[1, h])
                acc[:, _head(h, V_DIM)] += _dot(s * dec, v_ref[:, _head(h, V_DIM)])

        @pl.when(ki == nk - 1)
        def _():
            for h in range(N_HEADS):
                sl = _head(h, V_DIM)
                o = acc[:, sl]
                o_ref[:, sl] = o
                y_ref[:, sl] = _group_norm_gate(o, g_ref[:, sl].astype(F32)).astype(y_ref.dtype)

        if ng:
            @pl.when(jnp.logical_and(b == B - 1, jnp.logical_and(qi == nq - 1, ki == nk - 1)))
            def _():
                _gather_finish(bufs, *sems)

    qw, vw = N_HEADS * QK_DIM, N_HEADS * V_DIM
    out = _call(
        body,
        out_shape=(_sds((T, vw), F32), _sds((T, vw), _ACT), *[_sds(g.shape, g.dtype) for g in gather]),
        grid=(B, nq, nk),
        in_specs=[
            pl.BlockSpec(memory_space=pltpu.SMEM),
            pl.BlockSpec((tq, qw), lambda b, qi, ki: (b * nq + qi, C_Q // qw)),
            pl.BlockSpec((tk, qw), lambda b, qi, ki: (b * nk + ki, C_K // qw)),
            pl.BlockSpec((tk, vw), lambda b, qi, ki: (b * nk + ki, C_V // vw)),
            pl.BlockSpec((tq, vw), lambda b, qi, ki: (b * nq + qi, C_G // vw)),
            pl.BlockSpec((tq, QK_DIM), lambda b, qi, ki: (qi, 0)),
            pl.BlockSpec((tq, QK_DIM), lambda b, qi, ki: (qi, 0)),
            pl.BlockSpec((tk, QK_DIM), lambda b, qi, ki: (ki, 0)),
            pl.BlockSpec((tk, QK_DIM), lambda b, qi, ki: (ki, 0)),
        ] + [ANY] * ng,
        out_specs=(
            pl.BlockSpec((tq, vw), lambda b, qi, ki: (b * nq + qi, 0)),
            pl.BlockSpec((tq, vw), lambda b, qi, ki: (b * nq + qi, 0)),
        ) + (ANY,) * ng,
        input_output_aliases={9 + i: 2 + i for i in range(ng)},
        scratch_shapes=[pltpu.VMEM((tq, qw), _MXU), pltpu.VMEM((tq, qw), _MXU), pltpu.VMEM((tq, qw), _MXU),
                        pltpu.VMEM((tq, vw), F32)] + (_gather_sems(ng) if ng else []),
        name="retention_fwd",
        compiler_params=_cp(("arbitrary", "arbitrary", "arbitrary")),
    )(lg, proj, proj, proj, proj, cos, sin, cos, sin, *gather)
    return out[0], out[1], list(out[2:])


def _gn_gate_bwd(dy, o, proj, dproj):
    T = dy.shape[0]
    tm = _row_tile(T, 512)

    def body(dy_ref, o_ref, g_ref, dproj_in, do_ref, dg_ref):
        for h in range(N_HEADS):
            sl = slice(h * V_DIM, (h + 1) * V_DIM)
            o = o_ref[:, sl]
            g = g_ref[:, sl].astype(F32)
            d = dy_ref[:, sl].astype(F32)
            mu = jnp.mean(o, axis=-1, keepdims=True)
            oc = o - mu
            rstd = lax.rsqrt(jnp.mean(oc * oc, axis=-1, keepdims=True) + EPS)
            on = oc * rstd
            sg = jax.nn.sigmoid(g)
            don = d * (g * sg)
            dg_ref[:, sl] = (d * on * (sg * (1.0 + g * (1.0 - sg)))).astype(dg_ref.dtype)
            do = rstd * (don - jnp.mean(don, axis=-1, keepdims=True) - on * jnp.mean(don * on, axis=-1, keepdims=True))
            do_ref[:, sl] = do.astype(do_ref.dtype)

    wide = N_HEADS * V_DIM
    return _call(
        body,
        out_shape=(_sds((T, wide), _ACT), _sds(dproj.shape, dproj.dtype)),
        grid=(T // tm,),
        in_specs=[
            pl.BlockSpec((tm, wide), lambda i: (i, 0)),
            pl.BlockSpec((tm, wide), lambda i: (i, 0)),
            pl.BlockSpec((tm, wide), lambda i: (i, C_G // wide)),
            ANY,
        ],
        out_specs=(pl.BlockSpec((tm, wide), lambda i: (i, 0)), pl.BlockSpec((tm, wide), lambda i: (i, C_G // wide))),
        input_output_aliases={3: 1},
        name="gn_gate_bwd", compiler_params=_cp(("parallel",)),
    )(dy, o, proj, dproj)


def _retention_bwd(proj, d_o, dproj, cos, sin, lg, B, S, scatter=()):
    T = B * S
    tq = tk = _row_tile(S, 512)
    nq, nk = S // tq, S // tk
    scale = QK_DIM ** -0.5
    qw, vw = N_HEADS * QK_DIM, N_HEADS * V_DIM

    ns = len(scatter)

    def body(lg_ref, q_ref, k_ref, v_ref, do_ref, cq_ref, sq_ref, ck_ref, sk_ref, dproj_in, *rest):
        sums = rest[:ns]
        dqkv_ref, dlg_ref = rest[ns], rest[ns + 1]
        parts = rest[ns + 2:2 * ns + 2]
        kr_scr, dq_acc, dk_acc, dv_acc, gl_acc = rest[2 * ns + 2:2 * ns + 7]
        sems = rest[2 * ns + 7:]
        b = pl.program_id(0)
        ki = pl.program_id(1)
        qi = pl.program_id(2)
        q_rows = pl.ds(pl.multiple_of(qi * tq, tq), tq)
        k_rows = pl.ds(pl.multiple_of(ki * tk, tk), tk)
        if ns:
            @pl.when(jnp.logical_and(b == 0, jnp.logical_and(qi == 0, ki == 0)))
            def _():
                _scatter_start(sums, parts, *sems)

        @pl.when(qi == 0)
        def _():
            for h in range(N_HEADS):
                sl = _head(h, QK_DIM)
                kr_scr[:, sl] = _rot(k_ref[:, sl].astype(F32), ck_ref[...], sk_ref[...]) * scale
            dk_acc[...] = jnp.zeros_like(dk_acc)
            dv_acc[...] = jnp.zeros_like(dv_acc)

        @pl.when(jnp.logical_and(qi == 0, ki == 0))
        def _():
            gl_acc[...] = jnp.zeros_like(gl_acc)

        @pl.when(ki == 0)
        def _():
            dq_acc[q_rows, :] = jnp.zeros((tq, qw), F32)

        def queries(h):
            return _rot(q_ref[:, _head(h, QK_DIM)].astype(F32), cq_ref[...], sq_ref[...])

        def one_sided(h, q_factor, k_factor, side, sign):
            sl, vsl = _head(h, QK_DIM), _head(h, V_DIM)
            qt = (queries(h) * q_factor).astype(_MXU)
            kt = (kr_scr[:, sl] * k_factor).astype(_MXU)
            d_out = do_ref[:, vsl]
            p = _dot(qt, kt, "nt")
            dv_acc[:, vsl] += _dot(p, d_out, "tn")
            dp = _dot(d_out, v_ref[:, vsl], "nt")
            dq_acc[q_rows, sl] += _dot(dp, kt) * q_factor
            dk_acc[:, sl] += _dot(dp, qt, "tn") * k_factor
            diff = _tile_diff(qi, ki, tq, tk)
            row = 2 * h + side
            gl_acc[row:row + 1, :] += sign * jnp.sum(dp * p * diff, axis=0, keepdims=True)

        @pl.when(ki < qi)
        def _():
            for h in range(N_HEADS):
                one_sided(h, _q_decay_fwd(tq, lg_ref[0, h]), _k_decay_fwd(tk, lg_ref[0, h], qi - ki), 0, 1.0)

        @pl.when(ki > qi)
        def _():
            for h in range(N_HEADS):
                one_sided(h, _q_decay_bwd(tq, lg_ref[1, h]), _k_decay_bwd(tk, lg_ref[1, h], ki - qi), 1, -1.0)

        @pl.when(ki == qi)
        def _():
            for h in range(N_HEADS):
                sl, vsl = _head(h, QK_DIM), _head(h, V_DIM)
                qr = queries(h).astype(_MXU)
                kr = kr_scr[:, sl].astype(_MXU)
                d_out = do_ref[:, vsl]
                s = _dot(qr, kr, "nt")
                diff, dec = _decay(qi, ki, tq, tk, lg_ref[0, h], lg_ref[1, h])
                p = s * dec
                dv_acc[:, vsl] += _dot(p, d_out, "tn")
                dp = _dot(d_out, v_ref[:, vsl], "nt")
                ds = dp * dec
                dq_acc[q_rows, sl] += _dot(ds, kr)
                dk_acc[:, sl] += _dot(ds, qr, "tn")
                gd = dp * p * diff
                gl_acc[2 * h:2 * h + 1, :] += jnp.sum(jnp.where(diff >= 0.0, gd, 0.0), axis=0, keepdims=True)
                gl_acc[2 * h + 1:2 * h + 2, :] += jnp.sum(jnp.where(diff < 0.0, -gd, 0.0), axis=0, keepdims=True)

        @pl.when(qi == nq - 1)
        def _():
            for h in range(N_HEADS):
                sl = _head(h, QK_DIM)
                dk = _rot_bwd(dk_acc[:, sl] * scale, ck_ref[...], sk_ref[...])
                dqkv_ref[k_rows, qw + h * QK_DIM:qw + (h + 1) * QK_DIM] = dk.astype(dqkv_ref.dtype)
            dqkv_ref[k_rows, 2 * qw:2 * qw + vw] = dv_acc[...].astype(dqkv_ref.dtype)

        @pl.when(ki == nk - 1)
        def _():
            for h in range(N_HEADS):
                sl = _head(h, QK_DIM)
                dqkv_ref[q_rows, sl] = _rot_bwd(dq_acc[q_rows, sl], cq_ref[...], sq_ref[...]).astype(dqkv_ref.dtype)

        @pl.when(jnp.logical_and(qi == nq - 1, ki == nk - 1))
        def _():
            dlg_ref[...] = gl_acc[...]

        if ns:
            @pl.when(jnp.logical_and(b == B - 1, jnp.logical_and(qi == nq - 1, ki == nk - 1)))
            def _():
                _scatter_finish(sums, parts, *sems)

    out = _call(
        body,
        out_shape=(_sds(dproj.shape, dproj.dtype), _sds((B, 2 * N_HEADS, tk), F32), *_scatter_shapes(scatter)),
        grid=(B, nk, nq),
        in_specs=[
            pl.BlockSpec(memory_space=pltpu.SMEM),
            pl.BlockSpec((tq, qw), lambda b, ki, qi: (b * nq + qi, C_Q // qw)),
            pl.BlockSpec((tk, qw), lambda b, ki, qi: (b * nk + ki, C_K // qw)),
            pl.BlockSpec((tk, vw), lambda b, ki, qi: (b * nk + ki, C_V // vw)),
            pl.BlockSpec((tq, vw), lambda b, ki, qi: (b * nq + qi, 0)),
            pl.BlockSpec((tq, QK_DIM), lambda b, ki, qi: (qi, 0)),
            pl.BlockSpec((tq, QK_DIM), lambda b, ki, qi: (qi, 0)),
            pl.BlockSpec((tk, QK_DIM), lambda b, ki, qi: (ki, 0)),
            pl.BlockSpec((tk, QK_DIM), lambda b, ki, qi: (ki, 0)),
            ANY,
        ] + [ANY] * ns,
        out_specs=(
            pl.BlockSpec((S, 2 * qw + vw), lambda b, ki, qi: (b, 0)),
            pl.BlockSpec((None, 2 * N_HEADS, tk), lambda b, ki, qi: (b, 0, 0)),
        ) + (ANY,) * ns,
        scratch_shapes=[
            pltpu.VMEM((tk, qw), F32), pltpu.VMEM((S, qw), F32),
            pltpu.VMEM((tk, qw), F32), pltpu.VMEM((tk, vw), F32), pltpu.VMEM((2 * N_HEADS, tk), F32),
        ] + (_scatter_sems(ns) if ns else []),
        input_output_aliases={9: 0},
        name="retention_bwd",
        compiler_params=_cp(("arbitrary", "arbitrary", "arbitrary")),
    )(lg, proj, proj, proj, d_o, cos, sin, cos, sin, dproj, *scatter)
    return out[0], out[1], list(out[2:])


POOL_PAD = 16


def _pad_rows(v):
    z = jnp.zeros((POOL_PAD, v.shape[1]), F32)
    return jnp.concatenate([z, v, z], axis=0)


def _window_sums(first, length, levels):
    s = first
    step = 1
    for _ in range(levels - 1):
        s = pltpu.roll(s, step, 0) + pltpu.roll(s, length - step, 0)
        step *= 2
    return s


def _pool_counts(S, hw):
    n = lax.broadcasted_iota(jnp.int32, (S, 1), 0)
    return (jnp.minimum(n + hw, S) - jnp.maximum(n - hw, 0)).astype(F32)


def _pool_mixed(pf, S, g):
    length = S + 2 * POOL_PAD
    xp = _pad_rows(pf)
    s = _window_sums(xp + pltpu.roll(xp, 1, 0), length, g + 1)[POOL_PAD:POOL_PAD + S]
    return s / _pool_counts(S, 1 << g) - pf


def _pool_mixed_bwd(dmixed, S, g):
    length = S + 2 * POOL_PAD
    ep = _pad_rows(dmixed / _pool_counts(S, 1 << g))
    t = _window_sums(ep + pltpu.roll(ep, length - 1, 0), length, g + 1)[POOL_PAD:POOL_PAD + S]
    return t - dmixed


def _pool_fwd(proj, w_grp, scale, B, S):
    T = B * S
    G = POOL_W // 4

    def body(p_ref, wg_ref, sc_ref, y_ref):
        for g in range(4):
            sl = slice(g * G, (g + 1) * G)
            mixed = _pool_mixed(p_ref[:, sl].astype(F32), S, g)
            y_ref[:, sl] = (_dot(mixed, wg_ref[g]) * sc_ref[:, sl]).astype(y_ref.dtype)

    return _call(
        body, out_shape=_sds((T, POOL_W), _ACT), grid=(B,),
        in_specs=[
            pl.BlockSpec((S, POOL_W), lambda b: (b, C_P // POOL_W)),
            pl.BlockSpec((4, G, G), lambda b: (0, 0, 0)),
            pl.BlockSpec((1, POOL_W), lambda b: (0, 0)),
        ],
        out_specs=pl.BlockSpec((S, POOL_W), lambda b: (b, 0)),
        name="pool_fwd", compiler_params=_cp(("parallel",)),
    )(proj, w_grp, scale)


def _pool_bwd(proj, dy, dproj, w_grp, scale, B, S):
    G = POOL_W // 4

    def body(p_ref, dy_ref, wg_ref, sc_ref, dproj_in, dp_ref, dwg_ref, dsc_ref):
        b = pl.program_id(0)
        for g in range(4):
            sl = slice(g * G, (g + 1) * G)
            mixed = _pool_mixed(p_ref[:, sl].astype(F32), S, g)
            z = _dot(mixed, wg_ref[g])
            d = dy_ref[:, sl].astype(F32)
            dsc = jnp.sum(d * z, axis=0, keepdims=True)
            dz = d * sc_ref[:, sl]
            dwg = _dot(mixed, dz, "tn")
            dmixed = _dot(dz, wg_ref[g], "nt")
            dp_ref[:, sl] = _pool_mixed_bwd(dmixed, S, g).astype(dp_ref.dtype)

            @pl.when(b == 0)
            def _():
                dwg_ref[g] = dwg
                dsc_ref[:, sl] = dsc

            @pl.when(b > 0)
            def _():
                dwg_ref[g] += dwg
                dsc_ref[:, sl] += dsc

    return _call(
        body,
        out_shape=(_sds(dproj.shape, dproj.dtype), _sds((4, G, G), F32), _sds((1, POOL_W), F32)),
        grid=(B,),
        in_specs=[
            pl.BlockSpec((S, POOL_W), lambda b: (b, C_P // POOL_W)),
            pl.BlockSpec((S, POOL_W), lambda b: (b, 0)),
            pl.BlockSpec((4, G, G), lambda b: (0, 0, 0)),
            pl.BlockSpec((1, POOL_W), lambda b: (0, 0)),
            ANY,
        ],
        out_specs=(
            pl.BlockSpec((S, POOL_W), lambda b: (b, C_P // POOL_W)),
            pl.BlockSpec((4, G, G), lambda b: (0, 0, 0)),
            pl.BlockSpec((1, POOL_W), lambda b: (0, 0)),
        ),
        input_output_aliases={4: 0},
        name="pool_bwd", compiler_params=_cp(("arbitrary",)),
    )(proj, dy, w_grp, scale, dproj)


def _mem_softmax(q, k):
    s = _dot(q, k, "nt") * (QK_DIM ** -0.5)
    e = jnp.exp(s - jnp.max(s, axis=-1, keepdims=True))
    return e / jnp.sum(e, axis=-1, keepdims=True)


def _mem_attn_fwd(proj, kv, B, S):
    T = B * S
    tq = _row_tile(S, 512)
    nq = S // tq

    def body(q_ref, kv_ref, o_ref):
        for h in range(N_HEADS):
            sl = slice(h * QK_DIM, (h + 1) * QK_DIM)
            a = _mem_softmax(q_ref[:, sl], kv_ref[:, sl])
            o_ref[:, sl] = _dot(a, kv_ref[:, MEMQ_W + h * QK_DIM:MEMQ_W + (h + 1) * QK_DIM]).astype(o_ref.dtype)

    return _call(
        body, out_shape=_sds((T, MEMQ_W), _ACT), grid=(B, nq),
        in_specs=[
            pl.BlockSpec((tq, MEMQ_W), lambda b, i: (b * nq + i, C_QM // MEMQ_W)),
            pl.BlockSpec((MEM_LEN, 2 * MEMQ_W), lambda b, i: (b, 0)),
        ],
        out_specs=pl.BlockSpec((tq, MEMQ_W), lambda b, i: (b * nq + i, 0)),
        name="mem_attn_fwd", compiler_params=_cp(("parallel", "parallel")),
    )(proj, kv)


def _mem_attn_bwd(proj, kv, d_o, dproj, B, S):
    tq = _row_tile(S, 512)
    nq = S // tq
    scale = QK_DIM ** -0.5

    def body(q_ref, kv_ref, do_ref, dproj_in, dq_ref, dkv_ref):
        i = pl.program_id(1)
        for h in range(N_HEADS):
            sl = slice(h * QK_DIM, (h + 1) * QK_DIM)
            vsl = slice(MEMQ_W + h * QK_DIM, MEMQ_W + (h + 1) * QK_DIM)
            q = q_ref[:, sl]
            a = _mem_softmax(q, kv_ref[:, sl])
            d = do_ref[:, sl]
            da = _dot(d, kv_ref[:, vsl], "nt")
            ds = a * (da - jnp.sum(a * da, axis=-1, keepdims=True)) * scale
            dq_ref[:, sl] = _dot(ds, kv_ref[:, sl]).astype(dq_ref.dtype)
            dk = _dot(ds, q, "tn")
            dv = _dot(a, d, "tn")

            @pl.when(i == 0)
            def _():
                dkv_ref[:, sl] = dk
                dkv_ref[:, vsl] = dv

            @pl.when(i > 0)
            def _():
                dkv_ref[:, sl] += dk
                dkv_ref[:, vsl] += dv

    return _call(
        body,
        out_shape=(_sds(dproj.shape, dproj.dtype), _sds((B * MEM_LEN, 2 * MEMQ_W), F32)),
        grid=(B, nq),
        in_specs=[
            pl.BlockSpec((tq, MEMQ_W), lambda b, i: (b * nq + i, C_QM // MEMQ_W)),
            pl.BlockSpec((MEM_LEN, 2 * MEMQ_W), lambda b, i: (b, 0)),
            pl.BlockSpec((tq, MEMQ_W), lambda b, i: (b * nq + i, 0)),
            ANY,
        ],
        out_specs=(
            pl.BlockSpec((tq, MEMQ_W), lambda b, i: (b * nq + i, C_QM // MEMQ_W)),
            pl.BlockSpec((MEM_LEN, 2 * MEMQ_W), lambda b, i: (b, 0)),
        ),
        input_output_aliases={3: 0},
        name="mem_attn_bwd", compiler_params=_cp(("parallel", "arbitrary")),
    )(proj, kv, d_o, dproj)


def _mem_norm(mem2d, g):
    M = mem2d.shape[0]
    tm = _row_tile(M, 512)

    def body(x_ref, g_ref, o_ref):
        o_ref[...] = _rms_rows(x_ref[...], g_ref[...]).astype(o_ref.dtype)

    return _call(
        body, out_shape=_sds((M, D_MODEL), _ACT), grid=(M // tm,),
        in_specs=[pl.BlockSpec((tm, D_MODEL), lambda i: (i, 0)), pl.BlockSpec((1, D_MODEL), lambda i: (0, 0))],
        out_specs=pl.BlockSpec((tm, D_MODEL), lambda i: (i, 0)),
        name="mem_norm", compiler_params=_cp(("parallel",)),
    )(mem2d, g)


def _mem_norm_wgrad(mem2d, d_memn):
    M = mem2d.shape[0]
    tm = _row_tile(M, 512)

    def body(x_ref, d_ref, dg_ref):
        i = pl.program_id(0)
        x = x_ref[...]
        xh = x * lax.rsqrt(jnp.mean(x * x, axis=-1, keepdims=True) + EPS)
        dg = jnp.sum(d_ref[...] * xh, axis=0, keepdims=True)

        @pl.when(i == 0)
        def _():
            dg_ref[...] = dg

        @pl.when(i > 0)
        def _():
            dg_ref[...] += dg

    return _call(
        body, out_shape=_sds((1, D_MODEL), F32), grid=(M // tm,),
        in_specs=[pl.BlockSpec((tm, D_MODEL), lambda i: (i, 0)), pl.BlockSpec((tm, D_MODEL), lambda i: (i, 0))],
        out_specs=pl.BlockSpec((1, D_MODEL), lambda i: (0, 0)),
        name="mem_norm_wgrad", compiler_params=_cp(("arbitrary",)),
    )(mem2d, d_memn)


def _row_mm(a, w_ref):
    ks = w_ref.shape[1]
    out = _dot(a[:, 0:ks], w_ref[0])
    for k in range(1, N_CHIPS):
        out += _dot(a[:, k * ks:(k + 1) * ks], w_ref[k])
    return out


def _row_mm_t(d, w_ref):
    return jnp.concatenate([_dot(d, w_ref[k], "nt") for k in range(N_CHIPS)], axis=1)


def _col_mm(a, w_ref):
    return jnp.concatenate([_dot(a, w_ref[k]) for k in range(N_CHIPS)], axis=1)


def _col_mm_t(d, w_ref):
    ns = w_ref.shape[2]
    out = _dot(d[:, 0:ns], w_ref[0], "nt")
    for k in range(1, N_CHIPS):
        out += _dot(d[:, k * ns:(k + 1) * ns], w_ref[k], "nt")
    return out


def _full_spec(w):
    nd = w.ndim
    return pl.BlockSpec(w.shape, lambda i: (0,) * nd)


def _merge_fwd(x, proj, y_r, y_p, o_m, w_ret_o, w_pool_o, w_mem_o, w_out):
    T = x.shape[0]
    tm = _row_tile(T, 256)

    def body(x_ref, gr_ref, gp_ref, gm_ref, yr_ref, yp_ref, om_ref, wr_ref, wp_ref, wm_ref, wo_ref,
             x1_ref, mg_ref, o3_ref):
        o_r = _row_mm(yr_ref[...], wr_ref)
        o_p = _col_mm(yp_ref[...], wp_ref)
        o_q = _col_mm(om_ref[...], wm_ref)
        merged = (jax.nn.sigmoid(gr_ref[...].astype(F32)) * o_r + jax.nn.sigmoid(gp_ref[...].astype(F32)) * o_p
                  + jax.nn.sigmoid(gm_ref[...].astype(F32)) * o_q)
        mg = merged.astype(mg_ref.dtype)
        mg_ref[...] = mg
        o3_ref[:, 0:D_MODEL] = o_r.astype(o3_ref.dtype)
        o3_ref[:, D_MODEL:2 * D_MODEL] = o_p.astype(o3_ref.dtype)
        o3_ref[:, 2 * D_MODEL:3 * D_MODEL] = o_q.astype(o3_ref.dtype)
        x1_ref[...] = x_ref[...] + _row_mm(mg, wo_ref)

    gb = C_GATE // D_MODEL
    row = lambda w: pl.BlockSpec((tm, w), lambda i: (i, 0))
    return _call(
        body,
        out_shape=(_sds((T, D_MODEL), F32), _sds((T, D_MODEL), _ACT), _sds((T, 3 * D_MODEL), _ACT)),
        grid=(T // tm,),
        in_specs=[
            row(D_MODEL),
            pl.BlockSpec((tm, D_MODEL), lambda i: (i, gb)),
            pl.BlockSpec((tm, D_MODEL), lambda i: (i, gb + 1)),
            pl.BlockSpec((tm, D_MODEL), lambda i: (i, gb + 2)),
            row(D_MODEL), row(POOL_W), row(MEMQ_W),
            _full_spec(w_ret_o), _full_spec(w_pool_o), _full_spec(w_mem_o), _full_spec(w_out),
        ],
        out_specs=(row(D_MODEL), row(D_MODEL), row(3 * D_MODEL)),
        name="merge_fwd", compiler_params=_cp(("parallel",)),
    )(x, proj, proj, proj, y_r, y_p, o_m, w_ret_o, w_pool_o, w_mem_o, w_out)


def _merge_bwd(dx1, proj, o3, w_ret_o, w_pool_o, w_mem_o, w_out):
    T = dx1.shape[0]
    tm = _row_tile(T, 256)

    def body(dx_ref, gr_ref, gp_ref, gm_ref, o3_ref, wr_ref, wp_ref, wm_ref, wo_ref,
             do3_ref, dgate_ref, dyr_ref, dyp_ref, dom_ref):
        dmerged = _row_mm_t(dx_ref[...], wo_ref)
        douts = []
        dgate_ref[:, 0:MEMQ_W] = jnp.zeros((tm, MEMQ_W), dgate_ref.dtype)
        for n, g_ref in enumerate((gr_ref, gp_ref, gm_ref)):
            sl = slice(n * D_MODEL, (n + 1) * D_MODEL)
            gate = jax.nn.sigmoid(g_ref[...].astype(F32))
            d_out = (dmerged * gate).astype(do3_ref.dtype)
            do3_ref[:, sl] = d_out
            dgate_ref[:, MEMQ_W + n * D_MODEL:MEMQ_W + (n + 1) * D_MODEL] = (
                dmerged * o3_ref[:, sl].astype(F32) * gate * (1.0 - gate)).astype(dgate_ref.dtype)
            douts.append(d_out)
        dyr_ref[...] = _row_mm_t(douts[0], wr_ref).astype(dyr_ref.dtype)
        dyp_ref[...] = _col_mm_t(douts[1], wp_ref).astype(dyp_ref.dtype)
        dom_ref[...] = _col_mm_t(douts[2], wm_ref).astype(dom_ref.dtype)

    gb = C_GATE // D_MODEL
    row = lambda w: pl.BlockSpec((tm, w), lambda i: (i, 0))
    return _call(
        body,
        out_shape=(
            _sds((T, 3 * D_MODEL), _ACT), _sds((T, N_PROJ), _ACT),
            _sds((T, D_MODEL), _ACT), _sds((T, POOL_W), _ACT), _sds((T, MEMQ_W), _ACT),
        ),
        grid=(T // tm,),
        in_specs=[
            row(D_MODEL),
            pl.BlockSpec((tm, D_MODEL), lambda i: (i, gb)),
            pl.BlockSpec((tm, D_MODEL), lambda i: (i, gb + 1)),
            pl.BlockSpec((tm, D_MODEL), lambda i: (i, gb + 2)),
            row(3 * D_MODEL),
            _full_spec(w_ret_o), _full_spec(w_pool_o), _full_spec(w_mem_o), _full_spec(w_out),
        ],
        out_specs=(row(3 * D_MODEL), pl.BlockSpec((tm, N_PROJ - C_QM), lambda i: (i, 1)),
                   row(D_MODEL), row(POOL_W), row(MEMQ_W)),
        name="merge_bwd", compiler_params=_cp(("parallel",)),
    )(dx1, proj, proj, proj, o3, w_ret_o, w_pool_o, w_mem_o, w_out)


def _loss_head(x, g, target):
    T = x.shape[0]
    tm = _row_tile(T, 512)

    def body(x_ref, g_ref, t_ref, dx_ref, sq_ref, dg_ref):
        i = pl.program_id(0)
        x = x_ref[...]
        gg = g_ref[...]
        r = lax.rsqrt(jnp.mean(x * x, axis=-1, keepdims=True) + EPS)
        xh = x * r
        err = xh * gg - t_ref[...]
        dy = err * (1.0 / D_MODEL)
        dxh = dy * gg
        dx_ref[...] = r * (dxh - xh * jnp.mean(dxh * xh, axis=-1, keepdims=True))
        sq = jnp.sum(err * err, axis=0, keepdims=True)
        dg = jnp.sum(dy * xh, axis=0, keepdims=True)

        @pl.when(i == 0)
        def _():
            sq_ref[...] = sq
            dg_ref[...] = dg

        @pl.when(i > 0)
        def _():
            sq_ref[...] += sq
            dg_ref[...] += dg

    vec = pl.BlockSpec((1, D_MODEL), lambda i: (0, 0))
    row = pl.BlockSpec((tm, D_MODEL), lambda i: (i, 0))
    return _call(
        body,
        out_shape=(_sds((T, D_MODEL), F32), _sds((1, D_MODEL), F32), _sds((1, D_MODEL), F32)),
        grid=(T // tm,), in_specs=[row, vec, row], out_specs=(row, vec, vec),
        name="loss_head", compiler_params=_cp(("arbitrary",)),
    )(x, g, target)


def _block_rows(rows, cols, itemsize, cap_bytes=2 << 20):
    t = rows
    while t * cols * itemsize > cap_bytes and t % 2 == 0 and (t // 2) % 16 == 0:
        t //= 2
    return t


def _cast_into_slot(w3d, layer, me):
    _, R, C = w3d.shape
    tr = _block_rows(R, C, 4)

    def body(me_ref, w_ref, o_ref):
        o_ref[...] = w_ref[...].astype(o_ref.dtype)

    grid_spec = pltpu.PrefetchScalarGridSpec(
        num_scalar_prefetch=1, grid=(R // tr,),
        in_specs=[pl.BlockSpec((None, tr, C), lambda i, me: (layer, i, 0))],
        out_specs=pl.BlockSpec((None, tr, C), lambda i, me: (me[0], i, 0)),
    )
    return _call(body, out_shape=_sds((N_CHIPS, R, C), _COMM), grid_spec=grid_spec,
                 name="cast_into_slot", compiler_params=_cp(("parallel",)))(me, w3d)


def _adamw(w, g, m, v):
    R, C = w.shape
    tr = _block_rows(R, C, 4, 1 << 20)

    def body(w_ref, g_ref, m_ref, v_ref, d_ref, nm_ref, nv_ref):
        g = g_ref[...]
        m = ADAM_B1 * m_ref[...] + (1.0 - ADAM_B1) * g
        v = ADAM_B2 * v_ref[...] + (1.0 - ADAM_B2) * (g * g)
        m_hat = m / (1.0 - ADAM_B1 ** ADAM_STEP)
        v_hat = v / (1.0 - ADAM_B2 ** ADAM_STEP)
        d_ref[...] = -ADAM_LR * (m_hat / (jnp.sqrt(v_hat) + ADAM_EPS) + ADAM_WD * w_ref[...])
        nm_ref[...] = m
        nv_ref[...] = v

    spec = pl.BlockSpec((tr, C), lambda i: (i, 0))
    out = _sds((R, C), F32)
    return _call(body, out_shape=(out, out, out), grid=(R // tr,), in_specs=[spec] * 4, out_specs=(spec,) * 3,
                 name="adamw", compiler_params=_cp(("parallel",)))(w, g, m, v)


def _add_sibling_half(g_full, land, my_c):
    _, R, C = g_full.shape
    hr = R // 2
    tr = _block_rows(hr, C, 2, 1 << 20)
    nb = hr // tr

    def body(c_ref, g_ref, l_ref, o_ref):
        o_ref[...] = (g_ref[...].astype(F32) + l_ref[...].astype(F32)).astype(o_ref.dtype)

    grid_spec = pltpu.PrefetchScalarGridSpec(
        num_scalar_prefetch=1, grid=(N_CHIPS, nb),
        in_specs=[
            pl.BlockSpec((None, tr, C), lambda k, i, c: (k, c[0] * nb + i, 0)),
            pl.BlockSpec((None, tr, C), lambda k, i, c: (k, i, 0)),
        ],
        out_specs=pl.BlockSpec((None, tr, C), lambda k, i, c: (k, i, 0)),
    )
    return _call(body, out_shape=_sds((N_CHIPS, hr, C), _COMM), grid_spec=grid_spec,
                 name="add_sibling_half", compiler_params=_cp(("parallel", "parallel")))(my_c, g_full, land)


def _add_chips(land, sums, g_all, layer, my_chip, my_core):
    _, hr, C = land.shape
    tr = _block_rows(hr, C, 4, 1 << 20)
    nb = hr // tr

    def body(me_ref, core_ref, own_ref, a_ref, b_ref, c_ref, *rest):
        o_ref = rest[-1]
        o_ref[...] = ((own_ref[...].astype(F32) + a_ref[...].astype(F32)) + b_ref[...].astype(F32)) + c_ref[...].astype(F32)

    def other(d):
        return pl.BlockSpec((None, tr, C), lambda i, me, core: ((me[0] + d) % N_CHIPS, i, 0))

    operands = [my_chip, my_core, sums, land, land, land]
    in_specs = [other(0), other(1), other(2), other(3)]
    aliases = {}
    if g_all is not None:
        operands.append(g_all)
        in_specs.append(ANY)
        aliases = {len(operands) - 1: 0}
    grid_spec = pltpu.PrefetchScalarGridSpec(
        num_scalar_prefetch=2, grid=(nb,), in_specs=in_specs,
        out_specs=pl.BlockSpec((None, tr, C), lambda i, me, core: (layer, core[0] * nb + i, 0)),
    )
    return _call(body, out_shape=_sds((DEPTH, 2 * hr, C), F32), grid_spec=grid_spec, input_output_aliases=aliases,
                 name="add_chips", compiler_params=_cp(("parallel",)))(*operands)


def _place():
    x, y, c = lax.axis_index("x"), lax.axis_index("y"), lax.axis_index("c")
    chips = [(1 - x, y), (x, 1 - y), (1 - x, 1 - y)]
    return x, y, c, 2 * x + y, chips


def _remote(src, dst, send_sem, recv_sem, dev):
    return pltpu.make_async_remote_copy(src_ref=src, dst_ref=dst, send_sem=send_sem, recv_sem=recv_sem,
                                        device_id=dev, device_id_type=MESH)


def _gather_sems(n):
    return [pltpu.SemaphoreType.DMA((n, 6)), pltpu.SemaphoreType.DMA((n, 6))]


def _gather_start(bufs, send_sems, recv_sems):
    x, y, c, me, chips = _place()
    for w, buf in enumerate(bufs):
        hr = buf.shape[1] // 2
        own = buf.at[me, pl.ds(c * hr, hr)]
        for j, chip in enumerate(chips):
            _remote(own, own, send_sems.at[w, j], recv_sems.at[w, j], (*chip, c)).start()


def _gather_finish(bufs, send_sems, recv_sems):
    x, y, c, me, chips = _place()
    sibling = (x, y, 1 - c)
    waits = []
    for w, buf in enumerate(bufs):
        hr = buf.shape[1] // 2
        mine = pl.ds(c * hr, hr)
        for j, chip in enumerate(chips):
            kc = 2 * chip[0] + chip[1]
            got = buf.at[kc, mine]
            first = _remote(got, got, send_sems.at[w, j], recv_sems.at[w, j], (*chip, c))
            first.wait_recv()
            fwd = _remote(got, got, send_sems.at[w, 3 + j], recv_sems.at[w, 3 + j], sibling)
            fwd.start()
            waits += [first.wait_send, fwd.wait_send]
    for w, buf in enumerate(bufs):
        hr = buf.shape[1] // 2
        theirs = pl.ds((1 - c) * hr, hr)
        for j, chip in enumerate(chips):
            kc = 2 * chip[0] + chip[1]
            got = buf.at[kc, theirs]
            _remote(got, got, send_sems.at[w, 3 + j], recv_sems.at[w, 3 + j], sibling).wait_recv()
    for wait in waits:
        wait()


def _allgather_weights(bufs):
    n = len(bufs)

    def body(*refs):
        outs = refs[n:2 * n]
        send_sems, recv_sems = refs[2 * n:]
        _gather_start(outs, send_sems, recv_sems)
        _gather_finish(outs, send_sems, recv_sems)

    out_shape = tuple(_sds(b.shape, b.dtype) for b in bufs)
    return _call(
        body, out_shape=out_shape, in_specs=[ANY] * n, out_specs=(ANY,) * n,
        input_output_aliases={i: i for i in range(n)},
        scratch_shapes=_gather_sems(n),
        name="allgather_weights",
    )(*bufs)


def _swap_sibling_halves(grads):
    n = len(grads)

    def body(*refs):
        ins, outs = refs[:n], refs[n:2 * n]
        send_sems, recv_sems = refs[2 * n:]
        x, y, c, _, _ = _place()
        copies = []
        for w in range(n):
            hr = ins[w].shape[1] // 2
            cp = _remote(ins[w].at[:, pl.ds((1 - c) * hr, hr)], outs[w], send_sems.at[w], recv_sems.at[w], (x, y, 1 - c))
            cp.start()
            copies.append(cp)
        for cp in copies:
            cp.wait()

    out_shape = tuple(_sds((N_CHIPS, g.shape[1] // 2, g.shape[2]), g.dtype) for g in grads)
    return _call(
        body, out_shape=out_shape, in_specs=[ANY] * n, out_specs=(ANY,) * n,
        scratch_shapes=[pltpu.SemaphoreType.DMA((n,)), pltpu.SemaphoreType.DMA((n,))],
        name="swap_sibling_halves",
    )(*grads)


def _scatter_sems(n):
    return [pltpu.SemaphoreType.DMA((n, 3)), pltpu.SemaphoreType.DMA((n, 3))]


def _scatter_shapes(sums):
    return tuple(_sds(s.shape, s.dtype) for s in sums)


def _scatter_start(sums, parts, send_sems, recv_sems):
    x, y, c, me, chips = _place()
    for w in range(len(sums)):
        for j, chip in enumerate(chips):
            kc = 2 * chip[0] + chip[1]
            _remote(sums[w].at[kc], parts[w].at[me], send_sems.at[w, j], recv_sems.at[w, j], (*chip, c)).start()


def _scatter_finish(sums, parts, send_sems, recv_sems):
    x, y, c, me, chips = _place()
    for w in range(len(sums)):
        for j, chip in enumerate(chips):
            kc = 2 * chip[0] + chip[1]
            _remote(sums[w].at[kc], parts[w].at[kc], send_sems.at[w, j], recv_sems.at[w, j], (*chip, c)).wait()


def _join_sibling_halves(g_alls, layer):
    n = len(g_alls)

    def body(*refs):
        outs = refs[n:2 * n]
        send_sems, recv_sems = refs[2 * n:]
        x, y, c, _, _ = _place()
        copies = []
        for w in range(n):
            hr = outs[w].shape[1] // 2
            mine = outs[w].at[layer, pl.ds(c * hr, hr)]
            cp = _remote(mine, mine, send_sems.at[w], recv_sems.at[w], (x, y, 1 - c))
            cp.start()
            copies.append(cp)
        for cp in copies:
            cp.wait()

    out_shape = tuple(_sds(g.shape, g.dtype) for g in g_alls)
    return _call(
        body, out_shape=out_shape, in_specs=[ANY] * n, out_specs=(ANY,) * n,
        input_output_aliases={i: i for i in range(n)},
        scratch_shapes=[pltpu.SemaphoreType.DMA((n,)), pltpu.SemaphoreType.DMA((n,))],
        name="join_sibling_halves",
    )(*g_alls)


def _chip_sums(grads, my_core):
    land = _swap_sibling_halves(grads)
    return [_add_sibling_half(g, l, my_core) for g, l in zip(grads, land)]


def _finish_reduce_scatter(parts, sums, g_alls, layer, my_chip, my_core):
    g_alls = [_add_chips(p, s, ga, layer, my_chip, my_core) for p, s, ga in zip(parts, sums, g_alls)]
    return _join_sibling_halves(g_alls, layer)


def _allreduce_small(v):
    R = v.shape[0]

    def body(v_ref, out_ref, sib_ref, chip_ref, sum_ref, send_sems, recv_sems):
        x, y, c, me, chips = _place()
        swap = _remote(v_ref, sib_ref, send_sems.at[0], recv_sems.at[0], (x, y, 1 - c))
        swap.start()
        swap.wait()
        sum_ref[...] = v_ref[...] + sib_ref[...]
        copies = []
        for j, chip in enumerate(chips):
            cp = _remote(sum_ref, chip_ref.at[me], send_sems.at[1 + j], recv_sems.at[1 + j], (*chip, c))
            cp.start()
            copies.append(cp)
        chip_ref[me] = sum_ref[...]
        for cp in copies:
            cp.wait()
        out_ref[...] = ((chip_ref[0] + chip_ref[1]) + chip_ref[2]) + chip_ref[3]

    vm = pl.BlockSpec(memory_space=pltpu.VMEM)
    return _call(
        body, out_shape=_sds((R, 128), F32), in_specs=[vm], out_specs=vm,
        scratch_shapes=[
            pltpu.VMEM((R, 128), F32), pltpu.VMEM((N_CHIPS, R, 128), F32), pltpu.VMEM((R, 128), F32),
            pltpu.SemaphoreType.DMA((4,)), pltpu.SemaphoreType.DMA((4,)),
        ],
        name="allreduce_small", compiler_params=_cp(),
    )(v)


def _pack_small(parts):
    rows = []
    for p in parts:
        flat = p.reshape(-1).astype(F32)
        pad = (-flat.shape[0]) % 128
        rows.append(jnp.pad(flat, (0, pad)).reshape(-1, 128))
    packed = jnp.concatenate(rows, axis=0)
    pad_rows = (-packed.shape[0]) % 8
    return jnp.pad(packed, ((0, pad_rows), (0, 0)))


def _unpack_small(packed, like):
    out, r = [], 0
    for p in like:
        n = p.size
        nr = -(-n // 128)
        out.append(packed[r:r + nr].reshape(-1)[:n].reshape(p.shape))
        r += nr
    return out


BIG = ("w_in", "w_ret_o", "w_pool_o", "w_mem_kv", "w_mem_o", "w_out", "w_ff1", "w_ff2")


def _rope_tables(S):
    inv = ROPE_BASE ** (-jnp.arange(0, QK_DIM, 2, dtype=F32) / QK_DIM)
    ang = jnp.arange(S).astype(F32)[:, None] * inv[None, :]
    cos, sin = jnp.cos(ang), jnp.sin(ang)
    return jnp.concatenate([cos, cos], axis=1), jnp.concatenate([-sin, sin], axis=1)


MIXER_W = ("w_ret_o", "w_pool_o", "w_mem_kv", "w_mem_o", "w_out")
AFTER_IN = MIXER_W + ("w_ff1", "w_ff2")
GATHER_PLAN = {
    ("in_proj", 0): [(0, AFTER_IN)],
    ("retention", 0): [(1, ("w_in",) + MIXER_W)],
    ("ff1", 0): [(1, ("w_ff1",))],
    ("in_proj", 1): [(1, ("w_ff2",)), (2, ("w_in",))],
    ("retention", 1): [(2, AFTER_IN)],
    ("ff1", 1): [(3, ("w_in",))],
    ("in_proj", 2): [(3, AFTER_IN)],
}


def _layer_fwd(x, layer, wbuf, small, memn, cos, sin, B, S):
    g1, g2, lg, w_grp, scale = small

    def hosted(host):
        keys = [(l2, n) for l2, names in GATHER_PLAN.get((host, layer), ()) for n in names]
        return keys, [wbuf[k] for k in keys]

    W = lambda n: wbuf[(layer, n)]
    keys, bufs = hosted("in_proj")
    proj, h1, bufs = _rms_proj("in_proj", x, g1, W("w_in"), N_PROJ, gather=bufs)
    wbuf.update(zip(keys, bufs))
    keys, bufs = hosted("retention")
    o, y_r, bufs = _retention_fwd(proj, cos, sin, lg, B, S, gather=bufs)
    wbuf.update(zip(keys, bufs))
    y_p = _pool_fwd(proj, w_grp, scale, B, S)
    kv = _rowsharded_mm("mem_kv", memn, W("w_mem_kv"), None)
    o_m = _mem_attn_fwd(proj, kv, B, S)
    x1, merged, o3 = _merge_fwd(x, proj, y_r, y_p, o_m, W("w_ret_o"), W("w_pool_o"), W("w_mem_o"), W("w_out"))
    keys, bufs = hosted("ff1")
    u, h2, bufs = _rms_proj("ff1", x1, g2, W("w_ff1"), FFN_HIDDEN, gather=bufs)
    wbuf.update(zip(keys, bufs))
    x2 = _ffn_out(x1, u, W("w_ff2"))
    saved = dict(x=x, proj=proj, h1=h1, o=o, y_r=y_r, y_p=y_p, kv=kv, o_m=o_m, merged=merged, o3=o3, x1=x1, h2=h2, u=u)
    return x2, saved


EARLY_GRADS = ("w_ff1", "w_ff2", "w_out", "w_ret_o", "w_pool_o", "w_mem_o")
LATE_GRADS = ("w_in", "w_mem_kv")


def _layer_bwd(dx2, sv, W, small, memn, d_memn, cos, sin, B, S, big_grads, layer, my_chip, my_core):
    g1, g2, lg, w_grp, scale = small
    x, proj, x1, u = sv["x"], sv["proj"], sv["x1"], sv["u"]
    grads = {}
    du = _ffn_out_dgrad(dx2, u, W["w_ff2"])
    grads["w_ff2"] = _rowsharded_wgrad("ff2_wgrad", u, dx2, FFN_HIDDEN, pro=_relu2)
    dx1, dg2, _ = _dgrad_rms("ff1_dgrad", du, W["w_ff1"], x1, g2, dx2)
    grads["w_ff1"] = _colsharded_wgrad_full("ff1_wgrad", sv["h2"], du, FFN_HIDDEN, 1024)
    do3, dproj, dy_r, dy_p, do_m = _merge_bwd(dx1, proj, sv["o3"], W["w_ret_o"], W["w_pool_o"], W["w_mem_o"], W["w_out"])
    grads["w_out"], grads["w_ret_o"], grads["w_pool_o"], grads["w_mem_o"] = _mixer_wgrads(
        sv["merged"], sv["y_r"], sv["y_p"], sv["o_m"], dx1, do3)
    d_o, dproj = _gn_gate_bwd(dy_r, sv["o"], proj, dproj)
    sums = _chip_sums([grads[n] for n in EARLY_GRADS], my_core)
    dproj, dlg, parts = _retention_bwd(proj, d_o, dproj, cos, sin, lg, B, S, scatter=sums)
    done = _finish_reduce_scatter(parts, sums, [big_grads[n] for n in EARLY_GRADS], layer, my_chip, my_core)
    big_grads.update(zip(EARLY_GRADS, done))
    dproj, dw_grp, dscale = _pool_bwd(proj, dy_p, dproj, w_grp, scale, B, S)
    dproj, dkv = _mem_attn_bwd(proj, sv["kv"], do_m, dproj, B, S)
    grads["w_mem_kv"] = _rowsharded_wgrad("mem_kv_wgrad", memn, dkv, D_MODEL)
    if d_memn is None:
        d_memn = _rowsharded_dgrad("mem_kv_dgrad", dkv, W["w_mem_kv"], out_dtype=F32)
    else:
        d_memn = _rowsharded_dgrad("mem_kv_dgrad_acc", dkv, W["w_mem_kv"], epi=_add_res, epi_in=d_memn, out_dtype=F32)
    grads["w_in"] = _colsharded_wgrad_full("in_wgrad", sv["h1"], dproj, N_PROJ, 1792)
    sums = _chip_sums([grads[n] for n in LATE_GRADS], my_core)
    dx, dg1, parts = _dgrad_rms("in_dgrad", dproj, W["w_in"], x, g1, dx1, scatter=sums)
    done = _finish_reduce_scatter(parts, sums, [big_grads[n] for n in LATE_GRADS], layer, my_chip, my_core)
    big_grads.update(zip(LATE_GRADS, done))
    small_grads = dict(g1=dg1, g2=dg2, lg=jnp.sum(dlg, axis=(0, 2)).reshape(N_HEADS, 2).T, w_grp=dw_grp, scale=dscale)
    return dx, small_grads, d_memn


def kernel(x, mem, w_in, ret_decay_logit, w_ret_o, w_pool_grp, pool_scale, w_pool_o, w_mem_kv, w_mem_o, w_out, w_ff1, w_ff2, norm1_g, norm2_g, mem_norm_g, final_norm_g, loss_target, m_w_in, m_ret_decay_logit, m_w_ret_o, m_w_pool_grp, m_pool_scale, m_w_pool_o, m_w_mem_kv, m_w_mem_o, m_w_out, m_w_ff1, m_w_ff2, m_norm1_g, m_norm2_g, m_mem_norm_g, m_final_norm_g, v_w_in, v_ret_decay_logit, v_w_ret_o, v_w_pool_grp, v_pool_scale, v_w_pool_o, v_w_mem_kv, v_w_mem_o, v_w_out, v_w_ff1, v_w_ff2, v_norm1_g, v_norm2_g, v_mem_norm_g, v_final_norm_g):
    B, S, _ = x.shape
    T = B * S
    big_w = dict(w_in=w_in, w_ret_o=w_ret_o, w_pool_o=w_pool_o, w_mem_kv=w_mem_kv, w_mem_o=w_mem_o, w_out=w_out, w_ff1=w_ff1, w_ff2=w_ff2)
    big_m = dict(w_in=m_w_in, w_ret_o=m_w_ret_o, w_pool_o=m_w_pool_o, w_mem_kv=m_w_mem_kv, w_mem_o=m_w_mem_o, w_out=m_w_out, w_ff1=m_w_ff1, w_ff2=m_w_ff2)
    big_v = dict(w_in=v_w_in, w_ret_o=v_w_ret_o, w_pool_o=v_w_pool_o, w_mem_kv=v_w_mem_kv, w_mem_o=v_w_mem_o, w_out=v_w_out, w_ff1=v_w_ff1, w_ff2=v_w_ff2)
    small_w = [ret_decay_logit, w_pool_grp, pool_scale, norm1_g, norm2_g, mem_norm_g, final_norm_g]
    small_m = [m_ret_decay_logit, m_w_pool_grp, m_pool_scale, m_norm1_g, m_norm2_g, m_mem_norm_g, m_final_norm_g]
    small_v = [v_ret_decay_logit, v_w_pool_grp, v_pool_scale, v_norm1_g, v_norm2_g, v_mem_norm_g, v_final_norm_g]
    my_chip = (2 * lax.axis_index("x") + lax.axis_index("y")).astype(jnp.int32).reshape(1)
    my_core = lax.axis_index("c").astype(jnp.int32).reshape(1)

    wbuf = {(l, n): _cast_into_slot(big_w[n], l, my_chip) for l in range(DEPTH) for n in BIG}
    wbuf[(0, "w_in")] = _allgather_weights([wbuf[(0, "w_in")]])[0]

    cos, sin = _rope_tables(S)
    log_g = jax.nn.log_sigmoid(ret_decay_logit.astype(F32))
    mem2d = mem.reshape(B * MEM_LEN, D_MODEL)
    memn = _mem_norm(mem2d, mem_norm_g.reshape(1, D_MODEL))
    smalls = [(norm1_g[l].reshape(1, D_MODEL), norm2_g[l].reshape(1, D_MODEL), log_g[l], w_pool_grp[l],
               pool_scale[l].reshape(1, POOL_W)) for l in range(DEPTH)]

    h = x.reshape(T, D_MODEL)
    saved = []
    for l in range(DEPTH):
        h, sv = _layer_fwd(h, l, wbuf, smalls[l], memn, cos, sin, B, S)
        saved.append(sv)
    weights = [{n: wbuf[(l, n)] for n in BIG} for l in range(DEPTH)]

    dh, sq, d_final_g = _loss_head(h, final_norm_g.reshape(1, D_MODEL), loss_target.reshape(T, D_MODEL))
    loss = lax.psum(0.5 * jnp.sum(sq) / D_MODEL, ("x", "y", "c"))

    big_grads = dict.fromkeys(BIG)
    small_grads = [None] * DEPTH
    d_memn = None
    for l in reversed(range(DEPTH)):
        dh, small_grads[l], d_memn = _layer_bwd(
            dh, saved[l], weights[l], smalls[l], memn, d_memn, cos, sin, B, S, big_grads, l, my_chip, my_core)
    d_mem_g = _mem_norm_wgrad(mem2d, d_memn)

    d_logit = jnp.stack([sg["lg"] for sg in small_grads]) * jax.nn.sigmoid(-ret_decay_logit.astype(F32))
    small_g_local = [
        d_logit, jnp.stack([sg["w_grp"] for sg in small_grads]),
        jnp.stack([sg["scale"].reshape(POOL_W) for sg in small_grads]),
        jnp.stack([sg["g1"].reshape(D_MODEL) for sg in small_grads]),
        jnp.stack([sg["g2"].reshape(D_MODEL) for sg in small_grads]),
        d_mem_g.reshape(D_MODEL), d_final_g.reshape(D_MODEL),
    ]
    small_g = _allreduce_small(_pack_small(small_g_local))
    s_delta, s_m, s_v = _adamw(_pack_small(small_w), small_g, _pack_small(small_m), _pack_small(small_v))
    small_g, s_delta, s_m, s_v = (_unpack_small(a, small_w) for a in (small_g, s_delta, s_m, s_v))

    big_out = {}
    for n in BIG:
        w = big_w[n]
        g = big_grads[n]
        flat = lambda a: a.reshape(-1, a.shape[-1])
        d, nm, nv = _adamw(flat(w), flat(g), flat(big_m[n]), flat(big_v[n]))
        big_out[n] = (g, d.reshape(w.shape), nm.reshape(w.shape), nv.reshape(w.shape))

    order = ["w_in", "ret_decay_logit", "w_ret_o", "w_pool_grp", "pool_scale", "w_pool_o", "w_mem_kv", "w_mem_o",
             "w_out", "w_ff1", "w_ff2", "norm1_g", "norm2_g", "mem_norm_g", "final_norm_g"]
    small_names = ["ret_decay_logit", "w_pool_grp", "pool_scale", "norm1_g", "norm2_g", "mem_norm_g", "final_norm_g"]
    outs = [[], [], [], []]
    for n in order:
        if n in big_out:
            vals = big_out[n]
        else:
            i = small_names.index(n)
            vals = (small_g[i], s_delta[i], s_m[i], s_v[i])
        for k in range(4):
            outs[k].append(vals[k])
    return (loss, dh.reshape(B, S, D_MODEL), *outs[0], *outs[1], *outs[2], *outs[3])
```

```python
import functools

import jax
import jax.numpy as jnp
from jax import lax
from jax.experimental import pallas as pl
from jax.experimental.pallas import tpu as pltpu

F32 = jnp.float32
_MXU = jnp.bfloat16
_ACT = jnp.bfloat16
_COMM = jnp.bfloat16

D_MODEL = 1024
N_PROJ = 7168
FFN_HIDDEN = 4096
MEM_LEN = 256
N_HEADS = 4
QK_DIM = 128
V_DIM = 256
POOL_W = 512
MEMQ_W = 512
DEPTH = 4
N_CHIPS = 4
EPS = 1e-6
ROPE_BASE = 10000.0

C_Q, C_K, C_V, C_G, C_P, C_QM, C_GATE = 0, 512, 1024, 2048, 3072, 3584, 4096

ADAM_LR = 0.001
ADAM_B1 = 0.9
ADAM_B2 = 0.999
ADAM_EPS = 1e-08
ADAM_WD = 0.01
ADAM_STEP = 10

VMEM_LIMIT_BYTES = 56 * 1024 * 1024
MESH = pl.DeviceIdType.MESH
ANY = pl.BlockSpec(memory_space=pl.ANY)

_DN = {
    "nn": (((1,), (0,)), ((), ())),
    "nt": (((1,), (1,)), ((), ())),
    "tn": (((0,), (0,)), ((), ())),
}


def _call(body, **kw):
    return pl.pallas_call(body, **kw)


def _cp(sem=None):
    return pltpu.CompilerParams(dimension_semantics=sem, vmem_limit_bytes=VMEM_LIMIT_BYTES)


def _dot(a, b, kind="nn"):
    return lax.dot_general(a.astype(_MXU), b.astype(_MXU), _DN[kind], preferred_element_type=F32)


def _sds(shape, dtype):
    return jax.ShapeDtypeStruct(shape, dtype)


def _rms_rows(x, g):
    r = lax.rsqrt(jnp.mean(x * x, axis=-1, keepdims=True) + EPS)
    return x * r * g


def _relu2(u):
    r = jnp.maximum(u.astype(F32), 0.0)
    return r * r


def _add_res(r, e):
    return r + e.astype(F32)


def _relu2_bwd(r, u):
    return r * (2.0 * jnp.maximum(u.astype(F32), 0.0))


def _mm(name, kind, a, b, a_spec, b_spec, out_shape, o_spec, grid, acc_shape, *,
        pro=None, pro_in=(), pro_specs=(), epi=None, epi_in=(), epi_specs=()):
    nk = grid[2]
    npro, nepi = len(pro_in), len(epi_in)

    def body(*refs):
        a_ref, b_ref = refs[0], refs[1]
        pro_refs = refs[2:2 + npro]
        epi_refs = refs[2 + npro:2 + npro + nepi]
        o_ref = refs[2 + npro + nepi]
        av = a_ref[...]
        if pro is not None:
            av = pro(av, *[r[...] for r in pro_refs])
        part = _dot(av, b_ref[...], kind)

        def finish(r):
            if epi is not None:
                r = epi(r, *[e[...] for e in epi_refs])
            o_ref[...] = r.astype(o_ref.dtype)

        if nk == 1:
            finish(part)
        else:
            acc = refs[-1]
            k = pl.program_id(2)

            @pl.when(k == 0)
            def _():
                acc[...] = part

            @pl.when(k > 0)
            def _():
                acc[...] += part

            @pl.when(k == nk - 1)
            def _():
                finish(acc[...])

    scratch = [] if nk == 1 else [pltpu.VMEM(acc_shape, F32)]
    return _call(
        body, out_shape=out_shape, grid=grid,
        in_specs=[a_spec, b_spec, *pro_specs, *epi_specs], out_specs=o_spec,
        scratch_shapes=scratch, name=name,
        compiler_params=_cp(("parallel", "parallel", "arbitrary")),
    )(a, b, *pro_in, *epi_in)


def _row_tile(n, cap):
    t = min(n, cap)
    assert n % t == 0, (n, t)
    return t


def _resident(w):
    nd = w.ndim
    return pl.BlockSpec(w.shape, lambda i: (0,) * nd, pipeline_mode=pl.Buffered(1))


def _rms_proj(name, x, g, w, n_cols, gather=()):
    T = x.shape[0]
    tm = _row_tile(T, 512)
    ns = n_cols // N_CHIPS
    steps = T // tm
    ng = len(gather)

    def body(x_ref, g_ref, w_ref, *rest):
        o_ref, h_ref = rest[ng], rest[ng + 1]
        bufs, sems = rest[ng + 2:2 * ng + 2], rest[2 * ng + 2:]
        if ng:
            @pl.when(pl.program_id(0) == 0)
            def _():
                _gather_start(bufs, *sems)

        h = _rms_rows(x_ref[...], g_ref[...]).astype(h_ref.dtype)
        h_ref[...] = h
        for k in range(N_CHIPS):
            o_ref[:, k * ns:(k + 1) * ns] = _dot(h, w_ref[k]).astype(o_ref.dtype)

        if ng:
            @pl.when(pl.program_id(0) == steps - 1)
            def _():
                _gather_finish(bufs, *sems)

    row = pl.BlockSpec((tm, D_MODEL), lambda i: (i, 0))
    out = _call(
        body,
        out_shape=(_sds((T, n_cols), _ACT), _sds((T, D_MODEL), _ACT), *[_sds(b.shape, b.dtype) for b in gather]),
        grid=(steps,),
        in_specs=[row, pl.BlockSpec((1, D_MODEL), lambda i: (0, 0)), _resident(w)] + [ANY] * ng,
        out_specs=(pl.BlockSpec((tm, n_cols), lambda i: (i, 0)), row) + (ANY,) * ng,
        input_output_aliases={3 + i: 2 + i for i in range(ng)},
        scratch_shapes=_gather_sems(ng) if ng else [],
        name=name, compiler_params=_cp(("arbitrary",)),
    )(x, g, w, *gather)
    return out[0], out[1], list(out[2:])


def _colsharded_wgrad_full(name, h, dy, n_cols, tn):
    T = h.shape[0]
    tt = _row_tile(T, 1024)
    per = (n_cols // N_CHIPS) // tn
    return _mm(
        name, "tn", h, dy,
        pl.BlockSpec((tt, D_MODEL), lambda i, j, k: (k, 0)),
        pl.BlockSpec((tt, tn), lambda i, j, k: (k, j)),
        _sds((N_CHIPS, D_MODEL, n_cols // N_CHIPS), _COMM),
        pl.BlockSpec((None, D_MODEL, tn), lambda i, j, k: (j // per, 0, j % per)),
        (1, n_cols // tn, T // tt), (D_MODEL, tn),
    )


def _dgrad_rms(name, dy, w, x, g, dres, scatter=()):
    T, kd = dy.shape
    tm = _row_tile(T, 512)
    ns = kd // N_CHIPS
    steps = T // tm
    nsc = len(scatter)

    def body(dy_ref, w_ref, x_ref, g_ref, dres_ref, *rest):
        sums = rest[:nsc]
        dx_ref, dg_ref = rest[nsc], rest[nsc + 1]
        parts, sems = rest[nsc + 2:2 * nsc + 2], rest[2 * nsc + 2:]
        i = pl.program_id(0)
        if nsc:
            @pl.when(i == 0)
            def _():
                _scatter_start(sums, parts, *sems)

        dh = _dot(dy_ref[:, 0:ns], w_ref[0], "nt")
        for k in range(1, N_CHIPS):
            dh += _dot(dy_ref[:, k * ns:(k + 1) * ns], w_ref[k], "nt")
        x = x_ref[...]
        r = lax.rsqrt(jnp.mean(x * x, axis=-1, keepdims=True) + EPS)
        xh = x * r
        dxh = dh * g_ref[...]
        dx_ref[...] = dres_ref[...] + r * (dxh - xh * jnp.mean(dxh * xh, axis=-1, keepdims=True))
        dgp = jnp.sum(dh * xh, axis=0, keepdims=True)

        @pl.when(i == 0)
        def _():
            dg_ref[...] = dgp

        @pl.when(i > 0)
        def _():
            dg_ref[...] += dgp

        if nsc:
            @pl.when(i == steps - 1)
            def _():
                _scatter_finish(sums, parts, *sems)

    row = pl.BlockSpec((tm, D_MODEL), lambda i: (i, 0))
    vec = pl.BlockSpec((1, D_MODEL), lambda i: (0, 0))
    out = _call(
        body,
        out_shape=(_sds((T, D_MODEL), F32), _sds((1, D_MODEL), F32), *_scatter_shapes(scatter)),
        grid=(steps,),
        in_specs=[pl.BlockSpec((tm, kd), lambda i: (i, 0)), _resident(w), row, vec, row] + [ANY] * nsc,
        out_specs=(row, vec) + (ANY,) * nsc,
        scratch_shapes=_scatter_sems(nsc) if nsc else [],
        name=name, compiler_params=_cp(("arbitrary",)),
    )(dy, w, x, g, dres, *scatter)
    return out[0], out[1], list(out[2:])


def _ffn_out(x1, u, w):
    T = u.shape[0]
    tm = _row_tile(T, 512)
    ks = w.shape[1]

    def body(x_ref, u_ref, w_ref, o_ref):
        acc = x_ref[...]
        for k in range(N_CHIPS):
            acc += _dot(_relu2(u_ref[:, k * ks:(k + 1) * ks]), w_ref[k])
        o_ref[...] = acc

    row = pl.BlockSpec((tm, D_MODEL), lambda i: (i, 0))
    return _call(
        body, out_shape=_sds((T, D_MODEL), F32), grid=(T // tm,),
        in_specs=[row, pl.BlockSpec((tm, FFN_HIDDEN), lambda i: (i, 0)), _resident(w)], out_specs=row,
        name="ff2", compiler_params=_cp(("parallel",)),
    )(x1, u, w)


def _ffn_out_dgrad(dx2, u, w):
    T = u.shape[0]
    tm = _row_tile(T, 512)
    ks = w.shape[1]

    def body(d_ref, u_ref, w_ref, o_ref):
        d = d_ref[...].astype(_MXU)
        for k in range(N_CHIPS):
            sl = slice(k * ks, (k + 1) * ks)
            o_ref[:, sl] = _relu2_bwd(_dot(d, w_ref[k], "nt"), u_ref[:, sl]).astype(o_ref.dtype)

    wide = pl.BlockSpec((tm, FFN_HIDDEN), lambda i: (i, 0))
    return _call(
        body, out_shape=_sds((T, FFN_HIDDEN), _ACT), grid=(T // tm,),
        in_specs=[pl.BlockSpec((tm, D_MODEL), lambda i: (i, 0)), wide, _resident(w)], out_specs=wide,
        name="ff2_dgrad", compiler_params=_cp(("parallel",)),
    )(dx2, u, w)


def _rowsharded_mm(name, a, w, res, *, pro=None):
    T, K = a.shape
    ks = K // N_CHIPS
    tm = _row_tile(T, 512)
    epi = dict(epi=_add_res, epi_in=(res,), epi_specs=(pl.BlockSpec((tm, D_MODEL), lambda i, j, k: (i, 0)),)) if res is not None else {}
    return _mm(
        name, "nn", a, w,
        pl.BlockSpec((tm, ks), lambda i, j, k: (i, k)),
        pl.BlockSpec((None, ks, D_MODEL), lambda i, j, k: (k, 0, 0)),
        _sds((T, D_MODEL), F32 if res is not None else _ACT),
        pl.BlockSpec((tm, D_MODEL), lambda i, j, k: (i, 0)),
        (T // tm, 1, N_CHIPS), (tm, D_MODEL), pro=pro, **epi,
    )


def _rowsharded_dgrad(name, dy, w, *, epi=None, epi_in=None, out_dtype=_ACT, res=None):
    T = dy.shape[0]
    ks = w.shape[1]
    tm = _row_tile(T, 512)
    kw = {}
    if epi is not None:
        kw = dict(epi=epi, epi_in=(epi_in,), epi_specs=(pl.BlockSpec((tm, ks), lambda i, j, k: (i, j)),))
    return _mm(
        name, "nt", dy, w,
        pl.BlockSpec((tm, D_MODEL), lambda i, j, k: (i, 0)),
        pl.BlockSpec((None, ks, D_MODEL), lambda i, j, k: (j, 0, 0)),
        _sds((T, ks * N_CHIPS), out_dtype),
        pl.BlockSpec((tm, ks), lambda i, j, k: (i, j)),
        (T // tm, N_CHIPS, 1), None, **kw,
    )


def _rowsharded_wgrad(name, a, dy, K, *, pro=None):
    T = a.shape[0]
    ks = K // N_CHIPS
    tt = _row_tile(T, 1024)
    return _mm(
        name, "tn", a, dy,
        pl.BlockSpec((tt, ks), lambda i, j, k: (k, i)),
        pl.BlockSpec((tt, D_MODEL), lambda i, j, k: (k, 0)),
        _sds((N_CHIPS, ks, D_MODEL), _COMM),
        pl.BlockSpec((None, ks, D_MODEL), lambda i, j, k: (i, 0, 0)),
        (N_CHIPS, 1, T // tt), (ks, D_MODEL), pro=pro,
    )


def _mixer_wgrads(merged, y_r, y_p, o_m, dx1, do3):
    T = merged.shape[0]
    tt = _row_tile(T, 512)
    nt = T // tt
    rs = D_MODEL // N_CHIPS

    def body(mg_ref, yr_ref, yp_ref, om_ref, dx_ref, do3_ref, g_out, g_ret, g_pool, g_mem, a_out, a_ret, a_pool, a_mem):
        t = pl.program_id(0)

        def accumulate(acc, part):
            @pl.when(t == 0)
            def _():
                acc[...] = part

            @pl.when(t > 0)
            def _():
                acc[...] += part

        accumulate(a_out, _dot(mg_ref[...], dx_ref[...], "tn"))
        accumulate(a_ret, _dot(yr_ref[...], do3_ref[:, 0:D_MODEL], "tn"))
        accumulate(a_pool, _dot(yp_ref[...], do3_ref[:, D_MODEL:2 * D_MODEL], "tn"))
        accumulate(a_mem, _dot(om_ref[...], do3_ref[:, 2 * D_MODEL:3 * D_MODEL], "tn"))

        @pl.when(t == nt - 1)
        def _():
            for k in range(N_CHIPS):
                rows = slice(k * rs, (k + 1) * rs)
                g_out[k] = a_out[rows, :].astype(g_out.dtype)
                g_ret[k] = a_ret[rows, :].astype(g_ret.dtype)
                g_pool[k] = a_pool[:, rows].astype(g_pool.dtype)
                g_mem[k] = a_mem[:, rows].astype(g_mem.dtype)

    row = lambda w: pl.BlockSpec((tt, w), lambda t: (t, 0))
    whole = lambda s: pl.BlockSpec(s, lambda t: (0, 0, 0))
    rsh, csh = (N_CHIPS, rs, D_MODEL), (N_CHIPS, POOL_W, rs)
    return _call(
        body,
        out_shape=(_sds(rsh, _COMM), _sds(rsh, _COMM), _sds(csh, _COMM), _sds(csh, _COMM)),
        grid=(nt,),
        in_specs=[row(D_MODEL), row(D_MODEL), row(POOL_W), row(MEMQ_W), row(D_MODEL), row(3 * D_MODEL)],
        out_specs=(whole(rsh), whole(rsh), whole(csh), whole(csh)),
        scratch_shapes=[pltpu.VMEM((D_MODEL, D_MODEL), F32), pltpu.VMEM((D_MODEL, D_MODEL), F32),
                        pltpu.VMEM((POOL_W, D_MODEL), F32), pltpu.VMEM((MEMQ_W, D_MODEL), F32)],
        name="mixer_wgrads", compiler_params=_cp(("arbitrary",)),
    )(merged, y_r, y_p, o_m, dx1, do3)


def _rot(x, cos, sin):
    return x * cos + pltpu.roll(x, QK_DIM // 2, 1) * sin


def _rot_bwd(d, cos, sin):
    return d * cos + pltpu.roll(d * sin, QK_DIM // 2, 1)


def _tile_diff(qi, ki, tq, tk):
    n = qi * tq + lax.broadcasted_iota(jnp.int32, (tq, tk), 0)
    m = ki * tk + lax.broadcasted_iota(jnp.int32, (tq, tk), 1)
    return (n - m).astype(F32)


def _decay(qi, ki, tq, tk, lgf, lgb):
    diff = _tile_diff(qi, ki, tq, tk)
    return diff, jnp.exp(jnp.where(diff >= 0.0, lgf * diff, -(lgb * diff)))


def _head(h, width):
    return slice(h * width, (h + 1) * width)


def _row_index(t):
    return lax.broadcasted_iota(jnp.int32, (t, QK_DIM), 0).astype(F32)


def _q_decay_fwd(t, lgf):
    return jnp.exp(lgf * _row_index(t))


def _q_decay_bwd(t, lgb):
    return jnp.exp(lgb * (float(t) - _row_index(t)))


def _k_decay_fwd(t, lgf, tiles_apart):
    return jnp.exp(lgf * ((tiles_apart * t).astype(F32) - _row_index(t)))


def _k_decay_bwd(t, lgb, tiles_apart):
    return jnp.exp(lgb * (((tiles_apart - 1) * t).astype(F32) + _row_index(t)))


def _group_norm_gate(o, g):
    mu = jnp.mean(o, axis=-1, keepdims=True)
    oc = o - mu
    var = jnp.mean(oc * oc, axis=-1, keepdims=True)
    on = oc * lax.rsqrt(var + EPS)
    return on * (g * jax.nn.sigmoid(g))


def _retention_fwd(proj, cos, sin, lg, B, S, gather=()):
    T = B * S
    t = _row_tile(S, 512)
    nc = S // t
    scale = QK_DIM ** -0.5
    ng = len(gather)
    qw, vw = N_HEADS * QK_DIM, N_HEADS * V_DIM

    def chunk_of(s):
        return jnp.where(s < nc, s, 2 * nc - 1 - s)

    def body(lg_ref, q_ref, k_ref, v_ref, g_ref, cos_ref, sin_ref, *rest):
        o_ref, y_ref = rest[ng], rest[ng + 1]
        bufs = rest[ng + 2:2 * ng + 2]
        o_acc, state_f, state_b = rest[2 * ng + 2:2 * ng + 5]
        sems = rest[2 * ng + 5:]
        b = pl.program_id(0)
        s = pl.program_id(1)
        rows = pl.ds(pl.multiple_of(chunk_of(s) * t, t), t)
        if ng:
            @pl.when(jnp.logical_and(b == 0, s == 0))
            def _():
                _gather_start(bufs, *sems)

        @pl.when(s == 0)
        def _():
            state_f[...] = jnp.zeros_like(state_f)
            state_b[...] = jnp.zeros_like(state_b)

        def rotated(h):
            sl = _head(h, QK_DIM)
            qr = _rot(q_ref[:, sl].astype(F32), cos_ref[...], sin_ref[...])
            kr = _rot(k_ref[:, sl].astype(F32), cos_ref[...], sin_ref[...]) * scale
            return qr, kr

        r = _row_index(t)

        @pl.when(s < nc)
        def _():
            for h in range(N_HEADS):
                sl, vsl = _head(h, QK_DIM), _head(h, V_DIM)
                lgf, lgb = lg_ref[0, h], lg_ref[1, h]
                qr, kr = rotated(h)
                v = v_ref[:, vsl]
                diff = _tile_diff(0, 0, t, t)
                dec = jnp.exp(jnp.where(diff >= 0.0, lgf * diff, -(lgb * diff)))
                o = _dot(_dot(qr, kr, "nt") * dec, v)
                o += _dot(qr * jnp.exp(lgf * (r + 1.0)), state_f[sl, :])
                o_acc[rows, vsl] = o
                state_f[sl, :] = state_f[sl, :] * jnp.exp(lgf * float(t)) + _dot(kr * jnp.exp(lgf * (float(t - 1) - r)), v, "tn")

        @pl.when(s >= nc)
        def _():
            for h in range(N_HEADS):
                sl, vsl = _head(h, QK_DIM), _head(h, V_DIM)
                lgb = lg_ref[1, h]
                qr, kr = rotated(h)
                o = o_acc[rows, vsl] + _dot(qr * jnp.exp(lgb * (float(t) - r)), state_b[sl, :])
                o_ref[:, vsl] = o
                y_ref[:, vsl] = _group_norm_gate(o, g_ref[:, vsl].astype(F32)).astype(y_ref.dtype)
                state_b[sl, :] = state_b[sl, :] * jnp.exp(lgb * float(t)) + _dot(kr * jnp.exp(lgb * r), v_ref[:, vsl], "tn")

        if ng:
            @pl.when(jnp.logical_and(b == B - 1, s == 2 * nc - 1))
            def _():
                _gather_finish(bufs, *sems)

    def in_rows(width, col):
        return pl.BlockSpec((t, width), lambda b, s: (b * nc + chunk_of(s), col))

    out_rows = pl.BlockSpec((t, vw), lambda b, s: (b * nc + jnp.where(s < nc, nc - 1, 2 * nc - 1 - s), 0))
    table = pl.BlockSpec((t, QK_DIM), lambda b, s: (chunk_of(s), 0))
    out = _call(
        body,
        out_shape=(_sds((T, vw), F32), _sds((T, vw), _ACT), *[_sds(g.shape, g.dtype) for g in gather]),
        grid=(B, 2 * nc),
        in_specs=[
            pl.BlockSpec(memory_space=pltpu.SMEM),
            in_rows(qw, C_Q // qw), in_rows(qw, C_K // qw), in_rows(vw, C_V // vw), in_rows(vw, C_G // vw),
            table, table,
        ] + [ANY] * ng,
        out_specs=(out_rows, out_rows) + (ANY,) * ng,
        input_output_aliases={7 + i: 2 + i for i in range(ng)},
        scratch_shapes=[pltpu.VMEM((S, vw), F32), pltpu.VMEM((qw, V_DIM), F32), pltpu.VMEM((qw, V_DIM), F32)]
        + (_gather_sems(ng) if ng else []),
        name="retention_fwd",
        compiler_params=_cp(("arbitrary", "arbitrary")),
    )(lg, proj, proj, proj, proj, cos, sin, *gather)
    return out[0], out[1], list(out[2:])


def _gn_gate_bwd(dy, o, proj, dproj):
    T = dy.shape[0]
    tm = _row_tile(T, 512)

    def body(dy_ref, o_ref, g_ref, dproj_in, do_ref, dg_ref):
        for h in range(N_HEADS):
            sl = slice(h * V_DIM, (h + 1) * V_DIM)
            o = o_ref[:, sl]
            g = g_ref[:, sl].astype(F32)
            d = dy_ref[:, sl].astype(F32)
            mu = jnp.mean(o, axis=-1, keepdims=True)
            oc = o - mu
            rstd = lax.rsqrt(jnp.mean(oc * oc, axis=-1, keepdims=True) + EPS)
            on = oc * rstd
            sg = jax.nn.sigmoid(g)
            don = d * (g * sg)
            dg_ref[:, sl] = (d * on * (sg * (1.0 + g * (1.0 - sg)))).astype(dg_ref.dtype)
            do = rstd * (don - jnp.mean(don, axis=-1, keepdims=True) - on * jnp.mean(don * on, axis=-1, keepdims=True))
            do_ref[:, sl] = do.astype(do_ref.dtype)

    wide = N_HEADS * V_DIM
    return _call(
        body,
        out_shape=(_sds((T, wide), _ACT), _sds(dproj.shape, dproj.dtype)),
        grid=(T // tm,),
        in_specs=[
            pl.BlockSpec((tm, wide), lambda i: (i, 0)),
            pl.BlockSpec((tm, wide), lambda i: (i, 0)),
            pl.BlockSpec((tm, wide), lambda i: (i, C_G // wide)),
            ANY,
        ],
        out_specs=(pl.BlockSpec((tm, wide), lambda i: (i, 0)), pl.BlockSpec((tm, wide), lambda i: (i, C_G // wide))),
        input_output_aliases={3: 1},
        name="gn_gate_bwd", compiler_params=_cp(("parallel",)),
    )(dy, o, proj, dproj)


def _retention_bwd(proj, d_o, dproj, cos, sin, lg, B, S, scatter=()):
    T = B * S
    tq = tk = _row_tile(S, 512)
    nq, nk = S // tq, S // tk
    scale = QK_DIM ** -0.5
    qw, vw = N_HEADS * QK_DIM, N_HEADS * V_DIM

    ns = len(scatter)

    def body(lg_ref, q_ref, k_ref, v_ref, do_ref, cq_ref, sq_ref, ck_ref, sk_ref, dproj_in, *rest):
        sums = rest[:ns]
        dqkv_ref, dlg_ref = rest[ns], rest[ns + 1]
        parts = rest[ns + 2:2 * ns + 2]
        kr_scr, dq_acc, dk_acc, dv_acc, gl_acc = rest[2 * ns + 2:2 * ns + 7]
        sems = rest[2 * ns + 7:]
        b = pl.program_id(0)
        ki = pl.program_id(1)
        qi = pl.program_id(2)
        q_rows = pl.ds(pl.multiple_of(qi * tq, tq), tq)
        k_rows = pl.ds(pl.multiple_of(ki * tk, tk), tk)
        if ns:
            @pl.when(jnp.logical_and(b == 0, jnp.logical_and(qi == 0, ki == 0)))
            def _():
                _scatter_start(sums, parts, *sems)

        @pl.when(qi == 0)
        def _():
            for h in range(N_HEADS):
                sl = _head(h, QK_DIM)
                kr_scr[:, sl] = _rot(k_ref[:, sl].astype(F32), ck_ref[...], sk_ref[...]) * scale
            dk_acc[...] = jnp.zeros_like(dk_acc)
            dv_acc[...] = jnp.zeros_like(dv_acc)

        @pl.when(jnp.logical_and(qi == 0, ki == 0))
        def _():
            gl_acc[...] = jnp.zeros_like(gl_acc)

        @pl.when(ki == 0)
        def _():
            dq_acc[q_rows, :] = jnp.zeros((tq, qw), F32)

        def queries(h):
            return _rot(q_ref[:, _head(h, QK_DIM)].astype(F32), cq_ref[...], sq_ref[...])

        def one_sided(h, q_factor, k_factor, side, sign):
            sl, vsl = _head(h, QK_DIM), _head(h, V_DIM)
            qt = (queries(h) * q_factor).astype(_MXU)
            kt = (kr_scr[:, sl] * k_factor).astype(_MXU)
            d_out = do_ref[:, vsl]
            p = _dot(qt, kt, "nt")
            dv_acc[:, vsl] += _dot(p, d_out, "tn")
            dp = _dot(d_out, v_ref[:, vsl], "nt")
            dq_acc[q_rows, sl] += _dot(dp, kt) * q_factor
            dk_acc[:, sl] += _dot(dp, qt, "tn") * k_factor
            diff = _tile_diff(qi, ki, tq, tk)
            row = 2 * h + side
            gl_acc[row:row + 1, :] += sign * jnp.sum(dp * p * diff, axis=0, keepdims=True)

        @pl.when(ki < qi)
        def _():
            for h in range(N_HEADS):
                one_sided(h, _q_decay_fwd(tq, lg_ref[0, h]), _k_decay_fwd(tk, lg_ref[0, h], qi - ki), 0, 1.0)

        @pl.when(ki > qi)
        def _():
            for h in range(N_HEADS):
                one_sided(h, _q_decay_bwd(tq, lg_ref[1, h]), _k_decay_bwd(tk, lg_ref[1, h], ki - qi), 1, -1.0)

        @pl.when(ki == qi)
        def _():
            for h in range(N_HEADS):
                sl, vsl = _head(h, QK_DIM), _head(h, V_DIM)
                qr = queries(h).astype(_MXU)
                kr = kr_scr[:, sl].astype(_MXU)
                d_out = do_ref[:, vsl]
                s = _dot(qr, kr, "nt")
                diff, dec = _decay(qi, ki, tq, tk, lg_ref[0, h], lg_ref[1, h])
                p = s * dec
                dv_acc[:, vsl] += _dot(p, d_out, "tn")
                dp = _dot(d_out, v_ref[:, vsl], "nt")
                ds = dp * dec
                dq_acc[q_rows, sl] += _dot(ds, kr)
                dk_acc[:, sl] += _dot(ds, qr, "tn")
                gd = dp * p * diff
                gl_acc[2 * h:2 * h + 1, :] += jnp.sum(jnp.where(diff >= 0.0, gd, 0.0), axis=0, keepdims=True)
                gl_acc[2 * h + 1:2 * h + 2, :] += jnp.sum(jnp.where(diff < 0.0, -gd, 0.0), axis=0, keepdims=True)

        @pl.when(qi == nq - 1)
        def _():
            for h in range(N_HEADS):
                sl = _head(h, QK_DIM)
                dk = _rot_bwd(dk_acc[:, sl] * scale, ck_ref[...], sk_ref[...])
                dqkv_ref[k_rows, qw + h * QK_DIM:qw + (h + 1) * QK_DIM] = dk.astype(dqkv_ref.dtype)
            dqkv_ref[k_rows, 2 * qw:2 * qw + vw] = dv_acc[...].astype(dqkv_ref.dtype)

        @pl.when(ki == nk - 1)
        def _():
            for h in range(N_HEADS):
                sl = _head(h, QK_DIM)
                dqkv_ref[q_rows, sl] = _rot_bwd(dq_acc[q_rows, sl], cq_ref[...], sq_ref[...]).astype(dqkv_ref.dtype)

        @pl.when(jnp.logical_and(qi == nq - 1, ki == nk - 1))
        def _():
            dlg_ref[...] = gl_acc[...]

        if ns:
            @pl.when(jnp.logical_and(b == B - 1, jnp.logical_and(qi == nq - 1, ki == nk - 1)))
            def _():
                _scatter_finish(sums, parts, *sems)

    out = _call(
        body,
        out_shape=(_sds(dproj.shape, dproj.dtype), _sds((B, 2 * N_HEADS, tk), F32), *_scatter_shapes(scatter)),
        grid=(B, nk, nq),
        in_specs=[
            pl.BlockSpec(memory_space=pltpu.SMEM),
            pl.BlockSpec((tq, qw), lambda b, ki, qi: (b * nq + qi, C_Q // qw)),
            pl.BlockSpec((tk, qw), lambda b, ki, qi: (b * nk + ki, C_K // qw)),
            pl.BlockSpec((tk, vw), lambda b, ki, qi: (b * nk + ki, C_V // vw)),
            pl.BlockSpec((tq, vw), lambda b, ki, qi: (b * nq + qi, 0)),
            pl.BlockSpec((tq, QK_DIM), lambda b, ki, qi: (qi, 0)),
            pl.BlockSpec((tq, QK_DIM), lambda b, ki, qi: (qi, 0)),
            pl.BlockSpec((tk, QK_DIM), lambda b, ki, qi: (ki, 0)),
            pl.BlockSpec((tk, QK_DIM), lambda b, ki, qi: (ki, 0)),
            ANY,
        ] + [ANY] * ns,
        out_specs=(
            pl.BlockSpec((S, 2 * qw + vw), lambda b, ki, qi: (b, 0)),
            pl.BlockSpec((None, 2 * N_HEADS, tk), lambda b, ki, qi: (b, 0, 0)),
        ) + (ANY,) * ns,
        scratch_shapes=[
            pltpu.VMEM((tk, qw), F32), pltpu.VMEM((S, qw), F32),
            pltpu.VMEM((tk, qw), F32), pltpu.VMEM((tk, vw), F32), pltpu.VMEM((2 * N_HEADS, tk), F32),
        ] + (_scatter_sems(ns) if ns else []),
        input_output_aliases={9: 0},
        name="retention_bwd",
        compiler_params=_cp(("arbitrary", "arbitrary", "arbitrary")),
    )(lg, proj, proj, proj, d_o, cos, sin, cos, sin, dproj, *scatter)
    return out[0], out[1], list(out[2:])


POOL_PAD = 16


def _pad_rows(v):
    z = jnp.zeros((POOL_PAD, v.shape[1]), F32)
    return jnp.concatenate([z, v, z], axis=0)


def _window_sums(first, length, levels):
    s = first
    step = 1
    for _ in range(levels - 1):
        s = pltpu.roll(s, step, 0) + pltpu.roll(s, length - step, 0)
        step *= 2
    return s


def _pool_counts(S, hw):
    n = lax.broadcasted_iota(jnp.int32, (S, 1), 0)
    return (jnp.minimum(n + hw, S) - jnp.maximum(n - hw, 0)).astype(F32)


def _pool_mixed(pf, S, g):
    length = S + 2 * POOL_PAD
    xp = _pad_rows(pf)
    s = _window_sums(xp + pltpu.roll(xp, 1, 0), length, g + 1)[POOL_PAD:POOL_PAD + S]
    return s / _pool_counts(S, 1 << g) - pf


def _pool_mixed_bwd(dmixed, S, g):
    length = S + 2 * POOL_PAD
    ep = _pad_rows(dmixed / _pool_counts(S, 1 << g))
    t = _window_sums(ep + pltpu.roll(ep, length - 1, 0), length, g + 1)[POOL_PAD:POOL_PAD + S]
    return t - dmixed


def _pool_fwd(proj, w_grp, scale, B, S):
    T = B * S
    G = POOL_W // 4

    def body(p_ref, wg_ref, sc_ref, y_ref):
        for g in range(4):
            sl = slice(g * G, (g + 1) * G)
            mixed = _pool_mixed(p_ref[:, sl].astype(F32), S, g)
            y_ref[:, sl] = (_dot(mixed, wg_ref[g]) * sc_ref[:, sl]).astype(y_ref.dtype)

    return _call(
        body, out_shape=_sds((T, POOL_W), _ACT), grid=(B,),
        in_specs=[
            pl.BlockSpec((S, POOL_W), lambda b: (b, C_P // POOL_W)),
            pl.BlockSpec((4, G, G), lambda b: (0, 0, 0)),
            pl.BlockSpec((1, POOL_W), lambda b: (0, 0)),
        ],
        out_specs=pl.BlockSpec((S, POOL_W), lambda b: (b, 0)),
        name="pool_fwd", compiler_params=_cp(("parallel",)),
    )(proj, w_grp, scale)


def _pool_bwd(proj, dy, dproj, w_grp, scale, B, S):
    G = POOL_W // 4

    def body(p_ref, dy_ref, wg_ref, sc_ref, dproj_in, dp_ref, dwg_ref, dsc_ref):
        b = pl.program_id(0)
        for g in range(4):
            sl = slice(g * G, (g + 1) * G)
            mixed = _pool_mixed(p_ref[:, sl].astype(F32), S, g)
            z = _dot(mixed, wg_ref[g])
            d = dy_ref[:, sl].astype(F32)
            dsc = jnp.sum(d * z, axis=0, keepdims=True)
            dz = d * sc_ref[:, sl]
            dwg = _dot(mixed, dz, "tn")
            dmixed = _dot(dz, wg_ref[g], "nt")
            dp_ref[:, sl] = _pool_mixed_bwd(dmixed, S, g).astype(dp_ref.dtype)

            @pl.when(b == 0)
            def _():
                dwg_ref[g] = dwg
                dsc_ref[:, sl] = dsc

            @pl.when(b > 0)
            def _():
                dwg_ref[g] += dwg
                dsc_ref[:, sl] += dsc

    return _call(
        body,
        out_shape=(_sds(dproj.shape, dproj.dtype), _sds((4, G, G), F32), _sds((1, POOL_W), F32)),
        grid=(B,),
        in_specs=[
            pl.BlockSpec((S, POOL_W), lambda b: (b, C_P // POOL_W)),
            pl.BlockSpec((S, POOL_W), lambda b: (b, 0)),
            pl.BlockSpec((4, G, G), lambda b: (0, 0, 0)),
            pl.BlockSpec((1, POOL_W), lambda b: (0, 0)),
            ANY,
        ],
        out_specs=(
            pl.BlockSpec((S, POOL_W), lambda b: (b, C_P // POOL_W)),
            pl.BlockSpec((4, G, G), lambda b: (0, 0, 0)),
            pl.BlockSpec((1, POOL_W), lambda b: (0, 0)),
        ),
        input_output_aliases={4: 0},
        name="pool_bwd", compiler_params=_cp(("arbitrary",)),
    )(proj, dy, w_grp, scale, dproj)


def _mem_softmax(q, k):
    s = _dot(q, k, "nt") * (QK_DIM ** -0.5)
    e = jnp.exp(s - jnp.max(s, axis=-1, keepdims=True))
    return e / jnp.sum(e, axis=-1, keepdims=True)


def _mem_attn_fwd(proj, kv, B, S):
    T = B * S
    tq = _row_tile(S, 512)
    nq = S // tq

    def body(q_ref, kv_ref, o_ref):
        for h in range(N_HEADS):
            sl = slice(h * QK_DIM, (h + 1) * QK_DIM)
            a = _mem_softmax(q_ref[:, sl], kv_ref[:, sl])
            o_ref[:, sl] = _dot(a, kv_ref[:, MEMQ_W + h * QK_DIM:MEMQ_W + (h + 1) * QK_DIM]).astype(o_ref.dtype)

    return _call(
        body, out_shape=_sds((T, MEMQ_W), _ACT), grid=(B, nq),
        in_specs=[
            pl.BlockSpec((tq, MEMQ_W), lambda b, i: (b * nq + i, C_QM // MEMQ_W)),
            pl.BlockSpec((MEM_LEN, 2 * MEMQ_W), lambda b, i: (b, 0)),
        ],
        out_specs=pl.BlockSpec((tq, MEMQ_W), lambda b, i: (b * nq + i, 0)),
        name="mem_attn_fwd", compiler_params=_cp(("parallel", "parallel")),
    )(proj, kv)


def _mem_attn_bwd(proj, kv, d_o, dproj, B, S):
    tq = _row_tile(S, 512)
    nq = S // tq
    scale = QK_DIM ** -0.5

    def body(q_ref, kv_ref, do_ref, dproj_in, dq_ref, dkv_ref):
        i = pl.program_id(1)
        for h in range(N_HEADS):
            sl = slice(h * QK_DIM, (h + 1) * QK_DIM)
            vsl = slice(MEMQ_W + h * QK_DIM, MEMQ_W + (h + 1) * QK_DIM)
            q = q_ref[:, sl]
            a = _mem_softmax(q, kv_ref[:, sl])
            d = do_ref[:, sl]
            da = _dot(d, kv_ref[:, vsl], "nt")
            ds = a * (da - jnp.sum(a * da, axis=-1, keepdims=True)) * scale
            dq_ref[:, sl] = _dot(ds, kv_ref[:, sl]).astype(dq_ref.dtype)
            dk = _dot(ds, q, "tn")
            dv = _dot(a, d, "tn")

            @pl.when(i == 0)
            def _():
                dkv_ref[:, sl] = dk
                dkv_ref[:, vsl] = dv

            @pl.when(i > 0)
            def _():
                dkv_ref[:, sl] += dk
                dkv_ref[:, vsl] += dv

    return _call(
        body,
        out_shape=(_sds(dproj.shape, dproj.dtype), _sds((B * MEM_LEN, 2 * MEMQ_W), F32)),
        grid=(B, nq),
        in_specs=[
            pl.BlockSpec((tq, MEMQ_W), lambda b, i: (b * nq + i, C_QM // MEMQ_W)),
            pl.BlockSpec((MEM_LEN, 2 * MEMQ_W), lambda b, i: (b, 0)),
            pl.BlockSpec((tq, MEMQ_W), lambda b, i: (b * nq + i, 0)),
            ANY,
        ],
        out_specs=(
            pl.BlockSpec((tq, MEMQ_W), lambda b, i: (b * nq + i, C_QM // MEMQ_W)),
            pl.BlockSpec((MEM_LEN, 2 * MEMQ_W), lambda b, i: (b, 0)),
        ),
        input_output_aliases={3: 0},
        name="mem_attn_bwd", compiler_params=_cp(("parallel", "arbitrary")),
    )(proj, kv, d_o, dproj)


def _mem_norm(mem2d, g):
    M = mem2d.shape[0]
    tm = _row_tile(M, 512)

    def body(x_ref, g_ref, o_ref):
        o_ref[...] = _rms_rows(x_ref[...], g_ref[...]).astype(o_ref.dtype)

    return _call(
        body, out_shape=_sds((M, D_MODEL), _ACT), grid=(M // tm,),
        in_specs=[pl.BlockSpec((tm, D_MODEL), lambda i: (i, 0)), pl.BlockSpec((1, D_MODEL), lambda i: (0, 0))],
        out_specs=pl.BlockSpec((tm, D_MODEL), lambda i: (i, 0)),
        name="mem_norm", compiler_params=_cp(("parallel",)),
    )(mem2d, g)


def _mem_norm_wgrad(mem2d, d_memn):
    M = mem2d.shape[0]
    tm = _row_tile(M, 512)

    def body(x_ref, d_ref, dg_ref):
        i = pl.program_id(0)
        x = x_ref[...]
        xh = x * lax.rsqrt(jnp.mean(x * x, axis=-1, keepdims=True) + EPS)
        dg = jnp.sum(d_ref[...] * xh, axis=0, keepdims=True)

        @pl.when(i == 0)
        def _():
            dg_ref[...] = dg

        @pl.when(i > 0)
        def _():
            dg_ref[...] += dg

    return _call(
        body, out_shape=_sds((1, D_MODEL), F32), grid=(M // tm,),
        in_specs=[pl.BlockSpec((tm, D_MODEL), lambda i: (i, 0)), pl.BlockSpec((tm, D_MODEL), lambda i: (i, 0))],
        out_specs=pl.BlockSpec((1, D_MODEL), lambda i: (0, 0)),
        name="mem_norm_wgrad", compiler_params=_cp(("arbitrary",)),
    )(mem2d, d_memn)


def _row_mm(a, w_ref):
    ks = w_ref.shape[1]
    out = _dot(a[:, 0:ks], w_ref[0])
    for k in range(1, N_CHIPS):
        out += _dot(a[:, k * ks:(k + 1) * ks], w_ref[k])
    return out


def _row_mm_t(d, w_ref):
    return jnp.concatenate([_dot(d, w_ref[k], "nt") for k in range(N_CHIPS)], axis=1)


def _col_mm(a, w_ref):
    return jnp.concatenate([_dot(a, w_ref[k]) for k in range(N_CHIPS)], axis=1)


def _col_mm_t(d, w_ref):
    ns = w_ref.shape[2]
    out = _dot(d[:, 0:ns], w_ref[0], "nt")
    for k in range(1, N_CHIPS):
        out += _dot(d[:, k * ns:(k + 1) * ns], w_ref[k], "nt")
    return out


def _merge_fwd(x, proj, y_r, y_p, o_m, w_ret_o, w_pool_o, w_mem_o, w_out):
    T = x.shape[0]
    tm = _row_tile(T, 512)

    def body(x_ref, gr_ref, gp_ref, gm_ref, yr_ref, yp_ref, om_ref, wr_ref, wp_ref, wm_ref, wo_ref,
             x1_ref, mg_ref, o3_ref):
        o_r = _row_mm(yr_ref[...], wr_ref)
        o_p = _col_mm(yp_ref[...], wp_ref)
        o_q = _col_mm(om_ref[...], wm_ref)
        merged = (jax.nn.sigmoid(gr_ref[...].astype(F32)) * o_r + jax.nn.sigmoid(gp_ref[...].astype(F32)) * o_p
                  + jax.nn.sigmoid(gm_ref[...].astype(F32)) * o_q)
        mg = merged.astype(mg_ref.dtype)
        mg_ref[...] = mg
        o3_ref[:, 0:D_MODEL] = o_r.astype(o3_ref.dtype)
        o3_ref[:, D_MODEL:2 * D_MODEL] = o_p.astype(o3_ref.dtype)
        o3_ref[:, 2 * D_MODEL:3 * D_MODEL] = o_q.astype(o3_ref.dtype)
        x1_ref[...] = x_ref[...] + _row_mm(mg, wo_ref)

    gb = C_GATE // D_MODEL
    row = lambda w: pl.BlockSpec((tm, w), lambda i: (i, 0))
    return _call(
        body,
        out_shape=(_sds((T, D_MODEL), F32), _sds((T, D_MODEL), _ACT), _sds((T, 3 * D_MODEL), _ACT)),
        grid=(T // tm,),
        in_specs=[
            row(D_MODEL),
            pl.BlockSpec((tm, D_MODEL), lambda i: (i, gb)),
            pl.BlockSpec((tm, D_MODEL), lambda i: (i, gb + 1)),
            pl.BlockSpec((tm, D_MODEL), lambda i: (i, gb + 2)),
            row(D_MODEL), row(POOL_W), row(MEMQ_W),
            _resident(w_ret_o), _resident(w_pool_o), _resident(w_mem_o), _resident(w_out),
        ],
        out_specs=(row(D_MODEL), row(D_MODEL), row(3 * D_MODEL)),
        name="merge_fwd", compiler_params=_cp(("parallel",)),
    )(x, proj, proj, proj, y_r, y_p, o_m, w_ret_o, w_pool_o, w_mem_o, w_out)


def _merge_bwd(dx1, proj, o3, w_ret_o, w_pool_o, w_mem_o, w_out):
    T = dx1.shape[0]
    tm = _row_tile(T, 512)

    def body(dx_ref, gr_ref, gp_ref, gm_ref, o3_ref, wr_ref, wp_ref, wm_ref, wo_ref,
             do3_ref, dgate_ref, dyr_ref, dyp_ref, dom_ref):
        dmerged = _row_mm_t(dx_ref[...], wo_ref)
        douts = []
        dgate_ref[:, 0:MEMQ_W] = jnp.zeros((tm, MEMQ_W), dgate_ref.dtype)
        for n, g_ref in enumerate((gr_ref, gp_ref, gm_ref)):
            sl = slice(n * D_MODEL, (n + 1) * D_MODEL)
            gate = jax.nn.sigmoid(g_ref[...].astype(F32))
            d_out = (dmerged * gate).astype(do3_ref.dtype)
            do3_ref[:, sl] = d_out
            dgate_ref[:, MEMQ_W + n * D_MODEL:MEMQ_W + (n + 1) * D_MODEL] = (
                dmerged * o3_ref[:, sl].astype(F32) * gate * (1.0 - gate)).astype(dgate_ref.dtype)
            douts.append(d_out)
        dyr_ref[...] = _row_mm_t(douts[0], wr_ref).astype(dyr_ref.dtype)
        dyp_ref[...] = _col_mm_t(douts[1], wp_ref).astype(dyp_ref.dtype)
        dom_ref[...] = _col_mm_t(douts[2], wm_ref).astype(dom_ref.dtype)

    gb = C_GATE // D_MODEL
    row = lambda w: pl.BlockSpec((tm, w), lambda i: (i, 0))
    return _call(
        body,
        out_shape=(
            _sds((T, 3 * D_MODEL), _ACT), _sds((T, N_PROJ), _ACT),
            _sds((T, D_MODEL), _ACT), _sds((T, POOL_W), _ACT), _sds((T, MEMQ_W), _ACT),
        ),
        grid=(T // tm,),
        in_specs=[
            row(D_MODEL),
            pl.BlockSpec((tm, D_MODEL), lambda i: (i, gb)),
            pl.BlockSpec((tm, D_MODEL), lambda i: (i, gb + 1)),
            pl.BlockSpec((tm, D_MODEL), lambda i: (i, gb + 2)),
            row(3 * D_MODEL),
            _resident(w_ret_o), _resident(w_pool_o), _resident(w_mem_o), _resident(w_out),
        ],
        out_specs=(row(3 * D_MODEL), pl.BlockSpec((tm, N_PROJ - C_QM), lambda i: (i, 1)),
                   row(D_MODEL), row(POOL_W), row(MEMQ_W)),
        name="merge_bwd", compiler_params=_cp(("parallel",)),
    )(dx1, proj, proj, proj, o3, w_ret_o, w_pool_o, w_mem_o, w_out)


def _loss_head(x, g, target):
    T = x.shape[0]
    tm = _row_tile(T, 512)

    def body(x_ref, g_ref, t_ref, dx_ref, sq_ref, dg_ref):
        i = pl.program_id(0)
        x = x_ref[...]
        gg = g_ref[...]
        r = lax.rsqrt(jnp.mean(x * x, axis=-1, keepdims=True) + EPS)
        xh = x * r
        err = xh * gg - t_ref[...]
        dy = err * (1.0 / D_MODEL)
        dxh = dy * gg
        dx_ref[...] = r * (dxh - xh * jnp.mean(dxh * xh, axis=-1, keepdims=True))
        sq = jnp.sum(err * err, axis=0, keepdims=True)
        dg = jnp.sum(dy * xh, axis=0, keepdims=True)

        @pl.when(i == 0)
        def _():
            sq_ref[...] = sq
            dg_ref[...] = dg

        @pl.when(i > 0)
        def _():
            sq_ref[...] += sq
            dg_ref[...] += dg

    vec = pl.BlockSpec((1, D_MODEL), lambda i: (0, 0))
    row = pl.BlockSpec((tm, D_MODEL), lambda i: (i, 0))
    return _call(
        body,
        out_shape=(_sds((T, D_MODEL), F32), _sds((1, D_MODEL), F32), _sds((1, D_MODEL), F32)),
        grid=(T // tm,), in_specs=[row, vec, row], out_specs=(row, vec, vec),
        name="loss_head", compiler_params=_cp(("arbitrary",)),
    )(x, g, target)


def _block_rows(rows, cols, itemsize, cap_bytes=2 << 20):
    t = rows
    while t * cols * itemsize > cap_bytes and t % 2 == 0 and (t // 2) % 16 == 0:
        t //= 2
    return t


def _cast_into_slot(w3d, layer, me):
    _, R, C = w3d.shape
    tr = _block_rows(R, C, 4)

    def body(me_ref, w_ref, o_ref):
        o_ref[...] = w_ref[...].astype(o_ref.dtype)

    grid_spec = pltpu.PrefetchScalarGridSpec(
        num_scalar_prefetch=1, grid=(R // tr,),
        in_specs=[pl.BlockSpec((None, tr, C), lambda i, me: (layer, i, 0))],
        out_specs=pl.BlockSpec((None, tr, C), lambda i, me: (me[0], i, 0)),
    )
    return _call(body, out_shape=_sds((N_CHIPS, R, C), _COMM), grid_spec=grid_spec,
                 name="cast_into_slot", compiler_params=_cp(("parallel",)))(me, w3d)


def _adamw(w, g, m, v):
    R, C = w.shape
    tr = _block_rows(R, C, 4, 1 << 20)

    def body(w_ref, g_ref, m_ref, v_ref, d_ref, nm_ref, nv_ref):
        g = g_ref[...]
        m = ADAM_B1 * m_ref[...] + (1.0 - ADAM_B1) * g
        v = ADAM_B2 * v_ref[...] + (1.0 - ADAM_B2) * (g * g)
        m_hat = m / (1.0 - ADAM_B1 ** ADAM_STEP)
        v_hat = v / (1.0 - ADAM_B2 ** ADAM_STEP)
        d_ref[...] = -ADAM_LR * (m_hat / (jnp.sqrt(v_hat) + ADAM_EPS) + ADAM_WD * w_ref[...])
        nm_ref[...] = m
        nv_ref[...] = v

    spec = pl.BlockSpec((tr, C), lambda i: (i, 0))
    out = _sds((R, C), F32)
    return _call(body, out_shape=(out, out, out), grid=(R // tr,), in_specs=[spec] * 4, out_specs=(spec,) * 3,
                 name="adamw", compiler_params=_cp(("parallel",)))(w, g, m, v)


def _add_sibling_half(g_full, land, my_c):
    _, R, C = g_full.shape
    hr = R // 2
    tr = _block_rows(hr, C, 2, 1 << 20)
    nb = hr // tr

    def body(c_ref, g_ref, l_ref, o_ref):
        o_ref[...] = (g_ref[...].astype(F32) + l_ref[...].astype(F32)).astype(o_ref.dtype)

    grid_spec = pltpu.PrefetchScalarGridSpec(
        num_scalar_prefetch=1, grid=(N_CHIPS, nb),
        in_specs=[
            pl.BlockSpec((None, tr, C), lambda k, i, c: (k, c[0] * nb + i, 0)),
            pl.BlockSpec((None, tr, C), lambda k, i, c: (k, i, 0)),
        ],
        out_specs=pl.BlockSpec((None, tr, C), lambda k, i, c: (k, i, 0)),
    )
    return _call(body, out_shape=_sds((N_CHIPS, hr, C), _COMM), grid_spec=grid_spec,
                 name="add_sibling_half", compiler_params=_cp(("parallel", "parallel")))(my_c, g_full, land)


def _add_chips(land, sums, g_all, layer, my_chip, my_core):
    _, hr, C = land.shape
    tr = _block_rows(hr, C, 4, 1 << 20)
    nb = hr // tr

    def body(me_ref, core_ref, own_ref, a_ref, b_ref, c_ref, *rest):
        o_ref = rest[-1]
        o_ref[...] = ((own_ref[...].astype(F32) + a_ref[...].astype(F32)) + b_ref[...].astype(F32)) + c_ref[...].astype(F32)

    def other(d):
        return pl.BlockSpec((None, tr, C), lambda i, me, core: ((me[0] + d) % N_CHIPS, i, 0))

    operands = [my_chip, my_core, sums, land, land, land]
    in_specs = [other(0), other(1), other(2), other(3)]
    aliases = {}
    if g_all is not None:
        operands.append(g_all)
        in_specs.append(ANY)
        aliases = {len(operands) - 1: 0}
    grid_spec = pltpu.PrefetchScalarGridSpec(
        num_scalar_prefetch=2, grid=(nb,), in_specs=in_specs,
        out_specs=pl.BlockSpec((None, tr, C), lambda i, me, core: (layer, core[0] * nb + i, 0)),
    )
    return _call(body, out_shape=_sds((DEPTH, 2 * hr, C), F32), grid_spec=grid_spec, input_output_aliases=aliases,
                 name="add_chips", compiler_params=_cp(("parallel",)))(*operands)


def _place():
    x, y, c = lax.axis_index("x"), lax.axis_index("y"), lax.axis_index("c")
    chips = [(1 - x, y), (x, 1 - y), (1 - x, 1 - y)]
    return x, y, c, 2 * x + y, chips


def _remote(src, dst, send_sem, recv_sem, dev):
    return pltpu.make_async_remote_copy(src_ref=src, dst_ref=dst, send_sem=send_sem, recv_sem=recv_sem,
                                        device_id=dev, device_id_type=MESH)


def _gather_sems(n):
    return [pltpu.SemaphoreType.DMA((n, 6)), pltpu.SemaphoreType.DMA((n, 6))]


def _gather_start(bufs, send_sems, recv_sems):
    x, y, c, me, chips = _place()
    for w, buf in enumerate(bufs):
        hr = buf.shape[1] // 2
        own = buf.at[me, pl.ds(c * hr, hr)]
        for j, chip in enumerate(chips):
            _remote(own, own, send_sems.at[w, j], recv_sems.at[w, j], (*chip, c)).start()


def _gather_finish(bufs, send_sems, recv_sems):
    x, y, c, me, chips = _place()
    sibling = (x, y, 1 - c)
    waits = []
    for w, buf in enumerate(bufs):
        hr = buf.shape[1] // 2
        mine = pl.ds(c * hr, hr)
        for j, chip in enumerate(chips):
            kc = 2 * chip[0] + chip[1]
            got = buf.at[kc, mine]
            first = _remote(got, got, send_sems.at[w, j], recv_sems.at[w, j], (*chip, c))
            first.wait_recv()
            fwd = _remote(got, got, send_sems.at[w, 3 + j], recv_sems.at[w, 3 + j], sibling)
            fwd.start()
            waits += [first.wait_send, fwd.wait_send]
    for w, buf in enumerate(bufs):
        hr = buf.shape[1] // 2
        theirs = pl.ds((1 - c) * hr, hr)
        for j, chip in enumerate(chips):
            kc = 2 * chip[0] + chip[1]
            got = buf.at[kc, theirs]
            _remote(got, got, send_sems.at[w, 3 + j], recv_sems.at[w, 3 + j], sibling).wait_recv()
    for wait in waits:
        wait()


def _allgather_weights(bufs):
    n = len(bufs)

    def body(*refs):
        outs = refs[n:2 * n]
        send_sems, recv_sems = refs[2 * n:]
        _gather_start(outs, send_sems, recv_sems)
        _gather_finish(outs, send_sems, recv_sems)

    out_shape = tuple(_sds(b.shape, b.dtype) for b in bufs)
    return _call(
        body, out_shape=out_shape, in_specs=[ANY] * n, out_specs=(ANY,) * n,
        input_output_aliases={i: i for i in range(n)},
        scratch_shapes=_gather_sems(n),
        name="allgather_weights",
    )(*bufs)


def _swap_sibling_halves(grads):
    n = len(grads)

    def body(*refs):
        ins, outs = refs[:n], refs[n:2 * n]
        send_sems, recv_sems = refs[2 * n:]
        x, y, c, _, _ = _place()
        copies = []
        for w in range(n):
            hr = ins[w].shape[1] // 2
            cp = _remote(ins[w].at[:, pl.ds((1 - c) * hr, hr)], outs[w], send_sems.at[w], recv_sems.at[w], (x, y, 1 - c))
            cp.start()
            copies.append(cp)
        for cp in copies:
            cp.wait()

    out_shape = tuple(_sds((N_CHIPS, g.shape[1] // 2, g.shape[2]), g.dtype) for g in grads)
    return _call(
        body, out_shape=out_shape, in_specs=[ANY] * n, out_specs=(ANY,) * n,
        scratch_shapes=[pltpu.SemaphoreType.DMA((n,)), pltpu.SemaphoreType.DMA((n,))],
        name="swap_sibling_halves",
    )(*grads)


def _scatter_sems(n):
    return [pltpu.SemaphoreType.DMA((n, 3)), pltpu.SemaphoreType.DMA((n, 3))]


def _scatter_shapes(sums):
    return tuple(_sds(s.shape, s.dtype) for s in sums)


def _scatter_start(sums, parts, send_sems, recv_sems):
    x, y, c, me, chips = _place()
    for w in range(len(sums)):
        for j, chip in enumerate(chips):
            kc = 2 * chip[0] + chip[1]
            _remote(sums[w].at[kc], parts[w].at[me], send_sems.at[w, j], recv_sems.at[w, j], (*chip, c)).start()


def _scatter_finish(sums, parts, send_sems, recv_sems):
    x, y, c, me, chips = _place()
    for w in range(len(sums)):
        for j, chip in enumerate(chips):
            kc = 2 * chip[0] + chip[1]
            _remote(sums[w].at[kc], parts[w].at[kc], send_sems.at[w, j], recv_sems.at[w, j], (*chip, c)).wait()


def _join_sibling_halves(g_alls, layer):
    n = len(g_alls)

    def body(*refs):
        outs = refs[n:2 * n]
        send_sems, recv_sems = refs[2 * n:]
        x, y, c, _, _ = _place()
        copies = []
        for w in range(n):
            hr = outs[w].shape[1] // 2
            mine = outs[w].at[layer, pl.ds(c * hr, hr)]
            cp = _remote(mine, mine, send_sems.at[w], recv_sems.at[w], (x, y, 1 - c))
            cp.start()
            copies.append(cp)
        for cp in copies:
            cp.wait()

    out_shape = tuple(_sds(g.shape, g.dtype) for g in g_alls)
    return _call(
        body, out_shape=out_shape, in_specs=[ANY] * n, out_specs=(ANY,) * n,
        input_output_aliases={i: i for i in range(n)},
        scratch_shapes=[pltpu.SemaphoreType.DMA((n,)), pltpu.SemaphoreType.DMA((n,))],
        name="join_sibling_halves",
    )(*g_alls)


def _chip_sums(grads, my_core):
    land = _swap_sibling_halves(grads)
    return [_add_sibling_half(g, l, my_core) for g, l in zip(grads, land)]


def _finish_reduce_scatter(parts, sums, g_alls, layer, my_chip, my_core):
    g_alls = [_add_chips(p, s, ga, layer, my_chip, my_core) for p, s, ga in zip(parts, sums, g_alls)]
    return _join_sibling_halves(g_alls, layer)


def _allreduce_small(v):
    R = v.shape[0]

    def body(v_ref, out_ref, sib_ref, chip_ref, sum_ref, send_sems, recv_sems):
        x, y, c, me, chips = _place()
        swap = _remote(v_ref, sib_ref, send_sems.at[0], recv_sems.at[0], (x, y, 1 - c))
        swap.start()
        swap.wait()
        sum_ref[...] = v_ref[...] + sib_ref[...]
        copies = []
        for j, chip in enumerate(chips):
            cp = _remote(sum_ref, chip_ref.at[me], send_sems.at[1 + j], recv_sems.at[1 + j], (*chip, c))
            cp.start()
            copies.append(cp)
        chip_ref[me] = sum_ref[...]
        for cp in copies:
            cp.wait()
        out_ref[...] = ((chip_ref[0] + chip_ref[1]) + chip_ref[2]) + chip_ref[3]

    vm = pl.BlockSpec(memory_space=pltpu.VMEM)
    return _call(
        body, out_shape=_sds((R, 128), F32), in_specs=[vm], out_specs=vm,
        scratch_shapes=[
            pltpu.VMEM((R, 128), F32), pltpu.VMEM((N_CHIPS, R, 128), F32), pltpu.VMEM((R, 128), F32),
            pltpu.SemaphoreType.DMA((4,)), pltpu.SemaphoreType.DMA((4,)),
        ],
        name="allreduce_small", compiler_params=_cp(),
    )(v)


def _pack_small(parts):
    rows = []
    for p in parts:
        flat = p.reshape(-1).astype(F32)
        pad = (-flat.shape[0]) % 128
        rows.append(jnp.pad(flat, (0, pad)).reshape(-1, 128))
    packed = jnp.concatenate(rows, axis=0)
    pad_rows = (-packed.shape[0]) % 8
    return jnp.pad(packed, ((0, pad_rows), (0, 0)))


def _unpack_small(packed, like):
    out, r = [], 0
    for p in like:
        n = p.size
        nr = -(-n // 128)
        out.append(packed[r:r + nr].reshape(-1)[:n].reshape(p.shape))
        r += nr
    return out


BIG = ("w_in", "w_ret_o", "w_pool_o", "w_mem_kv", "w_mem_o", "w_out", "w_ff1", "w_ff2")


def _rope_tables(S):
    inv = ROPE_BASE ** (-jnp.arange(0, QK_DIM, 2, dtype=F32) / QK_DIM)
    ang = jnp.arange(S).astype(F32)[:, None] * inv[None, :]
    cos, sin = jnp.cos(ang), jnp.sin(ang)
    return jnp.concatenate([cos, cos], axis=1), jnp.concatenate([-sin, sin], axis=1)


MIXER_W = ("w_ret_o", "w_pool_o", "w_mem_kv", "w_mem_o", "w_out")
AFTER_IN = MIXER_W + ("w_ff1", "w_ff2")
GATHER_PLAN = {
    ("in_proj", 0): [(0, AFTER_IN)],
    ("retention", 0): [(1, ("w_in",) + MIXER_W)],
    ("ff1", 0): [(1, ("w_ff1",))],
    ("in_proj", 1): [(1, ("w_ff2",)), (2, ("w_in",))],
    ("retention", 1): [(2, AFTER_IN)],
    ("ff1", 1): [(3, ("w_in",))],
    ("in_proj", 2): [(3, AFTER_IN)],
}


def _layer_fwd(x, layer, wbuf, small, memn, cos, sin, B, S):
    g1, g2, lg, w_grp, scale = small

    def hosted(host):
        keys = [(l2, n) for l2, names in GATHER_PLAN.get((host, layer), ()) for n in names]
        return keys, [wbuf[k] for k in keys]

    W = lambda n: wbuf[(layer, n)]
    keys, bufs = hosted("in_proj")
    proj, h1, bufs = _rms_proj("in_proj", x, g1, W("w_in"), N_PROJ, gather=bufs)
    wbuf.update(zip(keys, bufs))
    keys, bufs = hosted("retention")
    o, y_r, bufs = _retention_fwd(proj, cos, sin, lg, B, S, gather=bufs)
    wbuf.update(zip(keys, bufs))
    y_p = _pool_fwd(proj, w_grp, scale, B, S)
    kv = _rowsharded_mm("mem_kv", memn, W("w_mem_kv"), None)
    o_m = _mem_attn_fwd(proj, kv, B, S)
    x1, merged, o3 = _merge_fwd(x, proj, y_r, y_p, o_m, W("w_ret_o"), W("w_pool_o"), W("w_mem_o"), W("w_out"))
    keys, bufs = hosted("ff1")
    u, h2, bufs = _rms_proj("ff1", x1, g2, W("w_ff1"), FFN_HIDDEN, gather=bufs)
    wbuf.update(zip(keys, bufs))
    x2 = _ffn_out(x1, u, W("w_ff2"))
    saved = dict(x=x, proj=proj, h1=h1, o=o, y_r=y_r, y_p=y_p, kv=kv, o_m=o_m, merged=merged, o3=o3, x1=x1, h2=h2, u=u)
    return x2, saved


EARLY_GRADS = ("w_ff1", "w_ff2", "w_out", "w_ret_o", "w_pool_o", "w_mem_o")
LATE_GRADS = ("w_in", "w_mem_kv")


def _layer_bwd(dx2, sv, W, small, memn, d_memn, cos, sin, B, S, big_grads, layer, my_chip, my_core):
    g1, g2, lg, w_grp, scale = small
    x, proj, x1, u = sv["x"], sv["proj"], sv["x1"], sv["u"]
    grads = {}
    du = _ffn_out_dgrad(dx2, u, W["w_ff2"])
    grads["w_ff2"] = _rowsharded_wgrad("ff2_wgrad", u, dx2, FFN_HIDDEN, pro=_relu2)
    dx1, dg2, _ = _dgrad_rms("ff1_dgrad", du, W["w_ff1"], x1, g2, dx2)
    grads["w_ff1"] = _colsharded_wgrad_full("ff1_wgrad", sv["h2"], du, FFN_HIDDEN, 1024)
    do3, dproj, dy_r, dy_p, do_m = _merge_bwd(dx1, proj, sv["o3"], W["w_ret_o"], W["w_pool_o"], W["w_mem_o"], W["w_out"])
    grads["w_out"], grads["w_ret_o"], grads["w_pool_o"], grads["w_mem_o"] = _mixer_wgrads(
        sv["merged"], sv["y_r"], sv["y_p"], sv["o_m"], dx1, do3)
    d_o, dproj = _gn_gate_bwd(dy_r, sv["o"], proj, dproj)
    sums = _chip_sums([grads[n] for n in EARLY_GRADS], my_core)
    dproj, dlg, parts = _retention_bwd(proj, d_o, dproj, cos, sin, lg, B, S, scatter=sums)
    done = _finish_reduce_scatter(parts, sums, [big_grads[n] for n in EARLY_GRADS], layer, my_chip, my_core)
    big_grads.update(zip(EARLY_GRADS, done))
    dproj, dw_grp, dscale = _pool_bwd(proj, dy_p, dproj, w_grp, scale, B, S)
    dproj, dkv = _mem_attn_bwd(proj, sv["kv"], do_m, dproj, B, S)
    grads["w_mem_kv"] = _rowsharded_wgrad("mem_kv_wgrad", memn, dkv, D_MODEL)
    if d_memn is None:
        d_memn = _rowsharded_dgrad("mem_kv_dgrad", dkv, W["w_mem_kv"], out_dtype=F32)
    else:
        d_memn = _rowsharded_dgrad("mem_kv_dgrad_acc", dkv, W["w_mem_kv"], epi=_add_res, epi_in=d_memn, out_dtype=F32)
    grads["w_in"] = _colsharded_wgrad_full("in_wgrad", sv["h1"], dproj, N_PROJ, 1792)
    sums = _chip_sums([grads[n] for n in LATE_GRADS], my_core)
    dx, dg1, parts = _dgrad_rms("in_dgrad", dproj, W["w_in"], x, g1, dx1, scatter=sums)
    done = _finish_reduce_scatter(parts, sums, [big_grads[n] for n in LATE_GRADS], layer, my_chip, my_core)
    big_grads.update(zip(LATE_GRADS, done))
    small_grads = dict(g1=dg1, g2=dg2, lg=jnp.sum(dlg, axis=(0, 2)).reshape(N_HEADS, 2).T, w_grp=dw_grp, scale=dscale)
    return dx, small_grads, d_memn


def kernel(x, mem, w_in, ret_decay_logit, w_ret_o, w_pool_grp, pool_scale, w_pool_o, w_mem_kv, w_mem_o, w_out, w_ff1, w_ff2, norm1_g, norm2_g, mem_norm_g, final_norm_g, loss_target, m_w_in, m_ret_decay_logit, m_w_ret_o, m_w_pool_grp, m_pool_scale, m_w_pool_o, m_w_mem_kv, m_w_mem_o, m_w_out, m_w_ff1, m_w_ff2, m_norm1_g, m_norm2_g, m_mem_norm_g, m_final_norm_g, v_w_in, v_ret_decay_logit, v_w_ret_o, v_w_pool_grp, v_pool_scale, v_w_pool_o, v_w_mem_kv, v_w_mem_o, v_w_out, v_w_ff1, v_w_ff2, v_norm1_g, v_norm2_g, v_mem_norm_g, v_final_norm_g):
    B, S, _ = x.shape
    T = B * S
    big_w = dict(w_in=w_in, w_ret_o=w_ret_o, w_pool_o=w_pool_o, w_mem_kv=w_mem_kv, w_mem_o=w_mem_o, w_out=w_out, w_ff1=w_ff1, w_ff2=w_ff2)
    big_m = dict(w_in=m_w_in, w_ret_o=m_w_ret_o, w_pool_o=m_w_pool_o, w_mem_kv=m_w_mem_kv, w_mem_o=m_w_mem_o, w_out=m_w_out, w_ff1=m_w_ff1, w_ff2=m_w_ff2)
    big_v = dict(w_in=v_w_in, w_ret_o=v_w_ret_o, w_pool_o=v_w_pool_o, w_mem_kv=v_w_mem_kv, w_mem_o=v_w_mem_o, w_out=v_w_out, w_ff1=v_w_ff1, w_ff2=v_w_ff2)
    small_w = [ret_decay_logit, w_pool_grp, pool_scale, norm1_g, norm2_g, mem_norm_g, final_norm_g]
    small_m = [m_ret_decay_logit, m_w_pool_grp, m_pool_scale, m_norm1_g, m_norm2_g, m_mem_norm_g, m_final_norm_g]
    small_v = [v_ret_decay_logit, v_w_pool_grp, v_pool_scale, v_norm1_g, v_norm2_g, v_mem_norm_g, v_final_norm_g]
    my_chip = (2 * lax.axis_index("x") + lax.axis_index("y")).astype(jnp.int32).reshape(1)
    my_core = lax.axis_index("c").astype(jnp.int32).reshape(1)

    wbuf = {(l, n): _cast_into_slot(big_w[n], l, my_chip) for l in range(DEPTH) for n in BIG}
    wbuf[(0, "w_in")] = _allgather_weights([wbuf[(0, "w_in")]])[0]

    cos, sin = _rope_tables(S)
    log_g = jax.nn.log_sigmoid(ret_decay_logit.astype(F32))
    mem2d = mem.reshape(B * MEM_LEN, D_MODEL)
    memn = _mem_norm(mem2d, mem_norm_g.reshape(1, D_MODEL))
    smalls = [(norm1_g[l].reshape(1, D_MODEL), norm2_g[l].reshape(1, D_MODEL), log_g[l], w_pool_grp[l],
               pool_scale[l].reshape(1, POOL_W)) for l in range(DEPTH)]

    h = x.reshape(T, D_MODEL)
    saved = []
    for l in range(DEPTH):
        h, sv = _layer_fwd(h, l, wbuf, smalls[l], memn, cos, sin, B, S)
        saved.append(sv)
    weights = [{n: wbuf[(l, n)] for n in BIG} for l in range(DEPTH)]

    dh, sq, d_final_g = _loss_head(h, final_norm_g.reshape(1, D_MODEL), loss_target.reshape(T, D_MODEL))
    loss = lax.psum(0.5 * jnp.sum(sq) / D_MODEL, ("x", "y", "c"))

    big_grads = dict.fromkeys(BIG)
    small_grads = [None] * DEPTH
    d_memn = None
    for l in reversed(range(DEPTH)):
        dh, small_grads[l], d_memn = _layer_bwd(
            dh, saved[l], weights[l], smalls[l], memn, d_memn, cos, sin, B, S, big_grads, l, my_chip, my_core)
    d_mem_g = _mem_norm_wgrad(mem2d, d_memn)

    d_logit = jnp.stack([sg["lg"] for sg in small_grads]) * jax.nn.sigmoid(-ret_decay_logit.astype(F32))
    small_g_local = [
        d_logit, jnp.stack([sg["w_grp"] for sg in small_grads]),
        jnp.stack([sg["scale"].reshape(POOL_W) for sg in small_grads]),
        jnp.stack([sg["g1"].reshape(D_MODEL) for sg in small_grads]),
        jnp.stack([sg["g2"].reshape(D_MODEL) for sg in small_grads]),
        d_mem_g.reshape(D_MODEL), d_final_g.reshape(D_MODEL),
    ]
    small_g = _allreduce_small(_pack_small(small_g_local))
    s_delta, s_m, s_v = _adamw(_pack_small(small_w), small_g, _pack_small(small_m), _pack_small(small_v))
    small_g, s_delta, s_m, s_v = (_unpack_small(a, small_w) for a in (small_g, s_delta, s_m, s_v))

    big_out = {}
    for n in BIG:
        w = big_w[n]
        g = big_grads[n]
        flat = lambda a: a.reshape(-1, a.shape[-1])
        d, nm, nv = _adamw(flat(w), flat(g), flat(big_m[n]), flat(big_v[n]))
        big_out[n] = (g, d.reshape(w.shape), nm.reshape(w.shape), nv.reshape(w.shape))

    order = ["w_in", "ret_decay_logit", "w_ret_o", "w_pool_grp", "pool_scale", "w_pool_o", "w_mem_kv", "w_mem_o",
             "w_out", "w_ff1", "w_ff2", "norm1_g", "norm2_g", "mem_norm_g", "final_norm_g"]
    small_names = ["ret_decay_logit", "w_pool_grp", "pool_scale", "norm1_g", "norm2_g", "mem_norm_g", "final_norm_g"]
    outs = [[], [], [], []]
    for n in order:
        if n in big_out:
            vals = big_out[n]
        else:
            i = small_names.index(n)
            vals = (small_g[i], s_delta[i], s_m[i], s_v[i])
        for k in range(4):
            outs[k].append(vals[k])
    return (loss, dh.reshape(B, S, D_MODEL), *outs[0], *outs[1], *outs[2], *outs[3])
```

```python
import functools

import jax
import jax.numpy as jnp
from jax import lax
from jax.experimental import pallas as pl
from jax.experimental.pallas import tpu as pltpu

F32 = jnp.float32
_MXU = jnp.bfloat16
_ACT = jnp.bfloat16
_COMM = jnp.bfloat16

D_MODEL = 1024
N_PROJ = 7168
FFN_HIDDEN = 4096
MEM_LEN = 256
N_HEADS = 4
QK_DIM = 128
V_DIM = 256
POOL_W = 512
MEMQ_W = 512
DEPTH = 4
N_CHIPS = 4
EPS = 1e-6
ROPE_BASE = 10000.0

C_Q, C_K, C_V, C_G, C_P, C_QM, C_GATE = 0, 512, 1024, 2048, 3072, 3584, 4096

ADAM_LR = 0.001
ADAM_B1 = 0.9
ADAM_B2 = 0.999
ADAM_EPS = 1e-08
ADAM_WD = 0.01
ADAM_STEP = 10

VMEM_LIMIT_BYTES = 56 * 1024 * 1024
MESH = pl.DeviceIdType.MESH
ANY = pl.BlockSpec(memory_space=pl.ANY)

_DN = {
    "nn": (((1,), (0,)), ((), ())),
    "nt": (((1,), (1,)), ((), ())),
    "tn": (((0,), (0,)), ((), ())),
}


def _call(body, **kw):
    return pl.pallas_call(body, **kw)


def _cp(sem=None):
    return pltpu.CompilerParams(dimension_semantics=sem, vmem_limit_bytes=VMEM_LIMIT_BYTES)


def _dot(a, b, kind="nn"):
    return lax.dot_general(a.astype(_MXU), b.astype(_MXU), _DN[kind], preferred_element_type=F32)


def _sds(shape, dtype):
    return jax.ShapeDtypeStruct(shape, dtype)


def _rms_rows(x, g):
    r = lax.rsqrt(jnp.mean(x * x, axis=-1, keepdims=True) + EPS)
    return x * r * g


def _relu2(u):
    r = jnp.maximum(u.astype(F32), 0.0)
    return r * r


def _add_res(r, e):
    return r + e.astype(F32)


def _relu2_bwd(r, u):
    return r * (2.0 * jnp.maximum(u.astype(F32), 0.0))


def _mm(name, kind, a, b, a_spec, b_spec, out_shape, o_spec, grid, acc_shape, *,
        pro=None, pro_in=(), pro_specs=(), epi=None, epi_in=(), epi_specs=()):
    nk = grid[2]
    npro, nepi = len(pro_in), len(epi_in)

    def body(*refs):
        a_ref, b_ref = refs[0], refs[1]
        pro_refs = refs[2:2 + npro]
        epi_refs = refs[2 + npro:2 + npro + nepi]
        o_ref = refs[2 + npro + nepi]
        av = a_ref[...]
        if pro is not None:
            av = pro(av, *[r[...] for r in pro_refs])
        part = _dot(av, b_ref[...], kind)

        def finish(r):
            if epi is not None:
                r = epi(r, *[e[...] for e in epi_refs])
            o_ref[...] = r.astype(o_ref.dtype)

        if nk == 1:
            finish(part)
        else:
            acc = refs[-1]
            k = pl.program_id(2)

            @pl.when(k == 0)
            def _():
                acc[...] = part

            @pl.when(k > 0)
            def _():
                acc[...] += part

            @pl.when(k == nk - 1)
            def _():
                finish(acc[...])

    scratch = [] if nk == 1 else [pltpu.VMEM(acc_shape, F32)]
    return _call(
        body, out_shape=out_shape, grid=grid,
        in_specs=[a_spec, b_spec, *pro_specs, *epi_specs], out_specs=o_spec,
        scratch_shapes=scratch, name=name,
        compiler_params=_cp(("parallel", "parallel", "arbitrary")),
    )(a, b, *pro_in, *epi_in)


def _row_tile(n, cap):
    t = min(n, cap)
    assert n % t == 0, (n, t)
    return t


def _resident(w):
    nd = w.ndim
    return pl.BlockSpec(w.shape, lambda i: (0,) * nd, pipeline_mode=pl.Buffered(1))


def _rms_proj(name, x, g, w, n_cols, gather=()):
    T = x.shape[0]
    tm = _row_tile(T, 512)
    ns = n_cols // N_CHIPS
    steps = T // tm
    ng = len(gather)

    def body(x_ref, g_ref, w_ref, *rest):
        o_ref, h_ref = rest[ng], rest[ng + 1]
        bufs, sems = rest[ng + 2:2 * ng + 2], rest[2 * ng + 2:]
        if ng:
            @pl.when(pl.program_id(0) == 0)
            def _():
                _gather_start(bufs, *sems)

        h = _rms_rows(x_ref[...], g_ref[...]).astype(h_ref.dtype)
        h_ref[...] = h
        for k in range(N_CHIPS):
            o_ref[:, k * ns:(k + 1) * ns] = _dot(h, w_ref[k]).astype(o_ref.dtype)

        if ng:
            @pl.when(pl.program_id(0) == steps - 1)
            def _():
                _gather_finish(bufs, *sems)

    row = pl.BlockSpec((tm, D_MODEL), lambda i: (i, 0))
    out = _call(
        body,
        out_shape=(_sds((T, n_cols), _ACT), _sds((T, D_MODEL), _ACT), *[_sds(b.shape, b.dtype) for b in gather]),
        grid=(steps,),
        in_specs=[row, pl.BlockSpec((1, D_MODEL), lambda i: (0, 0)), _resident(w)] + [ANY] * ng,
        out_specs=(pl.BlockSpec((tm, n_cols), lambda i: (i, 0)), row) + (ANY,) * ng,
        input_output_aliases={3 + i: 2 + i for i in range(ng)},
        scratch_shapes=_gather_sems(ng) if ng else [],
        name=name, compiler_params=_cp(("arbitrary",)),
    )(x, g, w, *gather)
    return out[0], out[1], list(out[2:])


def _colsharded_wgrad_full(name, h, dy, n_cols, tn):
    T = h.shape[0]
    tt = _row_tile(T, 2048)
    per = (n_cols // N_CHIPS) // tn
    return _mm(
        name, "tn", h, dy,
        pl.BlockSpec((tt, D_MODEL), lambda i, j, k: (k, 0)),
        pl.BlockSpec((tt, tn), lambda i, j, k: (k, j)),
        _sds((N_CHIPS, D_MODEL, n_cols // N_CHIPS), _COMM),
        pl.BlockSpec((None, D_MODEL, tn), lambda i, j, k: (j // per, 0, j % per)),
        (1, n_cols // tn, T // tt), (D_MODEL, tn),
    )


def _dgrad_rms(name, dy, w, x, g, dres, scatter=()):
    T, kd = dy.shape
    tm = _row_tile(T, 512)
    ns = kd // N_CHIPS
    steps = T // tm
    nsc = len(scatter)

    def body(dy_ref, w_ref, x_ref, g_ref, dres_ref, *rest):
        sums = rest[:nsc]
        dx_ref, dg_ref = rest[nsc], rest[nsc + 1]
        parts, sems = rest[nsc + 2:2 * nsc + 2], rest[2 * nsc + 2:]
        i = pl.program_id(0)
        if nsc:
            @pl.when(i == 0)
            def _():
                _scatter_start(sums, parts, *sems)

        dh = _dot(dy_ref[:, 0:ns], w_ref[0], "nt")
        for k in range(1, N_CHIPS):
            dh += _dot(dy_ref[:, k * ns:(k + 1) * ns], w_ref[k], "nt")
        x = x_ref[...]
        r = lax.rsqrt(jnp.mean(x * x, axis=-1, keepdims=True) + EPS)
        xh = x * r
        dxh = dh * g_ref[...]
        dx_ref[...] = dres_ref[...] + r * (dxh - xh * jnp.mean(dxh * xh, axis=-1, keepdims=True))
        dgp = jnp.sum(dh * xh, axis=0, keepdims=True)

        @pl.when(i == 0)
        def _():
            dg_ref[...] = dgp

        @pl.when(i > 0)
        def _():
            dg_ref[...] += dgp

        if nsc:
            @pl.when(i == steps - 1)
            def _():
                _scatter_finish(sums, parts, *sems)

    row = pl.BlockSpec((tm, D_MODEL), lambda i: (i, 0))
    vec = pl.BlockSpec((1, D_MODEL), lambda i: (0, 0))
    out = _call(
        body,
        out_shape=(_sds((T, D_MODEL), F32), _sds((1, D_MODEL), F32), *_scatter_shapes(scatter)),
        grid=(steps,),
        in_specs=[pl.BlockSpec((tm, kd), lambda i: (i, 0)), _resident(w), row, vec, row] + [ANY] * nsc,
        out_specs=(row, vec) + (ANY,) * nsc,
        scratch_shapes=_scatter_sems(nsc) if nsc else [],
        name=name, compiler_params=_cp(("arbitrary",)),
    )(dy, w, x, g, dres, *scatter)
    return out[0], out[1], list(out[2:])


def _ffn_out(x1, u, w, gather=()):
    T = u.shape[0]
    tm = _row_tile(T, 512)
    ks = w.shape[1]
    steps = T // tm
    ng = len(gather)

    def body(x_ref, u_ref, w_ref, *rest):
        o_ref = rest[ng]
        bufs, sems = rest[ng + 1:2 * ng + 1], rest[2 * ng + 1:]
        if ng:
            @pl.when(pl.program_id(0) == 0)
            def _():
                _gather_start(bufs, *sems)

        acc = x_ref[...]
        for k in range(N_CHIPS):
            acc += _dot(_relu2(u_ref[:, k * ks:(k + 1) * ks]), w_ref[k])
        o_ref[...] = acc

        if ng:
            @pl.when(pl.program_id(0) == steps - 1)
            def _():
                _gather_finish(bufs, *sems)

    row = pl.BlockSpec((tm, D_MODEL), lambda i: (i, 0))
    out = _call(
        body, out_shape=(_sds((T, D_MODEL), F32), *[_sds(b.shape, b.dtype) for b in gather]), grid=(steps,),
        in_specs=[row, pl.BlockSpec((tm, FFN_HIDDEN), lambda i: (i, 0)), _resident(w)] + [ANY] * ng,
        out_specs=(row,) + (ANY,) * ng,
        input_output_aliases={3 + i: 1 + i for i in range(ng)},
        scratch_shapes=_gather_sems(ng) if ng else [],
        name="ff2", compiler_params=_cp(("arbitrary",)),
    )(x1, u, w, *gather)
    return out[0], list(out[1:])


def _ffn_out_dgrad(dx2, u, w):
    T = u.shape[0]
    tm = _row_tile(T, 512)
    ks = w.shape[1]

    def body(d_ref, u_ref, w_ref, o_ref):
        d = d_ref[...].astype(_MXU)
        for k in range(N_CHIPS):
            sl = slice(k * ks, (k + 1) * ks)
            o_ref[:, sl] = _relu2_bwd(_dot(d, w_ref[k], "nt"), u_ref[:, sl]).astype(o_ref.dtype)

    wide = pl.BlockSpec((tm, FFN_HIDDEN), lambda i: (i, 0))
    return _call(
        body, out_shape=_sds((T, FFN_HIDDEN), _ACT), grid=(T // tm,),
        in_specs=[pl.BlockSpec((tm, D_MODEL), lambda i: (i, 0)), wide, _resident(w)], out_specs=wide,
        name="ff2_dgrad", compiler_params=_cp(("parallel",)),
    )(dx2, u, w)


def _rowsharded_mm(name, a, w, res, *, pro=None):
    T, K = a.shape
    ks = K // N_CHIPS
    tm = _row_tile(T, 512)
    epi = dict(epi=_add_res, epi_in=(res,), epi_specs=(pl.BlockSpec((tm, D_MODEL), lambda i, j, k: (i, 0)),)) if res is not None else {}
    return _mm(
        name, "nn", a, w,
        pl.BlockSpec((tm, ks), lambda i, j, k: (i, k)),
        pl.BlockSpec((None, ks, D_MODEL), lambda i, j, k: (k, 0, 0)),
        _sds((T, D_MODEL), F32 if res is not None else _ACT),
        pl.BlockSpec((tm, D_MODEL), lambda i, j, k: (i, 0)),
        (T // tm, 1, N_CHIPS), (tm, D_MODEL), pro=pro, **epi,
    )


def _rowsharded_dgrad(name, dy, w, *, epi=None, epi_in=None, out_dtype=_ACT, res=None):
    T = dy.shape[0]
    ks = w.shape[1]
    tm = _row_tile(T, 512)
    kw = {}
    if epi is not None:
        kw = dict(epi=epi, epi_in=(epi_in,), epi_specs=(pl.BlockSpec((tm, ks), lambda i, j, k: (i, j)),))
    return _mm(
        name, "nt", dy, w,
        pl.BlockSpec((tm, D_MODEL), lambda i, j, k: (i, 0)),
        pl.BlockSpec((None, ks, D_MODEL), lambda i, j, k: (j, 0, 0)),
        _sds((T, ks * N_CHIPS), out_dtype),
        pl.BlockSpec((tm, ks), lambda i, j, k: (i, j)),
        (T // tm, N_CHIPS, 1), None, **kw,
    )


def _rowsharded_wgrad(name, a, dy, K, *, pro=None):
    T = a.shape[0]
    ks = K // N_CHIPS
    tt = _row_tile(T, 1024)
    return _mm(
        name, "tn", a, dy,
        pl.BlockSpec((tt, ks), lambda i, j, k: (k, i)),
        pl.BlockSpec((tt, D_MODEL), lambda i, j, k: (k, 0)),
        _sds((N_CHIPS, ks, D_MODEL), _COMM),
        pl.BlockSpec((None, ks, D_MODEL), lambda i, j, k: (i, 0, 0)),
        (N_CHIPS, 1, T // tt), (ks, D_MODEL), pro=pro,
    )


def _mixer_wgrads(merged, y_r, y_p, o_m, dx1, do3):
    T = merged.shape[0]
    tt = _row_tile(T, 512)
    nt = T // tt
    rs = D_MODEL // N_CHIPS

    def body(mg_ref, yr_ref, yp_ref, om_ref, dx_ref, do3_ref, g_out, g_ret, g_pool, g_mem, a_out, a_ret, a_pool, a_mem):
        t = pl.program_id(0)

        def accumulate(acc, part):
            @pl.when(t == 0)
            def _():
                acc[...] = part

            @pl.when(t > 0)
            def _():
                acc[...] += part

        accumulate(a_out, _dot(mg_ref[...], dx_ref[...], "tn"))
        accumulate(a_ret, _dot(yr_ref[...], do3_ref[:, 0:D_MODEL], "tn"))
        accumulate(a_pool, _dot(yp_ref[...], do3_ref[:, D_MODEL:2 * D_MODEL], "tn"))
        accumulate(a_mem, _dot(om_ref[...], do3_ref[:, 2 * D_MODEL:3 * D_MODEL], "tn"))

        @pl.when(t == nt - 1)
        def _():
            for k in range(N_CHIPS):
                rows = slice(k * rs, (k + 1) * rs)
                g_out[k] = a_out[rows, :].astype(g_out.dtype)
                g_ret[k] = a_ret[rows, :].astype(g_ret.dtype)
                g_pool[k] = a_pool[:, rows].astype(g_pool.dtype)
                g_mem[k] = a_mem[:, rows].astype(g_mem.dtype)

    row = lambda w: pl.BlockSpec((tt, w), lambda t: (t, 0))
    whole = lambda s: pl.BlockSpec(s, lambda t: (0, 0, 0))
    rsh, csh = (N_CHIPS, rs, D_MODEL), (N_CHIPS, POOL_W, rs)
    return _call(
        body,
        out_shape=(_sds(rsh, _COMM), _sds(rsh, _COMM), _sds(csh, _COMM), _sds(csh, _COMM)),
        grid=(nt,),
        in_specs=[row(D_MODEL), row(D_MODEL), row(POOL_W), row(MEMQ_W), row(D_MODEL), row(3 * D_MODEL)],
        out_specs=(whole(rsh), whole(rsh), whole(csh), whole(csh)),
        scratch_shapes=[pltpu.VMEM((D_MODEL, D_MODEL), F32), pltpu.VMEM((D_MODEL, D_MODEL), F32),
                        pltpu.VMEM((POOL_W, D_MODEL), F32), pltpu.VMEM((MEMQ_W, D_MODEL), F32)],
        name="mixer_wgrads", compiler_params=_cp(("arbitrary",)),
    )(merged, y_r, y_p, o_m, dx1, do3)


def _rot(x, cos, sin):
    return x * cos + pltpu.roll(x, QK_DIM // 2, 1) * sin


def _rot_bwd(d, cos, sin):
    return d * cos + pltpu.roll(d * sin, QK_DIM // 2, 1)


def _tile_diff(qi, ki, tq, tk):
    n = qi * tq + lax.broadcasted_iota(jnp.int32, (tq, tk), 0)
    m = ki * tk + lax.broadcasted_iota(jnp.int32, (tq, tk), 1)
    return (n - m).astype(F32)


def _decay(qi, ki, tq, tk, lgf, lgb):
    diff = _tile_diff(qi, ki, tq, tk)
    return diff, jnp.exp(jnp.where(diff >= 0.0, lgf * diff, -(lgb * diff)))


def _head(h, width):
    return slice(h * width, (h + 1) * width)


def _row_index(t):
    return lax.broadcasted_iota(jnp.int32, (t, QK_DIM), 0).astype(F32)


def _q_decay_fwd(t, lgf):
    return jnp.exp(lgf * _row_index(t))


def _q_decay_bwd(t, lgb):
    return jnp.exp(lgb * (float(t) - _row_index(t)))


def _k_decay_fwd(t, lgf, tiles_apart):
    return jnp.exp(lgf * ((tiles_apart * t).astype(F32) - _row_index(t)))


def _k_decay_bwd(t, lgb, tiles_apart):
    return jnp.exp(lgb * (((tiles_apart - 1) * t).astype(F32) + _row_index(t)))


def _group_norm_gate(o, g):
    mu = jnp.mean(o, axis=-1, keepdims=True)
    oc = o - mu
    var = jnp.mean(oc * oc, axis=-1, keepdims=True)
    on = oc * lax.rsqrt(var + EPS)
    return on * (g * jax.nn.sigmoid(g))


def _retention_fwd(proj, cos, sin, lg, B, S, gather=()):
    T = B * S
    t = _row_tile(S, 512)
    nc = S // t
    scale = QK_DIM ** -0.5
    ng = len(gather)
    qw, vw = N_HEADS * QK_DIM, N_HEADS * V_DIM

    def chunk_of(s):
        return jnp.where(s < nc, s, 2 * nc - 1 - s)

    def body(lg_ref, q_ref, k_ref, v_ref, g_ref, cos_ref, sin_ref, *rest):
        o_ref, y_ref = rest[ng], rest[ng + 1]
        bufs = rest[ng + 2:2 * ng + 2]
        o_acc, state_f, state_b = rest[2 * ng + 2:2 * ng + 5]
        sems = rest[2 * ng + 5:]
        b = pl.program_id(0)
        s = pl.program_id(1)
        rows = pl.ds(pl.multiple_of(chunk_of(s) * t, t), t)
        if ng:
            @pl.when(jnp.logical_and(b == 0, s == 0))
            def _():
                _gather_start(bufs, *sems)

        @pl.when(s == 0)
        def _():
            state_f[...] = jnp.zeros_like(state_f)
            state_b[...] = jnp.zeros_like(state_b)

        def rotated(h):
            sl = _head(h, QK_DIM)
            qr = _rot(q_ref[:, sl].astype(F32), cos_ref[...], sin_ref[...])
            kr = _rot(k_ref[:, sl].astype(F32), cos_ref[...], sin_ref[...]) * scale
            return qr, kr

        r = _row_index(t)

        @pl.when(s < nc)
        def _():
            for h in range(N_HEADS):
                sl, vsl = _head(h, QK_DIM), _head(h, V_DIM)
                lgf, lgb = lg_ref[0, h], lg_ref[1, h]
                qr, kr = rotated(h)
                v = v_ref[:, vsl]
                diff = _tile_diff(0, 0, t, t)
                dec = jnp.exp(jnp.where(diff >= 0.0, lgf * diff, -(lgb * diff)))
                o = _dot(_dot(qr, kr, "nt") * dec, v)
                o += _dot(qr * jnp.exp(lgf * (r + 1.0)), state_f[sl, :])
                o_acc[rows, vsl] = o
                state_f[sl, :] = state_f[sl, :] * jnp.exp(lgf * float(t)) + _dot(kr * jnp.exp(lgf * (float(t - 1) - r)), v, "tn")

        @pl.when(s >= nc)
        def _():
            for h in range(N_HEADS):
                sl, vsl = _head(h, QK_DIM), _head(h, V_DIM)
                lgb = lg_ref[1, h]
                qr, kr = rotated(h)
                o = o_acc[rows, vsl] + _dot(qr * jnp.exp(lgb * (float(t) - r)), state_b[sl, :])
                o_ref[:, vsl] = o
                y_ref[:, vsl] = _group_norm_gate(o, g_ref[:, vsl].astype(F32)).astype(y_ref.dtype)
                state_b[sl, :] = state_b[sl, :] * jnp.exp(lgb * float(t)) + _dot(kr * jnp.exp(lgb * r), v_ref[:, vsl], "tn")

        if ng:
            @pl.when(jnp.logical_and(b == B - 1, s == 2 * nc - 1))
            def _():
                _gather_finish(bufs, *sems)

    def in_rows(width, col):
        return pl.BlockSpec((t, width), lambda b, s: (b * nc + chunk_of(s), col))

    out_rows = pl.BlockSpec((t, vw), lambda b, s: (b * nc + jnp.where(s < nc, nc - 1, 2 * nc - 1 - s), 0))
    table = pl.BlockSpec((t, QK_DIM), lambda b, s: (chunk_of(s), 0))
    out = _call(
        body,
        out_shape=(_sds((T, vw), F32), _sds((T, vw), _ACT), *[_sds(g.shape, g.dtype) for g in gather]),
        grid=(B, 2 * nc),
        in_specs=[
            pl.BlockSpec(memory_space=pltpu.SMEM),
            in_rows(qw, C_Q // qw), in_rows(qw, C_K // qw), in_rows(vw, C_V // vw), in_rows(vw, C_G // vw),
            table, table,
        ] + [ANY] * ng,
        out_specs=(out_rows, out_rows) + (ANY,) * ng,
        input_output_aliases={7 + i: 2 + i for i in range(ng)},
        scratch_shapes=[pltpu.VMEM((S, vw), F32), pltpu.VMEM((qw, V_DIM), F32), pltpu.VMEM((qw, V_DIM), F32)]
        + (_gather_sems(ng) if ng else []),
        name="retention_fwd",
        compiler_params=_cp(("arbitrary", "arbitrary")),
    )(lg, proj, proj, proj, proj, cos, sin, *gather)
    return out[0], out[1], list(out[2:])


def _gn_gate_bwd(dy, o, proj, dproj):
    T = dy.shape[0]
    tm = _row_tile(T, 512)

    def body(dy_ref, o_ref, g_ref, dproj_in, do_ref, dg_ref):
        for h in range(N_HEADS):
            sl = slice(h * V_DIM, (h + 1) * V_DIM)
            o = o_ref[:, sl]
            g = g_ref[:, sl].astype(F32)
            d = dy_ref[:, sl].astype(F32)
            mu = jnp.mean(o, axis=-1, keepdims=True)
            oc = o - mu
            rstd = lax.rsqrt(jnp.mean(oc * oc, axis=-1, keepdims=True) + EPS)
            on = oc * rstd
            sg = jax.nn.sigmoid(g)
            don = d * (g * sg)
            dg_ref[:, sl] = (d * on * (sg * (1.0 + g * (1.0 - sg)))).astype(dg_ref.dtype)
            do = rstd * (don - jnp.mean(don, axis=-1, keepdims=True) - on * jnp.mean(don * on, axis=-1, keepdims=True))
            do_ref[:, sl] = do.astype(do_ref.dtype)

    wide = N_HEADS * V_DIM
    return _call(
        body,
        out_shape=(_sds((T, wide), _ACT), _sds(dproj.shape, dproj.dtype)),
        grid=(T // tm,),
        in_specs=[
            pl.BlockSpec((tm, wide), lambda i: (i, 0)),
            pl.BlockSpec((tm, wide), lambda i: (i, 0)),
            pl.BlockSpec((tm, wide), lambda i: (i, C_G // wide)),
            ANY,
        ],
        out_specs=(pl.BlockSpec((tm, wide), lambda i: (i, 0)), pl.BlockSpec((tm, wide), lambda i: (i, C_G // wide))),
        input_output_aliases={3: 1},
        name="gn_gate_bwd", compiler_params=_cp(("parallel",)),
    )(dy, o, proj, dproj)


def _retention_bwd(proj, d_o, dproj, cos, sin, lg, B, S, scatter=()):
    T = B * S
    tq = tk = _row_tile(S, 512)
    nq, nk = S // tq, S // tk
    scale = QK_DIM ** -0.5
    qw, vw = N_HEADS * QK_DIM, N_HEADS * V_DIM

    ns = len(scatter)

    def body(lg_ref, q_ref, k_ref, v_ref, do_ref, cq_ref, sq_ref, ck_ref, sk_ref, dproj_in, *rest):
        sums = rest[:ns]
        dqkv_ref, dlg_ref = rest[ns], rest[ns + 1]
        parts = rest[ns + 2:2 * ns + 2]
        kr_scr, dq_acc, dk_acc, dv_acc, gl_acc = rest[2 * ns + 2:2 * ns + 7]
        sems = rest[2 * ns + 7:]
        b = pl.program_id(0)
        ki = pl.program_id(1)
        qi = pl.program_id(2)
        q_rows = pl.ds(pl.multiple_of(qi * tq, tq), tq)
        k_rows = pl.ds(pl.multiple_of(ki * tk, tk), tk)
        if ns:
            @pl.when(jnp.logical_and(b == 0, jnp.logical_and(qi == 0, ki == 0)))
            def _():
                _scatter_start(sums, parts, *sems)

        @pl.when(qi == 0)
        def _():
            for h in range(N_HEADS):
                sl = _head(h, QK_DIM)
                kr_scr[:, sl] = _rot(k_ref[:, sl].astype(F32), ck_ref[...], sk_ref[...]) * scale
            dk_acc[...] = jnp.zeros_like(dk_acc)
            dv_acc[...] = jnp.zeros_like(dv_acc)

        @pl.when(jnp.logical_and(qi == 0, ki == 0))
        def _():
            gl_acc[...] = jnp.zeros_like(gl_acc)

        @pl.when(ki == 0)
        def _():
            dq_acc[q_rows, :] = jnp.zeros((tq, qw), F32)

        def queries(h):
            return _rot(q_ref[:, _head(h, QK_DIM)].astype(F32), cq_ref[...], sq_ref[...])

        def one_sided(h, q_factor, k_factor, side, sign):
            sl, vsl = _head(h, QK_DIM), _head(h, V_DIM)
            qt = (queries(h) * q_factor).astype(_MXU)
            kt = (kr_scr[:, sl] * k_factor).astype(_MXU)
            d_out = do_ref[:, vsl]
            p = _dot(qt, kt, "nt")
            dv_acc[:, vsl] += _dot(p, d_out, "tn")
            dp = _dot(d_out, v_ref[:, vsl], "nt")
            dq_acc[q_rows, sl] += _dot(dp, kt) * q_factor
            dk_acc[:, sl] += _dot(dp, qt, "tn") * k_factor
            diff = _tile_diff(qi, ki, tq, tk)
            row = 2 * h + side
            gl_acc[row:row + 1, :] += sign * jnp.sum(dp * p * diff, axis=0, keepdims=True)

        @pl.when(ki < qi)
        def _():
            for h in range(N_HEADS):
                one_sided(h, _q_decay_fwd(tq, lg_ref[0, h]), _k_decay_fwd(tk, lg_ref[0, h], qi - ki), 0, 1.0)

        @pl.when(ki > qi)
        def _():
            for h in range(N_HEADS):
                one_sided(h, _q_decay_bwd(tq, lg_ref[1, h]), _k_decay_bwd(tk, lg_ref[1, h], ki - qi), 1, -1.0)

        @pl.when(ki == qi)
        def _():
            for h in range(N_HEADS):
                sl, vsl = _head(h, QK_DIM), _head(h, V_DIM)
                qr = queries(h).astype(_MXU)
                kr = kr_scr[:, sl].astype(_MXU)
                d_out = do_ref[:, vsl]
                s = _dot(qr, kr, "nt")
                diff, dec = _decay(qi, ki, tq, tk, lg_ref[0, h], lg_ref[1, h])
                p = s * dec
                dv_acc[:, vsl] += _dot(p, d_out, "tn")
                dp = _dot(d_out, v_ref[:, vsl], "nt")
                ds = dp * dec
                dq_acc[q_rows, sl] += _dot(ds, kr)
                dk_acc[:, sl] += _dot(ds, qr, "tn")
                gd = dp * p * diff
                gl_acc[2 * h:2 * h + 1, :] += jnp.sum(jnp.where(diff >= 0.0, gd, 0.0), axis=0, keepdims=True)
                gl_acc[2 * h + 1:2 * h + 2, :] += jnp.sum(jnp.where(diff < 0.0, -gd, 0.0), axis=0, keepdims=True)

        @pl.when(qi == nq - 1)
        def _():
            for h in range(N_HEADS):
                sl = _head(h, QK_DIM)
                dk = _rot_bwd(dk_acc[:, sl] * scale, ck_ref[...], sk_ref[...])
                dqkv_ref[k_rows, qw + h * QK_DIM:qw + (h + 1) * QK_DIM] = dk.astype(dqkv_ref.dtype)
            dqkv_ref[k_rows, 2 * qw:2 * qw + vw] = dv_acc[...].astype(dqkv_ref.dtype)

        @pl.when(ki == nk - 1)
        def _():
            for h in range(N_HEADS):
                sl = _head(h, QK_DIM)
                dqkv_ref[q_rows, sl] = _rot_bwd(dq_acc[q_rows, sl], cq_ref[...], sq_ref[...]).astype(dqkv_ref.dtype)

        @pl.when(jnp.logical_and(qi == nq - 1, ki == nk - 1))
        def _():
            dlg_ref[...] = gl_acc[...]

        if ns:
            @pl.when(jnp.logical_and(b == B - 1, jnp.logical_and(qi == nq - 1, ki == nk - 1)))
            def _():
                _scatter_finish(sums, parts, *sems)

    out = _call(
        body,
        out_shape=(_sds(dproj.shape, dproj.dtype), _sds((B, 2 * N_HEADS, tk), F32), *_scatter_shapes(scatter)),
        grid=(B, nk, nq),
        in_specs=[
            pl.BlockSpec(memory_space=pltpu.SMEM),
            pl.BlockSpec((tq, qw), lambda b, ki, qi: (b * nq + qi, C_Q // qw)),
            pl.BlockSpec((tk, qw), lambda b, ki, qi: (b * nk + ki, C_K // qw)),
            pl.BlockSpec((tk, vw), lambda b, ki, qi: (b * nk + ki, C_V // vw)),
            pl.BlockSpec((tq, vw), lambda b, ki, qi: (b * nq + qi, 0)),
            pl.BlockSpec((tq, QK_DIM), lambda b, ki, qi: (qi, 0)),
            pl.BlockSpec((tq, QK_DIM), lambda b, ki, qi: (qi, 0)),
            pl.BlockSpec((tk, QK_DIM), lambda b, ki, qi: (ki, 0)),
            pl.BlockSpec((tk, QK_DIM), lambda b, ki, qi: (ki, 0)),
            ANY,
        ] + [ANY] * ns,
        out_specs=(
            pl.BlockSpec((S, 2 * qw + vw), lambda b, ki, qi: (b, 0)),
            pl.BlockSpec((None, 2 * N_HEADS, tk), lambda b, ki, qi: (b, 0, 0)),
        ) + (ANY,) * ns,
        scratch_shapes=[
            pltpu.VMEM((tk, qw), F32), pltpu.VMEM((S, qw), F32),
            pltpu.VMEM((tk, qw), F32), pltpu.VMEM((tk, vw), F32), pltpu.VMEM((2 * N_HEADS, tk), F32),
        ] + (_scatter_sems(ns) if ns else []),
        input_output_aliases={9: 0},
        name="retention_bwd",
        compiler_params=_cp(("arbitrary", "arbitrary", "arbitrary")),
    )(lg, proj, proj, proj, d_o, cos, sin, cos, sin, dproj, *scatter)
    return out[0], out[1], list(out[2:])


POOL_PAD = 16


def _pad_rows(v):
    z = jnp.zeros((POOL_PAD, v.shape[1]), F32)
    return jnp.concatenate([z, v, z], axis=0)


def _window_sums(first, length, levels):
    s = first
    step = 1
    for _ in range(levels - 1):
        s = pltpu.roll(s, step, 0) + pltpu.roll(s, length - step, 0)
        step *= 2
    return s


def _pool_counts(S, hw):
    n = lax.broadcasted_iota(jnp.int32, (S, 1), 0)
    return (jnp.minimum(n + hw, S) - jnp.maximum(n - hw, 0)).astype(F32)


def _pool_mixed(pf, S, g):
    length = S + 2 * POOL_PAD
    xp = _pad_rows(pf)
    s = _window_sums(xp + pltpu.roll(xp, 1, 0), length, g + 1)[POOL_PAD:POOL_PAD + S]
    return s / _pool_counts(S, 1 << g) - pf


def _pool_mixed_bwd(dmixed, S, g):
    length = S + 2 * POOL_PAD
    ep = _pad_rows(dmixed / _pool_counts(S, 1 << g))
    t = _window_sums(ep + pltpu.roll(ep, length - 1, 0), length, g + 1)[POOL_PAD:POOL_PAD + S]
    return t - dmixed


def _pool_fwd(proj, w_grp, scale, B, S):
    T = B * S
    G = POOL_W // 4

    def body(p_ref, wg_ref, sc_ref, y_ref):
        for g in range(4):
            sl = slice(g * G, (g + 1) * G)
            mixed = _pool_mixed(p_ref[:, sl].astype(F32), S, g)
            y_ref[:, sl] = (_dot(mixed, wg_ref[g]) * sc_ref[:, sl]).astype(y_ref.dtype)

    return _call(
        body, out_shape=_sds((T, POOL_W), _ACT), grid=(B,),
        in_specs=[
            pl.BlockSpec((S, POOL_W), lambda b: (b, C_P // POOL_W)),
            pl.BlockSpec((4, G, G), lambda b: (0, 0, 0)),
            pl.BlockSpec((1, POOL_W), lambda b: (0, 0)),
        ],
        out_specs=pl.BlockSpec((S, POOL_W), lambda b: (b, 0)),
        name="pool_fwd", compiler_params=_cp(("parallel",)),
    )(proj, w_grp, scale)


def _pool_bwd(proj, dy, dproj, w_grp, scale, B, S):
    G = POOL_W // 4

    def body(p_ref, dy_ref, wg_ref, sc_ref, dproj_in, dp_ref, dwg_ref, dsc_ref):
        b = pl.program_id(0)
        for g in range(4):
            sl = slice(g * G, (g + 1) * G)
            mixed = _pool_mixed(p_ref[:, sl].astype(F32), S, g)
            z = _dot(mixed, wg_ref[g])
            d = dy_ref[:, sl].astype(F32)
            dsc = jnp.sum(d * z, axis=0, keepdims=True)
            dz = d * sc_ref[:, sl]
            dwg = _dot(mixed, dz, "tn")
            dmixed = _dot(dz, wg_ref[g], "nt")
            dp_ref[:, sl] = _pool_mixed_bwd(dmixed, S, g).astype(dp_ref.dtype)

            @pl.when(b == 0)
            def _():
                dwg_ref[g] = dwg
                dsc_ref[:, sl] = dsc

            @pl.when(b > 0)
            def _():
                dwg_ref[g] += dwg
                dsc_ref[:, sl] += dsc

    return _call(
        body,
        out_shape=(_sds(dproj.shape, dproj.dtype), _sds((4, G, G), F32), _sds((1, POOL_W), F32)),
        grid=(B,),
        in_specs=[
            pl.BlockSpec((S, POOL_W), lambda b: (b, C_P // POOL_W)),
            pl.BlockSpec((S, POOL_W), lambda b: (b, 0)),
            pl.BlockSpec((4, G, G), lambda b: (0, 0, 0)),
            pl.BlockSpec((1, POOL_W), lambda b: (0, 0)),
            ANY,
        ],
        out_specs=(
            pl.BlockSpec((S, POOL_W), lambda b: (b, C_P // POOL_W)),
            pl.BlockSpec((4, G, G), lambda b: (0, 0, 0)),
            pl.BlockSpec((1, POOL_W), lambda b: (0, 0)),
        ),
        input_output_aliases={4: 0},
        name="pool_bwd", compiler_params=_cp(("arbitrary",)),
    )(proj, dy, w_grp, scale, dproj)


def _mem_softmax(q, k):
    s = _dot(q, k, "nt") * (QK_DIM ** -0.5)
    e = jnp.exp(s - jnp.max(s, axis=-1, keepdims=True))
    return e / jnp.sum(e, axis=-1, keepdims=True)


def _mem_attn_fwd(proj, kv, B, S):
    T = B * S
    tq = _row_tile(S, 512)
    nq = S // tq

    def body(q_ref, kv_ref, o_ref):
        for h in range(N_HEADS):
            sl = slice(h * QK_DIM, (h + 1) * QK_DIM)
            a = _mem_softmax(q_ref[:, sl], kv_ref[:, sl])
            o_ref[:, sl] = _dot(a, kv_ref[:, MEMQ_W + h * QK_DIM:MEMQ_W + (h + 1) * QK_DIM]).astype(o_ref.dtype)

    return _call(
        body, out_shape=_sds((T, MEMQ_W), _ACT), grid=(B, nq),
        in_specs=[
            pl.BlockSpec((tq, MEMQ_W), lambda b, i: (b * nq + i, C_QM // MEMQ_W)),
            pl.BlockSpec((MEM_LEN, 2 * MEMQ_W), lambda b, i: (b, 0)),
        ],
        out_specs=pl.BlockSpec((tq, MEMQ_W), lambda b, i: (b * nq + i, 0)),
        name="mem_attn_fwd", compiler_params=_cp(("parallel", "parallel")),
    )(proj, kv)


def _mem_attn_bwd(proj, kv, d_o, dproj, B, S):
    tq = _row_tile(S, 512)
    nq = S // tq
    scale = QK_DIM ** -0.5

    def body(q_ref, kv_ref, do_ref, dproj_in, dq_ref, dkv_ref):
        i = pl.program_id(1)
        for h in range(N_HEADS):
            sl = slice(h * QK_DIM, (h + 1) * QK_DIM)
            vsl = slice(MEMQ_W + h * QK_DIM, MEMQ_W + (h + 1) * QK_DIM)
            q = q_ref[:, sl]
            a = _mem_softmax(q, kv_ref[:, sl])
            d = do_ref[:, sl]
            da = _dot(d, kv_ref[:, vsl], "nt")
            ds = a * (da - jnp.sum(a * da, axis=-1, keepdims=True)) * scale
            dq_ref[:, sl] = _dot(ds, kv_ref[:, sl]).astype(dq_ref.dtype)
            dk = _dot(ds, q, "tn")
            dv = _dot(a, d, "tn")

            @pl.when(i == 0)
            def _():
                dkv_ref[:, sl] = dk
                dkv_ref[:, vsl] = dv

            @pl.when(i > 0)
            def _():
                dkv_ref[:, sl] += dk
                dkv_ref[:, vsl] += dv

    return _call(
        body,
        out_shape=(_sds(dproj.shape, dproj.dtype), _sds((B * MEM_LEN, 2 * MEMQ_W), F32)),
        grid=(B, nq),
        in_specs=[
            pl.BlockSpec((tq, MEMQ_W), lambda b, i: (b * nq + i, C_QM // MEMQ_W)),
            pl.BlockSpec((MEM_LEN, 2 * MEMQ_W), lambda b, i: (b, 0)),
            pl.BlockSpec((tq, MEMQ_W), lambda b, i: (b * nq + i, 0)),
            ANY,
        ],
        out_specs=(
            pl.BlockSpec((tq, MEMQ_W), lambda b, i: (b * nq + i, C_QM // MEMQ_W)),
            pl.BlockSpec((MEM_LEN, 2 * MEMQ_W), lambda b, i: (b, 0)),
        ),
        input_output_aliases={3: 0},
        name="mem_attn_bwd", compiler_params=_cp(("parallel", "arbitrary")),
    )(proj, kv, d_o, dproj)


def _mem_norm(mem2d, g):
    M = mem2d.shape[0]
    tm = _row_tile(M, 512)

    def body(x_ref, g_ref, o_ref):
        o_ref[...] = _rms_rows(x_ref[...], g_ref[...]).astype(o_ref.dtype)

    return _call(
        body, out_shape=_sds((M, D_MODEL), _ACT), grid=(M // tm,),
        in_specs=[pl.BlockSpec((tm, D_MODEL), lambda i: (i, 0)), pl.BlockSpec((1, D_MODEL), lambda i: (0, 0))],
        out_specs=pl.BlockSpec((tm, D_MODEL), lambda i: (i, 0)),
        name="mem_norm", compiler_params=_cp(("parallel",)),
    )(mem2d, g)


def _mem_norm_wgrad(mem2d, d_memn):
    M = mem2d.shape[0]
    tm = _row_tile(M, 512)

    def body(x_ref, d_ref, dg_ref):
        i = pl.program_id(0)
        x = x_ref[...]
        xh = x * lax.rsqrt(jnp.mean(x * x, axis=-1, keepdims=True) + EPS)
        dg = jnp.sum(d_ref[...] * xh, axis=0, keepdims=True)

        @pl.when(i == 0)
        def _():
            dg_ref[...] = dg

        @pl.when(i > 0)
        def _():
            dg_ref[...] += dg

    return _call(
        body, out_shape=_sds((1, D_MODEL), F32), grid=(M // tm,),
        in_specs=[pl.BlockSpec((tm, D_MODEL), lambda i: (i, 0)), pl.BlockSpec((tm, D_MODEL), lambda i: (i, 0))],
        out_specs=pl.BlockSpec((1, D_MODEL), lambda i: (0, 0)),
        name="mem_norm_wgrad", compiler_params=_cp(("arbitrary",)),
    )(mem2d, d_memn)


def _row_mm(a, w_ref):
    ks = w_ref.shape[1]
    out = _dot(a[:, 0:ks], w_ref[0])
    for k in range(1, N_CHIPS):
        out += _dot(a[:, k * ks:(k + 1) * ks], w_ref[k])
    return out


def _row_mm_t(d, w_ref):
    return jnp.concatenate([_dot(d, w_ref[k], "nt") for k in range(N_CHIPS)], axis=1)


def _col_mm(a, w_ref):
    return jnp.concatenate([_dot(a, w_ref[k]) for k in range(N_CHIPS)], axis=1)


def _col_mm_t(d, w_ref):
    ns = w_ref.shape[2]
    out = _dot(d[:, 0:ns], w_ref[0], "nt")
    for k in range(1, N_CHIPS):
        out += _dot(d[:, k * ns:(k + 1) * ns], w_ref[k], "nt")
    return out


def _merge_fwd(x, proj, y_r, y_p, o_m, w_ret_o, w_pool_o, w_mem_o, w_out):
    T = x.shape[0]
    tm = _row_tile(T, 512)

    def body(x_ref, gr_ref, gp_ref, gm_ref, yr_ref, yp_ref, om_ref, wr_ref, wp_ref, wm_ref, wo_ref,
             x1_ref, mg_ref, o3_ref):
        o_r = _row_mm(yr_ref[...], wr_ref)
        o_p = _col_mm(yp_ref[...], wp_ref)
        o_q = _col_mm(om_ref[...], wm_ref)
        merged = (jax.nn.sigmoid(gr_ref[...].astype(F32)) * o_r + jax.nn.sigmoid(gp_ref[...].astype(F32)) * o_p
                  + jax.nn.sigmoid(gm_ref[...].astype(F32)) * o_q)
        mg = merged.astype(mg_ref.dtype)
        mg_ref[...] = mg
        o3_ref[:, 0:D_MODEL] = o_r.astype(o3_ref.dtype)
        o3_ref[:, D_MODEL:2 * D_MODEL] = o_p.astype(o3_ref.dtype)
        o3_ref[:, 2 * D_MODEL:3 * D_MODEL] = o_q.astype(o3_ref.dtype)
        x1_ref[...] = x_ref[...] + _row_mm(mg, wo_ref)

    gb = C_GATE // D_MODEL
    row = lambda w: pl.BlockSpec((tm, w), lambda i: (i, 0))
    return _call(
        body,
        out_shape=(_sds((T, D_MODEL), F32), _sds((T, D_MODEL), _ACT), _sds((T, 3 * D_MODEL), _ACT)),
        grid=(T // tm,),
        in_specs=[
            row(D_MODEL),
            pl.BlockSpec((tm, D_MODEL), lambda i: (i, gb)),
            pl.BlockSpec((tm, D_MODEL), lambda i: (i, gb + 1)),
            pl.BlockSpec((tm, D_MODEL), lambda i: (i, gb + 2)),
            row(D_MODEL), row(POOL_W), row(MEMQ_W),
            _resident(w_ret_o), _resident(w_pool_o), _resident(w_mem_o), _resident(w_out),
        ],
        out_specs=(row(D_MODEL), row(D_MODEL), row(3 * D_MODEL)),
        name="merge_fwd", compiler_params=_cp(("parallel",)),
    )(x, proj, proj, proj, y_r, y_p, o_m, w_ret_o, w_pool_o, w_mem_o, w_out)


def _merge_bwd(dx1, proj, o3, w_ret_o, w_pool_o, w_mem_o, w_out):
    T = dx1.shape[0]
    tm = _row_tile(T, 512)

    def body(dx_ref, gr_ref, gp_ref, gm_ref, o3_ref, wr_ref, wp_ref, wm_ref, wo_ref,
             do3_ref, dgate_ref, dyr_ref, dyp_ref, dom_ref):
        dmerged = _row_mm_t(dx_ref[...], wo_ref)
        douts = []
        dgate_ref[:, 0:MEMQ_W] = jnp.zeros((tm, MEMQ_W), dgate_ref.dtype)
        for n, g_ref in enumerate((gr_ref, gp_ref, gm_ref)):
            sl = slice(n * D_MODEL, (n + 1) * D_MODEL)
            gate = jax.nn.sigmoid(g_ref[...].astype(F32))
            d_out = (dmerged * gate).astype(do3_ref.dtype)
            do3_ref[:, sl] = d_out
            dgate_ref[:, MEMQ_W + n * D_MODEL:MEMQ_W + (n + 1) * D_MODEL] = (
                dmerged * o3_ref[:, sl].astype(F32) * gate * (1.0 - gate)).astype(dgate_ref.dtype)
            douts.append(d_out)
        dyr_ref[...] = _row_mm_t(douts[0], wr_ref).astype(dyr_ref.dtype)
        dyp_ref[...] = _col_mm_t(douts[1], wp_ref).astype(dyp_ref.dtype)
        dom_ref[...] = _col_mm_t(douts[2], wm_ref).astype(dom_ref.dtype)

    gb = C_GATE // D_MODEL
    row = lambda w: pl.BlockSpec((tm, w), lambda i: (i, 0))
    return _call(
        body,
        out_shape=(
            _sds((T, 3 * D_MODEL), _ACT), _sds((T, N_PROJ), _ACT),
            _sds((T, D_MODEL), _ACT), _sds((T, POOL_W), _ACT), _sds((T, MEMQ_W), _ACT),
        ),
        grid=(T // tm,),
        in_specs=[
            row(D_MODEL),
            pl.BlockSpec((tm, D_MODEL), lambda i: (i, gb)),
            pl.BlockSpec((tm, D_MODEL), lambda i: (i, gb + 1)),
            pl.BlockSpec((tm, D_MODEL), lambda i: (i, gb + 2)),
            row(3 * D_MODEL),
            _resident(w_ret_o), _resident(w_pool_o), _resident(w_mem_o), _resident(w_out),
        ],
        out_specs=(row(3 * D_MODEL), pl.BlockSpec((tm, N_PROJ - C_QM), lambda i: (i, 1)),
                   row(D_MODEL), row(POOL_W), row(MEMQ_W)),
        name="merge_bwd", compiler_params=_cp(("parallel",)),
    )(dx1, proj, proj, proj, o3, w_ret_o, w_pool_o, w_mem_o, w_out)


def _loss_head(x, g, target):
    T = x.shape[0]
    tm = _row_tile(T, 512)

    def body(x_ref, g_ref, t_ref, dx_ref, sq_ref, dg_ref):
        i = pl.program_id(0)
        x = x_ref[...]
        gg = g_ref[...]
        r = lax.rsqrt(jnp.mean(x * x, axis=-1, keepdims=True) + EPS)
        xh = x * r
        err = xh * gg - t_ref[...]
        dy = err * (1.0 / D_MODEL)
        dxh = dy * gg
        dx_ref[...] = r * (dxh - xh * jnp.mean(dxh * xh, axis=-1, keepdims=True))
        sq = jnp.sum(err * err, axis=0, keepdims=True)
        dg = jnp.sum(dy * xh, axis=0, keepdims=True)

        @pl.when(i == 0)
        def _():
            sq_ref[...] = sq
            dg_ref[...] = dg

        @pl.when(i > 0)
        def _():
            sq_ref[...] += sq
            dg_ref[...] += dg

    vec = pl.BlockSpec((1, D_MODEL), lambda i: (0, 0))
    row = pl.BlockSpec((tm, D_MODEL), lambda i: (i, 0))
    return _call(
        body,
        out_shape=(_sds((T, D_MODEL), F32), _sds((1, D_MODEL), F32), _sds((1, D_MODEL), F32)),
        grid=(T // tm,), in_specs=[row, vec, row], out_specs=(row, vec, vec),
        name="loss_head", compiler_params=_cp(("arbitrary",)),
    )(x, g, target)


def _block_rows(rows, cols, itemsize, cap_bytes=2 << 20):
    t = rows
    while t * cols * itemsize > cap_bytes and t % 2 == 0 and (t // 2) % 16 == 0:
        t //= 2
    return t


def _cast_into_slot(w3d, layer, me):
    _, R, C = w3d.shape
    tr = _block_rows(R, C, 4)

    def body(me_ref, w_ref, o_ref):
        o_ref[...] = w_ref[...].astype(o_ref.dtype)

    grid_spec = pltpu.PrefetchScalarGridSpec(
        num_scalar_prefetch=1, grid=(R // tr,),
        in_specs=[pl.BlockSpec((None, tr, C), lambda i, me: (layer, i, 0))],
        out_specs=pl.BlockSpec((None, tr, C), lambda i, me: (me[0], i, 0)),
    )
    return _call(body, out_shape=_sds((N_CHIPS, R, C), _COMM), grid_spec=grid_spec,
                 name="cast_into_slot", compiler_params=_cp(("parallel",)))(me, w3d)


def _adamw(w, g, m, v):
    R, C = w.shape
    tr = _block_rows(R, C, 4, 1 << 20)

    def body(w_ref, g_ref, m_ref, v_ref, d_ref, nm_ref, nv_ref):
        g = g_ref[...]
        m = ADAM_B1 * m_ref[...] + (1.0 - ADAM_B1) * g
        v = ADAM_B2 * v_ref[...] + (1.0 - ADAM_B2) * (g * g)
        m_hat = m / (1.0 - ADAM_B1 ** ADAM_STEP)
        v_hat = v / (1.0 - ADAM_B2 ** ADAM_STEP)
        d_ref[...] = -ADAM_LR * (m_hat / (jnp.sqrt(v_hat) + ADAM_EPS) + ADAM_WD * w_ref[...])
        nm_ref[...] = m
        nv_ref[...] = v

    spec = pl.BlockSpec((tr, C), lambda i: (i, 0))
    out = _sds((R, C), F32)
    return _call(body, out_shape=(out, out, out), grid=(R // tr,), in_specs=[spec] * 4, out_specs=(spec,) * 3,
                 name="adamw", compiler_params=_cp(("parallel",)))(w, g, m, v)


def _add_sibling_half(g_full, land, my_c):
    _, R, C = g_full.shape
    hr = R // 2
    tr = _block_rows(hr, C, 2, 1 << 20)
    nb = hr // tr

    def body(c_ref, g_ref, l_ref, o_ref):
        o_ref[...] = (g_ref[...].astype(F32) + l_ref[...].astype(F32)).astype(o_ref.dtype)

    grid_spec = pltpu.PrefetchScalarGridSpec(
        num_scalar_prefetch=1, grid=(N_CHIPS, nb),
        in_specs=[
            pl.BlockSpec((None, tr, C), lambda k, i, c: (k, c[0] * nb + i, 0)),
            pl.BlockSpec((None, tr, C), lambda k, i, c: (k, i, 0)),
        ],
        out_specs=pl.BlockSpec((None, tr, C), lambda k, i, c: (k, i, 0)),
    )
    return _call(body, out_shape=_sds((N_CHIPS, hr, C), _COMM), grid_spec=grid_spec,
                 name="add_sibling_half", compiler_params=_cp(("parallel", "parallel")))(my_c, g_full, land)


def _add_chips(land, sums, g_all, layer, my_chip, my_core):
    _, hr, C = land.shape
    tr = _block_rows(hr, C, 4, 1 << 20)
    nb = hr // tr

    def body(me_ref, core_ref, own_ref, a_ref, b_ref, c_ref, *rest):
        o_ref = rest[-1]
        o_ref[...] = ((own_ref[...].astype(F32) + a_ref[...].astype(F32)) + b_ref[...].astype(F32)) + c_ref[...].astype(F32)

    def other(d):
        return pl.BlockSpec((None, tr, C), lambda i, me, core: ((me[0] + d) % N_CHIPS, i, 0))

    operands = [my_chip, my_core, sums, land, land, land]
    in_specs = [other(0), other(1), other(2), other(3)]
    aliases = {}
    if g_all is not None:
        operands.append(g_all)
        in_specs.append(ANY)
        aliases = {len(operands) - 1: 0}
    grid_spec = pltpu.PrefetchScalarGridSpec(
        num_scalar_prefetch=2, grid=(nb,), in_specs=in_specs,
        out_specs=pl.BlockSpec((None, tr, C), lambda i, me, core: (layer, core[0] * nb + i, 0)),
    )
    return _call(body, out_shape=_sds((DEPTH, 2 * hr, C), F32), grid_spec=grid_spec, input_output_aliases=aliases,
                 name="add_chips", compiler_params=_cp(("parallel",)))(*operands)


def _place():
    x, y, c = lax.axis_index("x"), lax.axis_index("y"), lax.axis_index("c")
    chips = [(1 - x, y), (x, 1 - y), (1 - x, 1 - y)]
    return x, y, c, 2 * x + y, chips


def _remote(src, dst, send_sem, recv_sem, dev):
    return pltpu.make_async_remote_copy(src_ref=src, dst_ref=dst, send_sem=send_sem, recv_sem=recv_sem,
                                        device_id=dev, device_id_type=MESH)


def _gather_sems(n):
    return [pltpu.SemaphoreType.DMA((n, 6)), pltpu.SemaphoreType.DMA((n, 6))]


def _gather_start(bufs, send_sems, recv_sems):
    x, y, c, me, chips = _place()
    for w, buf in enumerate(bufs):
        hr = buf.shape[1] // 2
        own = buf.at[me, pl.ds(c * hr, hr)]
        for j, chip in enumerate(chips):
            _remote(own, own, send_sems.at[w, j], recv_sems.at[w, j], (*chip, c)).start()


def _gather_finish(bufs, send_sems, recv_sems):
    x, y, c, me, chips = _place()
    sibling = (x, y, 1 - c)
    waits = []
    for w, buf in enumerate(bufs):
        hr = buf.shape[1] // 2
        mine = pl.ds(c * hr, hr)
        for j, chip in enumerate(chips):
            kc = 2 * chip[0] + chip[1]
            got = buf.at[kc, mine]
            first = _remote(got, got, send_sems.at[w, j], recv_sems.at[w, j], (*chip, c))
            first.wait_recv()
            fwd = _remote(got, got, send_sems.at[w, 3 + j], recv_sems.at[w, 3 + j], sibling)
            fwd.start()
            waits += [first.wait_send, fwd.wait_send]
    for w, buf in enumerate(bufs):
        hr = buf.shape[1] // 2
        theirs = pl.ds((1 - c) * hr, hr)
        for j, chip in enumerate(chips):
            kc = 2 * chip[0] + chip[1]
            got = buf.at[kc, theirs]
            _remote(got, got, send_sems.at[w, 3 + j], recv_sems.at[w, 3 + j], sibling).wait_recv()
    for wait in waits:
        wait()


def _allgather_weights(bufs):
    n = len(bufs)

    def body(*refs):
        outs = refs[n:2 * n]
        send_sems, recv_sems = refs[2 * n:]
        _gather_start(outs, send_sems, recv_sems)
        _gather_finish(outs, send_sems, recv_sems)

    out_shape = tuple(_sds(b.shape, b.dtype) for b in bufs)
    return _call(
        body, out_shape=out_shape, in_specs=[ANY] * n, out_specs=(ANY,) * n,
        input_output_aliases={i: i for i in range(n)},
        scratch_shapes=_gather_sems(n),
        name="allgather_weights",
    )(*bufs)


def _swap_sibling_halves(grads):
    n = len(grads)

    def body(*refs):
        ins, outs = refs[:n], refs[n:2 * n]
        send_sems, recv_sems = refs[2 * n:]
        x, y, c, _, _ = _place()
        copies = []
        for w in range(n):
            hr = ins[w].shape[1] // 2
            cp = _remote(ins[w].at[:, pl.ds((1 - c) * hr, hr)], outs[w], send_sems.at[w], recv_sems.at[w], (x, y, 1 - c))
            cp.start()
            copies.append(cp)
        for cp in copies:
            cp.wait()

    out_shape = tuple(_sds((N_CHIPS, g.shape[1] // 2, g.shape[2]), g.dtype) for g in grads)
    return _call(
        body, out_shape=out_shape, in_specs=[ANY] * n, out_specs=(ANY,) * n,
        scratch_shapes=[pltpu.SemaphoreType.DMA((n,)), pltpu.SemaphoreType.DMA((n,))],
        name="swap_sibling_halves",
    )(*grads)


def _scatter_sems(n):
    return [pltpu.SemaphoreType.DMA((n, 3)), pltpu.SemaphoreType.DMA((n, 3))]


def _scatter_shapes(sums):
    return tuple(_sds(s.shape, s.dtype) for s in sums)


def _scatter_start(sums, parts, send_sems, recv_sems):
    x, y, c, me, chips = _place()
    for w in range(len(sums)):
        for j, chip in enumerate(chips):
            kc = 2 * chip[0] + chip[1]
            _remote(sums[w].at[kc], parts[w].at[me], send_sems.at[w, j], recv_sems.at[w, j], (*chip, c)).start()


def _scatter_finish(sums, parts, send_sems, recv_sems):
    x, y, c, me, chips = _place()
    for w in range(len(sums)):
        for j, chip in enumerate(chips):
            kc = 2 * chip[0] + chip[1]
            _remote(sums[w].at[kc], parts[w].at[kc], send_sems.at[w, j], recv_sems.at[w, j], (*chip, c)).wait()


def _join_sibling_halves(g_alls, layer):
    n = len(g_alls)

    def body(*refs):
        outs = refs[n:2 * n]
        send_sems, recv_sems = refs[2 * n:]
        x, y, c, _, _ = _place()
        copies = []
        for w in range(n):
            hr = outs[w].shape[1] // 2
            mine = outs[w].at[layer, pl.ds(c * hr, hr)]
            cp = _remote(mine, mine, send_sems.at[w], recv_sems.at[w], (x, y, 1 - c))
            cp.start()
            copies.append(cp)
        for cp in copies:
            cp.wait()

    out_shape = tuple(_sds(g.shape, g.dtype) for g in g_alls)
    return _call(
        body, out_shape=out_shape, in_specs=[ANY] * n, out_specs=(ANY,) * n,
        input_output_aliases={i: i for i in range(n)},
        scratch_shapes=[pltpu.SemaphoreType.DMA((n,)), pltpu.SemaphoreType.DMA((n,))],
        name="join_sibling_halves",
    )(*g_alls)


def _chip_sums(grads, my_core):
    land = _swap_sibling_halves(grads)
    return [_add_sibling_half(g, l, my_core) for g, l in zip(grads, land)]


def _finish_reduce_scatter(parts, sums, g_alls, layer, my_chip, my_core):
    g_alls = [_add_chips(p, s, ga, layer, my_chip, my_core) for p, s, ga in zip(parts, sums, g_alls)]
    return _join_sibling_halves(g_alls, layer)


def _allreduce_small(v):
    R = v.shape[0]

    def body(v_ref, out_ref, sib_ref, chip_ref, sum_ref, send_sems, recv_sems):
        x, y, c, me, chips = _place()
        swap = _remote(v_ref, sib_ref, send_sems.at[0], recv_sems.at[0], (x, y, 1 - c))
        swap.start()
        swap.wait()
        sum_ref[...] = v_ref[...] + sib_ref[...]
        copies = []
        for j, chip in enumerate(chips):
            cp = _remote(sum_ref, chip_ref.at[me], send_sems.at[1 + j], recv_sems.at[1 + j], (*chip, c))
            cp.start()
            copies.append(cp)
        chip_ref[me] = sum_ref[...]
        for cp in copies:
            cp.wait()
        out_ref[...] = ((chip_ref[0] + chip_ref[1]) + chip_ref[2]) + chip_ref[3]

    vm = pl.BlockSpec(memory_space=pltpu.VMEM)
    return _call(
        body, out_shape=_sds((R, 128), F32), in_specs=[vm], out_specs=vm,
        scratch_shapes=[
            pltpu.VMEM((R, 128), F32), pltpu.VMEM((N_CHIPS, R, 128), F32), pltpu.VMEM((R, 128), F32),
            pltpu.SemaphoreType.DMA((4,)), pltpu.SemaphoreType.DMA((4,)),
        ],
        name="allreduce_small", compiler_params=_cp(),
    )(v)


def _pack_small(parts):
    rows = []
    for p in parts:
        flat = p.reshape(-1).astype(F32)
        pad = (-flat.shape[0]) % 128
        rows.append(jnp.pad(flat, (0, pad)).reshape(-1, 128))
    packed = jnp.concatenate(rows, axis=0)
    pad_rows = (-packed.shape[0]) % 8
    return jnp.pad(packed, ((0, pad_rows), (0, 0)))


def _unpack_small(packed, like):
    out, r = [], 0
    for p in like:
        n = p.size
        nr = -(-n // 128)
        out.append(packed[r:r + nr].reshape(-1)[:n].reshape(p.shape))
        r += nr
    return out


BIG = ("w_in", "w_ret_o", "w_pool_o", "w_mem_kv", "w_mem_o", "w_out", "w_ff1", "w_ff2")


def _rope_tables(S):
    inv = ROPE_BASE ** (-jnp.arange(0, QK_DIM, 2, dtype=F32) / QK_DIM)
    ang = jnp.arange(S).astype(F32)[:, None] * inv[None, :]
    cos, sin = jnp.cos(ang), jnp.sin(ang)
    return jnp.concatenate([cos, cos], axis=1), jnp.concatenate([-sin, sin], axis=1)


MIXER_W = ("w_ret_o", "w_pool_o", "w_mem_kv", "w_mem_o", "w_out")
AFTER_IN = MIXER_W + ("w_ff1", "w_ff2")
GATHER_PLAN = {
    ("in_proj", 0): [(0, AFTER_IN)],
    ("retention", 0): [(1, ("w_in",))],
    ("ff1", 0): [(1, MIXER_W)],
    ("ff2", 0): [(1, ("w_ff1",))],
    ("in_proj", 1): [(1, ("w_ff2",)), (2, ("w_in",))],
    ("retention", 1): [(2, MIXER_W + ("w_ff1",))],
    ("ff1", 1): [(2, ("w_ff2",))],
    ("ff2", 1): [(3, ("w_in",))],
    ("in_proj", 2): [(3, AFTER_IN)],
}


def _layer_fwd(x, layer, wbuf, small, memn, cos, sin, B, S):
    g1, g2, lg, w_grp, scale = small

    def hosted(host):
        keys = [(l2, n) for l2, names in GATHER_PLAN.get((host, layer), ()) for n in names]
        return keys, [wbuf[k] for k in keys]

    W = lambda n: wbuf[(layer, n)]
    keys, bufs = hosted("in_proj")
    proj, h1, bufs = _rms_proj("in_proj", x, g1, W("w_in"), N_PROJ, gather=bufs)
    wbuf.update(zip(keys, bufs))
    keys, bufs = hosted("retention")
    o, y_r, bufs = _retention_fwd(proj, cos, sin, lg, B, S, gather=bufs)
    wbuf.update(zip(keys, bufs))
    y_p = _pool_fwd(proj, w_grp, scale, B, S)
    kv = _rowsharded_mm("mem_kv", memn, W("w_mem_kv"), None)
    o_m = _mem_attn_fwd(proj, kv, B, S)
    x1, merged, o3 = _merge_fwd(x, proj, y_r, y_p, o_m, W("w_ret_o"), W("w_pool_o"), W("w_mem_o"), W("w_out"))
    keys, bufs = hosted("ff1")
    u, h2, bufs = _rms_proj("ff1", x1, g2, W("w_ff1"), FFN_HIDDEN, gather=bufs)
    wbuf.update(zip(keys, bufs))
    keys, bufs = hosted("ff2")
    x2, bufs = _ffn_out(x1, u, W("w_ff2"), gather=bufs)
    wbuf.update(zip(keys, bufs))
    saved = dict(x=x, proj=proj, h1=h1, o=o, y_r=y_r, y_p=y_p, kv=kv, o_m=o_m, merged=merged, o3=o3, x1=x1, h2=h2, u=u)
    return x2, saved


EARLY_GRADS = ("w_ff1", "w_ff2", "w_out", "w_ret_o", "w_pool_o", "w_mem_o")
LATE_GRADS = ("w_in", "w_mem_kv")


def _layer_bwd(dx2, sv, W, small, memn, d_memn, cos, sin, B, S, big_grads, layer, my_chip, my_core):
    g1, g2, lg, w_grp, scale = small
    x, proj, x1, u = sv["x"], sv["proj"], sv["x1"], sv["u"]
    grads = {}
    du = _ffn_out_dgrad(dx2, u, W["w_ff2"])
    grads["w_ff2"] = _rowsharded_wgrad("ff2_wgrad", u, dx2, FFN_HIDDEN, pro=_relu2)
    dx1, dg2, _ = _dgrad_rms("ff1_dgrad", du, W["w_ff1"], x1, g2, dx2)
    grads["w_ff1"] = _colsharded_wgrad_full("ff1_wgrad", sv["h2"], du, FFN_HIDDEN, 1024)
    do3, dproj, dy_r, dy_p, do_m = _merge_bwd(dx1, proj, sv["o3"], W["w_ret_o"], W["w_pool_o"], W["w_mem_o"], W["w_out"])
    grads["w_out"], grads["w_ret_o"], grads["w_pool_o"], grads["w_mem_o"] = _mixer_wgrads(
        sv["merged"], sv["y_r"], sv["y_p"], sv["o_m"], dx1, do3)
    d_o, dproj = _gn_gate_bwd(dy_r, sv["o"], proj, dproj)
    sums = _chip_sums([grads[n] for n in EARLY_GRADS], my_core)
    dproj, dlg, parts = _retention_bwd(proj, d_o, dproj, cos, sin, lg, B, S, scatter=sums)
    done = _finish_reduce_scatter(parts, sums, [big_grads[n] for n in EARLY_GRADS], layer, my_chip, my_core)
    big_grads.update(zip(EARLY_GRADS, done))
    dproj, dw_grp, dscale = _pool_bwd(proj, dy_p, dproj, w_grp, scale, B, S)
    dproj, dkv = _mem_attn_bwd(proj, sv["kv"], do_m, dproj, B, S)
    grads["w_mem_kv"] = _rowsharded_wgrad("mem_kv_wgrad", memn, dkv, D_MODEL)
    if d_memn is None:
        d_memn = _rowsharded_dgrad("mem_kv_dgrad", dkv, W["w_mem_kv"], out_dtype=F32)
    else:
        d_memn = _rowsharded_dgrad("mem_kv_dgrad_acc", dkv, W["w_mem_kv"], epi=_add_res, epi_in=d_memn, out_dtype=F32)
    grads["w_in"] = _colsharded_wgrad_full("in_wgrad", sv["h1"], dproj, N_PROJ, 1792)
    sums = _chip_sums([grads[n] for n in LATE_GRADS], my_core)
    dx, dg1, parts = _dgrad_rms("in_dgrad", dproj, W["w_in"], x, g1, dx1, scatter=sums)
    done = _finish_reduce_scatter(parts, sums, [big_grads[n] for n in LATE_GRADS], layer, my_chip, my_core)
    big_grads.update(zip(LATE_GRADS, done))
    small_grads = dict(g1=dg1, g2=dg2, lg=jnp.sum(dlg, axis=(0, 2)).reshape(N_HEADS, 2).T, w_grp=dw_grp, scale=dscale)
    return dx, small_grads, d_memn


def kernel(x, mem, w_in, ret_decay_logit, w_ret_o, w_pool_grp, pool_scale, w_pool_o, w_mem_kv, w_mem_o, w_out, w_ff1, w_ff2, norm1_g, norm2_g, mem_norm_g, final_norm_g, loss_target, m_w_in, m_ret_decay_logit, m_w_ret_o, m_w_pool_grp, m_pool_scale, m_w_pool_o, m_w_mem_kv, m_w_mem_o, m_w_out, m_w_ff1, m_w_ff2, m_norm1_g, m_norm2_g, m_mem_norm_g, m_final_norm_g, v_w_in, v_ret_decay_logit, v_w_ret_o, v_w_pool_grp, v_pool_scale, v_w_pool_o, v_w_mem_kv, v_w_mem_o, v_w_out, v_w_ff1, v_w_ff2, v_norm1_g, v_norm2_g, v_mem_norm_g, v_final_norm_g):
    B, S, _ = x.shape
    T = B * S
    big_w = dict(w_in=w_in, w_ret_o=w_ret_o, w_pool_o=w_pool_o, w_mem_kv=w_mem_kv, w_mem_o=w_mem_o, w_out=w_out, w_ff1=w_ff1, w_ff2=w_ff2)
    big_m = dict(w_in=m_w_in, w_ret_o=m_w_ret_o, w_pool_o=m_w_pool_o, w_mem_kv=m_w_mem_kv, w_mem_o=m_w_mem_o, w_out=m_w_out, w_ff1=m_w_ff1, w_ff2=m_w_ff2)
    big_v = dict(w_in=v_w_in, w_ret_o=v_w_ret_o, w_pool_o=v_w_pool_o, w_mem_kv=v_w_mem_kv, w_mem_o=v_w_mem_o, w_out=v_w_out, w_ff1=v_w_ff1, w_ff2=v_w_ff2)
    small_w = [ret_decay_logit, w_pool_grp, pool_scale, norm1_g, norm2_g, mem_norm_g, final_norm_g]
    small_m = [m_ret_decay_logit, m_w_pool_grp, m_pool_scale, m_norm1_g, m_norm2_g, m_mem_norm_g, m_final_norm_g]
    small_v = [v_ret_decay_logit, v_w_pool_grp, v_pool_scale, v_norm1_g, v_norm2_g, v_mem_norm_g, v_final_norm_g]
    my_chip = (2 * lax.axis_index("x") + lax.axis_index("y")).astype(jnp.int32).reshape(1)
    my_core = lax.axis_index("c").astype(jnp.int32).reshape(1)

    wbuf = {(l, n): _cast_into_slot(big_w[n], l, my_chip) for l in range(DEPTH) for n in BIG}
    wbuf[(0, "w_in")] = _allgather_weights([wbuf[(0, "w_in")]])[0]

    cos, sin = _rope_tables(S)
    log_g = jax.nn.log_sigmoid(ret_decay_logit.astype(F32))
    mem2d = mem.reshape(B * MEM_LEN, D_MODEL)
    memn = _mem_norm(mem2d, mem_norm_g.reshape(1, D_MODEL))
    smalls = [(norm1_g[l].reshape(1, D_MODEL), norm2_g[l].reshape(1, D_MODEL), log_g[l], w_pool_grp[l],
               pool_scale[l].reshape(1, POOL_W)) for l in range(DEPTH)]

    h = x.reshape(T, D_MODEL)
    saved = []
    for l in range(DEPTH):
        h, sv = _layer_fwd(h, l, wbuf, smalls[l], memn, cos, sin, B, S)
        saved.append(sv)
    weights = [{n: wbuf[(l, n)] for n in BIG} for l in range(DEPTH)]

    dh, sq, d_final_g = _loss_head(h, final_norm_g.reshape(1, D_MODEL), loss_target.reshape(T, D_MODEL))
    loss = lax.psum(0.5 * jnp.sum(sq) / D_MODEL, ("x", "y", "c"))

    big_grads = dict.fromkeys(BIG)
    small_grads = [None] * DEPTH
    d_memn = None
    for l in reversed(range(DEPTH)):
        dh, small_grads[l], d_memn = _layer_bwd(
            dh, saved[l], weights[l], smalls[l], memn, d_memn, cos, sin, B, S, big_grads, l, my_chip, my_core)
    d_mem_g = _mem_norm_wgrad(mem2d, d_memn)

    d_logit = jnp.stack([sg["lg"] for sg in small_grads]) * jax.nn.sigmoid(-ret_decay_logit.astype(F32))
    small_g_local = [
        d_logit, jnp.stack([sg["w_grp"] for sg in small_grads]),
        jnp.stack([sg["scale"].reshape(POOL_W) for sg in small_grads]),
        jnp.stack([sg["g1"].reshape(D_MODEL) for sg in small_grads]),
        jnp.stack([sg["g2"].reshape(D_MODEL) for sg in small_grads]),
        d_mem_g.reshape(D_MODEL), d_final_g.reshape(D_MODEL),
    ]
    small_g = _allreduce_small(_pack_small(small_g_local))
    s_delta, s_m, s_v = _adamw(_pack_small(small_w), small_g, _pack_small(small_m), _pack_small(small_v))
    small_g, s_delta, s_m, s_v = (_unpack_small(a, small_w) for a in (small_g, s_delta, s_m, s_v))

    big_out = {}
    for n in BIG:
        w = big_w[n]
        g = big_grads[n]
        flat = lambda a: a.reshape(-1, a.shape[-1])
        d, nm, nv = _adamw(flat(w), flat(g), flat(big_m[n]), flat(big_v[n]))
        big_out[n] = (g, d.reshape(w.shape), nm.reshape(w.shape), nv.reshape(w.shape))

    order = ["w_in", "ret_decay_logit", "w_ret_o", "w_pool_grp", "pool_scale", "w_pool_o", "w_mem_kv", "w_mem_o",
             "w_out", "w_ff1", "w_ff2", "norm1_g", "norm2_g", "mem_norm_g", "final_norm_g"]
    small_names = ["ret_decay_logit", "w_pool_grp", "pool_scale", "norm1_g", "norm2_g", "mem_norm_g", "final_norm_g"]
    outs = [[], [], [], []]
    for n in order:
        if n in big_out:
            vals = big_out[n]
        else:
            i = small_names.index(n)
            vals = (small_g[i], s_delta[i], s_m[i], s_v[i])
        for k in range(4):
            outs[k].append(vals[k])
    return (loss, dh.reshape(B, S, D_MODEL), *outs[0], *outs[1], *outs[2], *outs[3])
```

```python
import functools

import jax
import jax.numpy as jnp
from jax import lax
from jax.experimental import pallas as pl
from jax.experimental.pallas import tpu as pltpu

F32 = jnp.float32
_MXU = jnp.bfloat16
_ACT = jnp.bfloat16
_COMM = jnp.bfloat16

D_MODEL = 1024
N_PROJ = 7168
FFN_HIDDEN = 4096
MEM_LEN = 256
N_HEADS = 4
QK_DIM = 128
V_DIM = 256
POOL_W = 512
MEMQ_W = 512
DEPTH = 4
N_CHIPS = 4
EPS = 1e-6
ROPE_BASE = 10000.0

C_Q, C_K, C_V, C_G, C_P, C_QM, C_GATE = 0, 512, 1024, 2048, 3072, 3584, 4096

ADAM_LR = 0.001
ADAM_B1 = 0.9
ADAM_B2 = 0.999
ADAM_EPS = 1e-08
ADAM_WD = 0.01
ADAM_STEP = 10

VMEM_LIMIT_BYTES = 56 * 1024 * 1024
MESH = pl.DeviceIdType.MESH
ANY = pl.BlockSpec(memory_space=pl.ANY)

_DN = {
    "nn": (((1,), (0,)), ((), ())),
    "nt": (((1,), (1,)), ((), ())),
    "tn": (((0,), (0,)), ((), ())),
}


def _call(body, **kw):
    return pl.pallas_call(body, **kw)


def _cp(sem=None):
    return pltpu.CompilerParams(dimension_semantics=sem, vmem_limit_bytes=VMEM_LIMIT_BYTES)


def _dot(a, b, kind="nn"):
    return lax.dot_general(a.astype(_MXU), b.astype(_MXU), _DN[kind], preferred_element_type=F32)


def _sds(shape, dtype):
    return jax.ShapeDtypeStruct(shape, dtype)


def _rms_rows(x, g):
    r = lax.rsqrt(jnp.mean(x * x, axis=-1, keepdims=True) + EPS)
    return x * r * g


def _relu2(u):
    r = jnp.maximum(u.astype(F32), 0.0)
    return r * r


def _add_res(r, e):
    return r + e.astype(F32)


def _relu2_bwd(r, u):
    return r * (2.0 * jnp.maximum(u.astype(F32), 0.0))


def _mm(name, kind, a, b, a_spec, b_spec, out_shape, o_spec, grid, acc_shape, *,
        pro=None, pro_in=(), pro_specs=(), epi=None, epi_in=(), epi_specs=()):
    nk = grid[2]
    npro, nepi = len(pro_in), len(epi_in)

    def body(*refs):
        a_ref, b_ref = refs[0], refs[1]
        pro_refs = refs[2:2 + npro]
        epi_refs = refs[2 + npro:2 + npro + nepi]
        o_ref = refs[2 + npro + nepi]
        av = a_ref[...]
        if pro is not None:
            av = pro(av, *[r[...] for r in pro_refs])
        part = _dot(av, b_ref[...], kind)

        def finish(r):
            if epi is not None:
                r = epi(r, *[e[...] for e in epi_refs])
            o_ref[...] = r.astype(o_ref.dtype)

        if nk == 1:
            finish(part)
        else:
            acc = refs[-1]
            k = pl.program_id(2)

            @pl.when(k == 0)
            def _():
                acc[...] = part

            @pl.when(k > 0)
            def _():
                acc[...] += part

            @pl.when(k == nk - 1)
            def _():
                finish(acc[...])

    scratch = [] if nk == 1 else [pltpu.VMEM(acc_shape, F32)]
    return _call(
        body, out_shape=out_shape, grid=grid,
        in_specs=[a_spec, b_spec, *pro_specs, *epi_specs], out_specs=o_spec,
        scratch_shapes=scratch, name=name,
        compiler_params=_cp(("parallel", "parallel", "arbitrary")),
    )(a, b, *pro_in, *epi_in)


def _row_tile(n, cap):
    t = min(n, cap)
    assert n % t == 0, (n, t)
    return t


def _resident(w):
    nd = w.ndim
    return pl.BlockSpec(w.shape, lambda i: (0,) * nd, pipeline_mode=pl.Buffered(1))


def _rms_proj(name, x, g, w, n_cols, gather=()):
    T = x.shape[0]
    tm = _row_tile(T, 512)
    ns = n_cols // N_CHIPS
    steps = T // tm
    ng = len(gather)

    def body(x_ref, g_ref, w_ref, *rest):
        o_ref, h_ref = rest[ng], rest[ng + 1]
        bufs, sems = rest[ng + 2:2 * ng + 2], rest[2 * ng + 2:]
        if ng:
            @pl.when(pl.program_id(0) == 0)
            def _():
                _gather_start(bufs, *sems)

        h = _rms_rows(x_ref[...], g_ref[...]).astype(h_ref.dtype)
        h_ref[...] = h
        for k in range(N_CHIPS):
            o_ref[:, k * ns:(k + 1) * ns] = _dot(h, w_ref[k]).astype(o_ref.dtype)

        if ng:
            @pl.when(pl.program_id(0) == steps - 1)
            def _():
                _gather_finish(bufs, *sems)

    row = pl.BlockSpec((tm, D_MODEL), lambda i: (i, 0))
    out = _call(
        body,
        out_shape=(_sds((T, n_cols), _ACT), _sds((T, D_MODEL), _ACT), *[_sds(b.shape, b.dtype) for b in gather]),
        grid=(steps,),
        in_specs=[row, pl.BlockSpec((1, D_MODEL), lambda i: (0, 0)), _resident(w)] + [ANY] * ng,
        out_specs=(pl.BlockSpec((tm, n_cols), lambda i: (i, 0)), row) + (ANY,) * ng,
        input_output_aliases={3 + i: 2 + i for i in range(ng)},
        scratch_shapes=_gather_sems(ng) if ng else [],
        name=name, compiler_params=_cp(("arbitrary",)),
    )(x, g, w, *gather)
    return out[0], out[1], list(out[2:])


def _colsharded_wgrad_full(name, h, dy, n_cols, tn):
    T = h.shape[0]
    tt = _row_tile(T, 2048)
    per = (n_cols // N_CHIPS) // tn
    return _mm(
        name, "tn", h, dy,
        pl.BlockSpec((tt, D_MODEL), lambda i, j, k: (k, 0)),
        pl.BlockSpec((tt, tn), lambda i, j, k: (k, j)),
        _sds((N_CHIPS, D_MODEL, n_cols // N_CHIPS), _COMM),
        pl.BlockSpec((None, D_MODEL, tn), lambda i, j, k: (j // per, 0, j % per)),
        (1, n_cols // tn, T // tt), (D_MODEL, tn),
    )


def _dgrad_rms(name, dy, w, x, g, dres, scatter=()):
    T, kd = dy.shape
    tm = _row_tile(T, 512)
    ns = kd // N_CHIPS
    steps = T // tm
    nsc = len(scatter)

    def body(dy_ref, w_ref, x_ref, g_ref, dres_ref, *rest):
        sums = rest[:nsc]
        dx_ref, dg_ref = rest[nsc], rest[nsc + 1]
        parts, sems = rest[nsc + 2:2 * nsc + 2], rest[2 * nsc + 2:]
        i = pl.program_id(0)
        if nsc:
            @pl.when(i == 0)
            def _():
                _scatter_start(sums, parts, *sems)

        dh = _dot(dy_ref[:, 0:ns], w_ref[0], "nt")
        for k in range(1, N_CHIPS):
            dh += _dot(dy_ref[:, k * ns:(k + 1) * ns], w_ref[k], "nt")
        x = x_ref[...]
        r = lax.rsqrt(jnp.mean(x * x, axis=-1, keepdims=True) + EPS)
        xh = x * r
        dxh = dh * g_ref[...]
        dx_ref[...] = dres_ref[...] + r * (dxh - xh * jnp.mean(dxh * xh, axis=-1, keepdims=True))
        dgp = jnp.sum(dh * xh, axis=0, keepdims=True)

        @pl.when(i == 0)
        def _():
            dg_ref[...] = dgp

        @pl.when(i > 0)
        def _():
            dg_ref[...] += dgp

        if nsc:
            @pl.when(i == steps - 1)
            def _():
                _scatter_finish(sums, parts, *sems)

    row = pl.BlockSpec((tm, D_MODEL), lambda i: (i, 0))
    vec = pl.BlockSpec((1, D_MODEL), lambda i: (0, 0))
    out = _call(
        body,
        out_shape=(_sds((T, D_MODEL), F32), _sds((1, D_MODEL), F32), *_scatter_shapes(scatter)),
        grid=(steps,),
        in_specs=[pl.BlockSpec((tm, kd), lambda i: (i, 0)), _resident(w), row, vec, row] + [ANY] * nsc,
        out_specs=(row, vec) + (ANY,) * nsc,
        scratch_shapes=_scatter_sems(nsc) if nsc else [],
        name=name, compiler_params=_cp(("arbitrary",)),
    )(dy, w, x, g, dres, *scatter)
    return out[0], out[1], list(out[2:])


def _ffn_out(x1, u, w, gather=()):
    T = u.shape[0]
    tm = _row_tile(T, 512)
    ks = w.shape[1]
    steps = T // tm
    ng = len(gather)

    def body(x_ref, u_ref, w_ref, *rest):
        o_ref = rest[ng]
        bufs, sems = rest[ng + 1:2 * ng + 1], rest[2 * ng + 1:]
        if ng:
            @pl.when(pl.program_id(0) == 0)
            def _():
                _gather_start(bufs, *sems)

        acc = x_ref[...]
        for k in range(N_CHIPS):
            acc += _dot(_relu2(u_ref[:, k * ks:(k + 1) * ks]), w_ref[k])
        o_ref[...] = acc

        if ng:
            @pl.when(pl.program_id(0) == steps - 1)
            def _():
                _gather_finish(bufs, *sems)

    row = pl.BlockSpec((tm, D_MODEL), lambda i: (i, 0))
    out = _call(
        body, out_shape=(_sds((T, D_MODEL), F32), *[_sds(b.shape, b.dtype) for b in gather]), grid=(steps,),
        in_specs=[row, pl.BlockSpec((tm, FFN_HIDDEN), lambda i: (i, 0)), _resident(w)] + [ANY] * ng,
        out_specs=(row,) + (ANY,) * ng,
        input_output_aliases={3 + i: 1 + i for i in range(ng)},
        scratch_shapes=_gather_sems(ng) if ng else [],
        name="ff2", compiler_params=_cp(("arbitrary",)),
    )(x1, u, w, *gather)
    return out[0], list(out[1:])


def _ffn_out_dgrad(dx2, u, w):
    T = u.shape[0]
    tm = _row_tile(T, 512)
    ks = w.shape[1]

    def body(d_ref, u_ref, w_ref, o_ref):
        d = d_ref[...].astype(_MXU)
        for k in range(N_CHIPS):
            sl = slice(k * ks, (k + 1) * ks)
            o_ref[:, sl] = _relu2_bwd(_dot(d, w_ref[k], "nt"), u_ref[:, sl]).astype(o_ref.dtype)

    wide = pl.BlockSpec((tm, FFN_HIDDEN), lambda i: (i, 0))
    return _call(
        body, out_shape=_sds((T, FFN_HIDDEN), _ACT), grid=(T // tm,),
        in_specs=[pl.BlockSpec((tm, D_MODEL), lambda i: (i, 0)), wide, _resident(w)], out_specs=wide,
        name="ff2_dgrad", compiler_params=_cp(("parallel",)),
    )(dx2, u, w)


def _rowsharded_mm(name, a, w, res, *, pro=None):
    T, K = a.shape
    ks = K // N_CHIPS
    tm = _row_tile(T, 512)
    epi = dict(epi=_add_res, epi_in=(res,), epi_specs=(pl.BlockSpec((tm, D_MODEL), lambda i, j, k: (i, 0)),)) if res is not None else {}
    return _mm(
        name, "nn", a, w,
        pl.BlockSpec((tm, ks), lambda i, j, k: (i, k)),
        pl.BlockSpec((None, ks, D_MODEL), lambda i, j, k: (k, 0, 0)),
        _sds((T, D_MODEL), F32 if res is not None else _ACT),
        pl.BlockSpec((tm, D_MODEL), lambda i, j, k: (i, 0)),
        (T // tm, 1, N_CHIPS), (tm, D_MODEL), pro=pro, **epi,
    )


def _rowsharded_dgrad(name, dy, w, *, epi=None, epi_in=None, out_dtype=_ACT, res=None):
    T = dy.shape[0]
    ks = w.shape[1]
    tm = _row_tile(T, 512)
    kw = {}
    if epi is not None:
        kw = dict(epi=epi, epi_in=(epi_in,), epi_specs=(pl.BlockSpec((tm, ks), lambda i, j, k: (i, j)),))
    return _mm(
        name, "nt", dy, w,
        pl.BlockSpec((tm, D_MODEL), lambda i, j, k: (i, 0)),
        pl.BlockSpec((None, ks, D_MODEL), lambda i, j, k: (j, 0, 0)),
        _sds((T, ks * N_CHIPS), out_dtype),
        pl.BlockSpec((tm, ks), lambda i, j, k: (i, j)),
        (T // tm, N_CHIPS, 1), None, **kw,
    )


def _rowsharded_wgrad(name, a, dy, K, *, pro=None):
    T = a.shape[0]
    ks = K // N_CHIPS
    tt = _row_tile(T, 2048)
    return _mm(
        name, "tn", a, dy,
        pl.BlockSpec((tt, ks), lambda i, j, k: (k, i)),
        pl.BlockSpec((tt, D_MODEL), lambda i, j, k: (k, 0)),
        _sds((N_CHIPS, ks, D_MODEL), _COMM),
        pl.BlockSpec((None, ks, D_MODEL), lambda i, j, k: (i, 0, 0)),
        (N_CHIPS, 1, T // tt), (ks, D_MODEL), pro=pro,
    )


def _mixer_wgrads(merged, y_r, y_p, o_m, dx1, do3):
    T = merged.shape[0]
    tt = _row_tile(T, 512)
    nt = T // tt
    rs = D_MODEL // N_CHIPS

    def body(mg_ref, yr_ref, yp_ref, om_ref, dx_ref, do3_ref, g_out, g_ret, g_pool, g_mem, a_out, a_ret, a_pool, a_mem):
        t = pl.program_id(0)

        def accumulate(acc, part):
            @pl.when(t == 0)
            def _():
                acc[...] = part

            @pl.when(t > 0)
            def _():
                acc[...] += part

        accumulate(a_out, _dot(mg_ref[...], dx_ref[...], "tn"))
        accumulate(a_ret, _dot(yr_ref[...], do3_ref[:, 0:D_MODEL], "tn"))
        accumulate(a_pool, _dot(yp_ref[...], do3_ref[:, D_MODEL:2 * D_MODEL], "tn"))
        accumulate(a_mem, _dot(om_ref[...], do3_ref[:, 2 * D_MODEL:3 * D_MODEL], "tn"))

        @pl.when(t == nt - 1)
        def _():
            for k in range(N_CHIPS):
                rows = slice(k * rs, (k + 1) * rs)
                g_out[k] = a_out[rows, :].astype(g_out.dtype)
                g_ret[k] = a_ret[rows, :].astype(g_ret.dtype)
                g_pool[k] = a_pool[:, rows].astype(g_pool.dtype)
                g_mem[k] = a_mem[:, rows].astype(g_mem.dtype)

    row = lambda w: pl.BlockSpec((tt, w), lambda t: (t, 0))
    whole = lambda s: pl.BlockSpec(s, lambda t: (0, 0, 0))
    rsh, csh = (N_CHIPS, rs, D_MODEL), (N_CHIPS, POOL_W, rs)
    return _call(
        body,
        out_shape=(_sds(rsh, _COMM), _sds(rsh, _COMM), _sds(csh, _COMM), _sds(csh, _COMM)),
        grid=(nt,),
        in_specs=[row(D_MODEL), row(D_MODEL), row(POOL_W), row(MEMQ_W), row(D_MODEL), row(3 * D_MODEL)],
        out_specs=(whole(rsh), whole(rsh), whole(csh), whole(csh)),
        scratch_shapes=[pltpu.VMEM((D_MODEL, D_MODEL), F32), pltpu.VMEM((D_MODEL, D_MODEL), F32),
                        pltpu.VMEM((POOL_W, D_MODEL), F32), pltpu.VMEM((MEMQ_W, D_MODEL), F32)],
        name="mixer_wgrads", compiler_params=_cp(("arbitrary",)),
    )(merged, y_r, y_p, o_m, dx1, do3)


def _rot(x, cos, sin):
    return x * cos + pltpu.roll(x, QK_DIM // 2, 1) * sin


def _rot_bwd(d, cos, sin):
    return d * cos + pltpu.roll(d * sin, QK_DIM // 2, 1)


def _tile_diff(qi, ki, tq, tk):
    n = qi * tq + lax.broadcasted_iota(jnp.int32, (tq, tk), 0)
    m = ki * tk + lax.broadcasted_iota(jnp.int32, (tq, tk), 1)
    return (n - m).astype(F32)


def _decay(qi, ki, tq, tk, lgf, lgb):
    diff = _tile_diff(qi, ki, tq, tk)
    return diff, jnp.exp(jnp.where(diff >= 0.0, lgf * diff, -(lgb * diff)))


def _head(h, width):
    return slice(h * width, (h + 1) * width)


def _row_index(t):
    return lax.broadcasted_iota(jnp.int32, (t, QK_DIM), 0).astype(F32)


def _q_decay_fwd(t, lgf):
    return jnp.exp(lgf * _row_index(t))


def _q_decay_bwd(t, lgb):
    return jnp.exp(lgb * (float(t) - _row_index(t)))


def _k_decay_fwd(t, lgf, tiles_apart):
    return jnp.exp(lgf * ((tiles_apart * t).astype(F32) - _row_index(t)))


def _k_decay_bwd(t, lgb, tiles_apart):
    return jnp.exp(lgb * (((tiles_apart - 1) * t).astype(F32) + _row_index(t)))


def _group_norm_gate(o, g):
    mu = jnp.mean(o, axis=-1, keepdims=True)
    oc = o - mu
    var = jnp.mean(oc * oc, axis=-1, keepdims=True)
    on = oc * lax.rsqrt(var + EPS)
    return on * (g * jax.nn.sigmoid(g))


def _retention_fwd(proj, cos, sin, lg, B, S, gather=()):
    T = B * S
    t = _row_tile(S, 512)
    nc = S // t
    scale = QK_DIM ** -0.5
    ng = len(gather)
    qw, vw = N_HEADS * QK_DIM, N_HEADS * V_DIM

    def chunk_of(s):
        return jnp.where(s < nc, s, 2 * nc - 1 - s)

    def body(lg_ref, q_ref, k_ref, v_ref, g_ref, cos_ref, sin_ref, *rest):
        o_ref, y_ref = rest[ng], rest[ng + 1]
        bufs = rest[ng + 2:2 * ng + 2]
        o_acc, state_f, state_b = rest[2 * ng + 2:2 * ng + 5]
        sems = rest[2 * ng + 5:]
        b = pl.program_id(0)
        s = pl.program_id(1)
        rows = pl.ds(pl.multiple_of(chunk_of(s) * t, t), t)
        if ng:
            @pl.when(jnp.logical_and(b == 0, s == 0))
            def _():
                _gather_start(bufs, *sems)

        @pl.when(s == 0)
        def _():
            state_f[...] = jnp.zeros_like(state_f)
            state_b[...] = jnp.zeros_like(state_b)

        def rotated(h):
            sl = _head(h, QK_DIM)
            qr = _rot(q_ref[:, sl].astype(F32), cos_ref[...], sin_ref[...])
            kr = _rot(k_ref[:, sl].astype(F32), cos_ref[...], sin_ref[...]) * scale
            return qr, kr

        r = _row_index(t)

        @pl.when(s < nc)
        def _():
            for h in range(N_HEADS):
                sl, vsl = _head(h, QK_DIM), _head(h, V_DIM)
                lgf, lgb = lg_ref[0, h], lg_ref[1, h]
                qr, kr = rotated(h)
                v = v_ref[:, vsl]
                diff = _tile_diff(0, 0, t, t)
                dec = jnp.exp(jnp.where(diff >= 0.0, lgf * diff, -(lgb * diff)))
                o = _dot(_dot(qr, kr, "nt") * dec, v)
                o += _dot(qr * jnp.exp(lgf * (r + 1.0)), state_f[sl, :])
                o_acc[rows, vsl] = o
                state_f[sl, :] = state_f[sl, :] * jnp.exp(lgf * float(t)) + _dot(kr * jnp.exp(lgf * (float(t - 1) - r)), v, "tn")

        @pl.when(s >= nc)
        def _():
            for h in range(N_HEADS):
                sl, vsl = _head(h, QK_DIM), _head(h, V_DIM)
                lgb = lg_ref[1, h]
                qr, kr = rotated(h)
                o = o_acc[rows, vsl] + _dot(qr * jnp.exp(lgb * (float(t) - r)), state_b[sl, :])
                o_ref[:, vsl] = o
                y_ref[:, vsl] = _group_norm_gate(o, g_ref[:, vsl].astype(F32)).astype(y_ref.dtype)
                state_b[sl, :] = state_b[sl, :] * jnp.exp(lgb * float(t)) + _dot(kr * jnp.exp(lgb * r), v_ref[:, vsl], "tn")

        if ng:
            @pl.when(jnp.logical_and(b == B - 1, s == 2 * nc - 1))
            def _():
                _gather_finish(bufs, *sems)

    def in_rows(width, col):
        return pl.BlockSpec((t, width), lambda b, s: (b * nc + chunk_of(s), col))

    out_rows = pl.BlockSpec((t, vw), lambda b, s: (b * nc + jnp.where(s < nc, nc - 1, 2 * nc - 1 - s), 0))
    table = pl.BlockSpec((t, QK_DIM), lambda b, s: (chunk_of(s), 0))
    out = _call(
        body,
        out_shape=(_sds((T, vw), F32), _sds((T, vw), _ACT), *[_sds(g.shape, g.dtype) for g in gather]),
        grid=(B, 2 * nc),
        in_specs=[
            pl.BlockSpec(memory_space=pltpu.SMEM),
            in_rows(qw, C_Q // qw), in_rows(qw, C_K // qw), in_rows(vw, C_V // vw), in_rows(vw, C_G // vw),
            table, table,
        ] + [ANY] * ng,
        out_specs=(out_rows, out_rows) + (ANY,) * ng,
        input_output_aliases={7 + i: 2 + i for i in range(ng)},
        scratch_shapes=[pltpu.VMEM((S, vw), F32), pltpu.VMEM((qw, V_DIM), F32), pltpu.VMEM((qw, V_DIM), F32)]
        + (_gather_sems(ng) if ng else []),
        name="retention_fwd",
        compiler_params=_cp(("arbitrary", "arbitrary")),
    )(lg, proj, proj, proj, proj, cos, sin, *gather)
    return out[0], out[1], list(out[2:])


def _gn_gate_bwd(dy, o, proj, dproj, swap=()):
    T = dy.shape[0]
    tm = _row_tile(T, 512)
    steps = T // tm
    nw = len(swap)

    def body(dy_ref, o_ref, g_ref, dproj_in, *rest):
        grads = rest[:nw]
        do_ref, dg_ref = rest[nw], rest[nw + 1]
        lands, sems = rest[nw + 2:2 * nw + 2], rest[2 * nw + 2:]
        if nw:
            @pl.when(pl.program_id(0) == 0)
            def _():
                for cp in _swap_copies(grads, lands, *sems):
                    cp.start()

            @pl.when(pl.program_id(0) == steps - 1)
            def _():
                for cp in _swap_copies(grads, lands, *sems):
                    cp.wait()

        for h in range(N_HEADS):
            sl = slice(h * V_DIM, (h + 1) * V_DIM)
            o = o_ref[:, sl]
            g = g_ref[:, sl].astype(F32)
            d = dy_ref[:, sl].astype(F32)
            mu = jnp.mean(o, axis=-1, keepdims=True)
            oc = o - mu
            rstd = lax.rsqrt(jnp.mean(oc * oc, axis=-1, keepdims=True) + EPS)
            on = oc * rstd
            sg = jax.nn.sigmoid(g)
            don = d * (g * sg)
            dg_ref[:, sl] = (d * on * (sg * (1.0 + g * (1.0 - sg)))).astype(dg_ref.dtype)
            do = rstd * (don - jnp.mean(don, axis=-1, keepdims=True) - on * jnp.mean(don * on, axis=-1, keepdims=True))
            do_ref[:, sl] = do.astype(do_ref.dtype)

    wide = N_HEADS * V_DIM
    out = _call(
        body,
        out_shape=(_sds((T, wide), _ACT), _sds(dproj.shape, dproj.dtype), *_swap_shapes(swap)),
        grid=(steps,),
        in_specs=[
            pl.BlockSpec((tm, wide), lambda i: (i, 0)),
            pl.BlockSpec((tm, wide), lambda i: (i, 0)),
            pl.BlockSpec((tm, wide), lambda i: (i, C_G // wide)),
            ANY,
        ] + [ANY] * nw,
        out_specs=(pl.BlockSpec((tm, wide), lambda i: (i, 0)), pl.BlockSpec((tm, wide), lambda i: (i, C_G // wide)))
        + (ANY,) * nw,
        input_output_aliases={3: 1},
        scratch_shapes=_swap_sems(nw) if nw else [],
        name="gn_gate_bwd", compiler_params=_cp(("arbitrary",)),
    )(dy, o, proj, dproj, *swap)
    return out[0], out[1], list(out[2:])


def _retention_bwd(proj, d_o, dproj, cos, sin, lg, B, S, scatter=()):
    T = B * S
    tq = tk = _row_tile(S, 512)
    nq, nk = S // tq, S // tk
    scale = QK_DIM ** -0.5
    qw, vw = N_HEADS * QK_DIM, N_HEADS * V_DIM

    ns = len(scatter)

    def body(lg_ref, q_ref, k_ref, v_ref, do_ref, cq_ref, sq_ref, ck_ref, sk_ref, dproj_in, *rest):
        sums = rest[:ns]
        dqkv_ref, dlg_ref = rest[ns], rest[ns + 1]
        parts = rest[ns + 2:2 * ns + 2]
        kr_scr, dq_acc, dk_acc, dv_acc, gl_acc = rest[2 * ns + 2:2 * ns + 7]
        sems = rest[2 * ns + 7:]
        b = pl.program_id(0)
        ki = pl.program_id(1)
        qi = pl.program_id(2)
        q_rows = pl.ds(pl.multiple_of(qi * tq, tq), tq)
        k_rows = pl.ds(pl.multiple_of(ki * tk, tk), tk)
        if ns:
            @pl.when(jnp.logical_and(b == 0, jnp.logical_and(qi == 0, ki == 0)))
            def _():
                _scatter_start(sums, parts, *sems)

        @pl.when(qi == 0)
        def _():
            for h in range(N_HEADS):
                sl = _head(h, QK_DIM)
                kr_scr[:, sl] = _rot(k_ref[:, sl].astype(F32), ck_ref[...], sk_ref[...]) * scale
            dk_acc[...] = jnp.zeros_like(dk_acc)
            dv_acc[...] = jnp.zeros_like(dv_acc)

        @pl.when(jnp.logical_and(qi == 0, ki == 0))
        def _():
            gl_acc[...] = jnp.zeros_like(gl_acc)

        @pl.when(ki == 0)
        def _():
            dq_acc[q_rows, :] = jnp.zeros((tq, qw), F32)

        def queries(h):
            return _rot(q_ref[:, _head(h, QK_DIM)].astype(F32), cq_ref[...], sq_ref[...])

        def one_sided(h, q_factor, k_factor, side, sign):
            sl, vsl = _head(h, QK_DIM), _head(h, V_DIM)
            qt = (queries(h) * q_factor).astype(_MXU)
            kt = (kr_scr[:, sl] * k_factor).astype(_MXU)
            d_out = do_ref[:, vsl]
            p = _dot(qt, kt, "nt")
            dv_acc[:, vsl] += _dot(p, d_out, "tn")
            dp = _dot(d_out, v_ref[:, vsl], "nt")
            dq_acc[q_rows, sl] += _dot(dp, kt) * q_factor
            dk_acc[:, sl] += _dot(dp, qt, "tn") * k_factor
            diff = _tile_diff(qi, ki, tq, tk)
            row = 2 * h + side
            gl_acc[row:row + 1, :] += sign * jnp.sum(dp * p * diff, axis=0, keepdims=True)

        @pl.when(ki < qi)
        def _():
            for h in range(N_HEADS):
                one_sided(h, _q_decay_fwd(tq, lg_ref[0, h]), _k_decay_fwd(tk, lg_ref[0, h], qi - ki), 0, 1.0)

        @pl.when(ki > qi)
        def _():
            for h in range(N_HEADS):
                one_sided(h, _q_decay_bwd(tq, lg_ref[1, h]), _k_decay_bwd(tk, lg_ref[1, h], ki - qi), 1, -1.0)

        @pl.when(ki == qi)
        def _():
            for h in range(N_HEADS):
                sl, vsl = _head(h, QK_DIM), _head(h, V_DIM)
                qr = queries(h).astype(_MXU)
                kr = kr_scr[:, sl].astype(_MXU)
                d_out = do_ref[:, vsl]
                s = _dot(qr, kr, "nt")
                diff, dec = _decay(qi, ki, tq, tk, lg_ref[0, h], lg_ref[1, h])
                p = s * dec
                dv_acc[:, vsl] += _dot(p, d_out, "tn")
                dp = _dot(d_out, v_ref[:, vsl], "nt")
                ds = dp * dec
                dq_acc[q_rows, sl] += _dot(ds, kr)
                dk_acc[:, sl] += _dot(ds, qr, "tn")
                gd = dp * p * diff
                gl_acc[2 * h:2 * h + 1, :] += jnp.sum(jnp.where(diff >= 0.0, gd, 0.0), axis=0, keepdims=True)
                gl_acc[2 * h + 1:2 * h + 2, :] += jnp.sum(jnp.where(diff < 0.0, -gd, 0.0), axis=0, keepdims=True)

        @pl.when(qi == nq - 1)
        def _():
            for h in range(N_HEADS):
                sl = _head(h, QK_DIM)
                dk = _rot_bwd(dk_acc[:, sl] * scale, ck_ref[...], sk_ref[...])
                dqkv_ref[k_rows, qw + h * QK_DIM:qw + (h + 1) * QK_DIM] = dk.astype(dqkv_ref.dtype)
            dqkv_ref[k_rows, 2 * qw:2 * qw + vw] = dv_acc[...].astype(dqkv_ref.dtype)

        @pl.when(ki == nk - 1)
        def _():
            for h in range(N_HEADS):
                sl = _head(h, QK_DIM)
                dqkv_ref[q_rows, sl] = _rot_bwd(dq_acc[q_rows, sl], cq_ref[...], sq_ref[...]).astype(dqkv_ref.dtype)

        @pl.when(jnp.logical_and(qi == nq - 1, ki == nk - 1))
        def _():
            dlg_ref[...] = gl_acc[...]

        if ns:
            @pl.when(jnp.logical_and(b == B - 1, jnp.logical_and(qi == nq - 1, ki == nk - 1)))
            def _():
                _scatter_finish(sums, parts, *sems)

    out = _call(
        body,
        out_shape=(_sds(dproj.shape, dproj.dtype), _sds((B, 2 * N_HEADS, tk), F32), *_scatter_shapes(scatter)),
        grid=(B, nk, nq),
        in_specs=[
            pl.BlockSpec(memory_space=pltpu.SMEM),
            pl.BlockSpec((tq, qw), lambda b, ki, qi: (b * nq + qi, C_Q // qw)),
            pl.BlockSpec((tk, qw), lambda b, ki, qi: (b * nk + ki, C_K // qw)),
            pl.BlockSpec((tk, vw), lambda b, ki, qi: (b * nk + ki, C_V // vw)),
            pl.BlockSpec((tq, vw), lambda b, ki, qi: (b * nq + qi, 0)),
            pl.BlockSpec((tq, QK_DIM), lambda b, ki, qi: (qi, 0)),
            pl.BlockSpec((tq, QK_DIM), lambda b, ki, qi: (qi, 0)),
            pl.BlockSpec((tk, QK_DIM), lambda b, ki, qi: (ki, 0)),
            pl.BlockSpec((tk, QK_DIM), lambda b, ki, qi: (ki, 0)),
            ANY,
        ] + [ANY] * ns,
        out_specs=(
            pl.BlockSpec((S, 2 * qw + vw), lambda b, ki, qi: (b, 0)),
            pl.BlockSpec((None, 2 * N_HEADS, tk), lambda b, ki, qi: (b, 0, 0)),
        ) + (ANY,) * ns,
        scratch_shapes=[
            pltpu.VMEM((tk, qw), F32), pltpu.VMEM((S, qw), F32),
            pltpu.VMEM((tk, qw), F32), pltpu.VMEM((tk, vw), F32), pltpu.VMEM((2 * N_HEADS, tk), F32),
        ] + (_scatter_sems(ns) if ns else []),
        input_output_aliases={9: 0},
        name="retention_bwd",
        compiler_params=_cp(("arbitrary", "arbitrary", "arbitrary")),
    )(lg, proj, proj, proj, d_o, cos, sin, cos, sin, dproj, *scatter)
    return out[0], out[1], list(out[2:])


POOL_PAD = 16


def _pad_rows(v):
    z = jnp.zeros((POOL_PAD, v.shape[1]), F32)
    return jnp.concatenate([z, v, z], axis=0)


def _window_sums(first, length, levels):
    s = first
    step = 1
    for _ in range(levels - 1):
        s = pltpu.roll(s, step, 0) + pltpu.roll(s, length - step, 0)
        step *= 2
    return s


def _pool_counts(S, hw):
    n = lax.broadcasted_iota(jnp.int32, (S, 1), 0)
    return (jnp.minimum(n + hw, S) - jnp.maximum(n - hw, 0)).astype(F32)


def _pool_mixed(pf, S, g):
    length = S + 2 * POOL_PAD
    xp = _pad_rows(pf)
    s = _window_sums(xp + pltpu.roll(xp, 1, 0), length, g + 1)[POOL_PAD:POOL_PAD + S]
    return s / _pool_counts(S, 1 << g) - pf


def _pool_mixed_bwd(dmixed, S, g):
    length = S + 2 * POOL_PAD
    ep = _pad_rows(dmixed / _pool_counts(S, 1 << g))
    t = _window_sums(ep + pltpu.roll(ep, length - 1, 0), length, g + 1)[POOL_PAD:POOL_PAD + S]
    return t - dmixed


def _pool_fwd(proj, w_grp, scale, B, S):
    T = B * S
    G = POOL_W // 4

    def body(p_ref, wg_ref, sc_ref, y_ref):
        for g in range(4):
            sl = slice(g * G, (g + 1) * G)
            mixed = _pool_mixed(p_ref[:, sl].astype(F32), S, g)
            y_ref[:, sl] = (_dot(mixed, wg_ref[g]) * sc_ref[:, sl]).astype(y_ref.dtype)

    return _call(
        body, out_shape=_sds((T, POOL_W), _ACT), grid=(B,),
        in_specs=[
            pl.BlockSpec((S, POOL_W), lambda b: (b, C_P // POOL_W)),
            pl.BlockSpec((4, G, G), lambda b: (0, 0, 0)),
            pl.BlockSpec((1, POOL_W), lambda b: (0, 0)),
        ],
        out_specs=pl.BlockSpec((S, POOL_W), lambda b: (b, 0)),
        name="pool_fwd", compiler_params=_cp(("parallel",)),
    )(proj, w_grp, scale)


def _pool_bwd(proj, dy, dproj, w_grp, scale, B, S):
    G = POOL_W // 4

    def body(p_ref, dy_ref, wg_ref, sc_ref, dproj_in, dp_ref, dwg_ref, dsc_ref):
        b = pl.program_id(0)
        for g in range(4):
            sl = slice(g * G, (g + 1) * G)
            mixed = _pool_mixed(p_ref[:, sl].astype(F32), S, g)
            z = _dot(mixed, wg_ref[g])
            d = dy_ref[:, sl].astype(F32)
            dsc = jnp.sum(d * z, axis=0, keepdims=True)
            dz = d * sc_ref[:, sl]
            dwg = _dot(mixed, dz, "tn")
            dmixed = _dot(dz, wg_ref[g], "nt")
            dp_ref[:, sl] = _pool_mixed_bwd(dmixed, S, g).astype(dp_ref.dtype)

            @pl.when(b == 0)
            def _():
                dwg_ref[g] = dwg
                dsc_ref[:, sl] = dsc

            @pl.when(b > 0)
            def _():
                dwg_ref[g] += dwg
                dsc_ref[:, sl] += dsc

    return _call(
        body,
        out_shape=(_sds(dproj.shape, dproj.dtype), _sds((4, G, G), F32), _sds((1, POOL_W), F32)),
        grid=(B,),
        in_specs=[
            pl.BlockSpec((S, POOL_W), lambda b: (b, C_P // POOL_W)),
            pl.BlockSpec((S, POOL_W), lambda b: (b, 0)),
            pl.BlockSpec((4, G, G), lambda b: (0, 0, 0)),
            pl.BlockSpec((1, POOL_W), lambda b: (0, 0)),
            ANY,
        ],
        out_specs=(
            pl.BlockSpec((S, POOL_W), lambda b: (b, C_P // POOL_W)),
            pl.BlockSpec((4, G, G), lambda b: (0, 0, 0)),
            pl.BlockSpec((1, POOL_W), lambda b: (0, 0)),
        ),
        input_output_aliases={4: 0},
        name="pool_bwd", compiler_params=_cp(("arbitrary",)),
    )(proj, dy, w_grp, scale, dproj)


def _mem_softmax(q, k):
    s = _dot(q, k, "nt") * (QK_DIM ** -0.5)
    e = jnp.exp(s - jnp.max(s, axis=-1, keepdims=True))
    return e / jnp.sum(e, axis=-1, keepdims=True)


def _mem_attn_fwd(proj, kv, B, S):
    T = B * S
    tq = _row_tile(S, 512)
    nq = S // tq

    def body(q_ref, kv_ref, o_ref):
        for h in range(N_HEADS):
            sl = slice(h * QK_DIM, (h + 1) * QK_DIM)
            a = _mem_softmax(q_ref[:, sl], kv_ref[:, sl])
            o_ref[:, sl] = _dot(a, kv_ref[:, MEMQ_W + h * QK_DIM:MEMQ_W + (h + 1) * QK_DIM]).astype(o_ref.dtype)

    return _call(
        body, out_shape=_sds((T, MEMQ_W), _ACT), grid=(B, nq),
        in_specs=[
            pl.BlockSpec((tq, MEMQ_W), lambda b, i: (b * nq + i, C_QM // MEMQ_W)),
            pl.BlockSpec((MEM_LEN, 2 * MEMQ_W), lambda b, i: (b, 0)),
        ],
        out_specs=pl.BlockSpec((tq, MEMQ_W), lambda b, i: (b * nq + i, 0)),
        name="mem_attn_fwd", compiler_params=_cp(("parallel", "parallel")),
    )(proj, kv)


def _mem_attn_bwd(proj, kv, d_o, dproj, B, S):
    tq = _row_tile(S, 512)
    nq = S // tq
    scale = QK_DIM ** -0.5

    def body(q_ref, kv_ref, do_ref, dproj_in, dq_ref, dkv_ref):
        i = pl.program_id(1)
        for h in range(N_HEADS):
            sl = slice(h * QK_DIM, (h + 1) * QK_DIM)
            vsl = slice(MEMQ_W + h * QK_DIM, MEMQ_W + (h + 1) * QK_DIM)
            q = q_ref[:, sl]
            a = _mem_softmax(q, kv_ref[:, sl])
            d = do_ref[:, sl]
            da = _dot(d, kv_ref[:, vsl], "nt")
            ds = a * (da - jnp.sum(a * da, axis=-1, keepdims=True)) * scale
            dq_ref[:, sl] = _dot(ds, kv_ref[:, sl]).astype(dq_ref.dtype)
            dk = _dot(ds, q, "tn")
            dv = _dot(a, d, "tn")

            @pl.when(i == 0)
            def _():
                dkv_ref[:, sl] = dk
                dkv_ref[:, vsl] = dv

            @pl.when(i > 0)
            def _():
                dkv_ref[:, sl] += dk
                dkv_ref[:, vsl] += dv

    return _call(
        body,
        out_shape=(_sds(dproj.shape, dproj.dtype), _sds((B * MEM_LEN, 2 * MEMQ_W), F32)),
        grid=(B, nq),
        in_specs=[
            pl.BlockSpec((tq, MEMQ_W), lambda b, i: (b * nq + i, C_QM // MEMQ_W)),
            pl.BlockSpec((MEM_LEN, 2 * MEMQ_W), lambda b, i: (b, 0)),
            pl.BlockSpec((tq, MEMQ_W), lambda b, i: (b * nq + i, 0)),
            ANY,
        ],
        out_specs=(
            pl.BlockSpec((tq, MEMQ_W), lambda b, i: (b * nq + i, C_QM // MEMQ_W)),
            pl.BlockSpec((MEM_LEN, 2 * MEMQ_W), lambda b, i: (b, 0)),
        ),
        input_output_aliases={3: 0},
        name="mem_attn_bwd", compiler_params=_cp(("parallel", "arbitrary")),
    )(proj, kv, d_o, dproj)


def _mem_norm(mem2d, g):
    M = mem2d.shape[0]
    tm = _row_tile(M, 512)

    def body(x_ref, g_ref, o_ref):
        o_ref[...] = _rms_rows(x_ref[...], g_ref[...]).astype(o_ref.dtype)

    return _call(
        body, out_shape=_sds((M, D_MODEL), _ACT), grid=(M // tm,),
        in_specs=[pl.BlockSpec((tm, D_MODEL), lambda i: (i, 0)), pl.BlockSpec((1, D_MODEL), lambda i: (0, 0))],
        out_specs=pl.BlockSpec((tm, D_MODEL), lambda i: (i, 0)),
        name="mem_norm", compiler_params=_cp(("parallel",)),
    )(mem2d, g)


def _mem_norm_wgrad(mem2d, d_memn):
    M = mem2d.shape[0]
    tm = _row_tile(M, 512)

    def body(x_ref, d_ref, dg_ref):
        i = pl.program_id(0)
        x = x_ref[...]
        xh = x * lax.rsqrt(jnp.mean(x * x, axis=-1, keepdims=True) + EPS)
        dg = jnp.sum(d_ref[...] * xh, axis=0, keepdims=True)

        @pl.when(i == 0)
        def _():
            dg_ref[...] = dg

        @pl.when(i > 0)
        def _():
            dg_ref[...] += dg

    return _call(
        body, out_shape=_sds((1, D_MODEL), F32), grid=(M // tm,),
        in_specs=[pl.BlockSpec((tm, D_MODEL), lambda i: (i, 0)), pl.BlockSpec((tm, D_MODEL), lambda i: (i, 0))],
        out_specs=pl.BlockSpec((1, D_MODEL), lambda i: (0, 0)),
        name="mem_norm_wgrad", compiler_params=_cp(("arbitrary",)),
    )(mem2d, d_memn)


def _row_mm(a, w_ref):
    ks = w_ref.shape[1]
    out = _dot(a[:, 0:ks], w_ref[0])
    for k in range(1, N_CHIPS):
        out += _dot(a[:, k * ks:(k + 1) * ks], w_ref[k])
    return out


def _row_mm_t(d, w_ref):
    return jnp.concatenate([_dot(d, w_ref[k], "nt") for k in range(N_CHIPS)], axis=1)


def _col_mm(a, w_ref):
    return jnp.concatenate([_dot(a, w_ref[k]) for k in range(N_CHIPS)], axis=1)


def _col_mm_t(d, w_ref):
    ns = w_ref.shape[2]
    out = _dot(d[:, 0:ns], w_ref[0], "nt")
    for k in range(1, N_CHIPS):
        out += _dot(d[:, k * ns:(k + 1) * ns], w_ref[k], "nt")
    return out


def _merge_fwd(x, proj, y_r, y_p, o_m, w_ret_o, w_pool_o, w_mem_o, w_out):
    T = x.shape[0]
    tm = _row_tile(T, 512)

    def body(x_ref, gr_ref, gp_ref, gm_ref, yr_ref, yp_ref, om_ref, wr_ref, wp_ref, wm_ref, wo_ref,
             x1_ref, mg_ref, o3_ref):
        o_r = _row_mm(yr_ref[...], wr_ref)
        o_p = _col_mm(yp_ref[...], wp_ref)
        o_q = _col_mm(om_ref[...], wm_ref)
        merged = (jax.nn.sigmoid(gr_ref[...].astype(F32)) * o_r + jax.nn.sigmoid(gp_ref[...].astype(F32)) * o_p
                  + jax.nn.sigmoid(gm_ref[...].astype(F32)) * o_q)
        mg = merged.astype(mg_ref.dtype)
        mg_ref[...] = mg
        o3_ref[:, 0:D_MODEL] = o_r.astype(o3_ref.dtype)
        o3_ref[:, D_MODEL:2 * D_MODEL] = o_p.astype(o3_ref.dtype)
        o3_ref[:, 2 * D_MODEL:3 * D_MODEL] = o_q.astype(o3_ref.dtype)
        x1_ref[...] = x_ref[...] + _row_mm(mg, wo_ref)

    gb = C_GATE // D_MODEL
    row = lambda w: pl.BlockSpec((tm, w), lambda i: (i, 0))
    return _call(
        body,
        out_shape=(_sds((T, D_MODEL), F32), _sds((T, D_MODEL), _ACT), _sds((T, 3 * D_MODEL), _ACT)),
        grid=(T // tm,),
        in_specs=[
            row(D_MODEL),
            pl.BlockSpec((tm, D_MODEL), lambda i: (i, gb)),
            pl.BlockSpec((tm, D_MODEL), lambda i: (i, gb + 1)),
            pl.BlockSpec((tm, D_MODEL), lambda i: (i, gb + 2)),
            row(D_MODEL), row(POOL_W), row(MEMQ_W),
            _resident(w_ret_o), _resident(w_pool_o), _resident(w_mem_o), _resident(w_out),
        ],
        out_specs=(row(D_MODEL), row(D_MODEL), row(3 * D_MODEL)),
        name="merge_fwd", compiler_params=_cp(("parallel",)),
    )(x, proj, proj, proj, y_r, y_p, o_m, w_ret_o, w_pool_o, w_mem_o, w_out)


def _merge_bwd(dx1, proj, o3, w_ret_o, w_pool_o, w_mem_o, w_out):
    T = dx1.shape[0]
    tm = _row_tile(T, 512)

    def body(dx_ref, gr_ref, gp_ref, gm_ref, o3_ref, wr_ref, wp_ref, wm_ref, wo_ref,
             do3_ref, dgate_ref, dyr_ref, dyp_ref, dom_ref):
        dmerged = _row_mm_t(dx_ref[...], wo_ref)
        douts = []
        dgate_ref[:, 0:MEMQ_W] = jnp.zeros((tm, MEMQ_W), dgate_ref.dtype)
        for n, g_ref in enumerate((gr_ref, gp_ref, gm_ref)):
            sl = slice(n * D_MODEL, (n + 1) * D_MODEL)
            gate = jax.nn.sigmoid(g_ref[...].astype(F32))
            d_out = (dmerged * gate).astype(do3_ref.dtype)
            do3_ref[:, sl] = d_out
            dgate_ref[:, MEMQ_W + n * D_MODEL:MEMQ_W + (n + 1) * D_MODEL] = (
                dmerged * o3_ref[:, sl].astype(F32) * gate * (1.0 - gate)).astype(dgate_ref.dtype)
            douts.append(d_out)
        dyr_ref[...] = _row_mm_t(douts[0], wr_ref).astype(dyr_ref.dtype)
        dyp_ref[...] = _col_mm_t(douts[1], wp_ref).astype(dyp_ref.dtype)
        dom_ref[...] = _col_mm_t(douts[2], wm_ref).astype(dom_ref.dtype)

    gb = C_GATE // D_MODEL
    row = lambda w: pl.BlockSpec((tm, w), lambda i: (i, 0))
    return _call(
        body,
        out_shape=(
            _sds((T, 3 * D_MODEL), _ACT), _sds((T, N_PROJ), _ACT),
            _sds((T, D_MODEL), _ACT), _sds((T, POOL_W), _ACT), _sds((T, MEMQ_W), _ACT),
        ),
        grid=(T // tm,),
        in_specs=[
            row(D_MODEL),
            pl.BlockSpec((tm, D_MODEL), lambda i: (i, gb)),
            pl.BlockSpec((tm, D_MODEL), lambda i: (i, gb + 1)),
            pl.BlockSpec((tm, D_MODEL), lambda i: (i, gb + 2)),
            row(3 * D_MODEL),
            _resident(w_ret_o), _resident(w_pool_o), _resident(w_mem_o), _resident(w_out),
        ],
        out_specs=(row(3 * D_MODEL), pl.BlockSpec((tm, N_PROJ - C_QM), lambda i: (i, 1)),
                   row(D_MODEL), row(POOL_W), row(MEMQ_W)),
        name="merge_bwd", compiler_params=_cp(("parallel",)),
    )(dx1, proj, proj, proj, o3, w_ret_o, w_pool_o, w_mem_o, w_out)


def _loss_head(x, g, target):
    T = x.shape[0]
    tm = _row_tile(T, 512)

    def body(x_ref, g_ref, t_ref, dx_ref, sq_ref, dg_ref):
        i = pl.program_id(0)
        x = x_ref[...]
        gg = g_ref[...]
        r = lax.rsqrt(jnp.mean(x * x, axis=-1, keepdims=True) + EPS)
        xh = x * r
        err = xh * gg - t_ref[...]
        dy = err * (1.0 / D_MODEL)
        dxh = dy * gg
        dx_ref[...] = r * (dxh - xh * jnp.mean(dxh * xh, axis=-1, keepdims=True))
        sq = jnp.sum(err * err, axis=0, keepdims=True)
        dg = jnp.sum(dy * xh, axis=0, keepdims=True)

        @pl.when(i == 0)
        def _():
            sq_ref[...] = sq
            dg_ref[...] = dg

        @pl.when(i > 0)
        def _():
            sq_ref[...] += sq
            dg_ref[...] += dg

    vec = pl.BlockSpec((1, D_MODEL), lambda i: (0, 0))
    row = pl.BlockSpec((tm, D_MODEL), lambda i: (i, 0))
    return _call(
        body,
        out_shape=(_sds((T, D_MODEL), F32), _sds((1, D_MODEL), F32), _sds((1, D_MODEL), F32)),
        grid=(T // tm,), in_specs=[row, vec, row], out_specs=(row, vec, vec),
        name="loss_head", compiler_params=_cp(("arbitrary",)),
    )(x, g, target)


def _block_rows(rows, cols, itemsize, cap_bytes=2 << 20):
    t = rows
    while t * cols * itemsize > cap_bytes and t % 2 == 0 and (t // 2) % 16 == 0:
        t //= 2
    return t


def _cast_into_slot(w3d, layer, me):
    _, R, C = w3d.shape
    tr = _block_rows(R, C, 4)

    def body(me_ref, w_ref, o_ref):
        o_ref[...] = w_ref[...].astype(o_ref.dtype)

    grid_spec = pltpu.PrefetchScalarGridSpec(
        num_scalar_prefetch=1, grid=(R // tr,),
        in_specs=[pl.BlockSpec((None, tr, C), lambda i, me: (layer, i, 0))],
        out_specs=pl.BlockSpec((None, tr, C), lambda i, me: (me[0], i, 0)),
    )
    return _call(body, out_shape=_sds((N_CHIPS, R, C), _COMM), grid_spec=grid_spec,
                 name="cast_into_slot", compiler_params=_cp(("parallel",)))(me, w3d)


def _adamw(w, g, m, v):
    R, C = w.shape
    tr = _block_rows(R, C, 4, 1 << 20)

    def body(w_ref, g_ref, m_ref, v_ref, d_ref, nm_ref, nv_ref):
        g = g_ref[...]
        m = ADAM_B1 * m_ref[...] + (1.0 - ADAM_B1) * g
        v = ADAM_B2 * v_ref[...] + (1.0 - ADAM_B2) * (g * g)
        m_hat = m / (1.0 - ADAM_B1 ** ADAM_STEP)
        v_hat = v / (1.0 - ADAM_B2 ** ADAM_STEP)
        d_ref[...] = -ADAM_LR * (m_hat / (jnp.sqrt(v_hat) + ADAM_EPS) + ADAM_WD * w_ref[...])
        nm_ref[...] = m
        nv_ref[...] = v

    spec = pl.BlockSpec((tr, C), lambda i: (i, 0))
    out = _sds((R, C), F32)
    return _call(body, out_shape=(out, out, out), grid=(R // tr,), in_specs=[spec] * 4, out_specs=(spec,) * 3,
                 name="adamw", compiler_params=_cp(("parallel",)))(w, g, m, v)


def _add_sibling_half(g_full, land, my_c):
    _, R, C = g_full.shape
    hr = R // 2
    tr = _block_rows(hr, C, 2, 1 << 20)
    nb = hr // tr

    def body(c_ref, g_ref, l_ref, o_ref):
        o_ref[...] = (g_ref[...].astype(F32) + l_ref[...].astype(F32)).astype(o_ref.dtype)

    grid_spec = pltpu.PrefetchScalarGridSpec(
        num_scalar_prefetch=1, grid=(N_CHIPS, nb),
        in_specs=[
            pl.BlockSpec((None, tr, C), lambda k, i, c: (k, c[0] * nb + i, 0)),
            pl.BlockSpec((None, tr, C), lambda k, i, c: (k, i, 0)),
        ],
        out_specs=pl.BlockSpec((None, tr, C), lambda k, i, c: (k, i, 0)),
    )
    return _call(body, out_shape=_sds((N_CHIPS, hr, C), _COMM), grid_spec=grid_spec,
                 name="add_sibling_half", compiler_params=_cp(("parallel", "parallel")))(my_c, g_full, land)


def _add_chips(land, sums, g_all, layer, my_chip, my_core):
    _, hr, C = land.shape
    tr = _block_rows(hr, C, 4, 1 << 20)
    nb = hr // tr

    def body(me_ref, core_ref, own_ref, a_ref, b_ref, c_ref, *rest):
        o_ref = rest[-1]
        o_ref[...] = ((own_ref[...].astype(F32) + a_ref[...].astype(F32)) + b_ref[...].astype(F32)) + c_ref[...].astype(F32)

    def other(d):
        return pl.BlockSpec((None, tr, C), lambda i, me, core: ((me[0] + d) % N_CHIPS, i, 0))

    operands = [my_chip, my_core, sums, land, land, land]
    in_specs = [other(0), other(1), other(2), other(3)]
    aliases = {}
    if g_all is not None:
        operands.append(g_all)
        in_specs.append(ANY)
        aliases = {len(operands) - 1: 0}
    grid_spec = pltpu.PrefetchScalarGridSpec(
        num_scalar_prefetch=2, grid=(nb,), in_specs=in_specs,
        out_specs=pl.BlockSpec((None, tr, C), lambda i, me, core: (layer, core[0] * nb + i, 0)),
    )
    return _call(body, out_shape=_sds((DEPTH, 2 * hr, C), F32), grid_spec=grid_spec, input_output_aliases=aliases,
                 name="add_chips", compiler_params=_cp(("parallel",)))(*operands)


def _place():
    x, y, c = lax.axis_index("x"), lax.axis_index("y"), lax.axis_index("c")
    chips = [(1 - x, y), (x, 1 - y), (1 - x, 1 - y)]
    return x, y, c, 2 * x + y, chips


def _remote(src, dst, send_sem, recv_sem, dev):
    return pltpu.make_async_remote_copy(src_ref=src, dst_ref=dst, send_sem=send_sem, recv_sem=recv_sem,
                                        device_id=dev, device_id_type=MESH)


def _gather_sems(n):
    return [pltpu.SemaphoreType.DMA((n, 6)), pltpu.SemaphoreType.DMA((n, 6))]


def _gather_start(bufs, send_sems, recv_sems):
    x, y, c, me, chips = _place()
    for w, buf in enumerate(bufs):
        hr = buf.shape[1] // 2
        own = buf.at[me, pl.ds(c * hr, hr)]
        for j, chip in enumerate(chips):
            _remote(own, own, send_sems.at[w, j], recv_sems.at[w, j], (*chip, c)).start()


def _gather_finish(bufs, send_sems, recv_sems):
    x, y, c, me, chips = _place()
    sibling = (x, y, 1 - c)
    waits = []
    for w, buf in enumerate(bufs):
        hr = buf.shape[1] // 2
        mine = pl.ds(c * hr, hr)
        for j, chip in enumerate(chips):
            kc = 2 * chip[0] + chip[1]
            got = buf.at[kc, mine]
            first = _remote(got, got, send_sems.at[w, j], recv_sems.at[w, j], (*chip, c))
            first.wait_recv()
            fwd = _remote(got, got, send_sems.at[w, 3 + j], recv_sems.at[w, 3 + j], sibling)
            fwd.start()
            waits += [first.wait_send, fwd.wait_send]
    for w, buf in enumerate(bufs):
        hr = buf.shape[1] // 2
        theirs = pl.ds((1 - c) * hr, hr)
        for j, chip in enumerate(chips):
            kc = 2 * chip[0] + chip[1]
            got = buf.at[kc, theirs]
            _remote(got, got, send_sems.at[w, 3 + j], recv_sems.at[w, 3 + j], sibling).wait_recv()
    for wait in waits:
        wait()


def _allgather_weights(bufs):
    n = len(bufs)

    def body(*refs):
        outs = refs[n:2 * n]
        send_sems, recv_sems = refs[2 * n:]
        _gather_start(outs, send_sems, recv_sems)
        _gather_finish(outs, send_sems, recv_sems)

    out_shape = tuple(_sds(b.shape, b.dtype) for b in bufs)
    return _call(
        body, out_shape=out_shape, in_specs=[ANY] * n, out_specs=(ANY,) * n,
        input_output_aliases={i: i for i in range(n)},
        scratch_shapes=_gather_sems(n),
        name="allgather_weights",
    )(*bufs)


def _swap_sems(n):
    return [pltpu.SemaphoreType.DMA((n,)), pltpu.SemaphoreType.DMA((n,))]


def _swap_shapes(grads):
    return tuple(_sds((N_CHIPS, g.shape[1] // 2, g.shape[2]), g.dtype) for g in grads)


def _swap_copies(grads, lands, send_sems, recv_sems):
    x, y, c, _, _ = _place()
    copies = []
    for w in range(len(grads)):
        hr = grads[w].shape[1] // 2
        copies.append(_remote(grads[w].at[:, pl.ds((1 - c) * hr, hr)], lands[w], send_sems.at[w], recv_sems.at[w], (x, y, 1 - c)))
    return copies


def _swap_sibling_halves(grads):
    n = len(grads)

    def body(*refs):
        copies = _swap_copies(refs[:n], refs[n:2 * n], *refs[2 * n:])
        for cp in copies:
            cp.start()
        for cp in copies:
            cp.wait()

    return _call(
        body, out_shape=_swap_shapes(grads), in_specs=[ANY] * n, out_specs=(ANY,) * n,
        scratch_shapes=_swap_sems(n), name="swap_sibling_halves",
    )(*grads)


def _scatter_sems(n):
    return [pltpu.SemaphoreType.DMA((n, 3)), pltpu.SemaphoreType.DMA((n, 3))]


def _scatter_shapes(sums):
    return tuple(_sds(s.shape, s.dtype) for s in sums)


def _scatter_start(sums, parts, send_sems, recv_sems):
    x, y, c, me, chips = _place()
    for w in range(len(sums)):
        for j, chip in enumerate(chips):
            kc = 2 * chip[0] + chip[1]
            _remote(sums[w].at[kc], parts[w].at[me], send_sems.at[w, j], recv_sems.at[w, j], (*chip, c)).start()


def _scatter_finish(sums, parts, send_sems, recv_sems):
    x, y, c, me, chips = _place()
    for w in range(len(sums)):
        for j, chip in enumerate(chips):
            kc = 2 * chip[0] + chip[1]
            _remote(sums[w].at[kc], parts[w].at[kc], send_sems.at[w, j], recv_sems.at[w, j], (*chip, c)).wait()


def _join_sibling_halves(g_alls, layer):
    n = len(g_alls)

    def body(*refs):
        outs = refs[n:2 * n]
        send_sems, recv_sems = refs[2 * n:]
        x, y, c, _, _ = _place()
        copies = []
        for w in range(n):
            hr = outs[w].shape[1] // 2
            mine = outs[w].at[layer, pl.ds(c * hr, hr)]
            cp = _remote(mine, mine, send_sems.at[w], recv_sems.at[w], (x, y, 1 - c))
            cp.start()
            copies.append(cp)
        for cp in copies:
            cp.wait()

    out_shape = tuple(_sds(g.shape, g.dtype) for g in g_alls)
    return _call(
        body, out_shape=out_shape, in_specs=[ANY] * n, out_specs=(ANY,) * n,
        input_output_aliases={i: i for i in range(n)},
        scratch_shapes=[pltpu.SemaphoreType.DMA((n,)), pltpu.SemaphoreType.DMA((n,))],
        name="join_sibling_halves",
    )(*g_alls)


def _chip_sums(grads, my_core, land=None):
    if land is None:
        land = _swap_sibling_halves(grads)
    return [_add_sibling_half(g, l, my_core) for g, l in zip(grads, land)]


def _finish_reduce_scatter(parts, sums, g_alls, layer, my_chip, my_core):
    g_alls = [_add_chips(p, s, ga, layer, my_chip, my_core) for p, s, ga in zip(parts, sums, g_alls)]
    return _join_sibling_halves(g_alls, layer)


def _allreduce_small(v):
    R = v.shape[0]

    def body(v_ref, out_ref, sib_ref, chip_ref, sum_ref, send_sems, recv_sems):
        x, y, c, me, chips = _place()
        swap = _remote(v_ref, sib_ref, send_sems.at[0], recv_sems.at[0], (x, y, 1 - c))
        swap.start()
        swap.wait()
        sum_ref[...] = v_ref[...] + sib_ref[...]
        copies = []
        for j, chip in enumerate(chips):
            cp = _remote(sum_ref, chip_ref.at[me], send_sems.at[1 + j], recv_sems.at[1 + j], (*chip, c))
            cp.start()
            copies.append(cp)
        chip_ref[me] = sum_ref[...]
        for cp in copies:
            cp.wait()
        out_ref[...] = ((chip_ref[0] + chip_ref[1]) + chip_ref[2]) + chip_ref[3]

    vm = pl.BlockSpec(memory_space=pltpu.VMEM)
    return _call(
        body, out_shape=_sds((R, 128), F32), in_specs=[vm], out_specs=vm,
        scratch_shapes=[
            pltpu.VMEM((R, 128), F32), pltpu.VMEM((N_CHIPS, R, 128), F32), pltpu.VMEM((R, 128), F32),
            pltpu.SemaphoreType.DMA((4,)), pltpu.SemaphoreType.DMA((4,)),
        ],
        name="allreduce_small", compiler_params=_cp(),
    )(v)


def _pack_small(parts):
    rows = []
    for p in parts:
        flat = p.reshape(-1).astype(F32)
        pad = (-flat.shape[0]) % 128
        rows.append(jnp.pad(flat, (0, pad)).reshape(-1, 128))
    packed = jnp.concatenate(rows, axis=0)
    pad_rows = (-packed.shape[0]) % 8
    return jnp.pad(packed, ((0, pad_rows), (0, 0)))


def _unpack_small(packed, like):
    out, r = [], 0
    for p in like:
        n = p.size
        nr = -(-n // 128)
        out.append(packed[r:r + nr].reshape(-1)[:n].reshape(p.shape))
        r += nr
    return out


BIG = ("w_in", "w_ret_o", "w_pool_o", "w_mem_kv", "w_mem_o", "w_out", "w_ff1", "w_ff2")


def _rope_tables(S):
    inv = ROPE_BASE ** (-jnp.arange(0, QK_DIM, 2, dtype=F32) / QK_DIM)
    ang = jnp.arange(S).astype(F32)[:, None] * inv[None, :]
    cos, sin = jnp.cos(ang), jnp.sin(ang)
    return jnp.concatenate([cos, cos], axis=1), jnp.concatenate([-sin, sin], axis=1)


MIXER_W = ("w_ret_o", "w_pool_o", "w_mem_kv", "w_mem_o", "w_out")
AFTER_IN = MIXER_W + ("w_ff1", "w_ff2")
GATHER_PLAN = {
    ("in_proj", 0): [(0, AFTER_IN)],
    ("retention", 0): [(1, ("w_in",))],
    ("ff1", 0): [(1, MIXER_W)],
    ("ff2", 0): [(1, ("w_ff1",))],
    ("in_proj", 1): [(1, ("w_ff2",)), (2, ("w_in",))],
    ("retention", 1): [(2, MIXER_W + ("w_ff1",))],
    ("ff1", 1): [(2, ("w_ff2",))],
    ("ff2", 1): [(3, ("w_in",))],
    ("in_proj", 2): [(3, AFTER_IN)],
}


def _layer_fwd(x, layer, wbuf, small, memn, cos, sin, B, S):
    g1, g2, lg, w_grp, scale = small

    def hosted(host):
        keys = [(l2, n) for l2, names in GATHER_PLAN.get((host, layer), ()) for n in names]
        return keys, [wbuf[k] for k in keys]

    W = lambda n: wbuf[(layer, n)]
    keys, bufs = hosted("in_proj")
    proj, h1, bufs = _rms_proj("in_proj", x, g1, W("w_in"), N_PROJ, gather=bufs)
    wbuf.update(zip(keys, bufs))
    keys, bufs = hosted("retention")
    o, y_r, bufs = _retention_fwd(proj, cos, sin, lg, B, S, gather=bufs)
    wbuf.update(zip(keys, bufs))
    y_p = _pool_fwd(proj, w_grp, scale, B, S)
    kv = _rowsharded_mm("mem_kv", memn, W("w_mem_kv"), None)
    o_m = _mem_attn_fwd(proj, kv, B, S)
    x1, merged, o3 = _merge_fwd(x, proj, y_r, y_p, o_m, W("w_ret_o"), W("w_pool_o"), W("w_mem_o"), W("w_out"))
    keys, bufs = hosted("ff1")
    u, h2, bufs = _rms_proj("ff1", x1, g2, W("w_ff1"), FFN_HIDDEN, gather=bufs)
    wbuf.update(zip(keys, bufs))
    keys, bufs = hosted("ff2")
    x2, bufs = _ffn_out(x1, u, W("w_ff2"), gather=bufs)
    wbuf.update(zip(keys, bufs))
    saved = dict(x=x, proj=proj, h1=h1, o=o, y_r=y_r, y_p=y_p, kv=kv, o_m=o_m, merged=merged, o3=o3, x1=x1, h2=h2, u=u)
    return x2, saved


EARLY_GRADS = ("w_ff1", "w_ff2", "w_out", "w_ret_o", "w_pool_o", "w_mem_o")
LATE_GRADS = ("w_in", "w_mem_kv")


def _layer_bwd(dx2, sv, W, small, memn, d_memn, cos, sin, B, S, big_grads, layer, my_chip, my_core):
    g1, g2, lg, w_grp, scale = small
    x, proj, x1, u = sv["x"], sv["proj"], sv["x1"], sv["u"]
    grads = {}
    du = _ffn_out_dgrad(dx2, u, W["w_ff2"])
    grads["w_ff2"] = _rowsharded_wgrad("ff2_wgrad", u, dx2, FFN_HIDDEN, pro=_relu2)
    dx1, dg2, _ = _dgrad_rms("ff1_dgrad", du, W["w_ff1"], x1, g2, dx2)
    grads["w_ff1"] = _colsharded_wgrad_full("ff1_wgrad", sv["h2"], du, FFN_HIDDEN, 1024)
    do3, dproj, dy_r, dy_p, do_m = _merge_bwd(dx1, proj, sv["o3"], W["w_ret_o"], W["w_pool_o"], W["w_mem_o"], W["w_out"])
    grads["w_out"], grads["w_ret_o"], grads["w_pool_o"], grads["w_mem_o"] = _mixer_wgrads(
        sv["merged"], sv["y_r"], sv["y_p"], sv["o_m"], dx1, do3)
    early = [grads[n] for n in EARLY_GRADS]
    d_o, dproj, land = _gn_gate_bwd(dy_r, sv["o"], proj, dproj, swap=early)
    sums = _chip_sums(early, my_core, land)
    dproj, dlg, parts = _retention_bwd(proj, d_o, dproj, cos, sin, lg, B, S, scatter=sums)
    done = _finish_reduce_scatter(parts, sums, [big_grads[n] for n in EARLY_GRADS], layer, my_chip, my_core)
    big_grads.update(zip(EARLY_GRADS, done))
    dproj, dw_grp, dscale = _pool_bwd(proj, dy_p, dproj, w_grp, scale, B, S)
    dproj, dkv = _mem_attn_bwd(proj, sv["kv"], do_m, dproj, B, S)
    grads["w_mem_kv"] = _rowsharded_wgrad("mem_kv_wgrad", memn, dkv, D_MODEL)
    if d_memn is None:
        d_memn = _rowsharded_dgrad("mem_kv_dgrad", dkv, W["w_mem_kv"], out_dtype=F32)
    else:
        d_memn = _rowsharded_dgrad("mem_kv_dgrad_acc", dkv, W["w_mem_kv"], epi=_add_res, epi_in=d_memn, out_dtype=F32)
    grads["w_in"] = _colsharded_wgrad_full("in_wgrad", sv["h1"], dproj, N_PROJ, 1792)
    sums = _chip_sums([grads[n] for n in LATE_GRADS], my_core)
    dx, dg1, parts = _dgrad_rms("in_dgrad", dproj, W["w_in"], x, g1, dx1, scatter=sums)
    done = _finish_reduce_scatter(parts, sums, [big_grads[n] for n in LATE_GRADS], layer, my_chip, my_core)
    big_grads.update(zip(LATE_GRADS, done))
    small_grads = dict(g1=dg1, g2=dg2, lg=jnp.sum(dlg, axis=(0, 2)).reshape(N_HEADS, 2).T, w_grp=dw_grp, scale=dscale)
    return dx, small_grads, d_memn


def kernel(x, mem, w_in, ret_decay_logit, w_ret_o, w_pool_grp, pool_scale, w_pool_o, w_mem_kv, w_mem_o, w_out, w_ff1, w_ff2, norm1_g, norm2_g, mem_norm_g, final_norm_g, loss_target, m_w_in, m_ret_decay_logit, m_w_ret_o, m_w_pool_grp, m_pool_scale, m_w_pool_o, m_w_mem_kv, m_w_mem_o, m_w_out, m_w_ff1, m_w_ff2, m_norm1_g, m_norm2_g, m_mem_norm_g, m_final_norm_g, v_w_in, v_ret_decay_logit, v_w_ret_o, v_w_pool_grp, v_pool_scale, v_w_pool_o, v_w_mem_kv, v_w_mem_o, v_w_out, v_w_ff1, v_w_ff2, v_norm1_g, v_norm2_g, v_mem_norm_g, v_final_norm_g):
    B, S, _ = x.shape
    T = B * S
    big_w = dict(w_in=w_in, w_ret_o=w_ret_o, w_pool_o=w_pool_o, w_mem_kv=w_mem_kv, w_mem_o=w_mem_o, w_out=w_out, w_ff1=w_ff1, w_ff2=w_ff2)
    big_m = dict(w_in=m_w_in, w_ret_o=m_w_ret_o, w_pool_o=m_w_pool_o, w_mem_kv=m_w_mem_kv, w_mem_o=m_w_mem_o, w_out=m_w_out, w_ff1=m_w_ff1, w_ff2=m_w_ff2)
    big_v = dict(w_in=v_w_in, w_ret_o=v_w_ret_o, w_pool_o=v_w_pool_o, w_mem_kv=v_w_mem_kv, w_mem_o=v_w_mem_o, w_out=v_w_out, w_ff1=v_w_ff1, w_ff2=v_w_ff2)
    small_w = [ret_decay_logit, w_pool_grp, pool_scale, norm1_g, norm2_g, mem_norm_g, final_norm_g]
    small_m = [m_ret_decay_logit, m_w_pool_grp, m_pool_scale, m_norm1_g, m_norm2_g, m_mem_norm_g, m_final_norm_g]
    small_v = [v_ret_decay_logit, v_w_pool_grp, v_pool_scale, v_norm1_g, v_norm2_g, v_mem_norm_g, v_final_norm_g]
    my_chip = (2 * lax.axis_index("x") + lax.axis_index("y")).astype(jnp.int32).reshape(1)
    my_core = lax.axis_index("c").astype(jnp.int32).reshape(1)

    wbuf = {(l, n): _cast_into_slot(big_w[n], l, my_chip) for l in range(DEPTH) for n in BIG}
    wbuf[(0, "w_in")] = _allgather_weights([wbuf[(0, "w_in")]])[0]

    cos, sin = _rope_tables(S)
    log_g = jax.nn.log_sigmoid(ret_decay_logit.astype(F32))
    mem2d = mem.reshape(B * MEM_LEN, D_MODEL)
    memn = _mem_norm(mem2d, mem_norm_g.reshape(1, D_MODEL))
    smalls = [(norm1_g[l].reshape(1, D_MODEL), norm2_g[l].reshape(1, D_MODEL), log_g[l], w_pool_grp[l],
               pool_scale[l].reshape(1, POOL_W)) for l in range(DEPTH)]

    h = x.reshape(T, D_MODEL)
    saved = []
    for l in range(DEPTH):
        h, sv = _layer_fwd(h, l, wbuf, smalls[l], memn, cos, sin, B, S)
        saved.append(sv)
    weights = [{n: wbuf[(l, n)] for n in BIG} for l in range(DEPTH)]

    dh, sq, d_final_g = _loss_head(h, final_norm_g.reshape(1, D_MODEL), loss_target.reshape(T, D_MODEL))
    loss = lax.psum(0.5 * jnp.sum(sq) / D_MODEL, ("x", "y", "c"))

    big_grads = dict.fromkeys(BIG)
    small_grads = [None] * DEPTH
    d_memn = None
    for l in reversed(range(DEPTH)):
        dh, small_grads[l], d_memn = _layer_bwd(
            dh, saved[l], weights[l], smalls[l], memn, d_memn, cos, sin, B, S, big_grads, l, my_chip, my_core)
    d_mem_g = _mem_norm_wgrad(mem2d, d_memn)

    d_logit = jnp.stack([sg["lg"] for sg in small_grads]) * jax.nn.sigmoid(-ret_decay_logit.astype(F32))
    small_g_local = [
        d_logit, jnp.stack([sg["w_grp"] for sg in small_grads]),
        jnp.stack([sg["scale"].reshape(POOL_W) for sg in small_grads]),
        jnp.stack([sg["g1"].reshape(D_MODEL) for sg in small_grads]),
        jnp.stack([sg["g2"].reshape(D_MODEL) for sg in small_grads]),
        d_mem_g.reshape(D_MODEL), d_final_g.reshape(D_MODEL),
    ]
    small_g = _allreduce_small(_pack_small(small_g_local))
    s_delta, s_m, s_v = _adamw(_pack_small(small_w), small_g, _pack_small(small_m), _pack_small(small_v))
    small_g, s_delta, s_m, s_v = (_unpack_small(a, small_w) for a in (small_g, s_delta, s_m, s_v))

    big_out = {}
    for n in BIG:
        w = big_w[n]
        g = big_grads[n]
        flat = lambda a: a.reshape(-1, a.shape[-1])
        d, nm, nv = _adamw(flat(w), flat(g), flat(big_m[n]), flat(big_v[n]))
        big_out[n] = (g, d.reshape(w.shape), nm.reshape(w.shape), nv.reshape(w.shape))

    order = ["w_in", "ret_decay_logit", "w_ret_o", "w_pool_grp", "pool_scale", "w_pool_o", "w_mem_kv", "w_mem_o",
             "w_out", "w_ff1", "w_ff2", "norm1_g", "norm2_g", "mem_norm_g", "final_norm_g"]
    small_names = ["ret_decay_logit", "w_pool_grp", "pool_scale", "norm1_g", "norm2_g", "mem_norm_g", "final_norm_g"]
    outs = [[], [], [], []]
    for n in order:
        if n in big_out:
            vals = big_out[n]
        else:
            i = small_names.index(n)
            vals = (small_g[i], s_delta[i], s_m[i], s_v[i])
        for k in range(4):
            outs[k].append(vals[k])
    return (loss, dh.reshape(B, S, D_MODEL), *outs[0], *outs[1], *outs[2], *outs[3])
```

```python
import functools

import jax
import jax.numpy as jnp
from jax import lax
from jax.experimental import pallas as pl
from jax.experimental.pallas import tpu as pltpu

F32 = jnp.float32
_MXU = jnp.bfloat16
_ACT = jnp.bfloat16
_COMM = jnp.bfloat16

D_MODEL = 1024
N_PROJ = 7168
FFN_HIDDEN = 4096
MEM_LEN = 256
N_HEADS = 4
QK_DIM = 128
V_DIM = 256
POOL_W = 512
MEMQ_W = 512
DEPTH = 4
N_CHIPS = 4
EPS = 1e-6
ROPE_BASE = 10000.0

C_Q, C_K, C_V, C_G, C_P, C_QM, C_GATE = 0, 512, 1024, 2048, 3072, 3584, 4096

ADAM_LR = 0.001
ADAM_B1 = 0.9
ADAM_B2 = 0.999
ADAM_EPS = 1e-08
ADAM_WD = 0.01
ADAM_STEP = 10

VMEM_LIMIT_BYTES = 56 * 1024 * 1024
MESH = pl.DeviceIdType.MESH
ANY = pl.BlockSpec(memory_space=pl.ANY)

_DN = {
    "nn": (((1,), (0,)), ((), ())),
    "nt": (((1,), (1,)), ((), ())),
    "tn": (((0,), (0,)), ((), ())),
}


def _call(body, **kw):
    return pl.pallas_call(body, **kw)


def _cp(sem=None):
    return pltpu.CompilerParams(dimension_semantics=sem, vmem_limit_bytes=VMEM_LIMIT_BYTES)


def _dot(a, b, kind="nn"):
    return lax.dot_general(a.astype(_MXU), b.astype(_MXU), _DN[kind], preferred_element_type=F32)


def _sds(shape, dtype):
    return jax.ShapeDtypeStruct(shape, dtype)


def _rms_rows(x, g):
    r = lax.rsqrt(jnp.mean(x * x, axis=-1, keepdims=True) + EPS)
    return x * r * g


def _relu2(u):
    r = jnp.maximum(u.astype(F32), 0.0)
    return r * r


def _add_res(r, e):
    return r + e.astype(F32)


def _relu2_bwd(r, u):
    return r * (2.0 * jnp.maximum(u.astype(F32), 0.0))


def _mm(name, kind, a, b, a_spec, b_spec, out_shape, o_spec, grid, acc_shape, *,
        pro=None, pro_in=(), pro_specs=(), epi=None, epi_in=(), epi_specs=()):
    nk = grid[2]
    npro, nepi = len(pro_in), len(epi_in)

    def body(*refs):
        a_ref, b_ref = refs[0], refs[1]
        pro_refs = refs[2:2 + npro]
        epi_refs = refs[2 + npro:2 + npro + nepi]
        o_ref = refs[2 + npro + nepi]
        av = a_ref[...]
        if pro is not None:
            av = pro(av, *[r[...] for r in pro_refs])
        part = _dot(av, b_ref[...], kind)

        def finish(r):
            if epi is not None:
                r = epi(r, *[e[...] for e in epi_refs])
            o_ref[...] = r.astype(o_ref.dtype)

        if nk == 1:
            finish(part)
        else:
            acc = refs[-1]
            k = pl.program_id(2)

            @pl.when(k == 0)
            def _():
                acc[...] = part

            @pl.when(k > 0)
            def _():
                acc[...] += part

            @pl.when(k == nk - 1)
            def _():
                finish(acc[...])

    scratch = [] if nk == 1 else [pltpu.VMEM(acc_shape, F32)]
    return _call(
        body, out_shape=out_shape, grid=grid,
        in_specs=[a_spec, b_spec, *pro_specs, *epi_specs], out_specs=o_spec,
        scratch_shapes=scratch, name=name,
        compiler_params=_cp(("parallel", "parallel", "arbitrary")),
    )(a, b, *pro_in, *epi_in)


def _row_tile(n, cap):
    t = min(n, cap)
    assert n % t == 0, (n, t)
    return t


def _resident(w):
    nd = w.ndim
    return pl.BlockSpec(w.shape, lambda i: (0,) * nd, pipeline_mode=pl.Buffered(1))


def _rms_proj(name, x, g, w, n_cols, gather=()):
    T = x.shape[0]
    tm = _row_tile(T, 512)
    ns = n_cols // N_CHIPS
    steps = T // tm
    ng = len(gather)

    def body(x_ref, g_ref, w_ref, *rest):
        o_ref, h_ref = rest[ng], rest[ng + 1]
        bufs, sems = rest[ng + 2:2 * ng + 2], rest[2 * ng + 2:]
        if ng:
            @pl.when(pl.program_id(0) == 0)
            def _():
                _gather_start(bufs, *sems)

        h = _rms_rows(x_ref[...], g_ref[...]).astype(h_ref.dtype)
        h_ref[...] = h
        for k in range(N_CHIPS):
            o_ref[:, k * ns:(k + 1) * ns] = _dot(h, w_ref[k]).astype(o_ref.dtype)

        if ng:
            @pl.when(pl.program_id(0) == steps - 1)
            def _():
                _gather_finish(bufs, *sems)

    row = pl.BlockSpec((tm, D_MODEL), lambda i: (i, 0))
    out = _call(
        body,
        out_shape=(_sds((T, n_cols), _ACT), _sds((T, D_MODEL), _ACT), *[_sds(b.shape, b.dtype) for b in gather]),
        grid=(steps,),
        in_specs=[row, pl.BlockSpec((1, D_MODEL), lambda i: (0, 0)), _resident(w)] + [ANY] * ng,
        out_specs=(pl.BlockSpec((tm, n_cols), lambda i: (i, 0)), row) + (ANY,) * ng,
        input_output_aliases={3 + i: 2 + i for i in range(ng)},
        scratch_shapes=_gather_sems(ng) if ng else [],
        name=name, compiler_params=_cp(("arbitrary",)),
    )(x, g, w, *gather)
    return out[0], out[1], list(out[2:])


def _colsharded_wgrad_full(name, h, dy, n_cols, tn):
    T = h.shape[0]
    tt = _row_tile(T, 2048)
    per = (n_cols // N_CHIPS) // tn
    return _mm(
        name, "tn", h, dy,
        pl.BlockSpec((tt, D_MODEL), lambda i, j, k: (k, 0)),
        pl.BlockSpec((tt, tn), lambda i, j, k: (k, j)),
        _sds((N_CHIPS, D_MODEL, n_cols // N_CHIPS), _COMM),
        pl.BlockSpec((None, D_MODEL, tn), lambda i, j, k: (j // per, 0, j % per)),
        (1, n_cols // tn, T // tt), (D_MODEL, tn),
    )


def _dgrad_rms(name, dy, w, x, g, dres, scatter=()):
    T, kd = dy.shape
    tm = _row_tile(T, 512)
    ns = kd // N_CHIPS
    steps = T // tm
    nsc = len(scatter)

    def body(dy_ref, w_ref, x_ref, g_ref, dres_ref, *rest):
        sums = rest[:nsc]
        dx_ref, dg_ref = rest[nsc], rest[nsc + 1]
        parts, sems = rest[nsc + 2:2 * nsc + 2], rest[2 * nsc + 2:]
        i = pl.program_id(0)
        if nsc:
            @pl.when(i == 0)
            def _():
                _scatter_start(sums, parts, *sems)

        dh = _dot(dy_ref[:, 0:ns], w_ref[0], "nt")
        for k in range(1, N_CHIPS):
            dh += _dot(dy_ref[:, k * ns:(k + 1) * ns], w_ref[k], "nt")
        x = x_ref[...]
        r = lax.rsqrt(jnp.mean(x * x, axis=-1, keepdims=True) + EPS)
        xh = x * r
        dxh = dh * g_ref[...]
        dx_ref[...] = dres_ref[...] + r * (dxh - xh * jnp.mean(dxh * xh, axis=-1, keepdims=True))
        dgp = jnp.sum(dh * xh, axis=0, keepdims=True)

        @pl.when(i == 0)
        def _():
            dg_ref[...] = dgp

        @pl.when(i > 0)
        def _():
            dg_ref[...] += dgp

        if nsc:
            @pl.when(i == steps - 1)
            def _():
                _scatter_finish(sums, parts, *sems)

    row = pl.BlockSpec((tm, D_MODEL), lambda i: (i, 0))
    vec = pl.BlockSpec((1, D_MODEL), lambda i: (0, 0))
    out = _call(
        body,
        out_shape=(_sds((T, D_MODEL), F32), _sds((1, D_MODEL), F32), *_scatter_shapes(scatter)),
        grid=(steps,),
        in_specs=[pl.BlockSpec((tm, kd), lambda i: (i, 0)), _resident(w), row, vec, row] + [ANY] * nsc,
        out_specs=(row, vec) + (ANY,) * nsc,
        scratch_shapes=_scatter_sems(nsc) if nsc else [],
        name=name, compiler_params=_cp(("arbitrary",)),
    )(dy, w, x, g, dres, *scatter)
    return out[0], out[1], list(out[2:])


def _ffn_out(x1, u, w, gather=()):
    T = u.shape[0]
    tm = _row_tile(T, 512)
    ks = w.shape[1]
    steps = T // tm
    ng = len(gather)

    def body(x_ref, u_ref, w_ref, *rest):
        o_ref = rest[ng]
        bufs, sems = rest[ng + 1:2 * ng + 1], rest[2 * ng + 1:]
        if ng:
            @pl.when(pl.program_id(0) == 0)
            def _():
                _gather_start(bufs, *sems)

        acc = x_ref[...]
        for k in range(N_CHIPS):
            acc += _dot(_relu2(u_ref[:, k * ks:(k + 1) * ks]), w_ref[k])
        o_ref[...] = acc

        if ng:
            @pl.when(pl.program_id(0) == steps - 1)
            def _():
                _gather_finish(bufs, *sems)

    row = pl.BlockSpec((tm, D_MODEL), lambda i: (i, 0))
    out = _call(
        body, out_shape=(_sds((T, D_MODEL), F32), *[_sds(b.shape, b.dtype) for b in gather]), grid=(steps,),
        in_specs=[row, pl.BlockSpec((tm, FFN_HIDDEN), lambda i: (i, 0)), _resident(w)] + [ANY] * ng,
        out_specs=(row,) + (ANY,) * ng,
        input_output_aliases={3 + i: 1 + i for i in range(ng)},
        scratch_shapes=_gather_sems(ng) if ng else [],
        name="ff2", compiler_params=_cp(("arbitrary",)),
    )(x1, u, w, *gather)
    return out[0], list(out[1:])


def _ffn_out_dgrad(dx2, u, w):
    T = u.shape[0]
    tm = _row_tile(T, 512)
    ks = w.shape[1]

    def body(d_ref, u_ref, w_ref, o_ref):
        d = d_ref[...].astype(_MXU)
        for k in range(N_CHIPS):
            sl = slice(k * ks, (k + 1) * ks)
            o_ref[:, sl] = _relu2_bwd(_dot(d, w_ref[k], "nt"), u_ref[:, sl]).astype(o_ref.dtype)

    wide = pl.BlockSpec((tm, FFN_HIDDEN), lambda i: (i, 0))
    return _call(
        body, out_shape=_sds((T, FFN_HIDDEN), _ACT), grid=(T // tm,),
        in_specs=[pl.BlockSpec((tm, D_MODEL), lambda i: (i, 0)), wide, _resident(w)], out_specs=wide,
        name="ff2_dgrad", compiler_params=_cp(("parallel",)),
    )(dx2, u, w)


def _rowsharded_mm(name, a, w, res, *, pro=None):
    T, K = a.shape
    ks = K // N_CHIPS
    tm = _row_tile(T, 512)
    epi = dict(epi=_add_res, epi_in=(res,), epi_specs=(pl.BlockSpec((tm, D_MODEL), lambda i, j, k: (i, 0)),)) if res is not None else {}
    return _mm(
        name, "nn", a, w,
        pl.BlockSpec((tm, ks), lambda i, j, k: (i, k)),
        pl.BlockSpec((None, ks, D_MODEL), lambda i, j, k: (k, 0, 0)),
        _sds((T, D_MODEL), F32 if res is not None else _ACT),
        pl.BlockSpec((tm, D_MODEL), lambda i, j, k: (i, 0)),
        (T // tm, 1, N_CHIPS), (tm, D_MODEL), pro=pro, **epi,
    )


def _rowsharded_dgrad(name, dy, w, *, epi=None, epi_in=None, out_dtype=_ACT, res=None):
    T = dy.shape[0]
    ks = w.shape[1]
    tm = _row_tile(T, 512)
    kw = {}
    if epi is not None:
        kw = dict(epi=epi, epi_in=(epi_in,), epi_specs=(pl.BlockSpec((tm, ks), lambda i, j, k: (i, j)),))
    return _mm(
        name, "nt", dy, w,
        pl.BlockSpec((tm, D_MODEL), lambda i, j, k: (i, 0)),
        pl.BlockSpec((None, ks, D_MODEL), lambda i, j, k: (j, 0, 0)),
        _sds((T, ks * N_CHIPS), out_dtype),
        pl.BlockSpec((tm, ks), lambda i, j, k: (i, j)),
        (T // tm, N_CHIPS, 1), None, **kw,
    )


def _rowsharded_wgrad(name, a, dy, K, *, pro=None):
    T = a.shape[0]
    ks = K // N_CHIPS
    tt = _row_tile(T, 2048)
    return _mm(
        name, "tn", a, dy,
        pl.BlockSpec((tt, ks), lambda i, j, k: (k, i)),
        pl.BlockSpec((tt, D_MODEL), lambda i, j, k: (k, 0)),
        _sds((N_CHIPS, ks, D_MODEL), _COMM),
        pl.BlockSpec((None, ks, D_MODEL), lambda i, j, k: (i, 0, 0)),
        (N_CHIPS, 1, T // tt), (ks, D_MODEL), pro=pro,
    )


def _mixer_wgrads(merged, y_r, y_p, o_m, dx1, do3):
    T = merged.shape[0]
    tt = _row_tile(T, 512)
    nt = T // tt
    rs = D_MODEL // N_CHIPS

    def body(mg_ref, yr_ref, yp_ref, om_ref, dx_ref, do3_ref, g_out, g_ret, g_pool, g_mem, a_out, a_ret, a_pool, a_mem):
        t = pl.program_id(0)

        def accumulate(acc, part):
            @pl.when(t == 0)
            def _():
                acc[...] = part

            @pl.when(t > 0)
            def _():
                acc[...] += part

        accumulate(a_out, _dot(mg_ref[...], dx_ref[...], "tn"))
        accumulate(a_ret, _dot(yr_ref[...], do3_ref[:, 0:D_MODEL], "tn"))
        accumulate(a_pool, _dot(yp_ref[...], do3_ref[:, D_MODEL:2 * D_MODEL], "tn"))
        accumulate(a_mem, _dot(om_ref[...], do3_ref[:, 2 * D_MODEL:3 * D_MODEL], "tn"))

        @pl.when(t == nt - 1)
        def _():
            for k in range(N_CHIPS):
                rows = slice(k * rs, (k + 1) * rs)
                g_out[k] = a_out[rows, :].astype(g_out.dtype)
                g_ret[k] = a_ret[rows, :].astype(g_ret.dtype)
                g_pool[k] = a_pool[:, rows].astype(g_pool.dtype)
                g_mem[k] = a_mem[:, rows].astype(g_mem.dtype)

    row = lambda w: pl.BlockSpec((tt, w), lambda t: (t, 0))
    whole = lambda s: pl.BlockSpec(s, lambda t: (0, 0, 0))
    rsh, csh = (N_CHIPS, rs, D_MODEL), (N_CHIPS, POOL_W, rs)
    return _call(
        body,
        out_shape=(_sds(rsh, _COMM), _sds(rsh, _COMM), _sds(csh, _COMM), _sds(csh, _COMM)),
        grid=(nt,),
        in_specs=[row(D_MODEL), row(D_MODEL), row(POOL_W), row(MEMQ_W), row(D_MODEL), row(3 * D_MODEL)],
        out_specs=(whole(rsh), whole(rsh), whole(csh), whole(csh)),
        scratch_shapes=[pltpu.VMEM((D_MODEL, D_MODEL), F32), pltpu.VMEM((D_MODEL, D_MODEL), F32),
                        pltpu.VMEM((POOL_W, D_MODEL), F32), pltpu.VMEM((MEMQ_W, D_MODEL), F32)],
        name="mixer_wgrads", compiler_params=_cp(("arbitrary",)),
    )(merged, y_r, y_p, o_m, dx1, do3)


def _rot(x, cos, sin):
    return x * cos + pltpu.roll(x, QK_DIM // 2, 1) * sin


def _rot_bwd(d, cos, sin):
    return d * cos + pltpu.roll(d * sin, QK_DIM // 2, 1)


def _tile_diff(qi, ki, tq, tk):
    n = qi * tq + lax.broadcasted_iota(jnp.int32, (tq, tk), 0)
    m = ki * tk + lax.broadcasted_iota(jnp.int32, (tq, tk), 1)
    return (n - m).astype(F32)


def _decay(qi, ki, tq, tk, lgf, lgb):
    diff = _tile_diff(qi, ki, tq, tk)
    return diff, jnp.exp(jnp.where(diff >= 0.0, lgf * diff, -(lgb * diff)))


def _head(h, width):
    return slice(h * width, (h + 1) * width)


def _row_index(t):
    return lax.broadcasted_iota(jnp.int32, (t, QK_DIM), 0).astype(F32)


def _q_decay_fwd(t, lgf):
    return jnp.exp(lgf * _row_index(t))


def _q_decay_bwd(t, lgb):
    return jnp.exp(lgb * (float(t) - _row_index(t)))


def _k_decay_fwd(t, lgf, tiles_apart):
    return jnp.exp(lgf * ((tiles_apart * t).astype(F32) - _row_index(t)))


def _k_decay_bwd(t, lgb, tiles_apart):
    return jnp.exp(lgb * (((tiles_apart - 1) * t).astype(F32) + _row_index(t)))


def _group_norm_gate(o, g):
    mu = jnp.mean(o, axis=-1, keepdims=True)
    oc = o - mu
    var = jnp.mean(oc * oc, axis=-1, keepdims=True)
    on = oc * lax.rsqrt(var + EPS)
    return on * (g * jax.nn.sigmoid(g))


def _retention_fwd(proj, cos, sin, lg, B, S, gather=()):
    T = B * S
    t = _row_tile(S, 512)
    nc = S // t
    scale = QK_DIM ** -0.5
    ng = len(gather)
    qw, vw = N_HEADS * QK_DIM, N_HEADS * V_DIM

    def chunk_of(s):
        return jnp.where(s < nc, s, 2 * nc - 1 - s)

    def body(lg_ref, q_ref, k_ref, v_ref, g_ref, cos_ref, sin_ref, *rest):
        o_ref, y_ref = rest[ng], rest[ng + 1]
        bufs = rest[ng + 2:2 * ng + 2]
        o_acc, state_f, state_b = rest[2 * ng + 2:2 * ng + 5]
        sems = rest[2 * ng + 5:]
        b = pl.program_id(0)
        s = pl.program_id(1)
        rows = pl.ds(pl.multiple_of(chunk_of(s) * t, t), t)
        if ng:
            @pl.when(jnp.logical_and(b == 0, s == 0))
            def _():
                _gather_start(bufs, *sems)

        @pl.when(s == 0)
        def _():
            state_f[...] = jnp.zeros_like(state_f)
            state_b[...] = jnp.zeros_like(state_b)

        def rotated(h):
            sl = _head(h, QK_DIM)
            qr = _rot(q_ref[:, sl].astype(F32), cos_ref[...], sin_ref[...])
            kr = _rot(k_ref[:, sl].astype(F32), cos_ref[...], sin_ref[...]) * scale
            return qr, kr

        r = _row_index(t)

        @pl.when(s < nc)
        def _():
            for h in range(N_HEADS):
                sl, vsl = _head(h, QK_DIM), _head(h, V_DIM)
                lgf, lgb = lg_ref[0, h], lg_ref[1, h]
                qr, kr = rotated(h)
                v = v_ref[:, vsl]
                diff = _tile_diff(0, 0, t, t)
                dec = jnp.exp(jnp.where(diff >= 0.0, lgf * diff, -(lgb * diff)))
                o = _dot(_dot(qr, kr, "nt") * dec, v)
                o += _dot(qr * jnp.exp(lgf * (r + 1.0)), state_f[sl, :])
                o_acc[rows, vsl] = o
                state_f[sl, :] = state_f[sl, :] * jnp.exp(lgf * float(t)) + _dot(kr * jnp.exp(lgf * (float(t - 1) - r)), v, "tn")

        @pl.when(s >= nc)
        def _():
            for h in range(N_HEADS):
                sl, vsl = _head(h, QK_DIM), _head(h, V_DIM)
                lgb = lg_ref[1, h]
                qr, kr = rotated(h)
                o = o_acc[rows, vsl] + _dot(qr * jnp.exp(lgb * (float(t) - r)), state_b[sl, :])
                o_ref[:, vsl] = o
                y_ref[:, vsl] = _group_norm_gate(o, g_ref[:, vsl].astype(F32)).astype(y_ref.dtype)
                state_b[sl, :] = state_b[sl, :] * jnp.exp(lgb * float(t)) + _dot(kr * jnp.exp(lgb * r), v_ref[:, vsl], "tn")

        if ng:
            @pl.when(jnp.logical_and(b == B - 1, s == 2 * nc - 1))
            def _():
                _gather_finish(bufs, *sems)

    def in_rows(width, col):
        return pl.BlockSpec((t, width), lambda b, s: (b * nc + chunk_of(s), col))

    out_rows = pl.BlockSpec((t, vw), lambda b, s: (b * nc + jnp.where(s < nc, nc - 1, 2 * nc - 1 - s), 0))
    table = pl.BlockSpec((t, QK_DIM), lambda b, s: (chunk_of(s), 0))
    out = _call(
        body,
        out_shape=(_sds((T, vw), F32), _sds((T, vw), _ACT), *[_sds(g.shape, g.dtype) for g in gather]),
        grid=(B, 2 * nc),
        in_specs=[
            pl.BlockSpec(memory_space=pltpu.SMEM),
            in_rows(qw, C_Q // qw), in_rows(qw, C_K // qw), in_rows(vw, C_V // vw), in_rows(vw, C_G // vw),
            table, table,
        ] + [ANY] * ng,
        out_specs=(out_rows, out_rows) + (ANY,) * ng,
        input_output_aliases={7 + i: 2 + i for i in range(ng)},
        scratch_shapes=[pltpu.VMEM((S, vw), F32), pltpu.VMEM((qw, V_DIM), F32), pltpu.VMEM((qw, V_DIM), F32)]
        + (_gather_sems(ng) if ng else []),
        name="retention_fwd",
        compiler_params=_cp(("arbitrary", "arbitrary")),
    )(lg, proj, proj, proj, proj, cos, sin, *gather)
    return out[0], out[1], list(out[2:])


def _gn_gate_bwd(dy, o, proj, dproj, swap=()):
    T = dy.shape[0]
    tm = _row_tile(T, 512)
    steps = T // tm
    nw = len(swap)

    def body(dy_ref, o_ref, g_ref, dproj_in, *rest):
        grads = rest[:nw]
        do_ref, dg_ref = rest[nw], rest[nw + 1]
        lands, sems = rest[nw + 2:2 * nw + 2], rest[2 * nw + 2:]
        if nw:
            @pl.when(pl.program_id(0) == 0)
            def _():
                for cp in _swap_copies(grads, lands, *sems):
                    cp.start()

            @pl.when(pl.program_id(0) == steps - 1)
            def _():
                for cp in _swap_copies(grads, lands, *sems):
                    cp.wait()

        for h in range(N_HEADS):
            sl = slice(h * V_DIM, (h + 1) * V_DIM)
            o = o_ref[:, sl]
            g = g_ref[:, sl].astype(F32)
            d = dy_ref[:, sl].astype(F32)
            mu = jnp.mean(o, axis=-1, keepdims=True)
            oc = o - mu
            rstd = lax.rsqrt(jnp.mean(oc * oc, axis=-1, keepdims=True) + EPS)
            on = oc * rstd
            sg = jax.nn.sigmoid(g)
            don = d * (g * sg)
            dg_ref[:, sl] = (d * on * (sg * (1.0 + g * (1.0 - sg)))).astype(dg_ref.dtype)
            do = rstd * (don - jnp.mean(don, axis=-1, keepdims=True) - on * jnp.mean(don * on, axis=-1, keepdims=True))
            do_ref[:, sl] = do.astype(do_ref.dtype)

    wide = N_HEADS * V_DIM
    out = _call(
        body,
        out_shape=(_sds((T, wide), _ACT), _sds(dproj.shape, dproj.dtype), *_swap_shapes(swap)),
        grid=(steps,),
        in_specs=[
            pl.BlockSpec((tm, wide), lambda i: (i, 0)),
            pl.BlockSpec((tm, wide), lambda i: (i, 0)),
            pl.BlockSpec((tm, wide), lambda i: (i, C_G // wide)),
            ANY,
        ] + [ANY] * nw,
        out_specs=(pl.BlockSpec((tm, wide), lambda i: (i, 0)), pl.BlockSpec((tm, wide), lambda i: (i, C_G // wide)))
        + (ANY,) * nw,
        input_output_aliases={3: 1},
        scratch_shapes=_swap_sems(nw) if nw else [],
        name="gn_gate_bwd", compiler_params=_cp(("arbitrary",)),
    )(dy, o, proj, dproj, *swap)
    return out[0], out[1], list(out[2:])


def _retention_bwd(proj, d_o, dproj, cos, sin, lg, B, S, scatter=()):
    T = B * S
    tq = tk = _row_tile(S, 512)
    nq, nk = S // tq, S // tk
    scale = QK_DIM ** -0.5
    qw, vw = N_HEADS * QK_DIM, N_HEADS * V_DIM

    ns = len(scatter)

    def body(lg_ref, q_ref, k_ref, v_ref, do_ref, cq_ref, sq_ref, ck_ref, sk_ref, dproj_in, *rest):
        sums = rest[:ns]
        dqkv_ref, dlg_ref = rest[ns], rest[ns + 1]
        parts = rest[ns + 2:2 * ns + 2]
        kr_scr, dq_acc, dk_acc, dv_acc, gl_acc = rest[2 * ns + 2:2 * ns + 7]
        sems = rest[2 * ns + 7:]
        b = pl.program_id(0)
        ki = pl.program_id(1)
        qi = pl.program_id(2)
        q_rows = pl.ds(pl.multiple_of(qi * tq, tq), tq)
        k_rows = pl.ds(pl.multiple_of(ki * tk, tk), tk)
        if ns:
            @pl.when(jnp.logical_and(b == 0, jnp.logical_and(qi == 0, ki == 0)))
            def _():
                _scatter_start(sums, parts, *sems)

        @pl.when(qi == 0)
        def _():
            for h in range(N_HEADS):
                sl = _head(h, QK_DIM)
                kr_scr[:, sl] = _rot(k_ref[:, sl].astype(F32), ck_ref[...], sk_ref[...]) * scale
            dk_acc[...] = jnp.zeros_like(dk_acc)
            dv_acc[...] = jnp.zeros_like(dv_acc)

        @pl.when(jnp.logical_and(qi == 0, ki == 0))
        def _():
            gl_acc[...] = jnp.zeros_like(gl_acc)

        @pl.when(ki == 0)
        def _():
            dq_acc[q_rows, :] = jnp.zeros((tq, qw), F32)

        def queries(h):
            return _rot(q_ref[:, _head(h, QK_DIM)].astype(F32), cq_ref[...], sq_ref[...])

        def one_sided(h, q_factor, k_factor, side, sign):
            sl, vsl = _head(h, QK_DIM), _head(h, V_DIM)
            qt = (queries(h) * q_factor).astype(_MXU)
            kt = (kr_scr[:, sl] * k_factor).astype(_MXU)
            d_out = do_ref[:, vsl]
            p = _dot(qt, kt, "nt")
            dv_acc[:, vsl] += _dot(p, d_out, "tn")
            dp = _dot(d_out, v_ref[:, vsl], "nt")
            dq_acc[q_rows, sl] += _dot(dp, kt) * q_factor
            dk_acc[:, sl] += _dot(dp, qt, "tn") * k_factor
            diff = _tile_diff(qi, ki, tq, tk)
            row = 2 * h + side
            gl_acc[row:row + 1, :] += sign * jnp.sum(dp * p * diff, axis=0, keepdims=True)

        @pl.when(ki < qi)
        def _():
            for h in range(N_HEADS):
                one_sided(h, _q_decay_fwd(tq, lg_ref[0, h]), _k_decay_fwd(tk, lg_ref[0, h], qi - ki), 0, 1.0)

        @pl.when(ki > qi)
        def _():
            for h in range(N_HEADS):
                one_sided(h, _q_decay_bwd(tq, lg_ref[1, h]), _k_decay_bwd(tk, lg_ref[1, h], ki - qi), 1, -1.0)

        @pl.when(ki == qi)
        def _():
            for h in range(N_HEADS):
                sl, vsl = _head(h, QK_DIM), _head(h, V_DIM)
                qr = queries(h).astype(_MXU)
                kr = kr_scr[:, sl].astype(_MXU)
                d_out = do_ref[:, vsl]
                s = _dot(qr, kr, "nt")
                diff, dec = _decay(qi, ki, tq, tk, lg_ref[0, h], lg_ref[1, h])
                p = s * dec
                dv_acc[:, vsl] += _dot(p, d_out, "tn")
                dp = _dot(d_out, v_ref[:, vsl], "nt")
                ds = dp * dec
                dq_acc[q_rows, sl] += _dot(ds, kr)
                dk_acc[:, sl] += _dot(ds, qr, "tn")
                gd = dp * p * diff
                gl_acc[2 * h:2 * h + 1, :] += jnp.sum(jnp.where(diff >= 0.0, gd, 0.0), axis=0, keepdims=True)
                gl_acc[2 * h + 1:2 * h + 2, :] += jnp.sum(jnp.where(diff < 0.0, -gd, 0.0), axis=0, keepdims=True)

        @pl.when(qi == nq - 1)
        def _():
            for h in range(N_HEADS):
                sl = _head(h, QK_DIM)
                dk = _rot_bwd(dk_acc[:, sl] * scale, ck_ref[...], sk_ref[...])
                dqkv_ref[k_rows, qw + h * QK_DIM:qw + (h + 1) * QK_DIM] = dk.astype(dqkv_ref.dtype)
            dqkv_ref[k_rows, 2 * qw:2 * qw + vw] = dv_acc[...].astype(dqkv_ref.dtype)

        @pl.when(ki == nk - 1)
        def _():
            for h in range(N_HEADS):
                sl = _head(h, QK_DIM)
                dqkv_ref[q_rows, sl] = _rot_bwd(dq_acc[q_rows, sl], cq_ref[...], sq_ref[...]).astype(dqkv_ref.dtype)

        @pl.when(jnp.logical_and(qi == nq - 1, ki == nk - 1))
        def _():
            dlg_ref[...] = gl_acc[...]

        if ns:
            @pl.when(jnp.logical_and(b == B - 1, jnp.logical_and(qi == nq - 1, ki == nk - 1)))
            def _():
                _scatter_finish(sums, parts, *sems)

    out = _call(
        body,
        out_shape=(_sds(dproj.shape, dproj.dtype), _sds((B, 2 * N_HEADS, tk), F32), *_scatter_shapes(scatter)),
        grid=(B, nk, nq),
        in_specs=[
            pl.BlockSpec(memory_space=pltpu.SMEM),
            pl.BlockSpec((tq, qw), lambda b, ki, qi: (b * nq + qi, C_Q // qw)),
            pl.BlockSpec((tk, qw), lambda b, ki, qi: (b * nk + ki, C_K // qw)),
            pl.BlockSpec((tk, vw), lambda b, ki, qi: (b * nk + ki, C_V // vw)),
            pl.BlockSpec((tq, vw), lambda b, ki, qi: (b * nq + qi, 0)),
            pl.BlockSpec((tq, QK_DIM), lambda b, ki, qi: (qi, 0)),
            pl.BlockSpec((tq, QK_DIM), lambda b, ki, qi: (qi, 0)),
            pl.BlockSpec((tk, QK_DIM), lambda b, ki, qi: (ki, 0)),
            pl.BlockSpec((tk, QK_DIM), lambda b, ki, qi: (ki, 0)),
            ANY,
        ] + [ANY] * ns,
        out_specs=(
            pl.BlockSpec((S, 2 * qw + vw), lambda b, ki, qi: (b, 0)),
            pl.BlockSpec((None, 2 * N_HEADS, tk), lambda b, ki, qi: (b, 0, 0)),
        ) + (ANY,) * ns,
        scratch_shapes=[
            pltpu.VMEM((tk, qw), F32), pltpu.VMEM((S, qw), F32),
            pltpu.VMEM((tk, qw), F32), pltpu.VMEM((tk, vw), F32), pltpu.VMEM((2 * N_HEADS, tk), F32),
        ] + (_scatter_sems(ns) if ns else []),
        input_output_aliases={9: 0},
        name="retention_bwd",
        compiler_params=_cp(("arbitrary", "arbitrary", "arbitrary")),
    )(lg, proj, proj, proj, d_o, cos, sin, cos, sin, dproj, *scatter)
    return out[0], out[1], list(out[2:])


POOL_PAD = 16


def _pad_rows(v):
    z = jnp.zeros((POOL_PAD, v.shape[1]), F32)
    return jnp.concatenate([z, v, z], axis=0)


def _window_sums(first, length, levels):
    s = first
    step = 1
    for _ in range(levels - 1):
        s = pltpu.roll(s, step, 0) + pltpu.roll(s, length - step, 0)
        step *= 2
    return s


def _pool_counts(S, hw):
    n = lax.broadcasted_iota(jnp.int32, (S, 1), 0)
    return (jnp.minimum(n + hw, S) - jnp.maximum(n - hw, 0)).astype(F32)


def _pool_mixed(pf, S, g):
    length = S + 2 * POOL_PAD
    xp = _pad_rows(pf)
    s = _window_sums(xp + pltpu.roll(xp, 1, 0), length, g + 1)[POOL_PAD:POOL_PAD + S]
    return s / _pool_counts(S, 1 << g) - pf


def _pool_mixed_bwd(dmixed, S, g):
    length = S + 2 * POOL_PAD
    ep = _pad_rows(dmixed / _pool_counts(S, 1 << g))
    t = _window_sums(ep + pltpu.roll(ep, length - 1, 0), length, g + 1)[POOL_PAD:POOL_PAD + S]
    return t - dmixed


def _pool_fwd(proj, w_grp, scale, B, S):
    T = B * S
    G = POOL_W // 4

    def body(p_ref, wg_ref, sc_ref, y_ref):
        for g in range(4):
            sl = slice(g * G, (g + 1) * G)
            mixed = _pool_mixed(p_ref[:, sl].astype(F32), S, g)
            y_ref[:, sl] = (_dot(mixed, wg_ref[g]) * sc_ref[:, sl]).astype(y_ref.dtype)

    return _call(
        body, out_shape=_sds((T, POOL_W), _ACT), grid=(B,),
        in_specs=[
            pl.BlockSpec((S, POOL_W), lambda b: (b, C_P // POOL_W)),
            pl.BlockSpec((4, G, G), lambda b: (0, 0, 0)),
            pl.BlockSpec((1, POOL_W), lambda b: (0, 0)),
        ],
        out_specs=pl.BlockSpec((S, POOL_W), lambda b: (b, 0)),
        name="pool_fwd", compiler_params=_cp(("parallel",)),
    )(proj, w_grp, scale)


def _pool_bwd(proj, dy, dproj, w_grp, scale, B, S, join=(), layer=0):
    G = POOL_W // 4
    nj = len(join)

    def body(p_ref, dy_ref, wg_ref, sc_ref, dproj_in, *rest):
        dp_ref, dwg_ref, dsc_ref = rest[nj:nj + 3]
        g_alls, sems = rest[nj + 3:2 * nj + 3], rest[2 * nj + 3:]
        b = pl.program_id(0)
        if nj:
            @pl.when(b == 0)
            def _():
                for cp in _join_copies(g_alls, layer, *sems):
                    cp.start()

            @pl.when(b == B - 1)
            def _():
                for cp in _join_copies(g_alls, layer, *sems):
                    cp.wait()

        for g in range(4):
            sl = slice(g * G, (g + 1) * G)
            mixed = _pool_mixed(p_ref[:, sl].astype(F32), S, g)
            z = _dot(mixed, wg_ref[g])
            d = dy_ref[:, sl].astype(F32)
            dsc = jnp.sum(d * z, axis=0, keepdims=True)
            dz = d * sc_ref[:, sl]
            dwg = _dot(mixed, dz, "tn")
            dmixed = _dot(dz, wg_ref[g], "nt")
            dp_ref[:, sl] = _pool_mixed_bwd(dmixed, S, g).astype(dp_ref.dtype)

            @pl.when(b == 0)
            def _():
                dwg_ref[g] = dwg
                dsc_ref[:, sl] = dsc

            @pl.when(b > 0)
            def _():
                dwg_ref[g] += dwg
                dsc_ref[:, sl] += dsc

    out = _call(
        body,
        out_shape=(_sds(dproj.shape, dproj.dtype), _sds((4, G, G), F32), _sds((1, POOL_W), F32),
                   *[_sds(g.shape, g.dtype) for g in join]),
        grid=(B,),
        in_specs=[
            pl.BlockSpec((S, POOL_W), lambda b: (b, C_P // POOL_W)),
            pl.BlockSpec((S, POOL_W), lambda b: (b, 0)),
            pl.BlockSpec((4, G, G), lambda b: (0, 0, 0)),
            pl.BlockSpec((1, POOL_W), lambda b: (0, 0)),
            ANY,
        ] + [ANY] * nj,
        out_specs=(
            pl.BlockSpec((S, POOL_W), lambda b: (b, C_P // POOL_W)),
            pl.BlockSpec((4, G, G), lambda b: (0, 0, 0)),
            pl.BlockSpec((1, POOL_W), lambda b: (0, 0)),
        ) + (ANY,) * nj,
        input_output_aliases={4: 0, **{5 + i: 3 + i for i in range(nj)}},
        scratch_shapes=[pltpu.SemaphoreType.DMA((nj,)), pltpu.SemaphoreType.DMA((nj,))] if nj else [],
        name="pool_bwd", compiler_params=_cp(("arbitrary",)),
    )(proj, dy, w_grp, scale, dproj, *join)
    return out[0], out[1], out[2], list(out[3:])


def _mem_softmax(q, k):
    s = _dot(q, k, "nt") * (QK_DIM ** -0.5)
    e = jnp.exp(s - jnp.max(s, axis=-1, keepdims=True))
    return e / jnp.sum(e, axis=-1, keepdims=True)


def _mem_attn_fwd(proj, kv, B, S):
    T = B * S
    tq = _row_tile(S, 512)
    nq = S // tq

    def body(q_ref, kv_ref, o_ref):
        for h in range(N_HEADS):
            sl = slice(h * QK_DIM, (h + 1) * QK_DIM)
            a = _mem_softmax(q_ref[:, sl], kv_ref[:, sl])
            o_ref[:, sl] = _dot(a, kv_ref[:, MEMQ_W + h * QK_DIM:MEMQ_W + (h + 1) * QK_DIM]).astype(o_ref.dtype)

    return _call(
        body, out_shape=_sds((T, MEMQ_W), _ACT), grid=(B, nq),
        in_specs=[
            pl.BlockSpec((tq, MEMQ_W), lambda b, i: (b * nq + i, C_QM // MEMQ_W)),
            pl.BlockSpec((MEM_LEN, 2 * MEMQ_W), lambda b, i: (b, 0)),
        ],
        out_specs=pl.BlockSpec((tq, MEMQ_W), lambda b, i: (b * nq + i, 0)),
        name="mem_attn_fwd", compiler_params=_cp(("parallel", "parallel")),
    )(proj, kv)


def _mem_attn_bwd(proj, kv, d_o, dproj, B, S):
    tq = _row_tile(S, 512)
    nq = S // tq
    scale = QK_DIM ** -0.5

    def body(q_ref, kv_ref, do_ref, dproj_in, dq_ref, dkv_ref):
        i = pl.program_id(1)
        for h in range(N_HEADS):
            sl = slice(h * QK_DIM, (h + 1) * QK_DIM)
            vsl = slice(MEMQ_W + h * QK_DIM, MEMQ_W + (h + 1) * QK_DIM)
            q = q_ref[:, sl]
            a = _mem_softmax(q, kv_ref[:, sl])
            d = do_ref[:, sl]
            da = _dot(d, kv_ref[:, vsl], "nt")
            ds = a * (da - jnp.sum(a * da, axis=-1, keepdims=True)) * scale
            dq_ref[:, sl] = _dot(ds, kv_ref[:, sl]).astype(dq_ref.dtype)
            dk = _dot(ds, q, "tn")
            dv = _dot(a, d, "tn")

            @pl.when(i == 0)
            def _():
                dkv_ref[:, sl] = dk
                dkv_ref[:, vsl] = dv

            @pl.when(i > 0)
            def _():
                dkv_ref[:, sl] += dk
                dkv_ref[:, vsl] += dv

    return _call(
        body,
        out_shape=(_sds(dproj.shape, dproj.dtype), _sds((B * MEM_LEN, 2 * MEMQ_W), F32)),
        grid=(B, nq),
        in_specs=[
            pl.BlockSpec((tq, MEMQ_W), lambda b, i: (b * nq + i, C_QM // MEMQ_W)),
            pl.BlockSpec((MEM_LEN, 2 * MEMQ_W), lambda b, i: (b, 0)),
            pl.BlockSpec((tq, MEMQ_W), lambda b, i: (b * nq + i, 0)),
            ANY,
        ],
        out_specs=(
            pl.BlockSpec((tq, MEMQ_W), lambda b, i: (b * nq + i, C_QM // MEMQ_W)),
            pl.BlockSpec((MEM_LEN, 2 * MEMQ_W), lambda b, i: (b, 0)),
        ),
        input_output_aliases={3: 0},
        name="mem_attn_bwd", compiler_params=_cp(("parallel", "arbitrary")),
    )(proj, kv, d_o, dproj)


def _mem_norm(mem2d, g):
    M = mem2d.shape[0]
    tm = _row_tile(M, 512)

    def body(x_ref, g_ref, o_ref):
        o_ref[...] = _rms_rows(x_ref[...], g_ref[...]).astype(o_ref.dtype)

    return _call(
        body, out_shape=_sds((M, D_MODEL), _ACT), grid=(M // tm,),
        in_specs=[pl.BlockSpec((tm, D_MODEL), lambda i: (i, 0)), pl.BlockSpec((1, D_MODEL), lambda i: (0, 0))],
        out_specs=pl.BlockSpec((tm, D_MODEL), lambda i: (i, 0)),
        name="mem_norm", compiler_params=_cp(("parallel",)),
    )(mem2d, g)


def _mem_norm_wgrad(mem2d, d_memn):
    M = mem2d.shape[0]
    tm = _row_tile(M, 512)

    def body(x_ref, d_ref, dg_ref):
        i = pl.program_id(0)
        x = x_ref[...]
        xh = x * lax.rsqrt(jnp.mean(x * x, axis=-1, keepdims=True) + EPS)
        dg = jnp.sum(d_ref[...] * xh, axis=0, keepdims=True)

        @pl.when(i == 0)
        def _():
            dg_ref[...] = dg

        @pl.when(i > 0)
        def _():
            dg_ref[...] += dg

    return _call(
        body, out_shape=_sds((1, D_MODEL), F32), grid=(M // tm,),
        in_specs=[pl.BlockSpec((tm, D_MODEL), lambda i: (i, 0)), pl.BlockSpec((tm, D_MODEL), lambda i: (i, 0))],
        out_specs=pl.BlockSpec((1, D_MODEL), lambda i: (0, 0)),
        name="mem_norm_wgrad", compiler_params=_cp(("arbitrary",)),
    )(mem2d, d_memn)


def _row_mm(a, w_ref):
    ks = w_ref.shape[1]
    out = _dot(a[:, 0:ks], w_ref[0])
    for k in range(1, N_CHIPS):
        out += _dot(a[:, k * ks:(k + 1) * ks], w_ref[k])
    return out


def _row_mm_t(d, w_ref):
    return jnp.concatenate([_dot(d, w_ref[k], "nt") for k in range(N_CHIPS)], axis=1)


def _col_mm(a, w_ref):
    return jnp.concatenate([_dot(a, w_ref[k]) for k in range(N_CHIPS)], axis=1)


def _col_mm_t(d, w_ref):
    ns = w_ref.shape[2]
    out = _dot(d[:, 0:ns], w_ref[0], "nt")
    for k in range(1, N_CHIPS):
        out += _dot(d[:, k * ns:(k + 1) * ns], w_ref[k], "nt")
    return out


def _merge_fwd(x, proj, y_r, y_p, o_m, w_ret_o, w_pool_o, w_mem_o, w_out):
    T = x.shape[0]
    tm = _row_tile(T, 512)

    def body(x_ref, gr_ref, gp_ref, gm_ref, yr_ref, yp_ref, om_ref, wr_ref, wp_ref, wm_ref, wo_ref,
             x1_ref, mg_ref, o3_ref):
        o_r = _row_mm(yr_ref[...], wr_ref)
        o_p = _col_mm(yp_ref[...], wp_ref)
        o_q = _col_mm(om_ref[...], wm_ref)
        merged = (jax.nn.sigmoid(gr_ref[...].astype(F32)) * o_r + jax.nn.sigmoid(gp_ref[...].astype(F32)) * o_p
                  + jax.nn.sigmoid(gm_ref[...].astype(F32)) * o_q)
        mg = merged.astype(mg_ref.dtype)
        mg_ref[...] = mg
        o3_ref[:, 0:D_MODEL] = o_r.astype(o3_ref.dtype)
        o3_ref[:, D_MODEL:2 * D_MODEL] = o_p.astype(o3_ref.dtype)
        o3_ref[:, 2 * D_MODEL:3 * D_MODEL] = o_q.astype(o3_ref.dtype)
        x1_ref[...] = x_ref[...] + _row_mm(mg, wo_ref)

    gb = C_GATE // D_MODEL
    row = lambda w: pl.BlockSpec((tm, w), lambda i: (i, 0))
    return _call(
        body,
        out_shape=(_sds((T, D_MODEL), F32), _sds((T, D_MODEL), _ACT), _sds((T, 3 * D_MODEL), _ACT)),
        grid=(T // tm,),
        in_specs=[
            row(D_MODEL),
            pl.BlockSpec((tm, D_MODEL), lambda i: (i, gb)),
            pl.BlockSpec((tm, D_MODEL), lambda i: (i, gb + 1)),
            pl.BlockSpec((tm, D_MODEL), lambda i: (i, gb + 2)),
            row(D_MODEL), row(POOL_W), row(MEMQ_W),
            _resident(w_ret_o), _resident(w_pool_o), _resident(w_mem_o), _resident(w_out),
        ],
        out_specs=(row(D_MODEL), row(D_MODEL), row(3 * D_MODEL)),
        name="merge_fwd", compiler_params=_cp(("parallel",)),
    )(x, proj, proj, proj, y_r, y_p, o_m, w_ret_o, w_pool_o, w_mem_o, w_out)


def _merge_bwd(dx1, proj, o3, w_ret_o, w_pool_o, w_mem_o, w_out):
    T = dx1.shape[0]
    tm = _row_tile(T, 512)

    def body(dx_ref, gr_ref, gp_ref, gm_ref, o3_ref, wr_ref, wp_ref, wm_ref, wo_ref,
             do3_ref, dgate_ref, dyr_ref, dyp_ref, dom_ref):
        dmerged = _row_mm_t(dx_ref[...], wo_ref)
        douts = []
        dgate_ref[:, 0:MEMQ_W] = jnp.zeros((tm, MEMQ_W), dgate_ref.dtype)
        for n, g_ref in enumerate((gr_ref, gp_ref, gm_ref)):
            sl = slice(n * D_MODEL, (n + 1) * D_MODEL)
            gate = jax.nn.sigmoid(g_ref[...].astype(F32))
            d_out = (dmerged * gate).astype(do3_ref.dtype)
            do3_ref[:, sl] = d_out
            dgate_ref[:, MEMQ_W + n * D_MODEL:MEMQ_W + (n + 1) * D_MODEL] = (
                dmerged * o3_ref[:, sl].astype(F32) * gate * (1.0 - gate)).astype(dgate_ref.dtype)
            douts.append(d_out)
        dyr_ref[...] = _row_mm_t(douts[0], wr_ref).astype(dyr_ref.dtype)
        dyp_ref[...] = _col_mm_t(douts[1], wp_ref).astype(dyp_ref.dtype)
        dom_ref[...] = _col_mm_t(douts[2], wm_ref).astype(dom_ref.dtype)

    gb = C_GATE // D_MODEL
    row = lambda w: pl.BlockSpec((tm, w), lambda i: (i, 0))
    return _call(
        body,
        out_shape=(
            _sds((T, 3 * D_MODEL), _ACT), _sds((T, N_PROJ), _ACT),
            _sds((T, D_MODEL), _ACT), _sds((T, POOL_W), _ACT), _sds((T, MEMQ_W), _ACT),
        ),
        grid=(T // tm,),
        in_specs=[
            row(D_MODEL),
            pl.BlockSpec((tm, D_MODEL), lambda i: (i, gb)),
            pl.BlockSpec((tm, D_MODEL), lambda i: (i, gb + 1)),
            pl.BlockSpec((tm, D_MODEL), lambda i: (i, gb + 2)),
            row(3 * D_MODEL),
            _resident(w_ret_o), _resident(w_pool_o), _resident(w_mem_o), _resident(w_out),
        ],
        out_specs=(row(3 * D_MODEL), pl.BlockSpec((tm, N_PROJ - C_QM), lambda i: (i, 1)),
                   row(D_MODEL), row(POOL_W), row(MEMQ_W)),
        name="merge_bwd", compiler_params=_cp(("parallel",)),
    )(dx1, proj, proj, proj, o3, w_ret_o, w_pool_o, w_mem_o, w_out)


def _loss_head(x, g, target):
    T = x.shape[0]
    tm = _row_tile(T, 512)

    def body(x_ref, g_ref, t_ref, dx_ref, sq_ref, dg_ref):
        i = pl.program_id(0)
        x = x_ref[...]
        gg = g_ref[...]
        r = lax.rsqrt(jnp.mean(x * x, axis=-1, keepdims=True) + EPS)
        xh = x * r
        err = xh * gg - t_ref[...]
        dy = err * (1.0 / D_MODEL)
        dxh = dy * gg
        dx_ref[...] = r * (dxh - xh * jnp.mean(dxh * xh, axis=-1, keepdims=True))
        sq = jnp.sum(err * err, axis=0, keepdims=True)
        dg = jnp.sum(dy * xh, axis=0, keepdims=True)

        @pl.when(i == 0)
        def _():
            sq_ref[...] = sq
            dg_ref[...] = dg

        @pl.when(i > 0)
        def _():
            sq_ref[...] += sq
            dg_ref[...] += dg

    vec = pl.BlockSpec((1, D_MODEL), lambda i: (0, 0))
    row = pl.BlockSpec((tm, D_MODEL), lambda i: (i, 0))
    return _call(
        body,
        out_shape=(_sds((T, D_MODEL), F32), _sds((1, D_MODEL), F32), _sds((1, D_MODEL), F32)),
        grid=(T // tm,), in_specs=[row, vec, row], out_specs=(row, vec, vec),
        name="loss_head", compiler_params=_cp(("arbitrary",)),
    )(x, g, target)


def _block_rows(rows, cols, itemsize, cap_bytes=2 << 20):
    t = rows
    while t * cols * itemsize > cap_bytes and t % 2 == 0 and (t // 2) % 16 == 0:
        t //= 2
    return t


def _cast_into_slot(w3d, layer, me):
    _, R, C = w3d.shape
    tr = _block_rows(R, C, 4)

    def body(me_ref, w_ref, o_ref):
        o_ref[...] = w_ref[...].astype(o_ref.dtype)

    grid_spec = pltpu.PrefetchScalarGridSpec(
        num_scalar_prefetch=1, grid=(R // tr,),
        in_specs=[pl.BlockSpec((None, tr, C), lambda i, me: (layer, i, 0))],
        out_specs=pl.BlockSpec((None, tr, C), lambda i, me: (me[0], i, 0)),
    )
    return _call(body, out_shape=_sds((N_CHIPS, R, C), _COMM), grid_spec=grid_spec,
                 name="cast_into_slot", compiler_params=_cp(("parallel",)))(me, w3d)


def _adamw(w, g, m, v):
    R, C = w.shape
    tr = _block_rows(R, C, 4, 1 << 20)

    def body(w_ref, g_ref, m_ref, v_ref, d_ref, nm_ref, nv_ref):
        g = g_ref[...]
        m = ADAM_B1 * m_ref[...] + (1.0 - ADAM_B1) * g
        v = ADAM_B2 * v_ref[...] + (1.0 - ADAM_B2) * (g * g)
        m_hat = m / (1.0 - ADAM_B1 ** ADAM_STEP)
        v_hat = v / (1.0 - ADAM_B2 ** ADAM_STEP)
        d_ref[...] = -ADAM_LR * (m_hat / (jnp.sqrt(v_hat) + ADAM_EPS) + ADAM_WD * w_ref[...])
        nm_ref[...] = m
        nv_ref[...] = v

    spec = pl.BlockSpec((tr, C), lambda i: (i, 0))
    out = _sds((R, C), F32)
    return _call(body, out_shape=(out, out, out), grid=(R // tr,), in_specs=[spec] * 4, out_specs=(spec,) * 3,
                 name="adamw", compiler_params=_cp(("parallel",)))(w, g, m, v)


def _add_sibling_half(g_full, land, my_c):
    _, R, C = g_full.shape
    hr = R // 2
    tr = _block_rows(hr, C, 2, 1 << 20)
    nb = hr // tr

    def body(c_ref, g_ref, l_ref, o_ref):
        o_ref[...] = (g_ref[...].astype(F32) + l_ref[...].astype(F32)).astype(o_ref.dtype)

    grid_spec = pltpu.PrefetchScalarGridSpec(
        num_scalar_prefetch=1, grid=(N_CHIPS, nb),
        in_specs=[
            pl.BlockSpec((None, tr, C), lambda k, i, c: (k, c[0] * nb + i, 0)),
            pl.BlockSpec((None, tr, C), lambda k, i, c: (k, i, 0)),
        ],
        out_specs=pl.BlockSpec((None, tr, C), lambda k, i, c: (k, i, 0)),
    )
    return _call(body, out_shape=_sds((N_CHIPS, hr, C), _COMM), grid_spec=grid_spec,
                 name="add_sibling_half", compiler_params=_cp(("parallel", "parallel")))(my_c, g_full, land)


def _add_chips(land, sums, g_all, layer, my_chip, my_core):
    _, hr, C = land.shape
    tr = _block_rows(hr, C, 4, 1 << 20)
    nb = hr // tr

    def body(me_ref, core_ref, own_ref, a_ref, b_ref, c_ref, *rest):
        o_ref = rest[-1]
        o_ref[...] = ((own_ref[...].astype(F32) + a_ref[...].astype(F32)) + b_ref[...].astype(F32)) + c_ref[...].astype(F32)

    def other(d):
        return pl.BlockSpec((None, tr, C), lambda i, me, core: ((me[0] + d) % N_CHIPS, i, 0))

    operands = [my_chip, my_core, sums, land, land, land]
    in_specs = [other(0), other(1), other(2), other(3)]
    aliases = {}
    if g_all is not None:
        operands.append(g_all)
        in_specs.append(ANY)
        aliases = {len(operands) - 1: 0}
    grid_spec = pltpu.PrefetchScalarGridSpec(
        num_scalar_prefetch=2, grid=(nb,), in_specs=in_specs,
        out_specs=pl.BlockSpec((None, tr, C), lambda i, me, core: (layer, core[0] * nb + i, 0)),
    )
    return _call(body, out_shape=_sds((DEPTH, 2 * hr, C), F32), grid_spec=grid_spec, input_output_aliases=aliases,
                 name="add_chips", compiler_params=_cp(("parallel",)))(*operands)


def _place():
    x, y, c = lax.axis_index("x"), lax.axis_index("y"), lax.axis_index("c")
    chips = [(1 - x, y), (x, 1 - y), (1 - x, 1 - y)]
    return x, y, c, 2 * x + y, chips


def _remote(src, dst, send_sem, recv_sem, dev):
    return pltpu.make_async_remote_copy(src_ref=src, dst_ref=dst, send_sem=send_sem, recv_sem=recv_sem,
                                        device_id=dev, device_id_type=MESH)


def _gather_sems(n):
    return [pltpu.SemaphoreType.DMA((n, 6)), pltpu.SemaphoreType.DMA((n, 6))]


def _gather_start(bufs, send_sems, recv_sems):
    x, y, c, me, chips = _place()
    for w, buf in enumerate(bufs):
        hr = buf.shape[1] // 2
        own = buf.at[me, pl.ds(c * hr, hr)]
        for j, chip in enumerate(chips):
            _remote(own, own, send_sems.at[w, j], recv_sems.at[w, j], (*chip, c)).start()


def _gather_finish(bufs, send_sems, recv_sems):
    x, y, c, me, chips = _place()
    sibling = (x, y, 1 - c)
    waits = []
    for w, buf in enumerate(bufs):
        hr = buf.shape[1] // 2
        mine = pl.ds(c * hr, hr)
        for j, chip in enumerate(chips):
            kc = 2 * chip[0] + chip[1]
            got = buf.at[kc, mine]
            first = _remote(got, got, send_sems.at[w, j], recv_sems.at[w, j], (*chip, c))
            first.wait_recv()
            fwd = _remote(got, got, send_sems.at[w, 3 + j], recv_sems.at[w, 3 + j], sibling)
            fwd.start()
            waits += [first.wait_send, fwd.wait_send]
    for w, buf in enumerate(bufs):
        hr = buf.shape[1] // 2
        theirs = pl.ds((1 - c) * hr, hr)
        for j, chip in enumerate(chips):
            kc = 2 * chip[0] + chip[1]
            got = buf.at[kc, theirs]
            _remote(got, got, send_sems.at[w, 3 + j], recv_sems.at[w, 3 + j], sibling).wait_recv()
    for wait in waits:
        wait()


def _allgather_weights(bufs):
    n = len(bufs)

    def body(*refs):
        outs = refs[n:2 * n]
        send_sems, recv_sems = refs[2 * n:]
        _gather_start(outs, send_sems, recv_sems)
        _gather_finish(outs, send_sems, recv_sems)

    out_shape = tuple(_sds(b.shape, b.dtype) for b in bufs)
    return _call(
        body, out_shape=out_shape, in_specs=[ANY] * n, out_specs=(ANY,) * n,
        input_output_aliases={i: i for i in range(n)},
        scratch_shapes=_gather_sems(n),
        name="allgather_weights",
    )(*bufs)


def _swap_sems(n):
    return [pltpu.SemaphoreType.DMA((n,)), pltpu.SemaphoreType.DMA((n,))]


def _swap_shapes(grads):
    return tuple(_sds((N_CHIPS, g.shape[1] // 2, g.shape[2]), g.dtype) for g in grads)


def _swap_copies(grads, lands, send_sems, recv_sems):
    x, y, c, _, _ = _place()
    copies = []
    for w in range(len(grads)):
        hr = grads[w].shape[1] // 2
        copies.append(_remote(grads[w].at[:, pl.ds((1 - c) * hr, hr)], lands[w], send_sems.at[w], recv_sems.at[w], (x, y, 1 - c)))
    return copies


def _swap_sibling_halves(grads):
    n = len(grads)

    def body(*refs):
        copies = _swap_copies(refs[:n], refs[n:2 * n], *refs[2 * n:])
        for cp in copies:
            cp.start()
        for cp in copies:
            cp.wait()

    return _call(
        body, out_shape=_swap_shapes(grads), in_specs=[ANY] * n, out_specs=(ANY,) * n,
        scratch_shapes=_swap_sems(n), name="swap_sibling_halves",
    )(*grads)


def _scatter_sems(n):
    return [pltpu.SemaphoreType.DMA((n, 3)), pltpu.SemaphoreType.DMA((n, 3))]


def _scatter_shapes(sums):
    return tuple(_sds(s.shape, s.dtype) for s in sums)


def _scatter_start(sums, parts, send_sems, recv_sems):
    x, y, c, me, chips = _place()
    for w in range(len(sums)):
        for j, chip in enumerate(chips):
            kc = 2 * chip[0] + chip[1]
            _remote(sums[w].at[kc], parts[w].at[me], send_sems.at[w, j], recv_sems.at[w, j], (*chip, c)).start()


def _scatter_finish(sums, parts, send_sems, recv_sems):
    x, y, c, me, chips = _place()
    for w in range(len(sums)):
        for j, chip in enumerate(chips):
            kc = 2 * chip[0] + chip[1]
            _remote(sums[w].at[kc], parts[w].at[kc], send_sems.at[w, j], recv_sems.at[w, j], (*chip, c)).wait()


def _join_copies(g_alls, layer, send_sems, recv_sems):
    x, y, c, _, _ = _place()
    copies = []
    for w in range(len(g_alls)):
        hr = g_alls[w].shape[1] // 2
        mine = g_alls[w].at[layer, pl.ds(c * hr, hr)]
        copies.append(_remote(mine, mine, send_sems.at[w], recv_sems.at[w], (x, y, 1 - c)))
    return copies


def _join_sibling_halves(g_alls, layer):
    n = len(g_alls)

    def body(*refs):
        copies = _join_copies(refs[n:2 * n], layer, *refs[2 * n:])
        for cp in copies:
            cp.start()
        for cp in copies:
            cp.wait()

    out_shape = tuple(_sds(g.shape, g.dtype) for g in g_alls)
    return _call(
        body, out_shape=out_shape, in_specs=[ANY] * n, out_specs=(ANY,) * n,
        input_output_aliases={i: i for i in range(n)},
        scratch_shapes=[pltpu.SemaphoreType.DMA((n,)), pltpu.SemaphoreType.DMA((n,))],
        name="join_sibling_halves",
    )(*g_alls)


def _chip_sums(grads, my_core, land=None):
    if land is None:
        land = _swap_sibling_halves(grads)
    return [_add_sibling_half(g, l, my_core) for g, l in zip(grads, land)]


def _finish_reduce_scatter(parts, sums, g_alls, layer, my_chip, my_core, join=True):
    g_alls = [_add_chips(p, s, ga, layer, my_chip, my_core) for p, s, ga in zip(parts, sums, g_alls)]
    return _join_sibling_halves(g_alls, layer) if join else g_alls


def _allreduce_small(v):
    R = v.shape[0]

    def body(v_ref, out_ref, sib_ref, chip_ref, sum_ref, send_sems, recv_sems):
        x, y, c, me, chips = _place()
        swap = _remote(v_ref, sib_ref, send_sems.at[0], recv_sems.at[0], (x, y, 1 - c))
        swap.start()
        swap.wait()
        sum_ref[...] = v_ref[...] + sib_ref[...]
        copies = []
        for j, chip in enumerate(chips):
            cp = _remote(sum_ref, chip_ref.at[me], send_sems.at[1 + j], recv_sems.at[1 + j], (*chip, c))
            cp.start()
            copies.append(cp)
        chip_ref[me] = sum_ref[...]
        for cp in copies:
            cp.wait()
        out_ref[...] = ((chip_ref[0] + chip_ref[1]) + chip_ref[2]) + chip_ref[3]

    vm = pl.BlockSpec(memory_space=pltpu.VMEM)
    return _call(
        body, out_shape=_sds((R, 128), F32), in_specs=[vm], out_specs=vm,
        scratch_shapes=[
            pltpu.VMEM((R, 128), F32), pltpu.VMEM((N_CHIPS, R, 128), F32), pltpu.VMEM((R, 128), F32),
            pltpu.SemaphoreType.DMA((4,)), pltpu.SemaphoreType.DMA((4,)),
        ],
        name="allreduce_small", compiler_params=_cp(),
    )(v)


def _pack_small(parts):
    rows = []
    for p in parts:
        flat = p.reshape(-1).astype(F32)
        pad = (-flat.shape[0]) % 128
        rows.append(jnp.pad(flat, (0, pad)).reshape(-1, 128))
    packed = jnp.concatenate(rows, axis=0)
    pad_rows = (-packed.shape[0]) % 8
    return jnp.pad(packed, ((0, pad_rows), (0, 0)))


def _unpack_small(packed, like):
    out, r = [], 0
    for p in like:
        n = p.size
        nr = -(-n // 128)
        out.append(packed[r:r + nr].reshape(-1)[:n].reshape(p.shape))
        r += nr
    return out


BIG = ("w_in", "w_ret_o", "w_pool_o", "w_mem_kv", "w_mem_o", "w_out", "w_ff1", "w_ff2")


def _rope_tables(S):
    inv = ROPE_BASE ** (-jnp.arange(0, QK_DIM, 2, dtype=F32) / QK_DIM)
    ang = jnp.arange(S).astype(F32)[:, None] * inv[None, :]
    cos, sin = jnp.cos(ang), jnp.sin(ang)
    return jnp.concatenate([cos, cos], axis=1), jnp.concatenate([-sin, sin], axis=1)


MIXER_W = ("w_ret_o", "w_pool_o", "w_mem_kv", "w_mem_o", "w_out")
AFTER_IN = MIXER_W + ("w_ff1", "w_ff2")
GATHER_PLAN = {
    ("in_proj", 0): [(0, AFTER_IN)],
    ("retention", 0): [(1, ("w_in",))],
    ("ff1", 0): [(1, MIXER_W)],
    ("ff2", 0): [(1, ("w_ff1",))],
    ("in_proj", 1): [(1, ("w_ff2",)), (2, ("w_in",))],
    ("retention", 1): [(2, MIXER_W + ("w_ff1",))],
    ("ff1", 1): [(2, ("w_ff2",))],
    ("ff2", 1): [(3, ("w_in",))],
    ("in_proj", 2): [(3, AFTER_IN)],
}


def _layer_fwd(x, layer, wbuf, small, memn, cos, sin, B, S):
    g1, g2, lg, w_grp, scale = small

    def hosted(host):
        keys = [(l2, n) for l2, names in GATHER_PLAN.get((host, layer), ()) for n in names]
        return keys, [wbuf[k] for k in keys]

    W = lambda n: wbuf[(layer, n)]
    keys, bufs = hosted("in_proj")
    proj, h1, bufs = _rms_proj("in_proj", x, g1, W("w_in"), N_PROJ, gather=bufs)
    wbuf.update(zip(keys, bufs))
    keys, bufs = hosted("retention")
    o, y_r, bufs = _retention_fwd(proj, cos, sin, lg, B, S, gather=bufs)
    wbuf.update(zip(keys, bufs))
    y_p = _pool_fwd(proj, w_grp, scale, B, S)
    kv = _rowsharded_mm("mem_kv", memn, W("w_mem_kv"), None)
    o_m = _mem_attn_fwd(proj, kv, B, S)
    x1, merged, o3 = _merge_fwd(x, proj, y_r, y_p, o_m, W("w_ret_o"), W("w_pool_o"), W("w_mem_o"), W("w_out"))
    keys, bufs = hosted("ff1")
    u, h2, bufs = _rms_proj("ff1", x1, g2, W("w_ff1"), FFN_HIDDEN, gather=bufs)
    wbuf.update(zip(keys, bufs))
    keys, bufs = hosted("ff2")
    x2, bufs = _ffn_out(x1, u, W("w_ff2"), gather=bufs)
    wbuf.update(zip(keys, bufs))
    saved = dict(x=x, proj=proj, h1=h1, o=o, y_r=y_r, y_p=y_p, kv=kv, o_m=o_m, merged=merged, o3=o3, x1=x1, h2=h2, u=u)
    return x2, saved


EARLY_GRADS = ("w_ff1", "w_ff2", "w_out", "w_ret_o", "w_pool_o", "w_mem_o")
LATE_GRADS = ("w_in", "w_mem_kv")


def _layer_bwd(dx2, sv, W, small, memn, d_memn, cos, sin, B, S, big_grads, layer, my_chip, my_core):
    g1, g2, lg, w_grp, scale = small
    x, proj, x1, u = sv["x"], sv["proj"], sv["x1"], sv["u"]
    grads = {}
    du = _ffn_out_dgrad(dx2, u, W["w_ff2"])
    grads["w_ff2"] = _rowsharded_wgrad("ff2_wgrad", u, dx2, FFN_HIDDEN, pro=_relu2)
    dx1, dg2, _ = _dgrad_rms("ff1_dgrad", du, W["w_ff1"], x1, g2, dx2)
    grads["w_ff1"] = _colsharded_wgrad_full("ff1_wgrad", sv["h2"], du, FFN_HIDDEN, 1024)
    do3, dproj, dy_r, dy_p, do_m = _merge_bwd(dx1, proj, sv["o3"], W["w_ret_o"], W["w_pool_o"], W["w_mem_o"], W["w_out"])
    grads["w_out"], grads["w_ret_o"], grads["w_pool_o"], grads["w_mem_o"] = _mixer_wgrads(
        sv["merged"], sv["y_r"], sv["y_p"], sv["o_m"], dx1, do3)
    early = [grads[n] for n in EARLY_GRADS]
    d_o, dproj, land = _gn_gate_bwd(dy_r, sv["o"], proj, dproj, swap=early)
    sums = _chip_sums(early, my_core, land)
    dproj, dlg, parts = _retention_bwd(proj, d_o, dproj, cos, sin, lg, B, S, scatter=sums)
    done = _finish_reduce_scatter(parts, sums, [big_grads[n] for n in EARLY_GRADS], layer, my_chip, my_core, join=False)
    dproj, dw_grp, dscale, done = _pool_bwd(proj, dy_p, dproj, w_grp, scale, B, S, join=done, layer=layer)
    big_grads.update(zip(EARLY_GRADS, done))
    dproj, dkv = _mem_attn_bwd(proj, sv["kv"], do_m, dproj, B, S)
    grads["w_mem_kv"] = _rowsharded_wgrad("mem_kv_wgrad", memn, dkv, D_MODEL)
    if d_memn is None:
        d_memn = _rowsharded_dgrad("mem_kv_dgrad", dkv, W["w_mem_kv"], out_dtype=F32)
    else:
        d_memn = _rowsharded_dgrad("mem_kv_dgrad_acc", dkv, W["w_mem_kv"], epi=_add_res, epi_in=d_memn, out_dtype=F32)
    grads["w_in"] = _colsharded_wgrad_full("in_wgrad", sv["h1"], dproj, N_PROJ, 1792)
    sums = _chip_sums([grads[n] for n in LATE_GRADS], my_core)
    dx, dg1, parts = _dgrad_rms("in_dgrad", dproj, W["w_in"], x, g1, dx1, scatter=sums)
    done = _finish_reduce_scatter(parts, sums, [big_grads[n] for n in LATE_GRADS], layer, my_chip, my_core)
    big_grads.update(zip(LATE_GRADS, done))
    small_grads = dict(g1=dg1, g2=dg2, lg=jnp.sum(dlg, axis=(0, 2)).reshape(N_HEADS, 2).T, w_grp=dw_grp, scale=dscale)
    return dx, small_grads, d_memn


def kernel(x, mem, w_in, ret_decay_logit, w_ret_o, w_pool_grp, pool_scale, w_pool_o, w_mem_kv, w_mem_o, w_out, w_ff1, w_ff2, norm1_g, norm2_g, mem_norm_g, final_norm_g, loss_target, m_w_in, m_ret_decay_logit, m_w_ret_o, m_w_pool_grp, m_pool_scale, m_w_pool_o, m_w_mem_kv, m_w_mem_o, m_w_out, m_w_ff1, m_w_ff2, m_norm1_g, m_norm2_g, m_mem_norm_g, m_final_norm_g, v_w_in, v_ret_decay_logit, v_w_ret_o, v_w_pool_grp, v_pool_scale, v_w_pool_o, v_w_mem_kv, v_w_mem_o, v_w_out, v_w_ff1, v_w_ff2, v_norm1_g, v_norm2_g, v_mem_norm_g, v_final_norm_g):
    B, S, _ = x.shape
    T = B * S
    big_w = dict(w_in=w_in, w_ret_o=w_ret_o, w_pool_o=w_pool_o, w_mem_kv=w_mem_kv, w_mem_o=w_mem_o, w_out=w_out, w_ff1=w_ff1, w_ff2=w_ff2)
    big_m = dict(w_in=m_w_in, w_ret_o=m_w_ret_o, w_pool_o=m_w_pool_o, w_mem_kv=m_w_mem_kv, w_mem_o=m_w_mem_o, w_out=m_w_out, w_ff1=m_w_ff1, w_ff2=m_w_ff2)
    big_v = dict(w_in=v_w_in, w_ret_o=v_w_ret_o, w_pool_o=v_w_pool_o, w_mem_kv=v_w_mem_kv, w_mem_o=v_w_mem_o, w_out=v_w_out, w_ff1=v_w_ff1, w_ff2=v_w_ff2)
    small_w = [ret_decay_logit, w_pool_grp, pool_scale, norm1_g, norm2_g, mem_norm_g, final_norm_g]
    small_m = [m_ret_decay_logit, m_w_pool_grp, m_pool_scale, m_norm1_g, m_norm2_g, m_mem_norm_g, m_final_norm_g]
    small_v = [v_ret_decay_logit, v_w_pool_grp, v_pool_scale, v_norm1_g, v_norm2_g, v_mem_norm_g, v_final_norm_g]
    my_chip = (2 * lax.axis_index("x") + lax.axis_index("y")).astype(jnp.int32).reshape(1)
    my_core = lax.axis_index("c").astype(jnp.int32).reshape(1)

    wbuf = {(l, n): _cast_into_slot(big_w[n], l, my_chip) for l in range(DEPTH) for n in BIG}
    wbuf[(0, "w_in")] = _allgather_weights([wbuf[(0, "w_in")]])[0]

    cos, sin = _rope_tables(S)
    log_g = jax.nn.log_sigmoid(ret_decay_logit.astype(F32))
    mem2d = mem.reshape(B * MEM_LEN, D_MODEL)
    memn = _mem_norm(mem2d, mem_norm_g.reshape(1, D_MODEL))
    smalls = [(norm1_g[l].reshape(1, D_MODEL), norm2_g[l].reshape(1, D_MODEL), log_g[l], w_pool_grp[l],
               pool_scale[l].reshape(1, POOL_W)) for l in range(DEPTH)]

    h = x.reshape(T, D_MODEL)
    saved = []
    for l in range(DEPTH):
        h, sv = _layer_fwd(h, l, wbuf, smalls[l], memn, cos, sin, B, S)
        saved.append(sv)
    weights = [{n: wbuf[(l, n)] for n in BIG} for l in range(DEPTH)]

    dh, sq, d_final_g = _loss_head(h, final_norm_g.reshape(1, D_MODEL), loss_target.reshape(T, D_MODEL))
    loss = lax.psum(0.5 * jnp.sum(sq) / D_MODEL, ("x", "y", "c"))

    big_grads = dict.fromkeys(BIG)
    small_grads = [None] * DEPTH
    d_memn = None
    for l in reversed(range(DEPTH)):
        dh, small_grads[l], d_memn = _layer_bwd(
            dh, saved[l], weights[l], smalls[l], memn, d_memn, cos, sin, B, S, big_grads, l, my_chip, my_core)
    d_mem_g = _mem_norm_wgrad(mem2d, d_memn)

    d_logit = jnp.stack([sg["lg"] for sg in small_grads]) * jax.nn.sigmoid(-ret_decay_logit.astype(F32))
    small_g_local = [
        d_logit, jnp.stack([sg["w_grp"] for sg in small_grads]),
        jnp.stack([sg["scale"].reshape(POOL_W) for sg in small_grads]),
        jnp.stack([sg["g1"].reshape(D_MODEL) for sg in small_grads]),
        jnp.stack([sg["g2"].reshape(D_MODEL) for sg in small_grads]),
        d_mem_g.reshape(D_MODEL), d_final_g.reshape(D_MODEL),
    ]
    small_g = _allreduce_small(_pack_small(small_g_local))
    s_delta, s_m, s_v = _adamw(_pack_small(small_w), small_g, _pack_small(small_m), _pack_small(small_v))
    small_g, s_delta, s_m, s_v = (_unpack_small(a, small_w) for a in (small_g, s_delta, s_m, s_v))

    big_out = {}
    for n in BIG:
        w = big_w[n]
        g = big_grads[n]
        flat = lambda a: a.reshape(-1, a.shape[-1])
        d, nm, nv = _adamw(flat(w), flat(g), flat(big_m[n]), flat(big_v[n]))
        big_out[n] = (g, d.reshape(w.shape), nm.reshape(w.shape), nv.reshape(w.shape))

    order = ["w_in", "ret_decay_logit", "w_ret_o", "w_pool_grp", "pool_scale", "w_pool_o", "w_mem_kv", "w_mem_o",
             "w_out", "w_ff1", "w_ff2", "norm1_g", "norm2_g", "mem_norm_g", "final_norm_g"]
    small_names = ["ret_decay_logit", "w_pool_grp", "pool_scale", "norm1_g", "norm2_g", "mem_norm_g", "final_norm_g"]
    outs = [[], [], [], []]
    for n in order:
        if n in big_out:
            vals = big_out[n]
        else:
            i = small_names.index(n)
            vals = (small_g[i], s_delta[i], s_m[i], s_v[i])
        for k in range(4):
            outs[k].append(vals[k])
    return (loss, dh.reshape(B, S, D_MODEL), *outs[0], *outs[1], *outs[2], *outs[3])
```

```python
import functools

import jax
import jax.numpy as jnp
from jax import lax
from jax.experimental import pallas as pl
from jax.experimental.pallas import tpu as pltpu

F32 = jnp.float32
_MXU = jnp.bfloat16
_ACT = jnp.bfloat16
_COMM = jnp.bfloat16

D_MODEL = 1024
N_PROJ = 7168
FFN_HIDDEN = 4096
MEM_LEN = 256
N_HEADS = 4
QK_DIM = 128
V_DIM = 256
POOL_W = 512
MEMQ_W = 512
DEPTH = 4
N_CHIPS = 4
EPS = 1e-6
ROPE_BASE = 10000.0

C_Q, C_K, C_V, C_G, C_P, C_QM, C_GATE = 0, 512, 1024, 2048, 3072, 3584, 4096

ADAM_LR = 0.001
ADAM_B1 = 0.9
ADAM_B2 = 0.999
ADAM_EPS = 1e-08
ADAM_WD = 0.01
ADAM_STEP = 10

VMEM_LIMIT_BYTES = 56 * 1024 * 1024
MESH = pl.DeviceIdType.MESH
ANY = pl.BlockSpec(memory_space=pl.ANY)

_DN = {
    "nn": (((1,), (0,)), ((), ())),
    "nt": (((1,), (1,)), ((), ())),
    "tn": (((0,), (0,)), ((), ())),
}


def _call(body, **kw):
    return pl.pallas_call(body, **kw)


def _cp(sem=None):
    return pltpu.CompilerParams(dimension_semantics=sem, vmem_limit_bytes=VMEM_LIMIT_BYTES)


def _dot(a, b, kind="nn"):
    return lax.dot_general(a.astype(_MXU), b.astype(_MXU), _DN[kind], preferred_element_type=F32)


def _sds(shape, dtype):
    return jax.ShapeDtypeStruct(shape, dtype)


def _rms_rows(x, g):
    r = lax.rsqrt(jnp.mean(x * x, axis=-1, keepdims=True) + EPS)
    return x * r * g


def _relu2(u):
    r = jnp.maximum(u.astype(F32), 0.0)
    return r * r


def _add_res(r, e):
    return r + e.astype(F32)


def _relu2_bwd(r, u):
    return r * (2.0 * jnp.maximum(u.astype(F32), 0.0))


def _mm(name, kind, a, b, a_spec, b_spec, out_shape, o_spec, grid, acc_shape, *,
        pro=None, pro_in=(), pro_specs=(), epi=None, epi_in=(), epi_specs=()):
    nk = grid[2]
    npro, nepi = len(pro_in), len(epi_in)

    def body(*refs):
        a_ref, b_ref = refs[0], refs[1]
        pro_refs = refs[2:2 + npro]
        epi_refs = refs[2 + npro:2 + npro + nepi]
        o_ref = refs[2 + npro + nepi]
        av = a_ref[...]
        if pro is not None:
            av = pro(av, *[r[...] for r in pro_refs])
        part = _dot(av, b_ref[...], kind)

        def finish(r):
            if epi is not None:
                r = epi(r, *[e[...] for e in epi_refs])
            o_ref[...] = r.astype(o_ref.dtype)

        if nk == 1:
            finish(part)
        else:
            acc = refs[-1]
            k = pl.program_id(2)

            @pl.when(k == 0)
            def _():
                acc[...] = part

            @pl.when(k > 0)
            def _():
                acc[...] += part

            @pl.when(k == nk - 1)
            def _():
                finish(acc[...])

    scratch = [] if nk == 1 else [pltpu.VMEM(acc_shape, F32)]
    return _call(
        body, out_shape=out_shape, grid=grid,
        in_specs=[a_spec, b_spec, *pro_specs, *epi_specs], out_specs=o_spec,
        scratch_shapes=scratch, name=name,
        compiler_params=_cp(("parallel", "parallel", "arbitrary")),
    )(a, b, *pro_in, *epi_in)


def _row_tile(n, cap):
    t = min(n, cap)
    assert n % t == 0, (n, t)
    return t


def _resident(w):
    nd = w.ndim
    return pl.BlockSpec(w.shape, lambda i: (0,) * nd, pipeline_mode=pl.Buffered(1))


def _rms_proj(name, x, g, w, n_cols, gather=()):
    T = x.shape[0]
    tm = _row_tile(T, 512)
    ns = n_cols // N_CHIPS
    steps = T // tm
    ng = len(gather)

    def body(x_ref, g_ref, w_ref, *rest):
        o_ref, h_ref = rest[ng], rest[ng + 1]
        bufs, sems = rest[ng + 2:2 * ng + 2], rest[2 * ng + 2:]
        if ng:
            @pl.when(pl.program_id(0) == 0)
            def _():
                _gather_start(bufs, *sems)

        h = _rms_rows(x_ref[...], g_ref[...]).astype(h_ref.dtype)
        h_ref[...] = h
        for k in range(N_CHIPS):
            o_ref[:, k * ns:(k + 1) * ns] = _dot(h, w_ref[k]).astype(o_ref.dtype)

        if ng:
            @pl.when(pl.program_id(0) == steps - 1)
            def _():
                _gather_finish(bufs, *sems)

    row = pl.BlockSpec((tm, D_MODEL), lambda i: (i, 0))
    out = _call(
        body,
        out_shape=(_sds((T, n_cols), _ACT), _sds((T, D_MODEL), _ACT), *[_sds(b.shape, b.dtype) for b in gather]),
        grid=(steps,),
        in_specs=[row, pl.BlockSpec((1, D_MODEL), lambda i: (0, 0)), _resident(w)] + [ANY] * ng,
        out_specs=(pl.BlockSpec((tm, n_cols), lambda i: (i, 0)), row) + (ANY,) * ng,
        input_output_aliases={3 + i: 2 + i for i in range(ng)},
        scratch_shapes=_gather_sems(ng) if ng else [],
        name=name, compiler_params=_cp(("arbitrary",)),
    )(x, g, w, *gather)
    return out[0], out[1], list(out[2:])


def _colsharded_wgrad_full(name, h, dy, n_cols, tn):
    T = h.shape[0]
    tt = _row_tile(T, 2048)
    per = (n_cols // N_CHIPS) // tn
    return _mm(
        name, "tn", h, dy,
        pl.BlockSpec((tt, D_MODEL), lambda i, j, k: (k, 0)),
        pl.BlockSpec((tt, tn), lambda i, j, k: (k, j)),
        _sds((N_CHIPS, D_MODEL, n_cols // N_CHIPS), _COMM),
        pl.BlockSpec((None, D_MODEL, tn), lambda i, j, k: (j // per, 0, j % per)),
        (1, n_cols // tn, T // tt), (D_MODEL, tn),
    )


def _dgrad_rms(name, dy, w, x, g, dres, scatter=()):
    T, kd = dy.shape
    tm = _row_tile(T, 512)
    ns = kd // N_CHIPS
    steps = T // tm
    nsc = len(scatter)

    def body(dy_ref, w_ref, x_ref, g_ref, dres_ref, *rest):
        sums = rest[:nsc]
        dx_ref, dg_ref = rest[nsc], rest[nsc + 1]
        parts, sems = rest[nsc + 2:2 * nsc + 2], rest[2 * nsc + 2:]
        i = pl.program_id(0)
        if nsc:
            @pl.when(i == 0)
            def _():
                _scatter_start(sums, parts, *sems)

        dh = _dot(dy_ref[:, 0:ns], w_ref[0], "nt")
        for k in range(1, N_CHIPS):
            dh += _dot(dy_ref[:, k * ns:(k + 1) * ns], w_ref[k], "nt")
        x = x_ref[...]
        r = lax.rsqrt(jnp.mean(x * x, axis=-1, keepdims=True) + EPS)
        xh = x * r
        dxh = dh * g_ref[...]
        dx_ref[...] = dres_ref[...] + r * (dxh - xh * jnp.mean(dxh * xh, axis=-1, keepdims=True))
        dgp = jnp.sum(dh * xh, axis=0, keepdims=True)

        @pl.when(i == 0)
        def _():
            dg_ref[...] = dgp

        @pl.when(i > 0)
        def _():
            dg_ref[...] += dgp

        if nsc:
            @pl.when(i == steps - 1)
            def _():
                _scatter_finish(sums, parts, *sems)

    row = pl.BlockSpec((tm, D_MODEL), lambda i: (i, 0))
    vec = pl.BlockSpec((1, D_MODEL), lambda i: (0, 0))
    out = _call(
        body,
        out_shape=(_sds((T, D_MODEL), F32), _sds((1, D_MODEL), F32), *_scatter_shapes(scatter)),
        grid=(steps,),
        in_specs=[pl.BlockSpec((tm, kd), lambda i: (i, 0)), _resident(w), row, vec, row] + [ANY] * nsc,
        out_specs=(row, vec) + (ANY,) * nsc,
        scratch_shapes=_scatter_sems(nsc) if nsc else [],
        name=name, compiler_params=_cp(("arbitrary",)),
    )(dy, w, x, g, dres, *scatter)
    return out[0], out[1], list(out[2:])


def _ffn_out(x1, u, w, gather=()):
    T = u.shape[0]
    tm = _row_tile(T, 512)
    ks = w.shape[1]
    steps = T // tm
    ng = len(gather)

    def body(x_ref, u_ref, w_ref, *rest):
        o_ref = rest[ng]
        bufs, sems = rest[ng + 1:2 * ng + 1], rest[2 * ng + 1:]
        if ng:
            @pl.when(pl.program_id(0) == 0)
            def _():
                _gather_start(bufs, *sems)

        acc = x_ref[...]
        for k in range(N_CHIPS):
            acc += _dot(_relu2(u_ref[:, k * ks:(k + 1) * ks]), w_ref[k])
        o_ref[...] = acc

        if ng:
            @pl.when(pl.program_id(0) == steps - 1)
            def _():
                _gather_finish(bufs, *sems)

    row = pl.BlockSpec((tm, D_MODEL), lambda i: (i, 0))
    out = _call(
        body, out_shape=(_sds((T, D_MODEL), F32), *[_sds(b.shape, b.dtype) for b in gather]), grid=(steps,),
        in_specs=[row, pl.BlockSpec((tm, FFN_HIDDEN), lambda i: (i, 0)), _resident(w)] + [ANY] * ng,
        out_specs=(row,) + (ANY,) * ng,
        input_output_aliases={3 + i: 1 + i for i in range(ng)},
        scratch_shapes=_gather_sems(ng) if ng else [],
        name="ff2", compiler_params=_cp(("arbitrary",)),
    )(x1, u, w, *gather)
    return out[0], list(out[1:])


def _ffn_out_dgrad(dx2, u, w):
    T = u.shape[0]
    tm = _row_tile(T, 512)
    ks = w.shape[1]

    def body(d_ref, u_ref, w_ref, o_ref):
        d = d_ref[...].astype(_MXU)
        for k in range(N_CHIPS):
            sl = slice(k * ks, (k + 1) * ks)
            o_ref[:, sl] = _relu2_bwd(_dot(d, w_ref[k], "nt"), u_ref[:, sl]).astype(o_ref.dtype)

    wide = pl.BlockSpec((tm, FFN_HIDDEN), lambda i: (i, 0))
    return _call(
        body, out_shape=_sds((T, FFN_HIDDEN), _ACT), grid=(T // tm,),
        in_specs=[pl.BlockSpec((tm, D_MODEL), lambda i: (i, 0)), wide, _resident(w)], out_specs=wide,
        name="ff2_dgrad", compiler_params=_cp(("parallel",)),
    )(dx2, u, w)


def _rowsharded_mm(name, a, w, res, *, pro=None):
    T, K = a.shape
    ks = K // N_CHIPS
    tm = _row_tile(T, 512)
    epi = dict(epi=_add_res, epi_in=(res,), epi_specs=(pl.BlockSpec((tm, D_MODEL), lambda i, j, k: (i, 0)),)) if res is not None else {}
    return _mm(
        name, "nn", a, w,
        pl.BlockSpec((tm, ks), lambda i, j, k: (i, k)),
        pl.BlockSpec((None, ks, D_MODEL), lambda i, j, k: (k, 0, 0)),
        _sds((T, D_MODEL), F32 if res is not None else _ACT),
        pl.BlockSpec((tm, D_MODEL), lambda i, j, k: (i, 0)),
        (T // tm, 1, N_CHIPS), (tm, D_MODEL), pro=pro, **epi,
    )


def _rowsharded_dgrad(name, dy, w, *, epi=None, epi_in=None, out_dtype=_ACT, res=None):
    T = dy.shape[0]
    ks = w.shape[1]
    tm = _row_tile(T, 512)
    kw = {}
    if epi is not None:
        kw = dict(epi=epi, epi_in=(epi_in,), epi_specs=(pl.BlockSpec((tm, ks), lambda i, j, k: (i, j)),))
    return _mm(
        name, "nt", dy, w,
        pl.BlockSpec((tm, D_MODEL), lambda i, j, k: (i, 0)),
        pl.BlockSpec((None, ks, D_MODEL), lambda i, j, k: (j, 0, 0)),
        _sds((T, ks * N_CHIPS), out_dtype),
        pl.BlockSpec((tm, ks), lambda i, j, k: (i, j)),
        (T // tm, N_CHIPS, 1), None, **kw,
    )


def _rowsharded_wgrad(name, a, dy, K, *, pro=None):
    T = a.shape[0]
    ks = K // N_CHIPS
    tt = _row_tile(T, 2048)
    return _mm(
        name, "tn", a, dy,
        pl.BlockSpec((tt, ks), lambda i, j, k: (k, i)),
        pl.BlockSpec((tt, D_MODEL), lambda i, j, k: (k, 0)),
        _sds((N_CHIPS, ks, D_MODEL), _COMM),
        pl.BlockSpec((None, ks, D_MODEL), lambda i, j, k: (i, 0, 0)),
        (N_CHIPS, 1, T // tt), (ks, D_MODEL), pro=pro,
    )


def _mixer_wgrads(merged, y_r, y_p, o_m, dx1, do3):
    T = merged.shape[0]
    tt = _row_tile(T, 512)
    nt = T // tt
    rs = D_MODEL // N_CHIPS

    def body(mg_ref, yr_ref, yp_ref, om_ref, dx_ref, do3_ref, g_out, g_ret, g_pool, g_mem, a_out, a_ret, a_pool, a_mem):
        t = pl.program_id(0)

        def accumulate(acc, part):
            @pl.when(t == 0)
            def _():
                acc[...] = part

            @pl.when(t > 0)
            def _():
                acc[...] += part

        accumulate(a_out, _dot(mg_ref[...], dx_ref[...], "tn"))
        accumulate(a_ret, _dot(yr_ref[...], do3_ref[:, 0:D_MODEL], "tn"))
        accumulate(a_pool, _dot(yp_ref[...], do3_ref[:, D_MODEL:2 * D_MODEL], "tn"))
        accumulate(a_mem, _dot(om_ref[...], do3_ref[:, 2 * D_MODEL:3 * D_MODEL], "tn"))

        @pl.when(t == nt - 1)
        def _():
            for k in range(N_CHIPS):
                rows = slice(k * rs, (k + 1) * rs)
                g_out[k] = a_out[rows, :].astype(g_out.dtype)
                g_ret[k] = a_ret[rows, :].astype(g_ret.dtype)
                g_pool[k] = a_pool[:, rows].astype(g_pool.dtype)
                g_mem[k] = a_mem[:, rows].astype(g_mem.dtype)

    row = lambda w: pl.BlockSpec((tt, w), lambda t: (t, 0))
    whole = lambda s: pl.BlockSpec(s, lambda t: (0, 0, 0))
    rsh, csh = (N_CHIPS, rs, D_MODEL), (N_CHIPS, POOL_W, rs)
    return _call(
        body,
        out_shape=(_sds(rsh, _COMM), _sds(rsh, _COMM), _sds(csh, _COMM), _sds(csh, _COMM)),
        grid=(nt,),
        in_specs=[row(D_MODEL), row(D_MODEL), row(POOL_W), row(MEMQ_W), row(D_MODEL), row(3 * D_MODEL)],
        out_specs=(whole(rsh), whole(rsh), whole(csh), whole(csh)),
        scratch_shapes=[pltpu.VMEM((D_MODEL, D_MODEL), F32), pltpu.VMEM((D_MODEL, D_MODEL), F32),
                        pltpu.VMEM((POOL_W, D_MODEL), F32), pltpu.VMEM((MEMQ_W, D_MODEL), F32)],
        name="mixer_wgrads", compiler_params=_cp(("arbitrary",)),
    )(merged, y_r, y_p, o_m, dx1, do3)


def _rot(x, cos, sin):
    return x * cos + pltpu.roll(x, QK_DIM // 2, 1) * sin


def _rot_bwd(d, cos, sin):
    return d * cos + pltpu.roll(d * sin, QK_DIM // 2, 1)


def _tile_diff(qi, ki, tq, tk):
    n = qi * tq + lax.broadcasted_iota(jnp.int32, (tq, tk), 0)
    m = ki * tk + lax.broadcasted_iota(jnp.int32, (tq, tk), 1)
    return (n - m).astype(F32)


def _decay(qi, ki, tq, tk, lgf, lgb):
    diff = _tile_diff(qi, ki, tq, tk)
    return diff, jnp.exp(jnp.where(diff >= 0.0, lgf * diff, -(lgb * diff)))


def _head(h, width):
    return slice(h * width, (h + 1) * width)


def _row_index(t):
    return lax.broadcasted_iota(jnp.int32, (t, QK_DIM), 0).astype(F32)


def _q_decay_fwd(t, lgf):
    return jnp.exp(lgf * _row_index(t))


def _q_decay_bwd(t, lgb):
    return jnp.exp(lgb * (float(t) - _row_index(t)))


def _k_decay_fwd(t, lgf, tiles_apart):
    return jnp.exp(lgf * ((tiles_apart * t).astype(F32) - _row_index(t)))


def _k_decay_bwd(t, lgb, tiles_apart):
    return jnp.exp(lgb * (((tiles_apart - 1) * t).astype(F32) + _row_index(t)))


def _group_norm_gate(o, g):
    mu = jnp.mean(o, axis=-1, keepdims=True)
    oc = o - mu
    var = jnp.mean(oc * oc, axis=-1, keepdims=True)
    on = oc * lax.rsqrt(var + EPS)
    return on * (g * jax.nn.sigmoid(g))


def _retention_fwd(proj, cos, sin, lg, B, S, gather=()):
    T = B * S
    t = _row_tile(S, 512)
    nc = S // t
    scale = QK_DIM ** -0.5
    ng = len(gather)
    qw, vw = N_HEADS * QK_DIM, N_HEADS * V_DIM

    def chunk_of(s):
        return jnp.where(s < nc, s, 2 * nc - 1 - s)

    def body(lg_ref, q_ref, k_ref, v_ref, g_ref, cos_ref, sin_ref, *rest):
        o_ref, y_ref = rest[ng], rest[ng + 1]
        bufs = rest[ng + 2:2 * ng + 2]
        o_acc, state_f, state_b = rest[2 * ng + 2:2 * ng + 5]
        sems = rest[2 * ng + 5:]
        b = pl.program_id(0)
        s = pl.program_id(1)
        rows = pl.ds(pl.multiple_of(chunk_of(s) * t, t), t)
        if ng:
            @pl.when(jnp.logical_and(b == 0, s == 0))
            def _():
                _gather_start(bufs, *sems)

        @pl.when(s == 0)
        def _():
            state_f[...] = jnp.zeros_like(state_f)
            state_b[...] = jnp.zeros_like(state_b)

        def rotated(h):
            sl = _head(h, QK_DIM)
            qr = _rot(q_ref[:, sl].astype(F32), cos_ref[...], sin_ref[...])
            kr = _rot(k_ref[:, sl].astype(F32), cos_ref[...], sin_ref[...]) * scale
            return qr, kr

        r = _row_index(t)

        @pl.when(s < nc)
        def _():
            for h in range(N_HEADS):
                sl, vsl = _head(h, QK_DIM), _head(h, V_DIM)
                lgf, lgb = lg_ref[0, h], lg_ref[1, h]
                qr, kr = rotated(h)
                v = v_ref[:, vsl]
                diff = _tile_diff(0, 0, t, t)
                dec = jnp.exp(jnp.where(diff >= 0.0, lgf * diff, -(lgb * diff)))
                o = _dot(_dot(qr, kr, "nt") * dec, v)
                o += _dot(qr * jnp.exp(lgf * (r + 1.0)), state_f[sl, :])
                o_acc[rows, vsl] = o
                state_f[sl, :] = state_f[sl, :] * jnp.exp(lgf * float(t)) + _dot(kr * jnp.exp(lgf * (float(t - 1) - r)), v, "tn")

        @pl.when(s >= nc)
        def _():
            for h in range(N_HEADS):
                sl, vsl = _head(h, QK_DIM), _head(h, V_DIM)
                lgb = lg_ref[1, h]
                qr, kr = rotated(h)
                o = o_acc[rows, vsl] + _dot(qr * jnp.exp(lgb * (float(t) - r)), state_b[sl, :])
                o_ref[:, vsl] = o
                y_ref[:, vsl] = _group_norm_gate(o, g_ref[:, vsl].astype(F32)).astype(y_ref.dtype)
                state_b[sl, :] = state_b[sl, :] * jnp.exp(lgb * float(t)) + _dot(kr * jnp.exp(lgb * r), v_ref[:, vsl], "tn")

        if ng:
            @pl.when(jnp.logical_and(b == B - 1, s == 2 * nc - 1))
            def _():
                _gather_finish(bufs, *sems)

    def in_rows(width, col):
        return pl.BlockSpec((t, width), lambda b, s: (b * nc + chunk_of(s), col))

    out_rows = pl.BlockSpec((t, vw), lambda b, s: (b * nc + jnp.where(s < nc, nc - 1, 2 * nc - 1 - s), 0))
    table = pl.BlockSpec((t, QK_DIM), lambda b, s: (chunk_of(s), 0))
    out = _call(
        body,
        out_shape=(_sds((T, vw), F32), _sds((T, vw), _ACT), *[_sds(g.shape, g.dtype) for g in gather]),
        grid=(B, 2 * nc),
        in_specs=[
            pl.BlockSpec(memory_space=pltpu.SMEM),
            in_rows(qw, C_Q // qw), in_rows(qw, C_K // qw), in_rows(vw, C_V // vw), in_rows(vw, C_G // vw),
            table, table,
        ] + [ANY] * ng,
        out_specs=(out_rows, out_rows) + (ANY,) * ng,
        input_output_aliases={7 + i: 2 + i for i in range(ng)},
        scratch_shapes=[pltpu.VMEM((S, vw), F32), pltpu.VMEM((qw, V_DIM), F32), pltpu.VMEM((qw, V_DIM), F32)]
        + (_gather_sems(ng) if ng else []),
        name="retention_fwd",
        compiler_params=_cp(("arbitrary", "arbitrary")),
    )(lg, proj, proj, proj, proj, cos, sin, *gather)
    return out[0], out[1], list(out[2:])


def _gn_gate_bwd(dy, o, proj, dproj, swap=()):
    T = dy.shape[0]
    tm = _row_tile(T, 512)
    steps = T // tm
    nw = len(swap)

    def body(dy_ref, o_ref, g_ref, dproj_in, *rest):
        grads = rest[:nw]
        do_ref, dg_ref = rest[nw], rest[nw + 1]
        lands, sems = rest[nw + 2:2 * nw + 2], rest[2 * nw + 2:]
        if nw:
            @pl.when(pl.program_id(0) == 0)
            def _():
                for cp in _swap_copies(grads, lands, *sems):
                    cp.start()

            @pl.when(pl.program_id(0) == steps - 1)
            def _():
                for cp in _swap_copies(grads, lands, *sems):
                    cp.wait()

        for h in range(N_HEADS):
            sl = slice(h * V_DIM, (h + 1) * V_DIM)
            o = o_ref[:, sl]
            g = g_ref[:, sl].astype(F32)
            d = dy_ref[:, sl].astype(F32)
            mu = jnp.mean(o, axis=-1, keepdims=True)
            oc = o - mu
            rstd = lax.rsqrt(jnp.mean(oc * oc, axis=-1, keepdims=True) + EPS)
            on = oc * rstd
            sg = jax.nn.sigmoid(g)
            don = d * (g * sg)
            dg_ref[:, sl] = (d * on * (sg * (1.0 + g * (1.0 - sg)))).astype(dg_ref.dtype)
            do = rstd * (don - jnp.mean(don, axis=-1, keepdims=True) - on * jnp.mean(don * on, axis=-1, keepdims=True))
            do_ref[:, sl] = do.astype(do_ref.dtype)

    wide = N_HEADS * V_DIM
    out = _call(
        body,
        out_shape=(_sds((T, wide), _ACT), _sds(dproj.shape, dproj.dtype), *_swap_shapes(swap)),
        grid=(steps,),
        in_specs=[
            pl.BlockSpec((tm, wide), lambda i: (i, 0)),
            pl.BlockSpec((tm, wide), lambda i: (i, 0)),
            pl.BlockSpec((tm, wide), lambda i: (i, C_G // wide)),
            ANY,
        ] + [ANY] * nw,
        out_specs=(pl.BlockSpec((tm, wide), lambda i: (i, 0)), pl.BlockSpec((tm, wide), lambda i: (i, C_G // wide)))
        + (ANY,) * nw,
        input_output_aliases={3: 1},
        scratch_shapes=_swap_sems(nw) if nw else [],
        name="gn_gate_bwd", compiler_params=_cp(("arbitrary",)),
    )(dy, o, proj, dproj, *swap)
    return out[0], out[1], list(out[2:])


def _retention_bwd(proj, d_o, dproj, cos, sin, lg, B, S, scatter=()):
    T = B * S
    tq = tk = _row_tile(S, 512)
    nq, nk = S // tq, S // tk
    scale = QK_DIM ** -0.5
    qw, vw = N_HEADS * QK_DIM, N_HEADS * V_DIM

    ns = len(scatter)

    def body(lg_ref, q_ref, k_ref, v_ref, do_ref, cq_ref, sq_ref, ck_ref, sk_ref, dproj_in, *rest):
        sums = rest[:ns]
        dqkv_ref, dlg_ref = rest[ns], rest[ns + 1]
        parts = rest[ns + 2:2 * ns + 2]
        kr_scr, dq_acc, dk_acc, dv_acc, gl_acc = rest[2 * ns + 2:2 * ns + 7]
        sems = rest[2 * ns + 7:]
        b = pl.program_id(0)
        ki = pl.program_id(1)
        qi = pl.program_id(2)
        q_rows = pl.ds(pl.multiple_of(qi * tq, tq), tq)
        k_rows = pl.ds(pl.multiple_of(ki * tk, tk), tk)
        if ns:
            @pl.when(jnp.logical_and(b == 0, jnp.logical_and(qi == 0, ki == 0)))
            def _():
                _scatter_start(sums, parts, *sems)

        @pl.when(qi == 0)
        def _():
            for h in range(N_HEADS):
                sl = _head(h, QK_DIM)
                kr_scr[:, sl] = _rot(k_ref[:, sl].astype(F32), ck_ref[...], sk_ref[...]) * scale
            dk_acc[...] = jnp.zeros_like(dk_acc)
            dv_acc[...] = jnp.zeros_like(dv_acc)

        @pl.when(jnp.logical_and(qi == 0, ki == 0))
        def _():
            gl_acc[...] = jnp.zeros_like(gl_acc)

        @pl.when(ki == 0)
        def _():
            dq_acc[q_rows, :] = jnp.zeros((tq, qw), F32)

        def queries(h):
            return _rot(q_ref[:, _head(h, QK_DIM)].astype(F32), cq_ref[...], sq_ref[...])

        def one_sided(h, q_factor, k_factor, side, sign):
            sl, vsl = _head(h, QK_DIM), _head(h, V_DIM)
            qt = (queries(h) * q_factor).astype(_MXU)
            kt = (kr_scr[:, sl] * k_factor).astype(_MXU)
            d_out = do_ref[:, vsl]
            p = _dot(qt, kt, "nt")
            dv_acc[:, vsl] += _dot(p, d_out, "tn")
            dp = _dot(d_out, v_ref[:, vsl], "nt")
            dq_acc[q_rows, sl] += _dot(dp, kt) * q_factor
            dk_acc[:, sl] += _dot(dp, qt, "tn") * k_factor
            diff = _tile_diff(qi, ki, tq, tk)
            row = 2 * h + side
            gl_acc[row:row + 1, :] += sign * jnp.sum(dp * p * diff, axis=0, keepdims=True)

        @pl.when(ki < qi)
        def _():
            for h in range(N_HEADS):
                one_sided(h, _q_decay_fwd(tq, lg_ref[0, h]), _k_decay_fwd(tk, lg_ref[0, h], qi - ki), 0, 1.0)

        @pl.when(ki > qi)
        def _():
            for h in range(N_HEADS):
                one_sided(h, _q_decay_bwd(tq, lg_ref[1, h]), _k_decay_bwd(tk, lg_ref[1, h], ki - qi), 1, -1.0)

        @pl.when(ki == qi)
        def _():
            for h in range(N_HEADS):
                sl, vsl = _head(h, QK_DIM), _head(h, V_DIM)
                qr = queries(h).astype(_MXU)
                kr = kr_scr[:, sl].astype(_MXU)
                d_out = do_ref[:, vsl]
                s = _dot(qr, kr, "nt")
                diff, dec = _decay(qi, ki, tq, tk, lg_ref[0, h], lg_ref[1, h])
                p = s * dec
                dv_acc[:, vsl] += _dot(p, d_out, "tn")
                dp = _dot(d_out, v_ref[:, vsl], "nt")
                ds = dp * dec
                dq_acc[q_rows, sl] += _dot(ds, kr)
                dk_acc[:, sl] += _dot(ds, qr, "tn")
                gd = dp * p * diff
                gl_acc[2 * h:2 * h + 1, :] += jnp.sum(jnp.where(diff >= 0.0, gd, 0.0), axis=0, keepdims=True)
                gl_acc[2 * h + 1:2 * h + 2, :] += jnp.sum(jnp.where(diff < 0.0, -gd, 0.0), axis=0, keepdims=True)

        @pl.when(qi == nq - 1)
        def _():
            for h in range(N_HEADS):
                sl = _head(h, QK_DIM)
                dk = _rot_bwd(dk_acc[:, sl] * scale, ck_ref[...], sk_ref[...])
                dqkv_ref[k_rows, qw + h * QK_DIM:qw + (h + 1) * QK_DIM] = dk.astype(dqkv_ref.dtype)
            dqkv_ref[k_rows, 2 * qw:2 * qw + vw] = dv_acc[...].astype(dqkv_ref.dtype)

        @pl.when(ki == nk - 1)
        def _():
            for h in range(N_HEADS):
                sl = _head(h, QK_DIM)
                dqkv_ref[q_rows, sl] = _rot_bwd(dq_acc[q_rows, sl], cq_ref[...], sq_ref[...]).astype(dqkv_ref.dtype)

        @pl.when(jnp.logical_and(qi == nq - 1, ki == nk - 1))
        def _():
            dlg_ref[...] = gl_acc[...]

        if ns:
            @pl.when(jnp.logical_and(b == B - 1, jnp.logical_and(qi == nq - 1, ki == nk - 1)))
            def _():
                _scatter_finish(sums, parts, *sems)

    out = _call(
        body,
        out_shape=(_sds(dproj.shape, dproj.dtype), _sds((B, 2 * N_HEADS, tk), F32), *_scatter_shapes(scatter)),
        grid=(B, nk, nq),
        in_specs=[
            pl.BlockSpec(memory_space=pltpu.SMEM),
            pl.BlockSpec((tq, qw), lambda b, ki, qi: (b * nq + qi, C_Q // qw)),
            pl.BlockSpec((tk, qw), lambda b, ki, qi: (b * nk + ki, C_K // qw)),
            pl.BlockSpec((tk, vw), lambda b, ki, qi: (b * nk + ki, C_V // vw)),
            pl.BlockSpec((tq, vw), lambda b, ki, qi: (b * nq + qi, 0)),
            pl.BlockSpec((tq, QK_DIM), lambda b, ki, qi: (qi, 0)),
            pl.BlockSpec((tq, QK_DIM), lambda b, ki, qi: (qi, 0)),
            pl.BlockSpec((tk, QK_DIM), lambda b, ki, qi: (ki, 0)),
            pl.BlockSpec((tk, QK_DIM), lambda b, ki, qi: (ki, 0)),
            ANY,
        ] + [ANY] * ns,
        out_specs=(
            pl.BlockSpec((S, 2 * qw + vw), lambda b, ki, qi: (b, 0)),
            pl.BlockSpec((None, 2 * N_HEADS, tk), lambda b, ki, qi: (b, 0, 0)),
        ) + (ANY,) * ns,
        scratch_shapes=[
            pltpu.VMEM((tk, qw), F32), pltpu.VMEM((S, qw), F32),
            pltpu.VMEM((tk, qw), F32), pltpu.VMEM((tk, vw), F32), pltpu.VMEM((2 * N_HEADS, tk), F32),
        ] + (_scatter_sems(ns) if ns else []),
        input_output_aliases={9: 0},
        name="retention_bwd",
        compiler_params=_cp(("arbitrary", "arbitrary", "arbitrary")),
    )(lg, proj, proj, proj, d_o, cos, sin, cos, sin, dproj, *scatter)
    return out[0], out[1], list(out[2:])


POOL_PAD = 16


def _pad_rows(v):
    z = jnp.zeros((POOL_PAD, v.shape[1]), F32)
    return jnp.concatenate([z, v, z], axis=0)


def _window_sums(first, length, levels):
    s = first
    step = 1
    for _ in range(levels - 1):
        s = pltpu.roll(s, step, 0) + pltpu.roll(s, length - step, 0)
        step *= 2
    return s


def _pool_counts(S, hw):
    n = lax.broadcasted_iota(jnp.int32, (S, 1), 0)
    return (jnp.minimum(n + hw, S) - jnp.maximum(n - hw, 0)).astype(F32)


def _pool_mixed(pf, S, g):
    length = S + 2 * POOL_PAD
    xp = _pad_rows(pf)
    s = _window_sums(xp + pltpu.roll(xp, 1, 0), length, g + 1)[POOL_PAD:POOL_PAD + S]
    return s / _pool_counts(S, 1 << g) - pf


def _pool_mixed_bwd(dmixed, S, g):
    length = S + 2 * POOL_PAD
    ep = _pad_rows(dmixed / _pool_counts(S, 1 << g))
    t = _window_sums(ep + pltpu.roll(ep, length - 1, 0), length, g + 1)[POOL_PAD:POOL_PAD + S]
    return t - dmixed


def _pool_fwd(proj, w_grp, scale, B, S):
    T = B * S
    G = POOL_W // 4

    def body(p_ref, wg_ref, sc_ref, y_ref):
        for g in range(4):
            sl = slice(g * G, (g + 1) * G)
            mixed = _pool_mixed(p_ref[:, sl].astype(F32), S, g)
            y_ref[:, sl] = (_dot(mixed, wg_ref[g]) * sc_ref[:, sl]).astype(y_ref.dtype)

    return _call(
        body, out_shape=_sds((T, POOL_W), _ACT), grid=(B,),
        in_specs=[
            pl.BlockSpec((S, POOL_W), lambda b: (b, C_P // POOL_W)),
            pl.BlockSpec((4, G, G), lambda b: (0, 0, 0)),
            pl.BlockSpec((1, POOL_W), lambda b: (0, 0)),
        ],
        out_specs=pl.BlockSpec((S, POOL_W), lambda b: (b, 0)),
        name="pool_fwd", compiler_params=_cp(("parallel",)),
    )(proj, w_grp, scale)


def _pool_bwd(proj, dy, dproj, w_grp, scale, B, S, join=(), layer=0):
    G = POOL_W // 4
    nj = len(join)

    def body(p_ref, dy_ref, wg_ref, sc_ref, dproj_in, *rest):
        dp_ref, dwg_ref, dsc_ref = rest[nj:nj + 3]
        g_alls, sems = rest[nj + 3:2 * nj + 3], rest[2 * nj + 3:]
        b = pl.program_id(0)
        if nj:
            @pl.when(b == 0)
            def _():
                for cp in _join_copies(g_alls, layer, *sems):
                    cp.start()

            @pl.when(b == B - 1)
            def _():
                for cp in _join_copies(g_alls, layer, *sems):
                    cp.wait()

        for g in range(4):
            sl = slice(g * G, (g + 1) * G)
            mixed = _pool_mixed(p_ref[:, sl].astype(F32), S, g)
            z = _dot(mixed, wg_ref[g])
            d = dy_ref[:, sl].astype(F32)
            dsc = jnp.sum(d * z, axis=0, keepdims=True)
            dz = d * sc_ref[:, sl]
            dwg = _dot(mixed, dz, "tn")
            dmixed = _dot(dz, wg_ref[g], "nt")
            dp_ref[:, sl] = _pool_mixed_bwd(dmixed, S, g).astype(dp_ref.dtype)

            @pl.when(b == 0)
            def _():
                dwg_ref[g] = dwg
                dsc_ref[:, sl] = dsc

            @pl.when(b > 0)
            def _():
                dwg_ref[g] += dwg
                dsc_ref[:, sl] += dsc

    out = _call(
        body,
        out_shape=(_sds(dproj.shape, dproj.dtype), _sds((4, G, G), F32), _sds((1, POOL_W), F32),
                   *[_sds(g.shape, g.dtype) for g in join]),
        grid=(B,),
        in_specs=[
            pl.BlockSpec((S, POOL_W), lambda b: (b, C_P // POOL_W)),
            pl.BlockSpec((S, POOL_W), lambda b: (b, 0)),
            pl.BlockSpec((4, G, G), lambda b: (0, 0, 0)),
            pl.BlockSpec((1, POOL_W), lambda b: (0, 0)),
            ANY,
        ] + [ANY] * nj,
        out_specs=(
            pl.BlockSpec((S, POOL_W), lambda b: (b, C_P // POOL_W)),
            pl.BlockSpec((4, G, G), lambda b: (0, 0, 0)),
            pl.BlockSpec((1, POOL_W), lambda b: (0, 0)),
        ) + (ANY,) * nj,
        input_output_aliases={4: 0, **{5 + i: 3 + i for i in range(nj)}},
        scratch_shapes=[pltpu.SemaphoreType.DMA((nj,)), pltpu.SemaphoreType.DMA((nj,))] if nj else [],
        name="pool_bwd", compiler_params=_cp(("arbitrary",)),
    )(proj, dy, w_grp, scale, dproj, *join)
    return out[0], out[1], out[2], list(out[3:])


def _mem_softmax(q, k):
    s = _dot(q, k, "nt") * (QK_DIM ** -0.5)
    e = jnp.exp(s - jnp.max(s, axis=-1, keepdims=True))
    return e / jnp.sum(e, axis=-1, keepdims=True)


def _mem_attn_fwd(proj, kv, B, S):
    T = B * S
    tq = _row_tile(S, 512)
    nq = S // tq

    def body(q_ref, kv_ref, o_ref):
        for h in range(N_HEADS):
            sl = slice(h * QK_DIM, (h + 1) * QK_DIM)
            a = _mem_softmax(q_ref[:, sl], kv_ref[:, sl])
            o_ref[:, sl] = _dot(a, kv_ref[:, MEMQ_W + h * QK_DIM:MEMQ_W + (h + 1) * QK_DIM]).astype(o_ref.dtype)

    return _call(
        body, out_shape=_sds((T, MEMQ_W), _ACT), grid=(B, nq),
        in_specs=[
            pl.BlockSpec((tq, MEMQ_W), lambda b, i: (b * nq + i, C_QM // MEMQ_W)),
            pl.BlockSpec((MEM_LEN, 2 * MEMQ_W), lambda b, i: (b, 0)),
        ],
        out_specs=pl.BlockSpec((tq, MEMQ_W), lambda b, i: (b * nq + i, 0)),
        name="mem_attn_fwd", compiler_params=_cp(("parallel", "parallel")),
    )(proj, kv)


def _mem_attn_bwd(proj, kv, d_o, dproj, B, S):
    tq = _row_tile(S, 512)
    nq = S // tq
    scale = QK_DIM ** -0.5

    def body(q_ref, kv_ref, do_ref, dproj_in, dq_ref, dkv_ref):
        i = pl.program_id(1)
        for h in range(N_HEADS):
            sl = slice(h * QK_DIM, (h + 1) * QK_DIM)
            vsl = slice(MEMQ_W + h * QK_DIM, MEMQ_W + (h + 1) * QK_DIM)
            q = q_ref[:, sl]
            a = _mem_softmax(q, kv_ref[:, sl])
            d = do_ref[:, sl]
            da = _dot(d, kv_ref[:, vsl], "nt")
            ds = a * (da - jnp.sum(a * da, axis=-1, keepdims=True)) * scale
            dq_ref[:, sl] = _dot(ds, kv_ref[:, sl]).astype(dq_ref.dtype)
            dk = _dot(ds, q, "tn")
            dv = _dot(a, d, "tn")

            @pl.when(i == 0)
            def _():
                dkv_ref[:, sl] = dk
                dkv_ref[:, vsl] = dv

            @pl.when(i > 0)
            def _():
                dkv_ref[:, sl] += dk
                dkv_ref[:, vsl] += dv

    return _call(
        body,
        out_shape=(_sds(dproj.shape, dproj.dtype), _sds((B * MEM_LEN, 2 * MEMQ_W), F32)),
        grid=(B, nq),
        in_specs=[
            pl.BlockSpec((tq, MEMQ_W), lambda b, i: (b * nq + i, C_QM // MEMQ_W)),
            pl.BlockSpec((MEM_LEN, 2 * MEMQ_W), lambda b, i: (b, 0)),
            pl.BlockSpec((tq, MEMQ_W), lambda b, i: (b * nq + i, 0)),
            ANY,
        ],
        out_specs=(
            pl.BlockSpec((tq, MEMQ_W), lambda b, i: (b * nq + i, C_QM // MEMQ_W)),
            pl.BlockSpec((MEM_LEN, 2 * MEMQ_W), lambda b, i: (b, 0)),
        ),
        input_output_aliases={3: 0},
        name="mem_attn_bwd", compiler_params=_cp(("parallel", "arbitrary")),
    )(proj, kv, d_o, dproj)


def _mem_norm(mem2d, g):
    M = mem2d.shape[0]
    tm = _row_tile(M, 512)

    def body(x_ref, g_ref, o_ref):
        o_ref[...] = _rms_rows(x_ref[...], g_ref[...]).astype(o_ref.dtype)

    return _call(
        body, out_shape=_sds((M, D_MODEL), _ACT), grid=(M // tm,),
        in_specs=[pl.BlockSpec((tm, D_MODEL), lambda i: (i, 0)), pl.BlockSpec((1, D_MODEL), lambda i: (0, 0))],
        out_specs=pl.BlockSpec((tm, D_MODEL), lambda i: (i, 0)),
        name="mem_norm", compiler_params=_cp(("parallel",)),
    )(mem2d, g)


def _mem_norm_wgrad(mem2d, d_memn):
    M = mem2d.shape[0]
    tm = _row_tile(M, 512)

    def body(x_ref, d_ref, dg_ref):
        i = pl.program_id(0)
        x = x_ref[...]
        xh = x * lax.rsqrt(jnp.mean(x * x, axis=-1, keepdims=True) + EPS)
        dg = jnp.sum(d_ref[...] * xh, axis=0, keepdims=True)

        @pl.when(i == 0)
        def _():
            dg_ref[...] = dg

        @pl.when(i > 0)
        def _():
            dg_ref[...] += dg

    return _call(
        body, out_shape=_sds((1, D_MODEL), F32), grid=(M // tm,),
        in_specs=[pl.BlockSpec((tm, D_MODEL), lambda i: (i, 0)), pl.BlockSpec((tm, D_MODEL), lambda i: (i, 0))],
        out_specs=pl.BlockSpec((1, D_MODEL), lambda i: (0, 0)),
        name="mem_norm_wgrad", compiler_params=_cp(("arbitrary",)),
    )(mem2d, d_memn)


def _row_mm(a, w_ref):
    ks = w_ref.shape[1]
    out = _dot(a[:, 0:ks], w_ref[0])
    for k in range(1, N_CHIPS):
        out += _dot(a[:, k * ks:(k + 1) * ks], w_ref[k])
    return out


def _row_mm_t(d, w_ref):
    return jnp.concatenate([_dot(d, w_ref[k], "nt") for k in range(N_CHIPS)], axis=1)


def _col_mm(a, w_ref):
    return jnp.concatenate([_dot(a, w_ref[k]) for k in range(N_CHIPS)], axis=1)


def _col_mm_t(d, w_ref):
    ns = w_ref.shape[2]
    out = _dot(d[:, 0:ns], w_ref[0], "nt")
    for k in range(1, N_CHIPS):
        out += _dot(d[:, k * ns:(k + 1) * ns], w_ref[k], "nt")
    return out


def _merge_fwd(x, proj, y_r, y_p, o_m, w_ret_o, w_pool_o, w_mem_o, w_out):
    T = x.shape[0]
    tm = _row_tile(T, 512)

    def body(x_ref, gr_ref, gp_ref, gm_ref, yr_ref, yp_ref, om_ref, wr_ref, wp_ref, wm_ref, wo_ref,
             x1_ref, mg_ref, o3_ref):
        o_r = _row_mm(yr_ref[...], wr_ref)
        o_p = _col_mm(yp_ref[...], wp_ref)
        o_q = _col_mm(om_ref[...], wm_ref)
        merged = (jax.nn.sigmoid(gr_ref[...].astype(F32)) * o_r + jax.nn.sigmoid(gp_ref[...].astype(F32)) * o_p
                  + jax.nn.sigmoid(gm_ref[...].astype(F32)) * o_q)
        mg = merged.astype(mg_ref.dtype)
        mg_ref[...] = mg
        o3_ref[:, 0:D_MODEL] = o_r.astype(o3_ref.dtype)
        o3_ref[:, D_MODEL:2 * D_MODEL] = o_p.astype(o3_ref.dtype)
        o3_ref[:, 2 * D_MODEL:3 * D_MODEL] = o_q.astype(o3_ref.dtype)
        x1_ref[...] = x_ref[...] + _row_mm(mg, wo_ref)

    gb = C_GATE // D_MODEL
    row = lambda w: pl.BlockSpec((tm, w), lambda i: (i, 0))
    return _call(
        body,
        out_shape=(_sds((T, D_MODEL), F32), _sds((T, D_MODEL), _ACT), _sds((T, 3 * D_MODEL), _ACT)),
        grid=(T // tm,),
        in_specs=[
            row(D_MODEL),
            pl.BlockSpec((tm, D_MODEL), lambda i: (i, gb)),
            pl.BlockSpec((tm, D_MODEL), lambda i: (i, gb + 1)),
            pl.BlockSpec((tm, D_MODEL), lambda i: (i, gb + 2)),
            row(D_MODEL), row(POOL_W), row(MEMQ_W),
            _resident(w_ret_o), _resident(w_pool_o), _resident(w_mem_o), _resident(w_out),
        ],
        out_specs=(row(D_MODEL), row(D_MODEL), row(3 * D_MODEL)),
        name="merge_fwd", compiler_params=_cp(("parallel",)),
    )(x, proj, proj, proj, y_r, y_p, o_m, w_ret_o, w_pool_o, w_mem_o, w_out)


def _merge_bwd(dx1, proj, o3, w_ret_o, w_pool_o, w_mem_o, w_out):
    T = dx1.shape[0]
    tm = _row_tile(T, 512)

    def body(dx_ref, gr_ref, gp_ref, gm_ref, o3_ref, wr_ref, wp_ref, wm_ref, wo_ref,
             do3_ref, dgate_ref, dyr_ref, dyp_ref, dom_ref):
        dmerged = _row_mm_t(dx_ref[...], wo_ref)
        douts = []
        dgate_ref[:, 0:MEMQ_W] = jnp.zeros((tm, MEMQ_W), dgate_ref.dtype)
        for n, g_ref in enumerate((gr_ref, gp_ref, gm_ref)):
            sl = slice(n * D_MODEL, (n + 1) * D_MODEL)
            gate = jax.nn.sigmoid(g_ref[...].astype(F32))
            d_out = (dmerged * gate).astype(do3_ref.dtype)
            do3_ref[:, sl] = d_out
            dgate_ref[:, MEMQ_W + n * D_MODEL:MEMQ_W + (n + 1) * D_MODEL] = (
                dmerged * o3_ref[:, sl].astype(F32) * gate * (1.0 - gate)).astype(dgate_ref.dtype)
            douts.append(d_out)
        dyr_ref[...] = _row_mm_t(douts[0], wr_ref).astype(dyr_ref.dtype)
        dyp_ref[...] = _col_mm_t(douts[1], wp_ref).astype(dyp_ref.dtype)
        dom_ref[...] = _col_mm_t(douts[2], wm_ref).astype(dom_ref.dtype)

    gb = C_GATE // D_MODEL
    row = lambda w: pl.BlockSpec((tm, w), lambda i: (i, 0))
    return _call(
        body,
        out_shape=(
            _sds((T, 3 * D_MODEL), _ACT), _sds((T, N_PROJ), _ACT),
            _sds((T, D_MODEL), _ACT), _sds((T, POOL_W), _ACT), _sds((T, MEMQ_W), _ACT),
        ),
        grid=(T // tm,),
        in_specs=[
            row(D_MODEL),
            pl.BlockSpec((tm, D_MODEL), lambda i: (i, gb)),
            pl.BlockSpec((tm, D_MODEL), lambda i: (i, gb + 1)),
            pl.BlockSpec((tm, D_MODEL), lambda i: (i, gb + 2)),
            row(3 * D_MODEL),
            _resident(w_ret_o), _resident(w_pool_o), _resident(w_mem_o), _resident(w_out),
        ],
        out_specs=(row(3 * D_MODEL), pl.BlockSpec((tm, N_PROJ - C_QM), lambda i: (i, 1)),
                   row(D_MODEL), row(POOL_W), row(MEMQ_W)),
        name="merge_bwd", compiler_params=_cp(("parallel",)),
    )(dx1, proj, proj, proj, o3, w_ret_o, w_pool_o, w_mem_o, w_out)


def _loss_head(x, g, target):
    T = x.shape[0]
    tm = _row_tile(T, 512)

    def body(x_ref, g_ref, t_ref, dx_ref, sq_ref, dg_ref):
        i = pl.program_id(0)
        x = x_ref[...]
        gg = g_ref[...]
        r = lax.rsqrt(jnp.mean(x * x, axis=-1, keepdims=True) + EPS)
        xh = x * r
        err = xh * gg - t_ref[...]
        dy = err * (1.0 / D_MODEL)
        dxh = dy * gg
        dx_ref[...] = r * (dxh - xh * jnp.mean(dxh * xh, axis=-1, keepdims=True))
        sq = jnp.sum(err * err, axis=0, keepdims=True)
        dg = jnp.sum(dy * xh, axis=0, keepdims=True)

        @pl.when(i == 0)
        def _():
            sq_ref[...] = sq
            dg_ref[...] = dg

        @pl.when(i > 0)
        def _():
            sq_ref[...] += sq
            dg_ref[...] += dg

    vec = pl.BlockSpec((1, D_MODEL), lambda i: (0, 0))
    row = pl.BlockSpec((tm, D_MODEL), lambda i: (i, 0))
    return _call(
        body,
        out_shape=(_sds((T, D_MODEL), F32), _sds((1, D_MODEL), F32), _sds((1, D_MODEL), F32)),
        grid=(T // tm,), in_specs=[row, vec, row], out_specs=(row, vec, vec),
        name="loss_head", compiler_params=_cp(("arbitrary",)),
    )(x, g, target)


def _block_rows(rows, cols, itemsize, cap_bytes=2 << 20):
    t = rows
    while t * cols * itemsize > cap_bytes and t % 2 == 0 and (t // 2) % 16 == 0:
        t //= 2
    return t


def _cast_into_slot(w3d, layer, me):
    _, R, C = w3d.shape
    tr = _block_rows(R, C, 4)

    def body(me_ref, w_ref, o_ref):
        o_ref[...] = w_ref[...].astype(o_ref.dtype)

    grid_spec = pltpu.PrefetchScalarGridSpec(
        num_scalar_prefetch=1, grid=(R // tr,),
        in_specs=[pl.BlockSpec((None, tr, C), lambda i, me: (layer, i, 0))],
        out_specs=pl.BlockSpec((None, tr, C), lambda i, me: (me[0], i, 0)),
    )
    return _call(body, out_shape=_sds((N_CHIPS, R, C), _COMM), grid_spec=grid_spec,
                 name="cast_into_slot", compiler_params=_cp(("parallel",)))(me, w3d)


def _adamw(w, g, m, v):
    R, C = w.shape
    tr = _block_rows(R, C, 4, 1 << 20)

    def body(w_ref, g_ref, m_ref, v_ref, d_ref, nm_ref, nv_ref):
        g = g_ref[...]
        m = ADAM_B1 * m_ref[...] + (1.0 - ADAM_B1) * g
        v = ADAM_B2 * v_ref[...] + (1.0 - ADAM_B2) * (g * g)
        m_hat = m / (1.0 - ADAM_B1 ** ADAM_STEP)
        v_hat = v / (1.0 - ADAM_B2 ** ADAM_STEP)
        d_ref[...] = -ADAM_LR * (m_hat / (jnp.sqrt(v_hat) + ADAM_EPS) + ADAM_WD * w_ref[...])
        nm_ref[...] = m
        nv_ref[...] = v

    spec = pl.BlockSpec((tr, C), lambda i: (i, 0))
    out = _sds((R, C), F32)
    return _call(body, out_shape=(out, out, out), grid=(R // tr,), in_specs=[spec] * 4, out_specs=(spec,) * 3,
                 name="adamw", compiler_params=_cp(("parallel",)))(w, g, m, v)


def _add_sibling_half(g_full, land, my_c):
    _, R, C = g_full.shape
    hr = R // 2
    tr = _block_rows(hr, C, 2, 2 << 20)
    nb = hr // tr

    def body(c_ref, g_ref, l_ref, o_ref):
        o_ref[...] = (g_ref[...].astype(F32) + l_ref[...].astype(F32)).astype(o_ref.dtype)

    grid_spec = pltpu.PrefetchScalarGridSpec(
        num_scalar_prefetch=1, grid=(N_CHIPS, nb),
        in_specs=[
            pl.BlockSpec((None, tr, C), lambda k, i, c: (k, c[0] * nb + i, 0)),
            pl.BlockSpec((None, tr, C), lambda k, i, c: (k, i, 0)),
        ],
        out_specs=pl.BlockSpec((None, tr, C), lambda k, i, c: (k, i, 0)),
    )
    return _call(body, out_shape=_sds((N_CHIPS, hr, C), _COMM), grid_spec=grid_spec,
                 name="add_sibling_half", compiler_params=_cp(("parallel", "parallel")))(my_c, g_full, land)


def _add_chips(land, sums, g_all, layer, my_chip, my_core):
    _, hr, C = land.shape
    tr = _block_rows(hr, C, 4, 4 << 20)
    nb = hr // tr

    def body(me_ref, core_ref, own_ref, a_ref, b_ref, c_ref, *rest):
        o_ref = rest[-1]
        o_ref[...] = ((own_ref[...].astype(F32) + a_ref[...].astype(F32)) + b_ref[...].astype(F32)) + c_ref[...].astype(F32)

    def other(d):
        return pl.BlockSpec((None, tr, C), lambda i, me, core: ((me[0] + d) % N_CHIPS, i, 0))

    operands = [my_chip, my_core, sums, land, land, land]
    in_specs = [other(0), other(1), other(2), other(3)]
    aliases = {}
    if g_all is not None:
        operands.append(g_all)
        in_specs.append(ANY)
        aliases = {len(operands) - 1: 0}
    grid_spec = pltpu.PrefetchScalarGridSpec(
        num_scalar_prefetch=2, grid=(nb,), in_specs=in_specs,
        out_specs=pl.BlockSpec((None, tr, C), lambda i, me, core: (layer, core[0] * nb + i, 0)),
    )
    return _call(body, out_shape=_sds((DEPTH, 2 * hr, C), F32), grid_spec=grid_spec, input_output_aliases=aliases,
                 name="add_chips", compiler_params=_cp(("parallel",)))(*operands)


def _place():
    x, y, c = lax.axis_index("x"), lax.axis_index("y"), lax.axis_index("c")
    chips = [(1 - x, y), (x, 1 - y), (1 - x, 1 - y)]
    return x, y, c, 2 * x + y, chips


def _remote(src, dst, send_sem, recv_sem, dev):
    return pltpu.make_async_remote_copy(src_ref=src, dst_ref=dst, send_sem=send_sem, recv_sem=recv_sem,
                                        device_id=dev, device_id_type=MESH)


def _gather_sems(n):
    return [pltpu.SemaphoreType.DMA((n, 6)), pltpu.SemaphoreType.DMA((n, 6))]


def _gather_start(bufs, send_sems, recv_sems):
    x, y, c, me, chips = _place()
    for w, buf in enumerate(bufs):
        hr = buf.shape[1] // 2
        own = buf.at[me, pl.ds(c * hr, hr)]
        for j, chip in enumerate(chips):
            _remote(own, own, send_sems.at[w, j], recv_sems.at[w, j], (*chip, c)).start()


def _gather_finish(bufs, send_sems, recv_sems):
    x, y, c, me, chips = _place()
    sibling = (x, y, 1 - c)
    waits = []
    for w, buf in enumerate(bufs):
        hr = buf.shape[1] // 2
        mine = pl.ds(c * hr, hr)
        for j, chip in enumerate(chips):
            kc = 2 * chip[0] + chip[1]
            got = buf.at[kc, mine]
            first = _remote(got, got, send_sems.at[w, j], recv_sems.at[w, j], (*chip, c))
            first.wait_recv()
            fwd = _remote(got, got, send_sems.at[w, 3 + j], recv_sems.at[w, 3 + j], sibling)
            fwd.start()
            waits += [first.wait_send, fwd.wait_send]
    for w, buf in enumerate(bufs):
        hr = buf.shape[1] // 2
        theirs = pl.ds((1 - c) * hr, hr)
        for j, chip in enumerate(chips):
            kc = 2 * chip[0] + chip[1]
            got = buf.at[kc, theirs]
            _remote(got, got, send_sems.at[w, 3 + j], recv_sems.at[w, 3 + j], sibling).wait_recv()
    for wait in waits:
        wait()


def _allgather_weights(bufs):
    n = len(bufs)

    def body(*refs):
        outs = refs[n:2 * n]
        send_sems, recv_sems = refs[2 * n:]
        _gather_start(outs, send_sems, recv_sems)
        _gather_finish(outs, send_sems, recv_sems)

    out_shape = tuple(_sds(b.shape, b.dtype) for b in bufs)
    return _call(
        body, out_shape=out_shape, in_specs=[ANY] * n, out_specs=(ANY,) * n,
        input_output_aliases={i: i for i in range(n)},
        scratch_shapes=_gather_sems(n),
        name="allgather_weights",
    )(*bufs)


def _swap_sems(n):
    return [pltpu.SemaphoreType.DMA((n,)), pltpu.SemaphoreType.DMA((n,))]


def _swap_shapes(grads):
    return tuple(_sds((N_CHIPS, g.shape[1] // 2, g.shape[2]), g.dtype) for g in grads)


def _swap_copies(grads, lands, send_sems, recv_sems):
    x, y, c, _, _ = _place()
    copies = []
    for w in range(len(grads)):
        hr = grads[w].shape[1] // 2
        copies.append(_remote(grads[w].at[:, pl.ds((1 - c) * hr, hr)], lands[w], send_sems.at[w], recv_sems.at[w], (x, y, 1 - c)))
    return copies


def _swap_sibling_halves(grads):
    n = len(grads)

    def body(*refs):
        copies = _swap_copies(refs[:n], refs[n:2 * n], *refs[2 * n:])
        for cp in copies:
            cp.start()
        for cp in copies:
            cp.wait()

    return _call(
        body, out_shape=_swap_shapes(grads), in_specs=[ANY] * n, out_specs=(ANY,) * n,
        scratch_shapes=_swap_sems(n), name="swap_sibling_halves",
    )(*grads)


def _scatter_sems(n):
    return [pltpu.SemaphoreType.DMA((n, 3)), pltpu.SemaphoreType.DMA((n, 3))]


def _scatter_shapes(sums):
    return tuple(_sds(s.shape, s.dtype) for s in sums)


def _scatter_start(sums, parts, send_sems, recv_sems):
    x, y, c, me, chips = _place()
    for w in range(len(sums)):
        for j, chip in enumerate(chips):
            kc = 2 * chip[0] + chip[1]
            _remote(sums[w].at[kc], parts[w].at[me], send_sems.at[w, j], recv_sems.at[w, j], (*chip, c)).start()


def _scatter_finish(sums, parts, send_sems, recv_sems):
    x, y, c, me, chips = _place()
    for w in range(len(sums)):
        for j, chip in enumerate(chips):
            kc = 2 * chip[0] + chip[1]
            _remote(sums[w].at[kc], parts[w].at[kc], send_sems.at[w, j], recv_sems.at[w, j], (*chip, c)).wait()


def _join_copies(g_alls, layer, send_sems, recv_sems):
    x, y, c, _, _ = _place()
    copies = []
    for w in range(len(g_alls)):
        hr = g_alls[w].shape[1] // 2
        mine = g_alls[w].at[layer, pl.ds(c * hr, hr)]
        copies.append(_remote(mine, mine, send_sems.at[w], recv_sems.at[w], (x, y, 1 - c)))
    return copies


def _join_sibling_halves(g_alls, layer):
    n = len(g_alls)

    def body(*refs):
        copies = _join_copies(refs[n:2 * n], layer, *refs[2 * n:])
        for cp in copies:
            cp.start()
        for cp in copies:
            cp.wait()

    out_shape = tuple(_sds(g.shape, g.dtype) for g in g_alls)
    return _call(
        body, out_shape=out_shape, in_specs=[ANY] * n, out_specs=(ANY,) * n,
        input_output_aliases={i: i for i in range(n)},
        scratch_shapes=[pltpu.SemaphoreType.DMA((n,)), pltpu.SemaphoreType.DMA((n,))],
        name="join_sibling_halves",
    )(*g_alls)


def _chip_sums(grads, my_core, land=None):
    if land is None:
        land = _swap_sibling_halves(grads)
    return [_add_sibling_half(g, l, my_core) for g, l in zip(grads, land)]


def _finish_reduce_scatter(parts, sums, g_alls, layer, my_chip, my_core, join=True):
    g_alls = [_add_chips(p, s, ga, layer, my_chip, my_core) for p, s, ga in zip(parts, sums, g_alls)]
    return _join_sibling_halves(g_alls, layer) if join else g_alls


def _allreduce_small(v):
    R = v.shape[0]

    def body(v_ref, out_ref, sib_ref, chip_ref, sum_ref, send_sems, recv_sems):
        x, y, c, me, chips = _place()
        swap = _remote(v_ref, sib_ref, send_sems.at[0], recv_sems.at[0], (x, y, 1 - c))
        swap.start()
        swap.wait()
        sum_ref[...] = v_ref[...] + sib_ref[...]
        copies = []
        for j, chip in enumerate(chips):
            cp = _remote(sum_ref, chip_ref.at[me], send_sems.at[1 + j], recv_sems.at[1 + j], (*chip, c))
            cp.start()
            copies.append(cp)
        chip_ref[me] = sum_ref[...]
        for cp in copies:
            cp.wait()
        out_ref[...] = ((chip_ref[0] + chip_ref[1]) + chip_ref[2]) + chip_ref[3]

    vm = pl.BlockSpec(memory_space=pltpu.VMEM)
    return _call(
        body, out_shape=_sds((R, 128), F32), in_specs=[vm], out_specs=vm,
        scratch_shapes=[
            pltpu.VMEM((R, 128), F32), pltpu.VMEM((N_CHIPS, R, 128), F32), pltpu.VMEM((R, 128), F32),
            pltpu.SemaphoreType.DMA((4,)), pltpu.SemaphoreType.DMA((4,)),
        ],
        name="allreduce_small", compiler_params=_cp(),
    )(v)


def _pack_small(parts):
    rows = []
    for p in parts:
        flat = p.reshape(-1).astype(F32)
        pad = (-flat.shape[0]) % 128
        rows.append(jnp.pad(flat, (0, pad)).reshape(-1, 128))
    packed = jnp.concatenate(rows, axis=0)
    pad_rows = (-packed.shape[0]) % 8
    return jnp.pad(packed, ((0, pad_rows), (0, 0)))


def _unpack_small(packed, like):
    out, r = [], 0
    for p in like:
        n = p.size
        nr = -(-n // 128)
        out.append(packed[r:r + nr].reshape(-1)[:n].reshape(p.shape))
        r += nr
    return out


BIG = ("w_in", "w_ret_o", "w_pool_o", "w_mem_kv", "w_mem_o", "w_out", "w_ff1", "w_ff2")


def _rope_tables(S):
    inv = ROPE_BASE ** (-jnp.arange(0, QK_DIM, 2, dtype=F32) / QK_DIM)
    ang = jnp.arange(S).astype(F32)[:, None] * inv[None, :]
    cos, sin = jnp.cos(ang), jnp.sin(ang)
    return jnp.concatenate([cos, cos], axis=1), jnp.concatenate([-sin, sin], axis=1)


MIXER_W = ("w_ret_o", "w_pool_o", "w_mem_kv", "w_mem_o", "w_out")
AFTER_IN = MIXER_W + ("w_ff1", "w_ff2")
GATHER_PLAN = {
    ("in_proj", 0): [(0, AFTER_IN)],
    ("retention", 0): [(1, ("w_in",))],
    ("ff1", 0): [(1, MIXER_W)],
    ("ff2", 0): [(1, ("w_ff1",))],
    ("in_proj", 1): [(1, ("w_ff2",)), (2, ("w_in",))],
    ("retention", 1): [(2, MIXER_W + ("w_ff1",))],
    ("ff1", 1): [(2, ("w_ff2",))],
    ("ff2", 1): [(3, ("w_in",))],
    ("in_proj", 2): [(3, AFTER_IN)],
}


def _layer_fwd(x, layer, wbuf, small, memn, cos, sin, B, S):
    g1, g2, lg, w_grp, scale = small

    def hosted(host):
        keys = [(l2, n) for l2, names in GATHER_PLAN.get((host, layer), ()) for n in names]
        return keys, [wbuf[k] for k in keys]

    W = lambda n: wbuf[(layer, n)]
    keys, bufs = hosted("in_proj")
    proj, h1, bufs = _rms_proj("in_proj", x, g1, W("w_in"), N_PROJ, gather=bufs)
    wbuf.update(zip(keys, bufs))
    keys, bufs = hosted("retention")
    o, y_r, bufs = _retention_fwd(proj, cos, sin, lg, B, S, gather=bufs)
    wbuf.update(zip(keys, bufs))
    y_p = _pool_fwd(proj, w_grp, scale, B, S)
    kv = _rowsharded_mm("mem_kv", memn, W("w_mem_kv"), None)
    o_m = _mem_attn_fwd(proj, kv, B, S)
    x1, merged, o3 = _merge_fwd(x, proj, y_r, y_p, o_m, W("w_ret_o"), W("w_pool_o"), W("w_mem_o"), W("w_out"))
    keys, bufs = hosted("ff1")
    u, h2, bufs = _rms_proj("ff1", x1, g2, W("w_ff1"), FFN_HIDDEN, gather=bufs)
    wbuf.update(zip(keys, bufs))
    keys, bufs = hosted("ff2")
    x2, bufs = _ffn_out(x1, u, W("w_ff2"), gather=bufs)
    wbuf.update(zip(keys, bufs))
    saved = dict(x=x, proj=proj, h1=h1, o=o, y_r=y_r, y_p=y_p, kv=kv, o_m=o_m, merged=merged, o3=o3, x1=x1, h2=h2, u=u)
    return x2, saved


EARLY_GRADS = ("w_ff1", "w_ff2", "w_out", "w_ret_o", "w_pool_o", "w_mem_o")
LATE_GRADS = ("w_in", "w_mem_kv")


def _layer_bwd(dx2, sv, W, small, memn, d_memn, cos, sin, B, S, big_grads, layer, my_chip, my_core):
    g1, g2, lg, w_grp, scale = small
    x, proj, x1, u = sv["x"], sv["proj"], sv["x1"], sv["u"]
    grads = {}
    du = _ffn_out_dgrad(dx2, u, W["w_ff2"])
    grads["w_ff2"] = _rowsharded_wgrad("ff2_wgrad", u, dx2, FFN_HIDDEN, pro=_relu2)
    dx1, dg2, _ = _dgrad_rms("ff1_dgrad", du, W["w_ff1"], x1, g2, dx2)
    grads["w_ff1"] = _colsharded_wgrad_full("ff1_wgrad", sv["h2"], du, FFN_HIDDEN, 1024)
    do3, dproj, dy_r, dy_p, do_m = _merge_bwd(dx1, proj, sv["o3"], W["w_ret_o"], W["w_pool_o"], W["w_mem_o"], W["w_out"])
    grads["w_out"], grads["w_ret_o"], grads["w_pool_o"], grads["w_mem_o"] = _mixer_wgrads(
        sv["merged"], sv["y_r"], sv["y_p"], sv["o_m"], dx1, do3)
    early = [grads[n] for n in EARLY_GRADS]
    d_o, dproj, land = _gn_gate_bwd(dy_r, sv["o"], proj, dproj, swap=early)
    sums = _chip_sums(early, my_core, land)
    dproj, dlg, parts = _retention_bwd(proj, d_o, dproj, cos, sin, lg, B, S, scatter=sums)
    done = _finish_reduce_scatter(parts, sums, [big_grads[n] for n in EARLY_GRADS], layer, my_chip, my_core, join=False)
    dproj, dw_grp, dscale, done = _pool_bwd(proj, dy_p, dproj, w_grp, scale, B, S, join=done, layer=layer)
    big_grads.update(zip(EARLY_GRADS, done))
    dproj, dkv = _mem_attn_bwd(proj, sv["kv"], do_m, dproj, B, S)
    grads["w_mem_kv"] = _rowsharded_wgrad("mem_kv_wgrad", memn, dkv, D_MODEL)
    if d_memn is None:
        d_memn = _rowsharded_dgrad("mem_kv_dgrad", dkv, W["w_mem_kv"], out_dtype=F32)
    else:
        d_memn = _rowsharded_dgrad("mem_kv_dgrad_acc", dkv, W["w_mem_kv"], epi=_add_res, epi_in=d_memn, out_dtype=F32)
    grads["w_in"] = _colsharded_wgrad_full("in_wgrad", sv["h1"], dproj, N_PROJ, 1792)
    sums = _chip_sums([grads[n] for n in LATE_GRADS], my_core)
    dx, dg1, parts = _dgrad_rms("in_dgrad", dproj, W["w_in"], x, g1, dx1, scatter=sums)
    done = _finish_reduce_scatter(parts, sums, [big_grads[n] for n in LATE_GRADS], layer, my_chip, my_core)
    big_grads.update(zip(LATE_GRADS, done))
    small_grads = dict(g1=dg1, g2=dg2, lg=jnp.sum(dlg, axis=(0, 2)).reshape(N_HEADS, 2).T, w_grp=dw_grp, scale=dscale)
    return dx, small_grads, d_memn


def kernel(x, mem, w_in, ret_decay_logit, w_ret_o, w_pool_grp, pool_scale, w_pool_o, w_mem_kv, w_mem_o, w_out, w_ff1, w_ff2, norm1_g, norm2_g, mem_norm_g, final_norm_g, loss_target, m_w_in, m_ret_decay_logit, m_w_ret_o, m_w_pool_grp, m_pool_scale, m_w_pool_o, m_w_mem_kv, m_w_mem_o, m_w_out, m_w_ff1, m_w_ff2, m_norm1_g, m_norm2_g, m_mem_norm_g, m_final_norm_g, v_w_in, v_ret_decay_logit, v_w_ret_o, v_w_pool_grp, v_pool_scale, v_w_pool_o, v_w_mem_kv, v_w_mem_o, v_w_out, v_w_ff1, v_w_ff2, v_norm1_g, v_norm2_g, v_mem_norm_g, v_final_norm_g):
    B, S, _ = x.shape
    T = B * S
    big_w = dict(w_in=w_in, w_ret_o=w_ret_o, w_pool_o=w_pool_o, w_mem_kv=w_mem_kv, w_mem_o=w_mem_o, w_out=w_out, w_ff1=w_ff1, w_ff2=w_ff2)
    big_m = dict(w_in=m_w_in, w_ret_o=m_w_ret_o, w_pool_o=m_w_pool_o, w_mem_kv=m_w_mem_kv, w_mem_o=m_w_mem_o, w_out=m_w_out, w_ff1=m_w_ff1, w_ff2=m_w_ff2)
    big_v = dict(w_in=v_w_in, w_ret_o=v_w_ret_o, w_pool_o=v_w_pool_o, w_mem_kv=v_w_mem_kv, w_mem_o=v_w_mem_o, w_out=v_w_out, w_ff1=v_w_ff1, w_ff2=v_w_ff2)
    small_w = [ret_decay_logit, w_pool_grp, pool_scale, norm1_g, norm2_g, mem_norm_g, final_norm_g]
    small_m = [m_ret_decay_logit, m_w_pool_grp, m_pool_scale, m_norm1_g, m_norm2_g, m_mem_norm_g, m_final_norm_g]
    small_v = [v_ret_decay_logit, v_w_pool_grp, v_pool_scale, v_norm1_g, v_norm2_g, v_mem_norm_g, v_final_norm_g]
    my_chip = (2 * lax.axis_index("x") + lax.axis_index("y")).astype(jnp.int32).reshape(1)
    my_core = lax.axis_index("c").astype(jnp.int32).reshape(1)

    wbuf = {(l, n): _cast_into_slot(big_w[n], l, my_chip) for l in range(DEPTH) for n in BIG}
    wbuf[(0, "w_in")] = _allgather_weights([wbuf[(0, "w_in")]])[0]

    cos, sin = _rope_tables(S)
    log_g = jax.nn.log_sigmoid(ret_decay_logit.astype(F32))
    mem2d = mem.reshape(B * MEM_LEN, D_MODEL)
    memn = _mem_norm(mem2d, mem_norm_g.reshape(1, D_MODEL))
    smalls = [(norm1_g[l].reshape(1, D_MODEL), norm2_g[l].reshape(1, D_MODEL), log_g[l], w_pool_grp[l],
               pool_scale[l].reshape(1, POOL_W)) for l in range(DEPTH)]

    h = x.reshape(T, D_MODEL)
    saved = []
    for l in range(DEPTH):
        h, sv = _layer_fwd(h, l, wbuf, smalls[l], memn, cos, sin, B, S)
        saved.append(sv)
    weights = [{n: wbuf[(l, n)] for n in BIG} for l in range(DEPTH)]

    dh, sq, d_final_g = _loss_head(h, final_norm_g.reshape(1, D_MODEL), loss_target.reshape(T, D_MODEL))
    loss = lax.psum(0.5 * jnp.sum(sq) / D_MODEL, ("x", "y", "c"))

    big_grads = dict.fromkeys(BIG)
    small_grads = [None] * DEPTH
    d_memn = None
    for l in reversed(range(DEPTH)):
        dh, small_grads[l], d_memn = _layer_bwd(
            dh, saved[l], weights[l], smalls[l], memn, d_memn, cos, sin, B, S, big_grads, l, my_chip, my_core)
    d_mem_g = _mem_norm_wgrad(mem2d, d_memn)

    d_logit = jnp.stack([sg["lg"] for sg in small_grads]) * jax.nn.sigmoid(-ret_decay_logit.astype(F32))
    small_g_local = [
        d_logit, jnp.stack([sg["w_grp"] for sg in small_grads]),
        jnp.stack([sg["scale"].reshape(POOL_W) for sg in small_grads]),
        jnp.stack([sg["g1"].reshape(D_MODEL) for sg in small_grads]),
        jnp.stack([sg["g2"].reshape(D_MODEL) for sg in small_grads]),
        d_mem_g.reshape(D_MODEL), d_final_g.reshape(D_MODEL),
    ]
    small_g = _allreduce_small(_pack_small(small_g_local))
    s_delta, s_m, s_v = _adamw(_pack_small(small_w), small_g, _pack_small(small_m), _pack_small(small_v))
    small_g, s_delta, s_m, s_v = (_unpack_small(a, small_w) for a in (small_g, s_delta, s_m, s_v))

    big_out = {}
    for n in BIG:
        w = big_w[n]
        g = big_grads[n]
        flat = lambda a: a.reshape(-1, a.shape[-1])
        d, nm, nv = _adamw(flat(w), flat(g), flat(big_m[n]), flat(big_v[n]))
        big_out[n] = (g, d.reshape(w.shape), nm.reshape(w.shape), nv.reshape(w.shape))

    order = ["w_in", "ret_decay_logit", "w_ret_o", "w_pool_grp", "pool_scale", "w_pool_o", "w_mem_kv", "w_mem_o",
             "w_out", "w_ff1", "w_ff2", "norm1_g", "norm2_g", "mem_norm_g", "final_norm_g"]
    small_names = ["ret_decay_logit", "w_pool_grp", "pool_scale", "norm1_g", "norm2_g", "mem_norm_g", "final_norm_g"]
    outs = [[], [], [], []]
    for n in order:
        if n in big_out:
            vals = big_out[n]
        else:
            i = small_names.index(n)
            vals = (small_g[i], s_delta[i], s_m[i], s_v[i])
        for k in range(4):
            outs[k].append(vals[k])
    return (loss, dh.reshape(B, S, D_MODEL), *outs[0], *outs[1], *outs[2], *outs[3])
```
